```python
import math
import jax, jax.numpy as jnp
from jax import lax
import numpy as np

D_MODEL = 1024
BATCH = 8
SEQ = 8192
DEPTH = 4

N_HEADS = 16
N_KV_HEADS = 4
HEAD_DIM = 64
GROUP = N_HEADS // N_KV_HEADS
WINDOW = 128
BLOCK = 128
ROPE_THETA = 10000.0
D_RNN = 1024
RNN_BLOCKS = 4
RNN_BLOCK_W = D_RNN // RNN_BLOCKS
CONV_W = 4
LRU_C = 8.0
D_FF = 2816
DEEPNORM_ALPHA = (2.0 * DEPTH) ** 0.25
DEEPNORM_BETA = (8.0 * DEPTH) ** -0.25
LN_EPS = 1e-5
N_MIXERS = 2
N_ATTN_LAYERS = (DEPTH + 1) // 2
N_LRU_LAYERS = DEPTH // 2
QKV_COLS = (N_HEADS + 2 * N_KV_HEADS) * HEAD_DIM

kernel_name = "hybrid_swa_sink_rglru_macaron_deepnorm"


def _layernorm(x, g, b):
    xf = x.astype(jnp.float32)
    mu = jnp.mean(xf, axis=-1, keepdims=True)
    xc = xf - mu
    var = jnp.mean(xc * xc, axis=-1, keepdims=True)
    y = xc * lax.rsqrt(var + LN_EPS) * g.astype(jnp.float32) + b.astype(jnp.float32)
    return y.astype(x.dtype)


def _swiglu(x, w_in, w_out):
    g, u = jnp.split(x @ w_in, 2, axis=-1)
    return (jax.nn.silu(g) * u) @ w_out


def _rope(t, cos, sin):
    t1, t2 = jnp.split(t, 2, axis=-1)
    c = cos[None, :, None, :]
    s = sin[None, :, None, :]
    out = jnp.concatenate([t1 * c - t2 * s, t2 * c + t1 * s], axis=-1)
    return out.astype(t.dtype)


def _band(t):
    prev = jnp.pad(t[:, :-1], ((0, 0), (1, 0), (0, 0), (0, 0), (0, 0)))
    return jnp.concatenate([prev, t], axis=2)


def _swa_sink_attention(x, w_qkv, sinks, w_o, cos, sin):
    B, S, _ = x.shape
    nb = S // BLOCK
    qkv = x @ w_qkv
    q, k, v = jnp.split(qkv, [N_HEADS * HEAD_DIM, (N_HEADS + N_KV_HEADS) * HEAD_DIM], axis=-1)
    q = _rope(q.reshape(B, S, N_HEADS, HEAD_DIM), cos, sin)
    k = _rope(k.reshape(B, S, N_KV_HEADS, HEAD_DIM), cos, sin)
    v = v.reshape(B, S, N_KV_HEADS, HEAD_DIM)
    q = q.reshape(B, nb, BLOCK, N_KV_HEADS, GROUP, HEAD_DIM)
    kb = _band(k.reshape(B, nb, BLOCK, N_KV_HEADS, HEAD_DIM))
    vb = _band(v.reshape(B, nb, BLOCK, N_KV_HEADS, HEAD_DIM))
    s = jnp.einsum('bnqkgd,bnjkd->bnkgqj', q, kb).astype(jnp.float32) * (HEAD_DIM ** -0.5)
    qi = jnp.arange(BLOCK)[:, None]
    kj = jnp.arange(2 * BLOCK)[None, :]
    dist = qi + BLOCK - kj
    in_window = (dist >= 0) & (dist < WINDOW)
    k_pos = jnp.arange(nb)[:, None] * BLOCK - BLOCK + jnp.arange(2 * BLOCK)[None, :]
    mask = in_window[None, :, :] & (k_pos >= 0)[:, None, :]
    s = jnp.where(mask[None, :, None, None, :, :], s, jnp.finfo(jnp.float32).min)
    sink = sinks.astype(jnp.float32).reshape(N_KV_HEADS, GROUP)[None, None, :, :, None, None]
    m = jnp.maximum(jnp.max(s, axis=-1, keepdims=True), sink)
    p = jnp.exp(s - m)
    denom = jnp.sum(p, axis=-1, keepdims=True) + jnp.exp(sink - m)
    p = (p / denom).astype(x.dtype)
    o = jnp.einsum('bnkgqj,bnjkd->bnqkgd', p, vb).reshape(B, S, N_HEADS * HEAD_DIM)
    return o @ w_o


def _lru_combine(left, right):
    a1, b1 = left
    a2, b2 = right
    return a1 * a2, a2 * b1 + b2


def _rglru_block(x, w_in, conv_w, conv_b, w_ra, b_ra, w_rx, b_rx, lam, w_out):
    B, S, _ = x.shape
    xb, gb = jnp.split(x @ w_in, 2, axis=-1)
    gate = jax.nn.gelu(gb)
    xc = lax.conv_general_dilated(
        xb, conv_w[:, None, :].astype(xb.dtype), window_strides=(1,), padding=[(CONV_W - 1, 0)],
        dimension_numbers=('NWC', 'WIO', 'NWC'), feature_group_count=D_RNN) + conv_b
    xr = xc.reshape(B, S, RNN_BLOCKS, RNN_BLOCK_W)
    r = jax.nn.sigmoid(jnp.einsum('bsnc,ncd->bsnd', xr, w_ra).reshape(B, S, D_RNN) + b_ra)
    i = jax.nn.sigmoid(jnp.einsum('bsnc,ncd->bsnd', xr, w_rx).reshape(B, S, D_RNN) + b_rx)
    log_a = LRU_C * r.astype(jnp.float32) * jax.nn.log_sigmoid(lam.astype(jnp.float32))
    a = jnp.exp(log_a)
    b = jnp.sqrt(-jnp.expm1(2.0 * log_a)) * (i * xc).astype(jnp.float32)
    _, h = lax.associative_scan(_lru_combine, (a, b), axis=1)
    y = h.astype(x.dtype) * gate
    return y @ w_out


def _fwd_setup_inputs(seed: int = 0) -> dict:
    key = jax.random.key(seed)
    ks = jax.random.split(key, 24)
    f32 = jnp.float32
    nrm = lambda k, shape, scale: jax.random.normal(k, shape, f32) * scale
    x = jax.random.normal(ks[0], (BATCH, SEQ, D_MODEL), f32)
    ffn1_w_in = nrm(ks[1], (DEPTH, D_MODEL, 2 * D_FF), D_MODEL ** -0.5)
    ffn1_w_out = nrm(ks[2], (DEPTH, D_FF, D_MODEL), D_FF ** -0.5 * DEEPNORM_BETA)
    ffn2_w_in = nrm(ks[3], (DEPTH, D_MODEL, 2 * D_FF), D_MODEL ** -0.5)
    ffn2_w_out = nrm(ks[4], (DEPTH, D_FF, D_MODEL), D_FF ** -0.5 * DEEPNORM_BETA)
    ln_g = 1.0 + nrm(ks[5], (DEPTH, 3, D_MODEL), 0.02)
    ln_b = nrm(ks[6], (DEPTH, 3, D_MODEL), 0.02)
    attn_w_qkv = nrm(ks[7], (N_ATTN_LAYERS, D_MODEL, QKV_COLS), D_MODEL ** -0.5)
    attn_sinks = nrm(ks[8], (N_ATTN_LAYERS, N_HEADS), 0.5)
    attn_w_o = nrm(ks[9], (N_ATTN_LAYERS, N_HEADS * HEAD_DIM, D_MODEL), (N_HEADS * HEAD_DIM) ** -0.5 * DEEPNORM_BETA)
    lru_w_in = nrm(ks[10], (N_LRU_LAYERS, D_MODEL, 2 * D_RNN), D_MODEL ** -0.5)
    lru_conv_w = nrm(ks[11], (N_LRU_LAYERS, CONV_W, D_RNN), CONV_W ** -0.5)
    lru_conv_b = nrm(ks[12], (N_LRU_LAYERS, D_RNN), 0.01)
    lru_w_ra = nrm(ks[13], (N_LRU_LAYERS, RNN_BLOCKS, RNN_BLOCK_W, RNN_BLOCK_W), RNN_BLOCK_W ** -0.5)
    lru_b_ra = nrm(ks[14], (N_LRU_LAYERS, D_RNN), 0.01)
    lru_w_rx = nrm(ks[15], (N_LRU_LAYERS, RNN_BLOCKS, RNN_BLOCK_W, RNN_BLOCK_W), RNN_BLOCK_W ** -0.5)
    lru_b_rx = nrm(ks[16], (N_LRU_LAYERS, D_RNN), 0.01)
    a_c = jax.random.uniform(ks[17], (N_LRU_LAYERS, D_RNN), f32, 0.9, 0.999)
    sig = a_c ** (1.0 / LRU_C)
    lru_lambda = jnp.log(sig) - jnp.log1p(-sig)
    lru_w_out = nrm(ks[18], (N_LRU_LAYERS, D_RNN, D_MODEL), D_RNN ** -0.5 * DEEPNORM_BETA)
    return {"x": x, "ffn1_w_in": ffn1_w_in, "ffn1_w_out": ffn1_w_out,
            "ffn2_w_in": ffn2_w_in, "ffn2_w_out": ffn2_w_out, "ln_g": ln_g, "ln_b": ln_b,
            "attn_w_qkv": attn_w_qkv, "attn_sinks": attn_sinks, "attn_w_o": attn_w_o,
            "lru_w_in": lru_w_in, "lru_conv_w": lru_conv_w, "lru_conv_b": lru_conv_b,
            "lru_w_ra": lru_w_ra, "lru_b_ra": lru_b_ra, "lru_w_rx": lru_w_rx, "lru_b_rx": lru_b_rx,
            "lru_lambda": lru_lambda, "lru_w_out": lru_w_out}


def _fwd_reference(x, ffn1_w_in, ffn1_w_out, ffn2_w_in, ffn2_w_out, ln_g, ln_b,
              attn_w_qkv, attn_sinks, attn_w_o,
              lru_w_in, lru_conv_w, lru_conv_b, lru_w_ra, lru_b_ra, lru_w_rx, lru_b_rx,
              lru_lambda, lru_w_out):
    S = x.shape[1]
    pos = jnp.arange(S, dtype=jnp.float32)
    inv_freq = ROPE_THETA ** (-jnp.arange(0, HEAD_DIM, 2, dtype=jnp.float32) / HEAD_DIM)
    ang = pos[:, None] * inv_freq[None, :]
    cos, sin = jnp.cos(ang), jnp.sin(ang)
    h = x
    for i in range(DEPTH):
        h = _layernorm(DEEPNORM_ALPHA * h + 0.5 * _swiglu(h, ffn1_w_in[i], ffn1_w_out[i]), ln_g[i, 0], ln_b[i, 0])
        j = i // N_MIXERS
        if i % N_MIXERS == 0:
            mix = _swa_sink_attention(h, attn_w_qkv[j], attn_sinks[j], attn_w_o[j], cos, sin)
        else:
            mix = _rglru_block(h, lru_w_in[j], lru_conv_w[j], lru_conv_b[j], lru_w_ra[j], lru_b_ra[j],
                               lru_w_rx[j], lru_b_rx[j], lru_lambda[j], lru_w_out[j])
        h = _layernorm(DEEPNORM_ALPHA * h + mix, ln_g[i, 1], ln_b[i, 1])
        h = _layernorm(DEEPNORM_ALPHA * h + 0.5 * _swiglu(h, ffn2_w_in[i], ffn2_w_out[i]), ln_g[i, 2], ln_b[i, 2])
    return h


import jax as _jax
import jax.numpy as _jnp

TWIN_FORMAT = 'train_step'
FWD_PARAMS = ['x', 'ffn1_w_in', 'ffn1_w_out', 'ffn2_w_in', 'ffn2_w_out', 'ln_g', 'ln_b', 'attn_w_qkv', 'attn_sinks', 'attn_w_o', 'lru_w_in', 'lru_conv_w', 'lru_conv_b', 'lru_w_ra', 'lru_b_ra', 'lru_w_rx', 'lru_b_rx', 'lru_lambda', 'lru_w_out']
TWIN_WEIGHTS = ['ffn1_w_in', 'ffn1_w_out', 'ffn2_w_in', 'ffn2_w_out', 'ln_g', 'ln_b', 'attn_w_qkv', 'attn_sinks', 'attn_w_o', 'lru_w_in', 'lru_conv_w', 'lru_conv_b', 'lru_w_ra', 'lru_b_ra', 'lru_w_rx', 'lru_b_rx', 'lru_lambda', 'lru_w_out']
TWIN_DIFF_INPUT = 'x'
TWIN_INPUTS = ['x', 'ffn1_w_in', 'ffn1_w_out', 'ffn2_w_in', 'ffn2_w_out', 'ln_g', 'ln_b', 'attn_w_qkv', 'attn_sinks', 'attn_w_o', 'lru_w_in', 'lru_conv_w', 'lru_conv_b', 'lru_w_ra', 'lru_b_ra', 'lru_w_rx', 'lru_b_rx', 'lru_lambda', 'lru_w_out', 'loss_target', 'm_ffn1_w_in', 'm_ffn1_w_out', 'm_ffn2_w_in', 'm_ffn2_w_out', 'm_ln_g', 'm_ln_b', 'm_attn_w_qkv', 'm_attn_sinks', 'm_attn_w_o', 'm_lru_w_in', 'm_lru_conv_w', 'm_lru_conv_b', 'm_lru_w_ra', 'm_lru_b_ra', 'm_lru_w_rx', 'm_lru_b_rx', 'm_lru_lambda', 'm_lru_w_out', 'v_ffn1_w_in', 'v_ffn1_w_out', 'v_ffn2_w_in', 'v_ffn2_w_out', 'v_ln_g', 'v_ln_b', 'v_attn_w_qkv', 'v_attn_sinks', 'v_attn_w_o', 'v_lru_w_in', 'v_lru_conv_w', 'v_lru_conv_b', 'v_lru_w_ra', 'v_lru_b_ra', 'v_lru_w_rx', 'v_lru_b_rx', 'v_lru_lambda', 'v_lru_w_out']
TWIN_OUTPUTS = ['loss', 'grad_x', 'grad_ffn1_w_in', 'grad_ffn1_w_out', 'grad_ffn2_w_in', 'grad_ffn2_w_out', 'grad_ln_g', 'grad_ln_b', 'grad_attn_w_qkv', 'grad_attn_sinks', 'grad_attn_w_o', 'grad_lru_w_in', 'grad_lru_conv_w', 'grad_lru_conv_b', 'grad_lru_w_ra', 'grad_lru_b_ra', 'grad_lru_w_rx', 'grad_lru_b_rx', 'grad_lru_lambda', 'grad_lru_w_out', 'delta_ffn1_w_in', 'delta_ffn1_w_out', 'delta_ffn2_w_in', 'delta_ffn2_w_out', 'delta_ln_g', 'delta_ln_b', 'delta_attn_w_qkv', 'delta_attn_sinks', 'delta_attn_w_o', 'delta_lru_w_in', 'delta_lru_conv_w', 'delta_lru_conv_b', 'delta_lru_w_ra', 'delta_lru_b_ra', 'delta_lru_w_rx', 'delta_lru_b_rx', 'delta_lru_lambda', 'delta_lru_w_out', 'new_m_ffn1_w_in', 'new_m_ffn1_w_out', 'new_m_ffn2_w_in', 'new_m_ffn2_w_out', 'new_m_ln_g', 'new_m_ln_b', 'new_m_attn_w_qkv', 'new_m_attn_sinks', 'new_m_attn_w_o', 'new_m_lru_w_in', 'new_m_lru_conv_w', 'new_m_lru_conv_b', 'new_m_lru_w_ra', 'new_m_lru_b_ra', 'new_m_lru_w_rx', 'new_m_lru_b_rx', 'new_m_lru_lambda', 'new_m_lru_w_out', 'new_v_ffn1_w_in', 'new_v_ffn1_w_out', 'new_v_ffn2_w_in', 'new_v_ffn2_w_out', 'new_v_ln_g', 'new_v_ln_b', 'new_v_attn_w_qkv', 'new_v_attn_sinks', 'new_v_attn_w_o', 'new_v_lru_w_in', 'new_v_lru_conv_w', 'new_v_lru_conv_b', 'new_v_lru_w_ra', 'new_v_lru_b_ra', 'new_v_lru_w_rx', 'new_v_lru_b_rx', 'new_v_lru_lambda', 'new_v_lru_w_out']
TWIN_LEAF_KINDS = {'loss': 'loss', 'grad_x': 'grad_x', 'grad_ffn1_w_in': 'grad_w', 'grad_ffn1_w_out': 'grad_w', 'grad_ffn2_w_in': 'grad_w', 'grad_ffn2_w_out': 'grad_w', 'grad_ln_g': 'grad_w', 'grad_ln_b': 'grad_w', 'grad_attn_w_qkv': 'grad_w', 'grad_attn_sinks': 'grad_w', 'grad_attn_w_o': 'grad_w', 'grad_lru_w_in': 'grad_w', 'grad_lru_conv_w': 'grad_w', 'grad_lru_conv_b': 'grad_w', 'grad_lru_w_ra': 'grad_w', 'grad_lru_b_ra': 'grad_w', 'grad_lru_w_rx': 'grad_w', 'grad_lru_b_rx': 'grad_w', 'grad_lru_lambda': 'grad_w', 'grad_lru_w_out': 'grad_w', 'delta_ffn1_w_in': 'delta_w', 'delta_ffn1_w_out': 'delta_w', 'delta_ffn2_w_in': 'delta_w', 'delta_ffn2_w_out': 'delta_w', 'delta_ln_g': 'delta_w', 'delta_ln_b': 'delta_w', 'delta_attn_w_qkv': 'delta_w', 'delta_attn_sinks': 'delta_w', 'delta_attn_w_o': 'delta_w', 'delta_lru_w_in': 'delta_w', 'delta_lru_conv_w': 'delta_w', 'delta_lru_conv_b': 'delta_w', 'delta_lru_w_ra': 'delta_w', 'delta_lru_b_ra': 'delta_w', 'delta_lru_w_rx': 'delta_w', 'delta_lru_b_rx': 'delta_w', 'delta_lru_lambda': 'delta_w', 'delta_lru_w_out': 'delta_w', 'new_m_ffn1_w_in': 'new_m', 'new_m_ffn1_w_out': 'new_m', 'new_m_ffn2_w_in': 'new_m', 'new_m_ffn2_w_out': 'new_m', 'new_m_ln_g': 'new_m', 'new_m_ln_b': 'new_m', 'new_m_attn_w_qkv': 'new_m', 'new_m_attn_sinks': 'new_m', 'new_m_attn_w_o': 'new_m', 'new_m_lru_w_in': 'new_m', 'new_m_lru_conv_w': 'new_m', 'new_m_lru_conv_b': 'new_m', 'new_m_lru_w_ra': 'new_m', 'new_m_lru_b_ra': 'new_m', 'new_m_lru_w_rx': 'new_m', 'new_m_lru_b_rx': 'new_m', 'new_m_lru_lambda': 'new_m', 'new_m_lru_w_out': 'new_m', 'new_v_ffn1_w_in': 'new_v', 'new_v_ffn1_w_out': 'new_v', 'new_v_ffn2_w_in': 'new_v', 'new_v_ffn2_w_out': 'new_v', 'new_v_ln_g': 'new_v', 'new_v_ln_b': 'new_v', 'new_v_attn_w_qkv': 'new_v', 'new_v_attn_sinks': 'new_v', 'new_v_attn_w_o': 'new_v', 'new_v_lru_w_in': 'new_v', 'new_v_lru_conv_w': 'new_v', 'new_v_lru_conv_b': 'new_v', 'new_v_lru_w_ra': 'new_v', 'new_v_lru_b_ra': 'new_v', 'new_v_lru_w_rx': 'new_v', 'new_v_lru_b_rx': 'new_v', 'new_v_lru_lambda': 'new_v', 'new_v_lru_w_out': 'new_v'}


def _forward(args):
    return _fwd_reference(*[args[k] for k in FWD_PARAMS])


def _output_shape():
    def fwd():
        inp = _fwd_setup_inputs(0)
        return _fwd_reference(*[inp[k] for k in FWD_PARAMS])
    out = _jax.eval_shape(fwd)
    return out.shape, out.dtype

N_MICROBATCH = 1
ADAM_LR = 0.001
ADAM_B1 = 0.9
ADAM_B2 = 0.999
ADAM_EPS = 1e-08
ADAM_WD = 0.01
ADAM_STEP = 10
PER_EXAMPLE_BATCH_AXIS = {'x': 0, 'loss_target': 0}
SHARED_INPUTS = []
_WEIGHT_DTYPES = {'ffn1_w_in': _jnp.float32, 'ffn1_w_out': _jnp.float32, 'ffn2_w_in': _jnp.float32, 'ffn2_w_out': _jnp.float32, 'ln_g': _jnp.float32, 'ln_b': _jnp.float32, 'attn_w_qkv': _jnp.float32, 'attn_sinks': _jnp.float32, 'attn_w_o': _jnp.float32, 'lru_w_in': _jnp.float32, 'lru_conv_w': _jnp.float32, 'lru_conv_b': _jnp.float32, 'lru_w_ra': _jnp.float32, 'lru_b_ra': _jnp.float32, 'lru_w_rx': _jnp.float32, 'lru_b_rx': _jnp.float32, 'lru_lambda': _jnp.float32, 'lru_w_out': _jnp.float32}
MOMENT_SCALE = {'ffn1_w_in': 1.165144e-02, 'ffn1_w_out': 4.527482e-02, 'ffn2_w_in': 1.159241e-02, 'ffn2_w_out': 4.504414e-02, 'ln_g': 1.869717e+01, 'ln_b': 1.750740e+00, 'attn_w_qkv': 1.799182e-02, 'attn_sinks': 9.060459e-03, 'attn_w_o': 3.531556e-02, 'lru_w_in': 3.484651e-02, 'lru_conv_w': 3.813115e-02, 'lru_conv_b': 6.250273e-01, 'lru_w_ra': 9.979134e-03, 'lru_b_ra': 1.039443e-02, 'lru_w_rx': 1.828487e-02, 'lru_b_rx': 1.443629e-02, 'lru_lambda': 2.121933e-02, 'lru_w_out': 9.202092e-02}


def _to_microbatches(a, axis):
    t = _jnp.moveaxis(a, axis, 0)
    t = t.reshape((N_MICROBATCH, t.shape[0] // N_MICROBATCH) + t.shape[1:])
    return _jnp.moveaxis(t, 1, axis + 1)


def setup_inputs(seed: int = 0) -> dict:
    inp = _fwd_setup_inputs(seed)
    key = _jax.random.fold_in(_jax.random.key(seed), 7919)
    shape, _ = _output_shape()
    out = dict(inp)
    out["loss_target"] = _jax.random.normal(_jax.random.fold_in(key, 0), shape, _jnp.float32)
    for i, name in enumerate(TWIN_WEIGHTS):
        w = inp[name].astype(_jnp.float32)
        if MOMENT_SCALE is None:
            s = _jnp.sqrt(_jnp.mean(_jnp.square(w)) + 1e-30)
        else:
            s = MOMENT_SCALE[name]
        km, kv = _jax.random.split(_jax.random.fold_in(key, i + 1))
        out[name] = w
        out["m_" + name] = s * _jax.random.normal(km, w.shape, _jnp.float32)
        out["v_" + name] = (s * s) * _jax.random.uniform(kv, w.shape, _jnp.float32, 0.5, 1.5)
    if N_MICROBATCH > 1:
        for name, axis in PER_EXAMPLE_BATCH_AXIS.items():
            out[name] = _to_microbatches(out[name], axis)
    return {'x': out['x'], 'ffn1_w_in': out['ffn1_w_in'], 'ffn1_w_out': out['ffn1_w_out'], 'ffn2_w_in': out['ffn2_w_in'], 'ffn2_w_out': out['ffn2_w_out'], 'ln_g': out['ln_g'], 'ln_b': out['ln_b'], 'attn_w_qkv': out['attn_w_qkv'], 'attn_sinks': out['attn_sinks'], 'attn_w_o': out['attn_w_o'], 'lru_w_in': out['lru_w_in'], 'lru_conv_w': out['lru_conv_w'], 'lru_conv_b': out['lru_conv_b'], 'lru_w_ra': out['lru_w_ra'], 'lru_b_ra': out['lru_b_ra'], 'lru_w_rx': out['lru_w_rx'], 'lru_b_rx': out['lru_b_rx'], 'lru_lambda': out['lru_lambda'], 'lru_w_out': out['lru_w_out'], 'loss_target': out['loss_target'], 'm_ffn1_w_in': out['m_ffn1_w_in'], 'm_ffn1_w_out': out['m_ffn1_w_out'], 'm_ffn2_w_in': out['m_ffn2_w_in'], 'm_ffn2_w_out': out['m_ffn2_w_out'], 'm_ln_g': out['m_ln_g'], 'm_ln_b': out['m_ln_b'], 'm_attn_w_qkv': out['m_attn_w_qkv'], 'm_attn_sinks': out['m_attn_sinks'], 'm_attn_w_o': out['m_attn_w_o'], 'm_lru_w_in': out['m_lru_w_in'], 'm_lru_conv_w': out['m_lru_conv_w'], 'm_lru_conv_b': out['m_lru_conv_b'], 'm_lru_w_ra': out['m_lru_w_ra'], 'm_lru_b_ra': out['m_lru_b_ra'], 'm_lru_w_rx': out['m_lru_w_rx'], 'm_lru_b_rx': out['m_lru_b_rx'], 'm_lru_lambda': out['m_lru_lambda'], 'm_lru_w_out': out['m_lru_w_out'], 'v_ffn1_w_in': out['v_ffn1_w_in'], 'v_ffn1_w_out': out['v_ffn1_w_out'], 'v_ffn2_w_in': out['v_ffn2_w_in'], 'v_ffn2_w_out': out['v_ffn2_w_out'], 'v_ln_g': out['v_ln_g'], 'v_ln_b': out['v_ln_b'], 'v_attn_w_qkv': out['v_attn_w_qkv'], 'v_attn_sinks': out['v_attn_sinks'], 'v_attn_w_o': out['v_attn_w_o'], 'v_lru_w_in': out['v_lru_w_in'], 'v_lru_conv_w': out['v_lru_conv_w'], 'v_lru_conv_b': out['v_lru_conv_b'], 'v_lru_w_ra': out['v_lru_w_ra'], 'v_lru_b_ra': out['v_lru_b_ra'], 'v_lru_w_rx': out['v_lru_w_rx'], 'v_lru_b_rx': out['v_lru_b_rx'], 'v_lru_lambda': out['v_lru_lambda'], 'v_lru_w_out': out['v_lru_w_out']}


def _loss(weights, diff, rest, loss_target):
    with _jax.named_scope("forward"):
        args = {**rest, TWIN_DIFF_INPUT: diff, **{k: w.astype(_WEIGHT_DTYPES[k]) for k, w in weights.items()}}
        y = _forward(args)
    with _jax.named_scope("loss_head"):
        err = _jnp.square(y.astype(_jnp.float32) - loss_target)
        return 0.5 * _jnp.sum(_jnp.mean(err, axis=-1)) if err.ndim else 0.5 * err


def _adamw(w, g, m, v):
    m = ADAM_B1 * m + (1.0 - ADAM_B1) * g
    v = ADAM_B2 * v + (1.0 - ADAM_B2) * _jnp.square(g)
    m_hat = m / (1.0 - ADAM_B1 ** ADAM_STEP)
    v_hat = v / (1.0 - ADAM_B2 ** ADAM_STEP)
    delta = -ADAM_LR * (m_hat / (_jnp.sqrt(v_hat) + ADAM_EPS) + ADAM_WD * w)
    return delta, m, v


def reference(x, ffn1_w_in, ffn1_w_out, ffn2_w_in, ffn2_w_out, ln_g, ln_b, attn_w_qkv, attn_sinks, attn_w_o, lru_w_in, lru_conv_w, lru_conv_b, lru_w_ra, lru_b_ra, lru_w_rx, lru_b_rx, lru_lambda, lru_w_out, loss_target, m_ffn1_w_in, m_ffn1_w_out, m_ffn2_w_in, m_ffn2_w_out, m_ln_g, m_ln_b, m_attn_w_qkv, m_attn_sinks, m_attn_w_o, m_lru_w_in, m_lru_conv_w, m_lru_conv_b, m_lru_w_ra, m_lru_b_ra, m_lru_w_rx, m_lru_b_rx, m_lru_lambda, m_lru_w_out, v_ffn1_w_in, v_ffn1_w_out, v_ffn2_w_in, v_ffn2_w_out, v_ln_g, v_ln_b, v_attn_w_qkv, v_attn_sinks, v_attn_w_o, v_lru_w_in, v_lru_conv_w, v_lru_conv_b, v_lru_w_ra, v_lru_b_ra, v_lru_w_rx, v_lru_b_rx, v_lru_lambda, v_lru_w_out):
    given = dict(x=x, ffn1_w_in=ffn1_w_in, ffn1_w_out=ffn1_w_out, ffn2_w_in=ffn2_w_in, ffn2_w_out=ffn2_w_out, ln_g=ln_g, ln_b=ln_b, attn_w_qkv=attn_w_qkv, attn_sinks=attn_sinks, attn_w_o=attn_w_o, lru_w_in=lru_w_in, lru_conv_w=lru_conv_w, lru_conv_b=lru_conv_b, lru_w_ra=lru_w_ra, lru_b_ra=lru_b_ra, lru_w_rx=lru_w_rx, lru_b_rx=lru_b_rx, lru_lambda=lru_lambda, lru_w_out=lru_w_out, loss_target=loss_target, m_ffn1_w_in=m_ffn1_w_in, m_ffn1_w_out=m_ffn1_w_out, m_ffn2_w_in=m_ffn2_w_in, m_ffn2_w_out=m_ffn2_w_out, m_ln_g=m_ln_g, m_ln_b=m_ln_b, m_attn_w_qkv=m_attn_w_qkv, m_attn_sinks=m_attn_sinks, m_attn_w_o=m_attn_w_o, m_lru_w_in=m_lru_w_in, m_lru_conv_w=m_lru_conv_w, m_lru_conv_b=m_lru_conv_b, m_lru_w_ra=m_lru_w_ra, m_lru_b_ra=m_lru_b_ra, m_lru_w_rx=m_lru_w_rx, m_lru_b_rx=m_lru_b_rx, m_lru_lambda=m_lru_lambda, m_lru_w_out=m_lru_w_out, v_ffn1_w_in=v_ffn1_w_in, v_ffn1_w_out=v_ffn1_w_out, v_ffn2_w_in=v_ffn2_w_in, v_ffn2_w_out=v_ffn2_w_out, v_ln_g=v_ln_g, v_ln_b=v_ln_b, v_attn_w_qkv=v_attn_w_qkv, v_attn_sinks=v_attn_sinks, v_attn_w_o=v_attn_w_o, v_lru_w_in=v_lru_w_in, v_lru_conv_w=v_lru_conv_w, v_lru_conv_b=v_lru_conv_b, v_lru_w_ra=v_lru_w_ra, v_lru_b_ra=v_lru_b_ra, v_lru_w_rx=v_lru_w_rx, v_lru_b_rx=v_lru_b_rx, v_lru_lambda=v_lru_lambda, v_lru_w_out=v_lru_w_out)
    weights = {n: given[n] for n in TWIN_WEIGHTS}
    shared = {n: given[n] for n in SHARED_INPUTS}
    per_example = {n: given[n] for n in ['x']}
    grad_fn = _jax.value_and_grad(_loss, argnums=(0, 1))

    def one_microbatch(ex, loss_target):
        ex = dict(ex)
        diff = ex.pop(TWIN_DIFF_INPUT)
        return grad_fn(weights, diff, {**shared, **ex}, loss_target)

    if N_MICROBATCH == 1:
        loss, (grad_w, grad_x) = one_microbatch(per_example, given["loss_target"])
    else:
        def body(carry, xs):
            loss_sum, grad_sum = carry
            l_k, (gw_k, gx_k) = one_microbatch(xs[0], xs[1])
            with _jax.named_scope("update"):
                return (loss_sum + l_k, _jax.tree.map(_jnp.add, grad_sum, gw_k)), gx_k

        init = (_jnp.zeros((), _jnp.float32), _jax.tree.map(_jnp.zeros_like, weights))
        (loss, grad_w), grad_x = _jax.lax.scan(body, init, (per_example, given["loss_target"]))
    with _jax.named_scope("update"):
        delta_w, new_m, new_v = {}, {}, {}
        for n in TWIN_WEIGHTS:
            delta_w[n], new_m[n], new_v[n] = _adamw(weights[n], grad_w[n], given["m_" + n], given["v_" + n])
    return (loss, grad_x, *[grad_w[n] for n in TWIN_WEIGHTS], *[delta_w[n] for n in TWIN_WEIGHTS],
            *[new_m[n] for n in TWIN_WEIGHTS], *[new_v[n] for n in TWIN_WEIGHTS])
```

```python
import functools
import math

import jax
import jax.numpy as jnp
from jax import lax
from jax.experimental import pallas as pl
from jax.experimental.pallas import tpu as pltpu

F32 = jnp.float32
BF16 = jnp.bfloat16
MESH = pl.DeviceIdType.MESH

N_HEADS = 16
N_KV_HEADS = 4
HEAD_DIM = 64
GROUP = N_HEADS // N_KV_HEADS
ATTN_BLOCK = 128
ROPE_THETA = 10000.0
RNN_BLOCKS = 4
CONV_W = 4
LRU_C = 8.0
LN_EPS = 1e-5
ADAM_LR = 0.001
ADAM_B1 = 0.9
ADAM_B2 = 0.999
ADAM_EPS = 1e-08
ADAM_WD = 0.01
ADAM_STEP = 10
N_CHIPS = 4
NEG_BIG = -1e30
VMEM_LIMIT_MB = 56
ROW_TILE = 512
SMALL_ROWS = 48


def _cparams(sem):
    return pltpu.CompilerParams(dimension_semantics=sem, vmem_limit_bytes=VMEM_LIMIT_MB << 20)


def _tile(n, pref):
    if n <= pref:
        return n
    for t in range(pref - pref % 16, 0, -16):
        if n % t == 0:
            return t
    raise ValueError((n, pref))


def _matmul(name, grid, a, b, outs, acc_shape, *, ta=False, tb=False, extras=(), epilogue=None,
            n_outer=False, alias_in=None):
    gm, gn, gk = grid
    if n_outer:
        g = (gn, gm, gk)
        ijk = lambda p, q, k: (q, p, k)
    else:
        g = (gm, gn, gk)
        ijk = lambda p, q, k: (p, q, k)
    w3 = lambda f: (lambda p, q, k: f(*ijk(p, q, k)))
    w2 = lambda f: (lambda p, q, k: f(*ijk(p, q, k)[:2]))
    in_specs = [pl.BlockSpec(a[1], w3(a[2])), pl.BlockSpec(b[1], w3(b[2]))]
    in_specs += [pl.BlockSpec(e[1], w2(e[2])) for e in extras]
    operands = [a[0], b[0]] + [e[0] for e in extras]
    io_alias = {}
    n_alias = 0
    if alias_in is not None:
        in_specs.append(pl.BlockSpec(memory_space=pl.ANY))
        operands.append(alias_in)
        io_alias = {len(operands) - 1: 0}
        n_alias = 1
    ne, no = len(extras), len(outs)
    dims = (((0 if ta else 1,), (1 if tb else 0,)), ((), ()))

    def body(*refs):
        a_ref, b_ref = refs[0], refs[1]
        e_refs = refs[2:2 + ne]
        o_refs = refs[2 + ne + n_alias:2 + ne + n_alias + no]
        part = lax.dot_general(a_ref[...], b_ref[...], dims, preferred_element_type=F32)

        def finish(acc):
            res = epilogue(acc, *[r[...] for r in e_refs]) if epilogue is not None else (acc,)
            for r, v in zip(o_refs, res):
                r[...] = v.astype(r.dtype)

        if gk == 1:
            finish(part)
        else:
            acc_ref = refs[-1]
            k = pl.program_id(2)

            @pl.when(k == 0)
            def _():
                acc_ref[...] = part

            @pl.when(k > 0)
            def _():
                acc_ref[...] += part

            @pl.when(k == gk - 1)
            def _():
                finish(acc_ref[...])

    res = pl.pallas_call(
        body, name=name, grid=g, in_specs=in_specs,
        out_specs=[pl.BlockSpec(o[2], w2(o[3])) for o in outs],
        out_shape=[jax.ShapeDtypeStruct(o[0], o[1]) for o in outs],
        scratch_shapes=[pltpu.VMEM(acc_shape, F32)] if gk > 1 else [],
        input_output_aliases=io_alias,
        compiler_params=_cparams(("parallel", "parallel", "arbitrary")),
    )(*operands)
    return res


def _rowwise(name, nsteps, fn, ins, outs, accs=(), prefetch=None):
    n_in, n_out, n_acc = len(ins), len(outs), len(accs)
    n_pre = 0 if prefetch is None else 1

    def body(*refs):
        refs = refs[n_pre:]
        i = pl.program_id(0)
        res = fn(i, *[r[...] for r in refs[:n_in]])
        for r, v in zip(refs[n_in:n_in + n_out], res[:n_out]):
            r[...] = v.astype(r.dtype)
        acc_refs = refs[n_in + n_out:n_in + n_out + n_acc]
        if n_acc:
            @pl.when(i == 0)
            def _():
                for r in acc_refs:
                    r[...] = jnp.zeros_like(r)

            for r, v in zip(acc_refs, res[n_out:]):
                r[...] += v

    if prefetch is None:
        zero = lambda shape: (lambda i: (0,) * len(shape))
    else:
        zero = lambda shape: (lambda i, p: (0,) * len(shape))
    in_specs = [pl.BlockSpec(b, m) for _, b, m in ins]
    out_specs = [pl.BlockSpec(o[2], o[3]) for o in outs] + [pl.BlockSpec(s, zero(s)) for s in accs]
    out_shape = [jax.ShapeDtypeStruct(o[0], o[1]) for o in outs] + [jax.ShapeDtypeStruct(s, F32) for s in accs]
    cp = _cparams(("arbitrary",))
    if prefetch is None:
        call = pl.pallas_call(body, name=name, grid=(nsteps,), in_specs=in_specs, out_specs=out_specs,
                              out_shape=out_shape, compiler_params=cp)
        return call(*[x[0] for x in ins])
    gs = pltpu.PrefetchScalarGridSpec(num_scalar_prefetch=1, grid=(nsteps,), in_specs=in_specs, out_specs=out_specs)
    call = pl.pallas_call(body, name=name, grid_spec=gs, out_shape=out_shape, compiler_params=cp)
    return call(prefetch, *[x[0] for x in ins])


def _rows(arr, tr, cols=None, cb=0):
    cols = arr.shape[1] if cols is None else cols
    return (arr, (tr, cols), lambda i: (i, cb))


def _whole(arr):
    return (arr, arr.shape, lambda i: (0,) * arr.ndim)


def _layernorm_fwd(z, g, b):
    mu = jnp.mean(z, axis=-1, keepdims=True)
    xc = z - mu
    var = jnp.mean(xc * xc, axis=-1, keepdims=True)
    return xc * lax.rsqrt(var + LN_EPS) * g + b


def _gelu_tanh(x):
    c = math.sqrt(2.0 / math.pi)
    return x * (0.5 * (1.0 + jnp.tanh(c * (x + 0.044715 * (x * x * x)))))


@jax.custom_jvp
def _expm1(x):
    return jnp.where(jnp.abs(x) < 0.5, jnp.tanh(0.5 * x) * (jnp.exp(x) + 1.0), jnp.exp(x) - 1.0)


@_expm1.defjvp
def _expm1_jvp(primals, tangents):
    (x,), (t,) = primals, tangents
    return _expm1(x), jnp.exp(x) * t


def _log_sigmoid(x):
    return jnp.minimum(x, 0.0) - jnp.log1p(jnp.exp(-jnp.abs(x)))


def _lru_gates(pre, xc, b_ra, b_rx, lam):
    w = xc.shape[-1]
    r = jax.nn.sigmoid(pre[:, :w] + b_ra)
    ig = jax.nn.sigmoid(pre[:, w:] + b_rx)
    log_a = LRU_C * r * _log_sigmoid(lam)
    a = jnp.exp(log_a)
    b = jnp.sqrt(-_expm1(2.0 * log_a)) * (ig * xc)
    return a, b


def _swap_halves(x):
    n = x.shape[1]
    first = (lax.broadcasted_iota(jnp.int32, x.shape, 1) % HEAD_DIM) < (HEAD_DIM // 2)
    return jnp.where(first, pltpu.roll(x, n - HEAD_DIM // 2, 1), pltpu.roll(x, HEAD_DIM // 2, 1))


def _shift_down(prev8, cur, s):
    ext = jnp.concatenate([prev8, cur], axis=0)
    return pltpu.roll(ext, s, 0)[8:]


def _shift_up(cur, next8, s):
    ext = jnp.concatenate([cur, next8], axis=0)
    return pltpu.roll(ext, ext.shape[0] - s, 0)[:cur.shape[0]]


def _ln_epilogue(alpha, scale):
    def epi(acc, hprev, g, b):
        z = alpha * hprev + scale * acc
        h = _layernorm_fwd(z, g, b)
        return z, h, h
    return epi


def _proj_ln(name, act, w, layer, hprev, g, b, alpha, scale):
    T, K = act.shape
    D = w.shape[2]
    bm = _tile(T, ROW_TILE)
    row = lambda i, j: (i, 0)
    return _matmul(
        name, (T // bm, 1, 1),
        (act, (bm, K), lambda i, j, k: (i, 0)), (w, (None, K, D), lambda i, j, k: (layer, 0, 0)),
        [((T, D), F32, (bm, D), row), ((T, D), F32, (bm, D), row), ((T, D), BF16, (bm, D), row)],
        (bm, D),
        extras=[(hprev, (bm, D), row), (g, (1, D), lambda i, j: (0, 0)), (b, (1, D), lambda i, j: (0, 0))],
        epilogue=_ln_epilogue(alpha, scale))


def _ln_bwd(name, dh, z, g, scale):
    T, D = z.shape
    tr = _tile(T, ROW_TILE)

    def fn(i, dh, z, g):
        mu = jnp.mean(z, axis=-1, keepdims=True)
        xc = z - mu
        var = jnp.mean(xc * xc, axis=-1, keepdims=True)
        rstd = lax.rsqrt(var + LN_EPS)
        xhat = xc * rstd
        dxh = dh * g
        dz = rstd * (dxh - jnp.mean(dxh, axis=-1, keepdims=True) - xhat * jnp.mean(dxh * xhat, axis=-1, keepdims=True))
        return (dz, scale * dz, jnp.sum(dh * xhat, axis=0, keepdims=True), jnp.sum(dh, axis=0, keepdims=True))

    row = lambda i: (i, 0)
    return _rowwise(name, T // tr, fn, [_rows(dh, tr), _rows(z, tr), _whole(g)],
                    [((T, D), F32, (tr, D), row), ((T, D), BF16, (tr, D), row)], accs=[(1, D), (1, D)])


def _grad_tn(name, a, b, stack, layer, bm, bn, bk=ROW_TILE, a_cb=None, b_cb=None):
    T, M = a.shape
    N = b.shape[1]
    bk = _tile(T, bk)
    lead = stack.ndim - 2
    if lead == 1:
        oblk, omap = (None, bm, bn), (lambda i, j: (layer, i, j))
    else:
        oblk, omap = (None, None, bm, bn), (lambda i, j: (layer, j, 0, 0))
    amap = (lambda i, j, k: (k, i)) if a_cb is None else (lambda i, j, k: (k, j))
    bmap = (lambda i, j, k: (k, j))
    gm = M // bm if a_cb is None else 1
    (out,) = _matmul(name, (gm, N // bn, T // bk), (a, (bk, bm), amap), (b, (bk, bn), bmap),
                     [(stack.shape, stack.dtype, oblk, omap)], (bm, bn), ta=True, alias_in=stack)
    return out


def _ffn_fwd(hprev, hb, w_in, w_out, layer, g, b, alpha):
    T, D = hb.shape
    N2 = w_in.shape[2]
    F = N2 // 2
    bn = N2 // 2
    bm = _tile(T, ROW_TILE)

    def swiglu(acc):
        w = acc.shape[1] // 2
        gg, uu = acc[:, :w], acc[:, w:]
        return acc, gg * jax.nn.sigmoid(gg) * uu

    gu, act = _matmul(
        "ffn_up", (T // bm, 2, 1),
        (hb, (bm, D), lambda i, j, k: (i, 0)), (w_in, (None, D, bn), lambda i, j, k: (layer, 0, j)),
        [((T, N2), BF16, (bm, bn), lambda i, j: (i, j)), ((T, F), BF16, (bm, bn // 2), lambda i, j: (i, j))],
        (bm, bn), epilogue=swiglu, n_outer=True)
    z, h, hb2 = _proj_ln("ffn_down_ln", act, w_out, layer, hprev, g, b, alpha, 0.5)
    return (z, h, hb2), (gu, act)


def _ffn_bwd(dh, z, g, hb_in, gu, act, w_in, w_out, layer, dw_in, dw_out, alpha):
    T, D = z.shape
    N2 = w_in.shape[2]
    F = N2 // 2
    bn = F // 2
    bm = _tile(T, ROW_TILE)
    dz, dyb, dg, db = _ln_bwd("ffn_ln_bwd", dh, z, g, 0.5)

    def dswiglu(dact, gu_blk):
        w = dact.shape[1]
        gg = gu_blk[:, :w].astype(F32)
        uu = gu_blk[:, w:].astype(F32)
        sg = jax.nn.sigmoid(gg)
        d_g = dact * uu * (sg * (1.0 + gg * (1.0 - sg)))
        d_u = dact * (gg * sg)
        return (jnp.concatenate([d_g, d_u], axis=1),)

    (dgu,) = _matmul(
        "ffn_dact", (T // bm, 2, 1),
        (dyb, (bm, D), lambda i, j, k: (i, 0)), (w_out, (None, bn, D), lambda i, j, k: (layer, j, 0)),
        [((T, N2), BF16, (bm, 2 * bn), lambda i, j: (i, j))], (bm, bn), tb=True,
        extras=[(gu, (bm, 2 * bn), lambda i, j: (i, j))], epilogue=dswiglu, n_outer=True)
    dw_out = _grad_tn("ffn_dwout", act, dyb, dw_out, layer, F // 2, D)
    dw_in = _grad_tn("ffn_dwin", hb_in, dgu, dw_in, layer, D, N2 // 4)
    bk = N2 // 4
    (dh_prev,) = _matmul(
        "ffn_dx", (T // bm, 1, 4),
        (dgu, (bm, bk), lambda i, j, k: (i, k)), (w_in, (None, D, bk), lambda i, j, k: (layer, 0, k)),
        [((T, D), F32, (bm, D), lambda i, j: (i, 0))], (bm, D), tb=True,
        extras=[(dz, (bm, D), lambda i, j: (i, 0))], epilogue=lambda acc, dzb: (alpha * dzb + acc,))
    return dh_prev, dg, db, dw_in, dw_out


def _input_grad(name, dy, w, layer, dz, alpha):
    T, N = dy.shape
    D = w.shape[1]
    bm = _tile(T, ROW_TILE)
    (out,) = _matmul(
        name, (T // bm, 1, 1),
        (dy, (bm, N), lambda i, j, k: (i, 0)), (w, (None, D, N), lambda i, j, k: (layer, 0, 0)),
        [((T, D), F32, (bm, D), lambda i, j: (i, 0))], (bm, D), tb=True,
        extras=[(dz, (bm, D), lambda i, j: (i, 0))], epilogue=lambda acc, dzb: (alpha * dzb + acc,))
    return out


def _back_proj(name, dy, w, layer, dtype):
    T, D = dy.shape
    K = w.shape[1]
    bm = _tile(T, ROW_TILE)
    (out,) = _matmul(
        name, (T // bm, 1, 1),
        (dy, (bm, D), lambda i, j, k: (i, 0)), (w, (None, K, D), lambda i, j, k: (layer, 0, 0)),
        [((T, K), dtype, (bm, K), lambda i, j: (i, 0))], (bm, K), tb=True)
    return out


def _rope_tables(T):
    pos = jnp.arange(T, dtype=F32)
    inv_freq = ROPE_THETA ** (-jnp.arange(0, HEAD_DIM, 2, dtype=F32) / HEAD_DIM)
    ang = pos[:, None] * inv_freq[None, :]
    cos, sin = jnp.cos(ang), jnp.sin(ang)
    c128 = jnp.tile(cos, (1, 4))
    s128 = jnp.tile(jnp.concatenate([-sin, sin], axis=1), (1, 2))
    return c128, s128


QK_COLS = (N_HEADS + N_KV_HEADS) * HEAD_DIM
Q_COLS = N_HEADS * HEAD_DIM
KV_COLS = N_KV_HEADS * HEAD_DIM
Q_SCALE = HEAD_DIM ** -0.5


def _qkv_rope(hb, w_qkv, layer, c128, s128):
    T, D = hb.shape
    N = w_qkv.shape[2]
    bm = _tile(T, ROW_TILE)

    def epi(acc, c, s):
        x = acc[:, :QK_COLS]
        rep = QK_COLS // 128
        r = x * jnp.tile(c, (1, rep)) + _swap_halves(x) * jnp.tile(s, (1, rep))
        return (jnp.concatenate([r[:, :Q_COLS] * Q_SCALE, r[:, Q_COLS:], acc[:, QK_COLS:]], axis=1),)

    (qkv,) = _matmul(
        "qkv_rope", (T // bm, 1, 1),
        (hb, (bm, D), lambda i, j, k: (i, 0)), (w_qkv, (None, D, N), lambda i, j, k: (layer, 0, 0)),
        [((T, N), BF16, (bm, N), lambda i, j: (i, 0))], (bm, N),
        extras=[(c128, (bm, 128), lambda i, j: (i, 0)), (s128, (bm, 128), lambda i, j: (i, 0))], epilogue=epi)
    return qkv


def _rope_bwd(dq, dkv, c128, s128):
    T = dq.shape[0]
    tr = _tile(T, ROW_TILE)

    def fn(i, dq, dkv, c, s):
        dx = jnp.concatenate([dq * Q_SCALE, dkv[:, :KV_COLS]], axis=1)
        rep = QK_COLS // 128
        d = dx * jnp.tile(c, (1, rep)) + _swap_halves(dx * jnp.tile(s, (1, rep)))
        return (jnp.concatenate([d, dkv[:, KV_COLS:]], axis=1),)

    N = Q_COLS + 2 * KV_COLS
    (out,) = _rowwise("rope_bwd", T // tr, fn, [_rows(dq, tr), _rows(dkv, tr), _rows(c128, tr), _rows(s128, tr)],
                      [((T, N), BF16, (tr, N), lambda i: (i, 0))])
    return out


def _attn_mask(first_block):
    q_pos = lax.broadcasted_iota(jnp.int32, (GROUP * ATTN_BLOCK, 2 * ATTN_BLOCK), 0) & (ATTN_BLOCK - 1)
    col = lax.broadcasted_iota(jnp.int32, (GROUP * ATTN_BLOCK, 2 * ATTN_BLOCK), 1)
    dist = q_pos + ATTN_BLOCK - col
    return (dist >= 0) & (dist < ATTN_BLOCK) & ((col >= ATTN_BLOCK) | jnp.logical_not(first_block))


def _sink_column(sk_ref, kvh):
    rg = lax.broadcasted_iota(jnp.int32, (GROUP * ATTN_BLOCK, 1), 0) // ATTN_BLOCK
    col = jnp.full((GROUP * ATTN_BLOCK, 1), sk_ref[0, kvh * GROUP], F32)
    for gi in range(1, GROUP):
        col = jnp.where(rg == gi, sk_ref[0, kvh * GROUP + gi], col)
    return col


def _stack_heads(x, kvh):
    return jnp.concatenate([x[:, (kvh * GROUP + gi) * HEAD_DIM:(kvh * GROUP + gi + 1) * HEAD_DIM] for gi in range(GROUP)], axis=0)


def _unstack_heads(parts):
    cols = []
    for p in parts:
        cols += [p[gi * ATTN_BLOCK:(gi + 1) * ATTN_BLOCK] for gi in range(GROUP)]
    return jnp.concatenate(cols, axis=1)


def _attn_softmax(q4, kb, mask, sink):
    s = lax.dot_general(q4, kb, (((1,), (1,)), ((), ())), preferred_element_type=F32)
    s = jnp.where(mask, s, NEG_BIG)
    m = jnp.maximum(jnp.max(s, axis=1, keepdims=True), sink)
    p = jnp.exp(s - m)
    e_sink = jnp.exp(sink - m)
    den = jnp.sum(p, axis=1, keepdims=True) + e_sink
    return p / den, e_sink / den


def _attn_fwd(qkv, sinks):
    T = qkv.shape[0]
    nb = T // ATTN_BLOCK
    kcb, vcb = Q_COLS // KV_COLS, Q_COLS // KV_COLS + 1

    def body(q_ref, kc_ref, kp_ref, vc_ref, vp_ref, sk_ref, o_ref):
        i = pl.program_id(0)
        mask = _attn_mask(i == 0)
        q = q_ref[...]
        kband = jnp.concatenate([kp_ref[...], kc_ref[...]], axis=0)
        vband = jnp.concatenate([vp_ref[...], vc_ref[...]], axis=0)
        parts = []
        for kvh in range(N_KV_HEADS):
            hs = slice(kvh * HEAD_DIM, (kvh + 1) * HEAD_DIM)
            pn, _ = _attn_softmax(_stack_heads(q, kvh), kband[:, hs], mask, _sink_column(sk_ref, kvh))
            parts.append(jnp.dot(pn.astype(BF16), vband[:, hs], preferred_element_type=F32))
        o_ref[...] = _unstack_heads(parts).astype(o_ref.dtype)

    prev = lambda i: jnp.maximum(i - 1, 0)
    return pl.pallas_call(
        body, name="attn_fwd", grid=(nb,),
        in_specs=[pl.BlockSpec((ATTN_BLOCK, Q_COLS), lambda i: (i, 0)),
                  pl.BlockSpec((ATTN_BLOCK, KV_COLS), lambda i: (i, kcb)),
                  pl.BlockSpec((ATTN_BLOCK, KV_COLS), lambda i: (prev(i), kcb)),
                  pl.BlockSpec((ATTN_BLOCK, KV_COLS), lambda i: (i, vcb)),
                  pl.BlockSpec((ATTN_BLOCK, KV_COLS), lambda i: (prev(i), vcb)),
                  pl.BlockSpec(memory_space=pltpu.SMEM)],
        out_specs=pl.BlockSpec((ATTN_BLOCK, Q_COLS), lambda i: (i, 0)),
        out_shape=jax.ShapeDtypeStruct((T, Q_COLS), BF16),
        compiler_params=_cparams(("arbitrary",)),
    )(qkv, qkv, qkv, qkv, qkv, sinks)


def _attn_bwd(qkv, do, sinks):
    T = qkv.shape[0]
    nb = T // ATTN_BLOCK
    kcb, vcb = Q_COLS // KV_COLS, Q_COLS // KV_COLS + 1
    B = ATTN_BLOCK

    def body(q_ref, kc_ref, kp_ref, vc_ref, vp_ref, do_ref, sk_ref, dq_ref, dkv_ref, dsk_ref, carry_ref):
        i = pl.program_id(0)

        @pl.when(i == 0)
        def _():
            carry_ref[...] = jnp.zeros_like(carry_ref)
            dsk_ref[...] = jnp.zeros_like(dsk_ref)

        @pl.when(i < nb)
        def _():
            mask = _attn_mask(i == 0)
            q = q_ref[...]
            do_blk = do_ref[...]
            kband = jnp.concatenate([kp_ref[...], kc_ref[...]], axis=0)
            vband = jnp.concatenate([vp_ref[...], vc_ref[...]], axis=0)
            dq_parts, dk_parts, dv_parts = [], [], []
            for kvh in range(N_KV_HEADS):
                hs = slice(kvh * HEAD_DIM, (kvh + 1) * HEAD_DIM)
                q4 = _stack_heads(q, kvh)
                do4 = _stack_heads(do_blk, kvh)
                kb, vb = kband[:, hs], vband[:, hs]
                pn, p_sink = _attn_softmax(q4, kb, mask, _sink_column(sk_ref, kvh))
                dp = lax.dot_general(do4, vb, (((1,), (1,)), ((), ())), preferred_element_type=F32)
                delta = jnp.sum(pn * dp, axis=1, keepdims=True)
                ds = (pn * (dp - delta)).astype(BF16)
                dsk_ref[kvh] += -(p_sink * delta)
                dq_parts.append(jnp.dot(ds, kb, preferred_element_type=F32))
                dk_parts.append(lax.dot_general(ds, q4, (((0,), (0,)), ((), ())), preferred_element_type=F32))
                dv_parts.append(lax.dot_general(pn.astype(BF16), do4, (((0,), (0,)), ((), ())), preferred_element_type=F32))
            dq_ref[...] = _unstack_heads(dq_parts)
            dkv = jnp.concatenate(dk_parts + dv_parts, axis=1)
            dkv_ref[...] = carry_ref[...] + dkv[:B]
            carry_ref[...] = dkv[B:]

        @pl.when(i == nb)
        def _():
            dkv_ref[...] = carry_ref[...]

    cur = lambda i: jnp.minimum(i, nb - 1)
    prev = lambda i: jnp.maximum(cur(i) - 1, 0)
    lag = lambda i: jnp.maximum(i - 1, 0)
    return pl.pallas_call(
        body, name="attn_bwd", grid=(nb + 1,),
        in_specs=[pl.BlockSpec((B, Q_COLS), lambda i: (cur(i), 0)),
                  pl.BlockSpec((B, KV_COLS), lambda i: (cur(i), kcb)),
                  pl.BlockSpec((B, KV_COLS), lambda i: (prev(i), kcb)),
                  pl.BlockSpec((B, KV_COLS), lambda i: (cur(i), vcb)),
                  pl.BlockSpec((B, KV_COLS), lambda i: (prev(i), vcb)),
                  pl.BlockSpec((B, Q_COLS), lambda i: (cur(i), 0)),
                  pl.BlockSpec(memory_space=pltpu.SMEM)],
        out_specs=[pl.BlockSpec((B, Q_COLS), lambda i: (cur(i), 0)),
                   pl.BlockSpec((B, 2 * KV_COLS), lambda i: (lag(i), 0)),
                   pl.BlockSpec((N_KV_HEADS, GROUP * B, 1), lambda i: (0, 0, 0))],
        out_shape=[jax.ShapeDtypeStruct((T, Q_COLS), F32), jax.ShapeDtypeStruct((T, 2 * KV_COLS), F32),
                   jax.ShapeDtypeStruct((N_KV_HEADS, GROUP * B, 1), F32)],
        scratch_shapes=[pltpu.VMEM((B, 2 * KV_COLS), F32)],
        compiler_params=_cparams(("arbitrary",)),
    )(qkv, qkv, qkv, qkv, qkv, do, sinks)


def _halo_prev(arr, tr, cols, cb=0):
    per = tr // 8
    return (arr, (8, cols), lambda i: (jnp.maximum(i * per - 1, 0), cb))


def _halo_next(arr, tr, cols, cb=0):
    per = tr // 8
    last = arr.shape[0] // 8 - 1
    return (arr, (8, cols), lambda i: (jnp.minimum((i + 1) * per, last), cb))


def _conv_fwd(xg, cw, cb):
    T = xg.shape[0]
    C = cb.shape[1]
    tr = _tile(T, ROW_TILE)

    def fn(i, cur, prev8, cb, *cw):
        prev8 = jnp.where(i == 0, 0.0, prev8)
        xc = cb + cw[CONV_W - 1] * cur
        for s in range(1, CONV_W):
            xc = xc + cw[CONV_W - 1 - s] * _shift_down(prev8, cur, s)
        return xc, xc

    row = lambda i: (i, 0)
    return _rowwise("lru_conv", T // tr, fn, [_rows(xg, tr, C), _halo_prev(xg, tr, C), _whole(cb)] + [_whole(w) for w in cw],
                    [((T, C), F32, (tr, C), row), ((T, C), BF16, (tr, C), row)])


def _conv_bwd(dxc, xg, dgb, cw):
    T, C = dxc.shape
    tr = _tile(T, ROW_TILE)
    nt = T // tr

    def fn(i, d_cur, d_next8, x_cur, x_prev8, dgb, *cw):
        d_next8 = jnp.where(i == nt - 1, 0.0, d_next8)
        x_prev8 = jnp.where(i == 0, 0.0, x_prev8)
        dxb = cw[CONV_W - 1] * d_cur
        dcw = [jnp.sum(d_cur * x_cur, axis=0, keepdims=True)]
        for s in range(1, CONV_W):
            dxb = dxb + cw[CONV_W - 1 - s] * _shift_up(d_cur, d_next8, s)
            dcw.append(jnp.sum(d_cur * _shift_down(x_prev8, x_cur, s), axis=0, keepdims=True))
        return (jnp.concatenate([dxb.astype(BF16), dgb], axis=1), dcw[3], dcw[2], dcw[1], dcw[0],
                jnp.sum(d_cur, axis=0, keepdims=True))

    return _rowwise("lru_conv_bwd", nt, fn,
                    [_rows(dxc, tr), _halo_next(dxc, tr, C), _rows(xg, tr, C), _halo_prev(xg, tr, C), _rows(dgb, tr)] + [_whole(w) for w in cw],
                    [((T, 2 * C), BF16, (tr, 2 * C), lambda i: (i, 0))], accs=[(1, C)] * (CONV_W + 1))


def _lru_gate_fwd(xc, xcb, w_rarx, layer, b_ra, b_rx, lam):
    T, C = xc.shape
    W = C // RNN_BLOCKS
    bm = _tile(T, ROW_TILE)

    def epi(acc, xc_blk, bra, brx, lm):
        a, b = _lru_gates(acc, xc_blk, bra, brx, lm)
        return acc, a, b

    blk = lambda i, j: (i, j)
    par = lambda i, j: (0, j)
    return _matmul(
        "lru_gates", (T // bm, RNN_BLOCKS, 1),
        (xcb, (bm, W), lambda i, j, k: (i, j)), (w_rarx, (None, None, W, 2 * W), lambda i, j, k: (layer, j, 0, 0)),
        [((T, 2 * C), F32, (bm, 2 * W), blk), ((T, C), F32, (bm, W), blk), ((T, C), F32, (bm, W), blk)],
        (bm, 2 * W),
        extras=[(xc, (bm, W), blk), (b_ra, (1, W), par), (b_rx, (1, W), par), (lam, (1, W), par)], epilogue=epi)


def _lru_gate_bwd(pre, xc, lam_adj, h, b_ra, b_rx, lam):
    T, C = xc.shape
    W = C // RNN_BLOCKS
    tr = _tile(T, ROW_TILE // 2)

    def fn(i, pre, xc, adj, h_cur, h_prev8, bra, brx, lm):
        h_prev8 = jnp.where(i == 0, 0.0, h_prev8)
        da = adj * _shift_down(h_prev8, h_cur, 1)
        dpre, dxc, dbra, dbrx, dlam = [], [], [], [], []
        for n in range(RNN_BLOCKS):
            cs = slice(n * W, (n + 1) * W)
            _, vjp = jax.vjp(_lru_gates, pre[:, 2 * n * W:2 * (n + 1) * W], xc[:, cs], bra[:, cs], brx[:, cs], lm[:, cs])
            g = vjp((da[:, cs], adj[:, cs]))
            for lst, v in zip((dpre, dxc, dbra, dbrx, dlam), g):
                lst.append(v)
        cat = lambda l: jnp.concatenate(l, axis=1)
        return cat(dpre), cat(dxc), cat(dbra), cat(dbrx), cat(dlam)

    row = lambda i: (i, 0)
    return _rowwise("lru_gates_bwd", T // tr, fn,
                    [_rows(pre, tr), _rows(xc, tr), _rows(lam_adj, tr), _rows(h, tr), _halo_prev(h, tr, C),
                     _whole(b_ra), _whole(b_rx), _whole(lam)],
                    [((T, 2 * C), BF16, (tr, 2 * C), row), ((T, C), F32, (tr, C), row)], accs=[(1, C)] * 3)


def _scan_fwd(a, b):
    T, C = a.shape
    tt = _tile(T, ROW_TILE)

    def body(a_ref, b_ref, o_ref, c_ref):
        @pl.when(pl.program_id(0) == 0)
        def _():
            c_ref[...] = jnp.zeros_like(c_ref)

        row = lax.broadcasted_iota(jnp.int32, (8, C), 0)

        def step(j, carry):
            sl = pl.ds(pl.multiple_of(j * 8, 8), 8)
            A, B = a_ref[sl, :], b_ref[sl, :]
            for d in (1, 2, 4):
                ok = row >= d
                B = jnp.where(ok, A * pltpu.roll(B, d, 0) + B, B)
                A = jnp.where(ok, A * pltpu.roll(A, d, 0), A)
            h = A * carry + B
            o_ref[sl, :] = h
            return jnp.sum(jnp.where(row == 7, h, 0.0), axis=0, keepdims=True)

        c_ref[0:1, :] = lax.fori_loop(0, tt // 8, step, c_ref[0:1, :])

    spec = pl.BlockSpec((tt, C), lambda i: (i, 0))
    return pl.pallas_call(body, name="lru_scan", grid=(T // tt,), in_specs=[spec, spec], out_specs=spec,
                          out_shape=jax.ShapeDtypeStruct((T, C), F32), scratch_shapes=[pltpu.VMEM((8, C), F32)],
                          compiler_params=_cparams(("arbitrary",)))(a, b)


def _scan_bwd(a, dh):
    T, C = a.shape
    tt = _tile(T, ROW_TILE)
    nt = T // tt

    def body(a_ref, d_ref, o_ref, c_ref):
        @pl.when(pl.program_id(0) == 0)
        def _():
            c_ref[...] = jnp.zeros_like(c_ref)

        row = lax.broadcasted_iota(jnp.int32, (8, C), 0)

        def step(jj, carry):
            adj_next, a_next = carry
            j = tt // 8 - 1 - jj
            sl = pl.ds(pl.multiple_of(j * 8, 8), 8)
            a_blk = a_ref[sl, :]
            A = jnp.where(row < 7, pltpu.roll(a_blk, 7, 0), a_next)
            B = d_ref[sl, :]
            for d in (1, 2, 4):
                ok = row < 8 - d
                B = jnp.where(ok, A * pltpu.roll(B, 8 - d, 0) + B, B)
                A = jnp.where(ok, A * pltpu.roll(A, 8 - d, 0), A)
            adj = A * adj_next + B
            o_ref[sl, :] = adj
            first = lambda v: jnp.sum(jnp.where(row == 0, v, 0.0), axis=0, keepdims=True)
            return first(adj), first(a_blk)

        adj0, a0 = lax.fori_loop(0, tt // 8, step, (c_ref[0:1, :], c_ref[1:2, :]))
        c_ref[0:1, :] = adj0
        c_ref[1:2, :] = a0

    spec = pl.BlockSpec((tt, C), lambda i: (nt - 1 - i, 0))
    return pl.pallas_call(body, name="lru_scan_bwd", grid=(nt,), in_specs=[spec, spec], out_specs=spec,
                          out_shape=jax.ShapeDtypeStruct((T, C), F32), scratch_shapes=[pltpu.VMEM((8, C), F32)],
                          compiler_params=_cparams(("arbitrary",)))(a, dh)


def _lru_out_fwd(h, xg):
    T, C = h.shape
    tr = _tile(T, ROW_TILE)
    (y,) = _rowwise("lru_out", T // tr, lambda i, h, gb: (h * _gelu_tanh(gb),), [_rows(h, tr), _rows(xg, tr, C, 1)],
                    [((T, C), BF16, (tr, C), lambda i: (i, 0))])
    return y


def _lru_out_bwd(dy, h, xg):
    T, C = h.shape
    tr = _tile(T, ROW_TILE)

    def fn(i, dy, h, gb):
        _, vjp = jax.vjp(lambda h, gb: h * _gelu_tanh(gb), h, gb)
        return vjp(dy)

    row = lambda i: (i, 0)
    return _rowwise("lru_out_bwd", T // tr, fn, [_rows(dy, tr), _rows(h, tr), _rows(xg, tr, C, 1)],
                    [((T, C), F32, (tr, C), row), ((T, C), BF16, (tr, C), row)])


class _Sharded:
    def __init__(self, kind, size, off=0, width=None):
        self.kind, self.size, self.off, self.width = kind, size, off, width

    def slot(self, cx, cy):
        return (2 * cy + cx) if self.kind == "perm" else (2 * cx + cy)

    def at(self, ref, cx, cy, layers):
        s = self.slot(cx, cy)
        start = s * self.size
        if not isinstance(start, int):
            start = pl.multiple_of(start, 8 if self.kind in ("rows", "rarx") else 128)
        if self.kind in ("cols", "perm"):
            return ref.at[layers, :, pl.ds(start, self.size)]
        if self.kind == "rows":
            return ref.at[layers, pl.ds(start, self.size), :]
        if self.kind == "rarx":
            return ref.at[layers, :, pl.ds(start, self.size), pl.ds(self.off, self.width)]
        raise ValueError(self.kind)


def _mesh_pos():
    return lax.axis_index("x"), lax.axis_index("y"), lax.axis_index("c")


def _peer_chips(x, y):
    return [(1 - x, y), (x, 1 - y), (1 - x, 1 - y)]


def _all_gather(items, out_shapes):
    n_it, n_out = len(items), len(out_shapes)
    K = 3 * n_it

    def body(*refs):
        ins, outs = refs[:n_it], refs[n_it:n_it + n_out]
        send, recv, loc = refs[n_it + n_out:]
        x, y, c = _mesh_pos()
        sib = (x, y, 1 - c)
        peers = _peer_chips(x, y)
        pending = []
        for t, (shard, oi, sh) in enumerate(items):
            lh = shard.shape[0] // 2
            mine = pltpu.make_async_copy(ins[t], sh.at(outs[oi], x, y, slice(None)), loc.at[t])
            mine.start()
            pending.append(mine.wait)
            half = pl.ds(c * lh, lh)
            for j, (px, py) in enumerate(peers):
                cp = pltpu.make_async_remote_copy(ins[t].at[half], sh.at(outs[oi], x, y, half), send.at[3 * t + j],
                                                  recv.at[3 * t + j], device_id=(px, py, c), device_id_type=MESH)
                cp.start()
                pending.append(cp.wait_send)
        for t, (shard, oi, sh) in enumerate(items):
            lh = shard.shape[0] // 2
            half = pl.ds(c * lh, lh)
            for j, (px, py) in enumerate(peers):
                region = sh.at(outs[oi], px, py, half)
                pltpu.make_async_remote_copy(region, region, send.at[3 * t + j], recv.at[3 * t + j],
                                             device_id=(px, py, c), device_id_type=MESH).wait_recv()
                fwd = pltpu.make_async_remote_copy(region, region, send.at[K + 3 * t + j], recv.at[K + 3 * t + j],
                                                   device_id=sib, device_id_type=MESH)
                fwd.start()
                pending.append(fwd.wait_send)
        for t, (shard, oi, sh) in enumerate(items):
            lh = shard.shape[0] // 2
            other = pl.ds((1 - c) * lh, lh)
            for j, (px, py) in enumerate(peers):
                region = sh.at(outs[oi], px, py, other)
                pltpu.make_async_remote_copy(region, region, send.at[K + 3 * t + j], recv.at[K + 3 * t + j],
                                             device_id=sib, device_id_type=MESH).wait_recv()
        for w in pending:
            w()

    hbm = pl.BlockSpec(memory_space=pl.ANY)
    return pl.pallas_call(
        body, name="weights_all_gather", in_specs=[hbm] * n_it, out_specs=[hbm] * n_out,
        out_shape=[jax.ShapeDtypeStruct(s, d) for s, d in out_shapes],
        scratch_shapes=[pltpu.SemaphoreType.DMA((2 * K,)), pltpu.SemaphoreType.DMA((2 * K,)), pltpu.SemaphoreType.DMA((n_it,))],
    )(*[it[0] for it in items])


def _sibling_swap(grads):
    n = len(grads)

    def body(*refs):
        ins, outs = refs[:n], refs[n:2 * n]
        send, recv = refs[2 * n:]
        x, y, c = _mesh_pos()
        cps = []
        for t in range(n):
            lh = grads[t].shape[0] // 2
            cp = pltpu.make_async_remote_copy(ins[t].at[pl.ds((1 - c) * lh, lh)], outs[t], send.at[t], recv.at[t],
                                              device_id=(x, y, 1 - c), device_id_type=MESH)
            cp.start()
            cps.append(cp)
        for cp in cps:
            cp.wait()

    hbm = pl.BlockSpec(memory_space=pl.ANY)
    return pl.pallas_call(
        body, name="grad_sibling_swap", in_specs=[hbm] * n, out_specs=[hbm] * n,
        out_shape=[jax.ShapeDtypeStruct((g.shape[0] // 2,) + g.shape[1:], g.dtype) for g in grads],
        scratch_shapes=[pltpu.SemaphoreType.DMA((n,)), pltpu.SemaphoreType.DMA((n,))],
    )(*grads)


def _shard_shape(full, sh):
    s = list(full)
    if sh.kind in ("cols", "perm"):
        s[-1] = sh.size
    else:
        s[-2] = sh.size
    return tuple(s)


def _chip_exchange(parts, shardings):
    n = len(parts)

    def body(*refs):
        ins, outs = refs[:n], refs[n:2 * n]
        send, recv, loc = refs[2 * n:]
        x, y, c = _mesh_pos()
        me = 2 * x + y
        pending = []
        for t, sh in enumerate(shardings):
            src_all = _Sharded("rarx", sh.size, 0, parts[t].shape[-1]) if sh.kind == "rarx" else sh
            mine = pltpu.make_async_copy(src_all.at(ins[t], x, y, slice(None)), outs[t].at[me], loc.at[t])
            mine.start()
            pending.append(mine.wait)
            for j, (px, py) in enumerate(_peer_chips(x, y)):
                cp = pltpu.make_async_remote_copy(src_all.at(ins[t], px, py, slice(None)), outs[t].at[me],
                                                  send.at[3 * t + j], recv.at[3 * t + j],
                                                  device_id=(px, py, c), device_id_type=MESH)
                cp.start()
                pending.append(cp.wait_send)
        for t, sh in enumerate(shardings):
            for j, (px, py) in enumerate(_peer_chips(x, y)):
                dst = outs[t].at[2 * px + py]
                pltpu.make_async_remote_copy(dst, dst, send.at[3 * t + j], recv.at[3 * t + j],
                                             device_id=(px, py, c), device_id_type=MESH).wait_recv()
        for w in pending:
            w()

    hbm = pl.BlockSpec(memory_space=pl.ANY)
    return pl.pallas_call(
        body, name="grad_chip_exchange", in_specs=[hbm] * n, out_specs=[hbm] * n,
        out_shape=[jax.ShapeDtypeStruct((N_CHIPS,) + _shard_shape(p.shape, sh), p.dtype) for p, sh in zip(parts, shardings)],
        scratch_shapes=[pltpu.SemaphoreType.DMA((3 * n,)), pltpu.SemaphoreType.DMA((3 * n,)), pltpu.SemaphoreType.DMA((n,))],
    )(*parts)


def _sibling_share(sums):
    n = len(sums)

    def body(*refs):
        ins, outs = refs[:n], refs[n:2 * n]
        send, recv, loc = refs[2 * n:]
        x, y, c = _mesh_pos()
        cps = []
        for t in range(n):
            lh = sums[t].shape[0]
            dst = outs[t].at[pl.ds(c * lh, lh)]
            mine = pltpu.make_async_copy(ins[t], dst, loc.at[t])
            mine.start()
            cp = pltpu.make_async_remote_copy(ins[t], dst, send.at[t], recv.at[t], device_id=(x, y, 1 - c), device_id_type=MESH)
            cp.start()
            cps += [mine, cp]
        for cp in cps:
            cp.wait()

    hbm = pl.BlockSpec(memory_space=pl.ANY)
    return pl.pallas_call(
        body, name="grad_sibling_share", in_specs=[hbm] * n, out_specs=[hbm] * n,
        out_shape=[jax.ShapeDtypeStruct((2 * s.shape[0],) + s.shape[1:], s.dtype) for s in sums],
        scratch_shapes=[pltpu.SemaphoreType.DMA((n,)), pltpu.SemaphoreType.DMA((n,)), pltpu.SemaphoreType.DMA((n,))],
    )(*sums)


def _flat2(shape):
    return (math.prod(shape[:-1]), shape[-1])


def _pair_add(grad, got, c_arr):
    lh = got.shape[0]
    R, C = _flat2(got.shape)
    tr = _tile(R, 256 if C > 2048 else ROW_TILE)
    nt = R // tr
    (out,) = _rowwise(
        "grad_pair_add", nt, lambda i, a, b: (a.astype(F32) + b.astype(F32),),
        [(grad.reshape(_flat2(grad.shape)), (tr, C), lambda i, p: (p[0] * nt + i, 0)), (got.reshape(R, C), (tr, C), lambda i, p: (i, 0))],
        [((R, C), got.dtype, (tr, C), lambda i, p: (i, 0))], prefetch=c_arr)
    return out.reshape(got.shape)


def _chip_sum(recv):
    shape = recv.shape[1:]
    R, C = _flat2(shape)
    tr = _tile(R, ROW_TILE)
    r3 = recv.reshape((N_CHIPS, R, C))

    def fn(i, blk):
        v = blk.astype(F32)
        return (((v[0] + v[1]) + v[2]) + v[3],)

    (out,) = _rowwise("grad_chip_sum", R // tr, fn, [(r3, (N_CHIPS, tr, C), lambda i: (0, i, 0))],
                      [((R, C), F32, (tr, C), lambda i: (i, 0))])
    return out.reshape(shape)


def _adamw(w, g, m, v):
    shape = w.shape
    R, C = _flat2(shape)
    tr = _tile(R, 256 if R % 256 == 0 else R)

    def fn(i, w, g, m, v):
        m2 = ADAM_B1 * m + (1.0 - ADAM_B1) * g
        v2 = ADAM_B2 * v + (1.0 - ADAM_B2) * (g * g)
        m_hat = m2 / (1.0 - ADAM_B1 ** ADAM_STEP)
        v_hat = v2 / (1.0 - ADAM_B2 ** ADAM_STEP)
        delta = -ADAM_LR * (m_hat / (jnp.sqrt(v_hat) + ADAM_EPS) + ADAM_WD * w)
        return delta, m2, v2

    row = lambda i: (i, 0)
    f2 = lambda a: a.reshape(R, C)
    outs = _rowwise("adamw", R // tr, fn, [_rows(f2(a), tr) for a in (w, g, m, v)], [((R, C), F32, (tr, C), row)] * 3)
    return [o.reshape(shape) for o in outs]


def _loss_and_grad(h, target):
    T, D = h.shape
    tr = _tile(T, ROW_TILE)

    def fn(i, h, t):
        err = h - t
        return err * (1.0 / D), jnp.sum(err * err, axis=0, keepdims=True)

    dh, sq = _rowwise("loss", T // tr, fn, [_rows(h, tr), _rows(target, tr)], [((T, D), F32, (tr, D), lambda i: (i, 0))], accs=[(1, D)])
    return dh, (0.5 / D) * jnp.sum(sq)


def kernel(x, ffn1_w_in, ffn1_w_out, ffn2_w_in, ffn2_w_out, ln_g, ln_b, attn_w_qkv, attn_sinks, attn_w_o, lru_w_in, lru_conv_w, lru_conv_b, lru_w_ra, lru_b_ra, lru_w_rx, lru_b_rx, lru_lambda, lru_w_out, loss_target, m_ffn1_w_in, m_ffn1_w_out, m_ffn2_w_in, m_ffn2_w_out, m_ln_g, m_ln_b, m_attn_w_qkv, m_attn_sinks, m_attn_w_o, m_lru_w_in, m_lru_conv_w, m_lru_conv_b, m_lru_w_ra, m_lru_b_ra, m_lru_w_rx, m_lru_b_rx, m_lru_lambda, m_lru_w_out, v_ffn1_w_in, v_ffn1_w_out, v_ffn2_w_in, v_ffn2_w_out, v_ln_g, v_ln_b, v_attn_w_qkv, v_attn_sinks, v_attn_w_o, v_lru_w_in, v_lru_conv_w, v_lru_conv_b, v_lru_w_ra, v_lru_b_ra, v_lru_w_rx, v_lru_b_rx, v_lru_lambda, v_lru_w_out):
    names = ["ffn1_w_in", "ffn1_w_out", "ffn2_w_in", "ffn2_w_out", "ln_g", "ln_b", "attn_w_qkv", "attn_sinks", "attn_w_o",
             "lru_w_in", "lru_conv_w", "lru_conv_b", "lru_w_ra", "lru_b_ra", "lru_w_rx", "lru_b_rx", "lru_lambda", "lru_w_out"]
    W = dict(zip(names, [ffn1_w_in, ffn1_w_out, ffn2_w_in, ffn2_w_out, ln_g, ln_b, attn_w_qkv, attn_sinks, attn_w_o,
                         lru_w_in, lru_conv_w, lru_conv_b, lru_w_ra, lru_b_ra, lru_w_rx, lru_b_rx, lru_lambda, lru_w_out]))
    M = dict(zip(names, [m_ffn1_w_in, m_ffn1_w_out, m_ffn2_w_in, m_ffn2_w_out, m_ln_g, m_ln_b, m_attn_w_qkv, m_attn_sinks, m_attn_w_o,
                         m_lru_w_in, m_lru_conv_w, m_lru_conv_b, m_lru_w_ra, m_lru_b_ra, m_lru_w_rx, m_lru_b_rx, m_lru_lambda, m_lru_w_out]))
    V = dict(zip(names, [v_ffn1_w_in, v_ffn1_w_out, v_ffn2_w_in, v_ffn2_w_out, v_ln_g, v_ln_b, v_attn_w_qkv, v_attn_sinks, v_attn_w_o,
                         v_lru_w_in, v_lru_conv_w, v_lru_conv_b, v_lru_w_ra, v_lru_b_ra, v_lru_w_rx, v_lru_b_rx, v_lru_lambda, v_lru_w_out]))

    T, D = x.shape[1], x.shape[2]
    L = ffn1_w_in.shape[0]
    LA, LR = attn_w_qkv.shape[0], lru_w_in.shape[0]
    N2 = ffn1_w_in.shape[2] * N_CHIPS
    F = N2 // 2
    C = lru_lambda.shape[1] * N_CHIPS
    CW = C // N_CHIPS
    alpha = (2.0 * L) ** 0.25
    c_arr = lax.axis_index("c").astype(jnp.int32).reshape(1)

    n_sink = attn_sinks.size
    assert n_sink <= CW

    def pack_small(d):
        sink_row = jnp.zeros((1, CW), F32).at[0, :n_sink].set(d["attn_sinks"].reshape(-1))
        rows = [d["ln_g"].reshape(-1, CW), d["ln_b"].reshape(-1, CW), d["lru_conv_w"].reshape(-1, CW), d["lru_conv_b"],
                d["lru_b_ra"], d["lru_b_rx"], d["lru_lambda"], sink_row]
        p = jnp.concatenate(rows, axis=0)
        return jnp.concatenate([p, jnp.zeros((SMALL_ROWS - p.shape[0], CW), F32)], axis=0)

    o_g, o_b = 0, 3 * L
    o_cw = 6 * L
    o_cb = o_cw + LR * CONV_W
    o_ra, o_rx, o_lam = o_cb + LR, o_cb + 2 * LR, o_cb + 3 * LR
    o_sink = o_cb + 4 * LR
    assert o_sink < SMALL_ROWS

    bf = lambda a: a.astype(BF16)
    cols = lambda a: _Sharded("cols", a.shape[-1])
    rows_ = lambda a: _Sharded("rows", a.shape[-2])
    RW = lru_w_ra.shape[2]
    sh_list = [_Sharded("perm", N2 // 4), rows_(ffn1_w_out), _Sharded("perm", N2 // 4), rows_(ffn2_w_out),
               cols(attn_w_qkv), rows_(attn_w_o), cols(lru_w_in), rows_(lru_w_out)]
    big = [ffn1_w_in, ffn1_w_out, ffn2_w_in, ffn2_w_out, attn_w_qkv, attn_w_o, lru_w_in, lru_w_out]
    items = [(bf(a), k, sh) for k, (a, sh) in enumerate(zip(big, sh_list))]
    blk_w = lru_w_ra.shape[3]
    items.append((bf(lru_w_ra), 8, _Sharded("rarx", RW, 0, blk_w)))
    items.append((bf(lru_w_rx), 8, _Sharded("rarx", RW, blk_w, blk_w)))
    small_sh = _Sharded("cols", CW)
    items.append((pack_small(W).reshape(2, SMALL_ROWS // 2, CW), 9, small_sh))
    full = lambda a, sh: tuple(a.shape[:-1]) + (a.shape[-1] * N_CHIPS,) if sh.kind in ("cols", "perm") else \
        tuple(a.shape[:-2]) + (a.shape[-2] * N_CHIPS, a.shape[-1])
    out_shapes = [(full(a, sh), BF16) for a, sh in zip(big, sh_list)]
    out_shapes.append(((LR, RNN_BLOCKS, RW * N_CHIPS, 2 * blk_w), BF16))
    out_shapes.append(((2, SMALL_ROWS // 2, C), F32))
    g_w_in1, g_w_out1, g_w_in2, g_w_out2, g_qkv, g_wo, g_lin, g_lout, g_rarx, g_small = _all_gather(items, out_shapes)
    small = g_small.reshape(SMALL_ROWS, C)
    row_of = lambda r: small[r:r + 1]
    assert D == C, "packed small parameters assume d_model == d_rnn"

    c128, s128 = _rope_tables(T)
    sink_rows = [attn_sinks[j:j + 1] for j in range(LA)]

    h = x.reshape(T, D)
    hb = h.astype(BF16)
    saved = []
    for i in range(L):
        j = i // 2
        lay = {}
        lay["hb0"] = hb
        (z, h, hb), (lay["gu1"], lay["act1"]) = _ffn_fwd(h, hb, g_w_in1, g_w_out1, i, row_of(o_g + 3 * i), row_of(o_b + 3 * i), alpha)
        lay["z1"], lay["hb1"] = z, hb
        if i % 2 == 0:
            qkv = _qkv_rope(hb, g_qkv, j, c128, s128)
            o = _attn_fwd(qkv, sink_rows[j])
            lay["qkv"], lay["o"] = qkv, o
            z, h, hb = _proj_ln("attn_out_ln", o, g_wo, j, h, row_of(o_g + 3 * i + 1), row_of(o_b + 3 * i + 1), alpha, 1.0)
        else:
            bm = _tile(T, ROW_TILE)
            (xg,) = _matmul("lru_in", (T // bm, 1, 1), (hb, (bm, D), lambda p, q, k: (p, 0)),
                            (g_lin, (None, D, 2 * C), lambda p, q, k, j=j: (j, 0, 0)),
                            [((T, 2 * C), F32, (bm, 2 * C), lambda p, q: (p, 0))], (bm, 2 * C))
            cw = [row_of(o_cw + j * CONV_W + k) for k in range(CONV_W)]
            xc, xcb = _conv_fwd(xg, cw, row_of(o_cb + j))
            pre, a, b = _lru_gate_fwd(xc, xcb, g_rarx, j, row_of(o_ra + j), row_of(o_rx + j), row_of(o_lam + j))
            hs = _scan_fwd(a, b)
            y = _lru_out_fwd(hs, xg)
            lay.update(xg=xg, xc=xc, xcb=xcb, pre=pre, a=a, hs=hs, y=y, cw=cw)
            z, h, hb = _proj_ln("lru_out_ln", y, g_lout, j, h, row_of(o_g + 3 * i + 1), row_of(o_b + 3 * i + 1), alpha, 1.0)
        lay["z2"], lay["hb2"] = z, hb
        (z, h, hb), (lay["gu2"], lay["act2"]) = _ffn_fwd(h, hb, g_w_in2, g_w_out2, i, row_of(o_g + 3 * i + 2), row_of(o_b + 3 * i + 2), alpha)
        lay["z3"] = z
        saved.append(lay)

    dh, loss_local = _loss_and_grad(h, loss_target.reshape(T, D))
    loss = lax.psum(loss_local, ("x", "y", "c"))

    zeros = lambda shape: jnp.zeros(shape, BF16)
    d_w_in1, d_w_out1 = zeros(g_w_in1.shape), zeros(g_w_out1.shape)
    d_w_in2, d_w_out2 = zeros(g_w_in2.shape), zeros(g_w_out2.shape)
    d_qkv, d_wo, d_lin, d_lout, d_rarx = zeros(g_qkv.shape), zeros(g_wo.shape), zeros(g_lin.shape), zeros(g_lout.shape), zeros(g_rarx.shape)
    sg = [None] * SMALL_ROWS
    d_sinks = [None] * LA
    for i in reversed(range(L)):
        j = i // 2
        lay = saved[i]
        dh, sg[o_g + 3 * i + 2], sg[o_b + 3 * i + 2], d_w_in2, d_w_out2 = _ffn_bwd(
            dh, lay["z3"], row_of(o_g + 3 * i + 2), lay["hb2"], lay["gu2"], lay["act2"], g_w_in2, g_w_out2, i, d_w_in2, d_w_out2, alpha)
        if i % 2 == 0:
            dz, dmb, sg[o_g + 3 * i + 1], sg[o_b + 3 * i + 1] = _ln_bwd("attn_ln_bwd", dh, lay["z2"], row_of(o_g + 3 * i + 1), 1.0)
            d_wo = _grad_tn("attn_dwo", lay["o"], dmb, d_wo, j, _tile(Q_COLS, 1024), D)
            do = _back_proj("attn_do", dmb, g_wo, j, BF16)
            dq, dkv, dsk = _attn_bwd(lay["qkv"], do, sink_rows[j])
            d_sinks[j] = jnp.sum(dsk.reshape(N_HEADS, ATTN_BLOCK), axis=1)
            dqkv = _rope_bwd(dq, dkv, c128, s128)
            d_qkv = _grad_tn("attn_dwqkv", lay["hb1"], dqkv, d_qkv, j, D, dqkv.shape[1])
            dh = _input_grad("attn_dx", dqkv, g_qkv, j, dz, alpha)
        else:
            dz, dmb, sg[o_g + 3 * i + 1], sg[o_b + 3 * i + 1] = _ln_bwd("lru_ln_bwd", dh, lay["z2"], row_of(o_g + 3 * i + 1), 1.0)
            d_lout = _grad_tn("lru_dwout", lay["y"], dmb, d_lout, j, C, D)
            dy = _back_proj("lru_dy", dmb, g_lout, j, F32)
            dhs, dgb = _lru_out_bwd(dy, lay["hs"], lay["xg"])
            adj = _scan_bwd(lay["a"], dhs)
            dpre, dxc_direct, sg[o_ra + j], sg[o_rx + j], sg[o_lam + j] = _lru_gate_bwd(
                lay["pre"], lay["xc"], adj, lay["hs"], row_of(o_ra + j), row_of(o_rx + j), row_of(o_lam + j))
            blk = C // RNN_BLOCKS
            d_rarx = _grad_tn("lru_dwgates", lay["xcb"], dpre, d_rarx, j, blk, 2 * blk, a_cb=True)
            bm = _tile(T, ROW_TILE)
            (dxc,) = _matmul("lru_dxc", (T // bm, RNN_BLOCKS, 1), (dpre, (bm, 2 * blk), lambda p, q, k: (p, q)),
                             (g_rarx, (None, None, blk, 2 * blk), lambda p, q, k, j=j: (j, q, 0, 0)),
                             [((T, C), F32, (bm, blk), lambda p, q: (p, q))], (bm, blk), tb=True,
                             extras=[(dxc_direct, (bm, blk), lambda p, q: (p, q))], epilogue=lambda acc, d: (acc + d,))
            res = _conv_bwd(dxc, lay["xg"], dgb, lay["cw"])
            dxg = res[0]
            for k in range(CONV_W):
                sg[o_cw + j * CONV_W + k] = res[1 + k]
            sg[o_cb + j] = res[1 + CONV_W]
            d_lin = _grad_tn("lru_dwin", lay["hb1"], dxg, d_lin, j, D, _tile(2 * C, 1024))
            dh = _input_grad("lru_dx", dxg, g_lin, j, dz, alpha)
        dh, sg[o_g + 3 * i], sg[o_b + 3 * i], d_w_in1, d_w_out1 = _ffn_bwd(
            dh, lay["z1"], row_of(o_g + 3 * i), lay["hb0"], lay["gu1"], lay["act1"], g_w_in1, g_w_out1, i, d_w_in1, d_w_out1, alpha)
    grad_x = dh.reshape(x.shape)

    sink_vec = jnp.concatenate(d_sinks).reshape(1, n_sink)
    sg[o_sink] = jnp.tile(jnp.concatenate([sink_vec, jnp.zeros((1, CW - n_sink), F32)], axis=1), (1, N_CHIPS))
    zero_row = jnp.zeros((1, C), F32)
    d_small = jnp.concatenate([zero_row if r is None else r for r in sg], axis=0).reshape(2, SMALL_ROWS // 2, C)

    grads = [d_w_in1, d_w_out1, d_w_in2, d_w_out2, d_qkv, d_wo, d_lin, d_lout, d_rarx, d_small]
    gsh = sh_list + [_Sharded("rarx", RW, 0, 2 * blk_w), small_sh]
    got = _sibling_swap(grads)
    parts = [_pair_add(g, r, c_arr) for g, r in zip(grads, got)]
    recv = _chip_exchange(parts, gsh)
    sums = [_chip_sum(r) for r in recv]
    tot = _sibling_share(sums)
    t_w_in1, t_w_out1, t_w_in2, t_w_out2, t_qkv, t_wo, t_lin, t_lout, t_rarx, t_small = tot
    t_small = t_small.reshape(SMALL_ROWS, CW)

    G = {"ffn1_w_in": t_w_in1, "ffn1_w_out": t_w_out1, "ffn2_w_in": t_w_in2, "ffn2_w_out": t_w_out2,
         "attn_w_qkv": t_qkv, "attn_w_o": t_wo, "lru_w_in": t_lin, "lru_w_out": t_lout,
         "lru_w_ra": t_rarx[..., :blk_w], "lru_w_rx": t_rarx[..., blk_w:]}

    def unpack_small(p):
        return {"ln_g": p[o_g:o_g + 3 * L].reshape(ln_g.shape), "ln_b": p[o_b:o_b + 3 * L].reshape(ln_b.shape),
                "lru_conv_w": p[o_cw:o_cw + LR * CONV_W].reshape(lru_conv_w.shape), "lru_conv_b": p[o_cb:o_cb + LR],
                "lru_b_ra": p[o_ra:o_ra + LR], "lru_b_rx": p[o_rx:o_rx + LR], "lru_lambda": p[o_lam:o_lam + LR],
                "attn_sinks": p[o_sink, :n_sink].reshape(attn_sinks.shape)}

    G.update(unpack_small(t_small))

    delta, new_m, new_v = {}, {}, {}
    small_names = ["ln_g", "ln_b", "lru_conv_w", "lru_conv_b", "lru_b_ra", "lru_b_rx", "lru_lambda", "attn_sinks"]
    for n in names:
        if n not in small_names:
            delta[n], new_m[n], new_v[n] = _adamw(W[n], G[n], M[n], V[n])
    ds, ms, vs = _adamw(pack_small(W), t_small, pack_small(M), pack_small(V))
    for d, p in ((delta, ds), (new_m, ms), (new_v, vs)):
        d.update(unpack_small(p))

    return (loss, grad_x, *[G[n] for n in names], *[delta[n] for n in names], *[new_m[n] for n in names], *[new_v[n] for n in names])
```

```python
import functools
import math

import jax
import jax.numpy as jnp
from jax import lax
from jax.experimental import pallas as pl
from jax.experimental.pallas import tpu as pltpu

F32 = jnp.float32
BF16 = jnp.bfloat16
MESH = pl.DeviceIdType.MESH

N_HEADS = 16
N_KV_HEADS = 4
HEAD_DIM = 64
GROUP = N_HEADS // N_KV_HEADS
ATTN_BLOCK = 128
ROPE_THETA = 10000.0
RNN_BLOCKS = 4
CONV_W = 4
LRU_C = 8.0
LN_EPS = 1e-5
ADAM_LR = 0.001
ADAM_B1 = 0.9
ADAM_B2 = 0.999
ADAM_EPS = 1e-08
ADAM_WD = 0.01
ADAM_STEP = 10
N_CHIPS = 4
NEG_BIG = -1e30
VMEM_LIMIT_MB = 56
ROW_TILE = 512
SMALL_ROWS = 48


def _cparams(sem):
    return pltpu.CompilerParams(dimension_semantics=sem, vmem_limit_bytes=VMEM_LIMIT_MB << 20)


def _tile(n, pref):
    if n <= pref:
        return n
    for t in range(pref - pref % 16, 0, -16):
        if n % t == 0:
            return t
    raise ValueError((n, pref))


def _matmul(name, grid, a, b, outs, acc_shape, *, ta=False, tb=False, extras=(), epilogue=None,
            n_outer=False, alias_in=None):
    gm, gn, gk = grid
    if n_outer:
        g = (gn, gm, gk)
        ijk = lambda p, q, k: (q, p, k)
    else:
        g = (gm, gn, gk)
        ijk = lambda p, q, k: (p, q, k)
    w3 = lambda f: (lambda p, q, k: f(*ijk(p, q, k)))
    w2 = lambda f: (lambda p, q, k: f(*ijk(p, q, k)[:2]))
    in_specs = [pl.BlockSpec(a[1], w3(a[2])), pl.BlockSpec(b[1], w3(b[2]))]
    in_specs += [pl.BlockSpec(e[1], w2(e[2])) for e in extras]
    operands = [a[0], b[0]] + [e[0] for e in extras]
    io_alias = {}
    n_alias = 0
    if alias_in is not None:
        in_specs.append(pl.BlockSpec(memory_space=pl.ANY))
        operands.append(alias_in)
        io_alias = {len(operands) - 1: 0}
        n_alias = 1
    ne, no = len(extras), len(outs)
    dims = (((0 if ta else 1,), (1 if tb else 0,)), ((), ()))

    def body(*refs):
        a_ref, b_ref = refs[0], refs[1]
        e_refs = refs[2:2 + ne]
        o_refs = refs[2 + ne + n_alias:2 + ne + n_alias + no]
        part = lax.dot_general(a_ref[...], b_ref[...], dims, preferred_element_type=F32)

        def finish(acc):
            res = epilogue(acc, *[r[...] for r in e_refs]) if epilogue is not None else (acc,)
            for r, v in zip(o_refs, res):
                r[...] = v.astype(r.dtype)

        if gk == 1:
            finish(part)
        else:
            acc_ref = refs[-1]
            k = pl.program_id(2)

            @pl.when(k == 0)
            def _():
                acc_ref[...] = part

            @pl.when(k > 0)
            def _():
                acc_ref[...] += part

            @pl.when(k == gk - 1)
            def _():
                finish(acc_ref[...])

    res = pl.pallas_call(
        body, name=name, grid=g, in_specs=in_specs,
        out_specs=[pl.BlockSpec(o[2], w2(o[3])) for o in outs],
        out_shape=[jax.ShapeDtypeStruct(o[0], o[1]) for o in outs],
        scratch_shapes=[pltpu.VMEM(acc_shape, F32)] if gk > 1 else [],
        input_output_aliases=io_alias,
        compiler_params=_cparams(("parallel", "parallel", "arbitrary")),
    )(*operands)
    return res


def _rowwise(name, nsteps, fn, ins, outs, accs=(), prefetch=None):
    n_in, n_out, n_acc = len(ins), len(outs), len(accs)
    n_pre = 0 if prefetch is None else 1

    def body(*refs):
        refs = refs[n_pre:]
        i = pl.program_id(0)
        res = fn(i, *[r[...] for r in refs[:n_in]])
        for r, v in zip(refs[n_in:n_in + n_out], res[:n_out]):
            r[...] = v.astype(r.dtype)
        acc_refs = refs[n_in + n_out:n_in + n_out + n_acc]
        if n_acc:
            @pl.when(i == 0)
            def _():
                for r in acc_refs:
                    r[...] = jnp.zeros_like(r)

            for r, v in zip(acc_refs, res[n_out:]):
                r[...] += v

    if prefetch is None:
        zero = lambda shape: (lambda i: (0,) * len(shape))
    else:
        zero = lambda shape: (lambda i, p: (0,) * len(shape))
    in_specs = [pl.BlockSpec(b, m) for _, b, m in ins]
    out_specs = [pl.BlockSpec(o[2], o[3]) for o in outs] + [pl.BlockSpec(s, zero(s)) for s in accs]
    out_shape = [jax.ShapeDtypeStruct(o[0], o[1]) for o in outs] + [jax.ShapeDtypeStruct(s, F32) for s in accs]
    cp = _cparams(("arbitrary",))
    if prefetch is None:
        call = pl.pallas_call(body, name=name, grid=(nsteps,), in_specs=in_specs, out_specs=out_specs,
                              out_shape=out_shape, compiler_params=cp)
        return call(*[x[0] for x in ins])
    gs = pltpu.PrefetchScalarGridSpec(num_scalar_prefetch=1, grid=(nsteps,), in_specs=in_specs, out_specs=out_specs)
    call = pl.pallas_call(body, name=name, grid_spec=gs, out_shape=out_shape, compiler_params=cp)
    return call(prefetch, *[x[0] for x in ins])


def _rows(arr, tr, cols=None, cb=0):
    cols = arr.shape[1] if cols is None else cols
    return (arr, (tr, cols), lambda i: (i, cb))


def _whole(arr):
    return (arr, arr.shape, lambda i: (0,) * arr.ndim)


def _layernorm_fwd(z, g, b):
    mu = jnp.mean(z, axis=-1, keepdims=True)
    xc = z - mu
    var = jnp.mean(xc * xc, axis=-1, keepdims=True)
    return xc * lax.rsqrt(var + LN_EPS) * g + b


def _gelu_tanh(x):
    c = math.sqrt(2.0 / math.pi)
    return x * (0.5 * (1.0 + jnp.tanh(c * (x + 0.044715 * (x * x * x)))))


@jax.custom_jvp
def _expm1(x):
    return jnp.where(jnp.abs(x) < 0.5, jnp.tanh(0.5 * x) * (jnp.exp(x) + 1.0), jnp.exp(x) - 1.0)


@_expm1.defjvp
def _expm1_jvp(primals, tangents):
    (x,), (t,) = primals, tangents
    return _expm1(x), jnp.exp(x) * t


def _log_sigmoid(x):
    return jnp.minimum(x, 0.0) - jnp.log1p(jnp.exp(-jnp.abs(x)))


def _lru_gates(pre, xc, b_ra, b_rx, lam):
    w = xc.shape[-1]
    r = jax.nn.sigmoid(pre[:, :w] + b_ra)
    ig = jax.nn.sigmoid(pre[:, w:] + b_rx)
    log_a = LRU_C * r * _log_sigmoid(lam)
    a = jnp.exp(log_a)
    b = jnp.sqrt(-_expm1(2.0 * log_a)) * (ig * xc)
    return a, b


def _swap_halves(x):
    n = x.shape[1]
    first = (lax.broadcasted_iota(jnp.int32, x.shape, 1) % HEAD_DIM) < (HEAD_DIM // 2)
    return jnp.where(first, pltpu.roll(x, n - HEAD_DIM // 2, 1), pltpu.roll(x, HEAD_DIM // 2, 1))


def _shift_down(prev8, cur, s):
    ext = jnp.concatenate([prev8, cur], axis=0)
    return pltpu.roll(ext, s, 0)[8:]


def _shift_up(cur, next8, s):
    ext = jnp.concatenate([cur, next8], axis=0)
    return pltpu.roll(ext, ext.shape[0] - s, 0)[:cur.shape[0]]


def _ln_epilogue(alpha, scale):
    def epi(acc, hprev, g, b):
        z = alpha * hprev + scale * acc
        h = _layernorm_fwd(z, g, b)
        return z, h, h
    return epi


def _proj_ln(name, act, w, layer, hprev, g, b, alpha, scale):
    T, K = act.shape
    D = w.shape[2]
    bm = _tile(T, ROW_TILE)
    row = lambda i, j: (i, 0)
    return _matmul(
        name, (T // bm, 1, 1),
        (act, (bm, K), lambda i, j, k: (i, 0)), (w, (None, K, D), lambda i, j, k: (layer, 0, 0)),
        [((T, D), F32, (bm, D), row), ((T, D), F32, (bm, D), row), ((T, D), BF16, (bm, D), row)],
        (bm, D),
        extras=[(hprev, (bm, D), row), (g, (1, D), lambda i, j: (0, 0)), (b, (1, D), lambda i, j: (0, 0))],
        epilogue=_ln_epilogue(alpha, scale))


def _ln_bwd(name, dh, z, g, scale):
    T, D = z.shape
    tr = _tile(T, ROW_TILE)

    def fn(i, dh, z, g):
        mu = jnp.mean(z, axis=-1, keepdims=True)
        xc = z - mu
        var = jnp.mean(xc * xc, axis=-1, keepdims=True)
        rstd = lax.rsqrt(var + LN_EPS)
        xhat = xc * rstd
        dxh = dh * g
        dz = rstd * (dxh - jnp.mean(dxh, axis=-1, keepdims=True) - xhat * jnp.mean(dxh * xhat, axis=-1, keepdims=True))
        return (dz, scale * dz, jnp.sum(dh * xhat, axis=0, keepdims=True), jnp.sum(dh, axis=0, keepdims=True))

    row = lambda i: (i, 0)
    return _rowwise(name, T // tr, fn, [_rows(dh, tr), _rows(z, tr), _whole(g)],
                    [((T, D), F32, (tr, D), row), ((T, D), BF16, (tr, D), row)], accs=[(1, D), (1, D)])


def _grad_tn(name, a, b, stack, layer, bm, bn, bk=ROW_TILE, a_cb=None, b_cb=None):
    T, M = a.shape
    N = b.shape[1]
    bk = _tile(T, bk)
    lead = stack.ndim - 2
    if lead == 1:
        oblk, omap = (None, bm, bn), (lambda i, j: (layer, i, j))
    else:
        oblk, omap = (None, None, bm, bn), (lambda i, j: (layer, j, 0, 0))
    amap = (lambda i, j, k: (k, i)) if a_cb is None else (lambda i, j, k: (k, j))
    bmap = (lambda i, j, k: (k, j))
    gm = M // bm if a_cb is None else 1
    (out,) = _matmul(name, (gm, N // bn, T // bk), (a, (bk, bm), amap), (b, (bk, bn), bmap),
                     [(stack.shape, stack.dtype, oblk, omap)], (bm, bn), ta=True, alias_in=stack)
    return out


def _ffn_fwd(hprev, hb, w_in, w_out, layer, g, b, alpha):
    T, D = hb.shape
    N2 = w_in.shape[2]
    F = N2 // 2
    bn = N2 // 2
    bm = _tile(T, ROW_TILE)

    def swiglu(acc):
        w = acc.shape[1] // 2
        gg, uu = acc[:, :w], acc[:, w:]
        return acc, gg * jax.nn.sigmoid(gg) * uu

    gu, act = _matmul(
        "ffn_up", (T // bm, 2, 1),
        (hb, (bm, D), lambda i, j, k: (i, 0)), (w_in, (None, D, bn), lambda i, j, k: (layer, 0, j)),
        [((T, N2), BF16, (bm, bn), lambda i, j: (i, j)), ((T, F), BF16, (bm, bn // 2), lambda i, j: (i, j))],
        (bm, bn), epilogue=swiglu, n_outer=True)
    z, h, hb2 = _proj_ln("ffn_down_ln", act, w_out, layer, hprev, g, b, alpha, 0.5)
    return (z, h, hb2), (gu, act)


def _ffn_bwd(dh, z, g, hb_in, gu, act, w_in, w_out, layer, dw_in, dw_out, alpha):
    T, D = z.shape
    N2 = w_in.shape[2]
    F = N2 // 2
    bn = F // 2
    bm = _tile(T, ROW_TILE)
    dz, dyb, dg, db = _ln_bwd("ffn_ln_bwd", dh, z, g, 0.5)

    def dswiglu(dact, gu_blk):
        w = dact.shape[1]
        gg = gu_blk[:, :w].astype(F32)
        uu = gu_blk[:, w:].astype(F32)
        sg = jax.nn.sigmoid(gg)
        d_g = dact * uu * (sg * (1.0 + gg * (1.0 - sg)))
        d_u = dact * (gg * sg)
        return (jnp.concatenate([d_g, d_u], axis=1),)

    (dgu,) = _matmul(
        "ffn_dact", (T // bm, 2, 1),
        (dyb, (bm, D), lambda i, j, k: (i, 0)), (w_out, (None, bn, D), lambda i, j, k: (layer, j, 0)),
        [((T, N2), BF16, (bm, 2 * bn), lambda i, j: (i, j))], (bm, bn), tb=True,
        extras=[(gu, (bm, 2 * bn), lambda i, j: (i, j))], epilogue=dswiglu, n_outer=True)
    dw_out = _grad_tn("ffn_dwout", act, dyb, dw_out, layer, F // 2, D)
    dw_in = _grad_tn("ffn_dwin", hb_in, dgu, dw_in, layer, D, N2 // 4)
    bk = N2 // 4
    (dh_prev,) = _matmul(
        "ffn_dx", (T // bm, 1, 4),
        (dgu, (bm, bk), lambda i, j, k: (i, k)), (w_in, (None, D, bk), lambda i, j, k: (layer, 0, k)),
        [((T, D), F32, (bm, D), lambda i, j: (i, 0))], (bm, D), tb=True,
        extras=[(dz, (bm, D), lambda i, j: (i, 0))], epilogue=lambda acc, dzb: (alpha * dzb + acc,))
    return dh_prev, dg, db, dw_in, dw_out


def _input_grad(name, dy, w, layer, dz, alpha):
    T, N = dy.shape
    D = w.shape[1]
    bm = _tile(T, ROW_TILE)
    (out,) = _matmul(
        name, (T // bm, 1, 1),
        (dy, (bm, N), lambda i, j, k: (i, 0)), (w, (None, D, N), lambda i, j, k: (layer, 0, 0)),
        [((T, D), F32, (bm, D), lambda i, j: (i, 0))], (bm, D), tb=True,
        extras=[(dz, (bm, D), lambda i, j: (i, 0))], epilogue=lambda acc, dzb: (alpha * dzb + acc,))
    return out


def _back_proj(name, dy, w, layer, dtype):
    T, D = dy.shape
    K = w.shape[1]
    bm = _tile(T, ROW_TILE)
    (out,) = _matmul(
        name, (T // bm, 1, 1),
        (dy, (bm, D), lambda i, j, k: (i, 0)), (w, (None, K, D), lambda i, j, k: (layer, 0, 0)),
        [((T, K), dtype, (bm, K), lambda i, j: (i, 0))], (bm, K), tb=True)
    return out


def _rope_tables(T):
    pos = jnp.arange(T, dtype=F32)
    inv_freq = ROPE_THETA ** (-jnp.arange(0, HEAD_DIM, 2, dtype=F32) / HEAD_DIM)
    ang = pos[:, None] * inv_freq[None, :]
    cos, sin = jnp.cos(ang), jnp.sin(ang)
    c128 = jnp.tile(cos, (1, 4))
    s128 = jnp.tile(jnp.concatenate([-sin, sin], axis=1), (1, 2))
    return c128, s128


QK_COLS = (N_HEADS + N_KV_HEADS) * HEAD_DIM
Q_COLS = N_HEADS * HEAD_DIM
KV_COLS = N_KV_HEADS * HEAD_DIM
Q_SCALE = HEAD_DIM ** -0.5


def _qkv_rope(hb, w_qkv, layer, c128, s128):
    T, D = hb.shape
    N = w_qkv.shape[2]
    bm = _tile(T, ROW_TILE)

    def epi(acc, c, s):
        x = acc[:, :QK_COLS]
        rep = QK_COLS // 128
        r = x * jnp.tile(c, (1, rep)) + _swap_halves(x) * jnp.tile(s, (1, rep))
        return (jnp.concatenate([r[:, :Q_COLS] * Q_SCALE, r[:, Q_COLS:], acc[:, QK_COLS:]], axis=1),)

    (qkv,) = _matmul(
        "qkv_rope", (T // bm, 1, 1),
        (hb, (bm, D), lambda i, j, k: (i, 0)), (w_qkv, (None, D, N), lambda i, j, k: (layer, 0, 0)),
        [((T, N), BF16, (bm, N), lambda i, j: (i, 0))], (bm, N),
        extras=[(c128, (bm, 128), lambda i, j: (i, 0)), (s128, (bm, 128), lambda i, j: (i, 0))], epilogue=epi)
    return qkv


def _rope_bwd(dq, dkv, c128, s128):
    T = dq.shape[0]
    tr = _tile(T, ROW_TILE)

    def fn(i, dq, dkv, c, s):
        dx = jnp.concatenate([dq * Q_SCALE, dkv[:, :KV_COLS]], axis=1)
        rep = QK_COLS // 128
        d = dx * jnp.tile(c, (1, rep)) + _swap_halves(dx * jnp.tile(s, (1, rep)))
        return (jnp.concatenate([d, dkv[:, KV_COLS:]], axis=1),)

    N = Q_COLS + 2 * KV_COLS
    (out,) = _rowwise("rope_bwd", T // tr, fn, [_rows(dq, tr), _rows(dkv, tr), _rows(c128, tr), _rows(s128, tr)],
                      [((T, N), BF16, (tr, N), lambda i: (i, 0))])
    return out


def _attn_mask(first_block):
    q_pos = lax.broadcasted_iota(jnp.int32, (GROUP * ATTN_BLOCK, 2 * ATTN_BLOCK), 0) & (ATTN_BLOCK - 1)
    col = lax.broadcasted_iota(jnp.int32, (GROUP * ATTN_BLOCK, 2 * ATTN_BLOCK), 1)
    dist = q_pos + ATTN_BLOCK - col
    return (dist >= 0) & (dist < ATTN_BLOCK) & ((col >= ATTN_BLOCK) | jnp.logical_not(first_block))


def _sink_column(sk_ref, kvh):
    rg = lax.broadcasted_iota(jnp.int32, (GROUP * ATTN_BLOCK, 1), 0) // ATTN_BLOCK
    col = jnp.full((GROUP * ATTN_BLOCK, 1), sk_ref[0, kvh * GROUP], F32)
    for gi in range(1, GROUP):
        col = jnp.where(rg == gi, sk_ref[0, kvh * GROUP + gi], col)
    return col


def _stack_heads(x, kvh):
    return jnp.concatenate([x[:, (kvh * GROUP + gi) * HEAD_DIM:(kvh * GROUP + gi + 1) * HEAD_DIM] for gi in range(GROUP)], axis=0)


def _unstack_heads(parts):
    cols = []
    for p in parts:
        cols += [p[gi * ATTN_BLOCK:(gi + 1) * ATTN_BLOCK] for gi in range(GROUP)]
    return jnp.concatenate(cols, axis=1)


def _attn_softmax(q4, kb, mask, sink):
    s = lax.dot_general(q4, kb, (((1,), (1,)), ((), ())), preferred_element_type=F32)
    s = jnp.where(mask, s, NEG_BIG)
    m = jnp.maximum(jnp.max(s, axis=1, keepdims=True), sink)
    p = jnp.exp(s - m)
    e_sink = jnp.exp(sink - m)
    den = jnp.sum(p, axis=1, keepdims=True) + e_sink
    return p / den, e_sink / den


def _attn_fwd(qkv, sinks):
    T = qkv.shape[0]
    nb = T // ATTN_BLOCK
    kcb, vcb = Q_COLS // KV_COLS, Q_COLS // KV_COLS + 1

    def body(q_ref, kc_ref, kp_ref, vc_ref, vp_ref, sk_ref, o_ref):
        i = pl.program_id(0)
        mask = _attn_mask(i == 0)
        q = q_ref[...]
        kband = jnp.concatenate([kp_ref[...], kc_ref[...]], axis=0)
        vband = jnp.concatenate([vp_ref[...], vc_ref[...]], axis=0)
        parts = []
        for kvh in range(N_KV_HEADS):
            hs = slice(kvh * HEAD_DIM, (kvh + 1) * HEAD_DIM)
            pn, _ = _attn_softmax(_stack_heads(q, kvh), kband[:, hs], mask, _sink_column(sk_ref, kvh))
            parts.append(jnp.dot(pn.astype(BF16), vband[:, hs], preferred_element_type=F32))
        o_ref[...] = _unstack_heads(parts).astype(o_ref.dtype)

    prev = lambda i: jnp.maximum(i - 1, 0)
    return pl.pallas_call(
        body, name="attn_fwd", grid=(nb,),
        in_specs=[pl.BlockSpec((ATTN_BLOCK, Q_COLS), lambda i: (i, 0)),
                  pl.BlockSpec((ATTN_BLOCK, KV_COLS), lambda i: (i, kcb)),
                  pl.BlockSpec((ATTN_BLOCK, KV_COLS), lambda i: (prev(i), kcb)),
                  pl.BlockSpec((ATTN_BLOCK, KV_COLS), lambda i: (i, vcb)),
                  pl.BlockSpec((ATTN_BLOCK, KV_COLS), lambda i: (prev(i), vcb)),
                  pl.BlockSpec(memory_space=pltpu.SMEM)],
        out_specs=pl.BlockSpec((ATTN_BLOCK, Q_COLS), lambda i: (i, 0)),
        out_shape=jax.ShapeDtypeStruct((T, Q_COLS), BF16),
        compiler_params=_cparams(("arbitrary",)),
    )(qkv, qkv, qkv, qkv, qkv, sinks)


def _attn_bwd(qkv, do, sinks):
    T = qkv.shape[0]
    nb = T // ATTN_BLOCK
    kcb, vcb = Q_COLS // KV_COLS, Q_COLS // KV_COLS + 1
    B = ATTN_BLOCK

    def body(q_ref, kc_ref, kp_ref, vc_ref, vp_ref, do_ref, sk_ref, dq_ref, dkv_ref, dsk_ref, carry_ref):
        i = pl.program_id(0)

        @pl.when(i == 0)
        def _():
            carry_ref[...] = jnp.zeros_like(carry_ref)
            dsk_ref[...] = jnp.zeros_like(dsk_ref)

        @pl.when(i < nb)
        def _():
            mask = _attn_mask(i == 0)
            q = q_ref[...]
            do_blk = do_ref[...]
            kband = jnp.concatenate([kp_ref[...], kc_ref[...]], axis=0)
            vband = jnp.concatenate([vp_ref[...], vc_ref[...]], axis=0)
            dq_parts, dk_parts, dv_parts = [], [], []
            for kvh in range(N_KV_HEADS):
                hs = slice(kvh * HEAD_DIM, (kvh + 1) * HEAD_DIM)
                q4 = _stack_heads(q, kvh)
                do4 = _stack_heads(do_blk, kvh)
                kb, vb = kband[:, hs], vband[:, hs]
                pn, p_sink = _attn_softmax(q4, kb, mask, _sink_column(sk_ref, kvh))
                dp = lax.dot_general(do4, vb, (((1,), (1,)), ((), ())), preferred_element_type=F32)
                delta = jnp.sum(pn * dp, axis=1, keepdims=True)
                ds = (pn * (dp - delta)).astype(BF16)
                dsk_ref[kvh] += -(p_sink * delta)
                dq_parts.append(jnp.dot(ds, kb, preferred_element_type=F32))
                dk_parts.append(lax.dot_general(ds, q4, (((0,), (0,)), ((), ())), preferred_element_type=F32))
                dv_parts.append(lax.dot_general(pn.astype(BF16), do4, (((0,), (0,)), ((), ())), preferred_element_type=F32))
            dq_ref[...] = _unstack_heads(dq_parts)
            dkv = jnp.concatenate(dk_parts + dv_parts, axis=1)
            dkv_ref[...] = carry_ref[...] + dkv[:B]
            carry_ref[...] = dkv[B:]

        @pl.when(i == nb)
        def _():
            dkv_ref[...] = carry_ref[...]

    cur = lambda i: jnp.minimum(i, nb - 1)
    prev = lambda i: jnp.maximum(cur(i) - 1, 0)
    lag = lambda i: jnp.maximum(i - 1, 0)
    return pl.pallas_call(
        body, name="attn_bwd", grid=(nb + 1,),
        in_specs=[pl.BlockSpec((B, Q_COLS), lambda i: (cur(i), 0)),
                  pl.BlockSpec((B, KV_COLS), lambda i: (cur(i), kcb)),
                  pl.BlockSpec((B, KV_COLS), lambda i: (prev(i), kcb)),
                  pl.BlockSpec((B, KV_COLS), lambda i: (cur(i), vcb)),
                  pl.BlockSpec((B, KV_COLS), lambda i: (prev(i), vcb)),
                  pl.BlockSpec((B, Q_COLS), lambda i: (cur(i), 0)),
                  pl.BlockSpec(memory_space=pltpu.SMEM)],
        out_specs=[pl.BlockSpec((B, Q_COLS), lambda i: (cur(i), 0)),
                   pl.BlockSpec((B, 2 * KV_COLS), lambda i: (lag(i), 0)),
                   pl.BlockSpec((N_KV_HEADS, GROUP * B, 1), lambda i: (0, 0, 0))],
        out_shape=[jax.ShapeDtypeStruct((T, Q_COLS), F32), jax.ShapeDtypeStruct((T, 2 * KV_COLS), F32),
                   jax.ShapeDtypeStruct((N_KV_HEADS, GROUP * B, 1), F32)],
        scratch_shapes=[pltpu.VMEM((B, 2 * KV_COLS), F32)],
        compiler_params=_cparams(("arbitrary",)),
    )(qkv, qkv, qkv, qkv, qkv, do, sinks)


def _halo_prev(arr, tr, cols, cb=0):
    per = tr // 8
    return (arr, (8, cols), lambda i: (jnp.maximum(i * per - 1, 0), cb))


def _halo_next(arr, tr, cols, cb=0):
    per = tr // 8
    last = arr.shape[0] // 8 - 1
    return (arr, (8, cols), lambda i: (jnp.minimum((i + 1) * per, last), cb))


def _conv_fwd(xg, cw, cb):
    T = xg.shape[0]
    C = cb.shape[1]
    tr = _tile(T, ROW_TILE)

    def fn(i, cur, prev8, cb, *cw):
        prev8 = jnp.where(i == 0, 0.0, prev8)
        xc = cb + cw[CONV_W - 1] * cur
        for s in range(1, CONV_W):
            xc = xc + cw[CONV_W - 1 - s] * _shift_down(prev8, cur, s)
        return xc, xc

    row = lambda i: (i, 0)
    return _rowwise("lru_conv", T // tr, fn, [_rows(xg, tr, C), _halo_prev(xg, tr, C), _whole(cb)] + [_whole(w) for w in cw],
                    [((T, C), F32, (tr, C), row), ((T, C), BF16, (tr, C), row)])


def _conv_bwd(dxc, xg, dgb, cw):
    T, C = dxc.shape
    tr = _tile(T, ROW_TILE)
    nt = T // tr

    def fn(i, d_cur, d_next8, x_cur, x_prev8, dgb, *cw):
        d_next8 = jnp.where(i == nt - 1, 0.0, d_next8)
        x_prev8 = jnp.where(i == 0, 0.0, x_prev8)
        dxb = cw[CONV_W - 1] * d_cur
        dcw = [jnp.sum(d_cur * x_cur, axis=0, keepdims=True)]
        for s in range(1, CONV_W):
            dxb = dxb + cw[CONV_W - 1 - s] * _shift_up(d_cur, d_next8, s)
            dcw.append(jnp.sum(d_cur * _shift_down(x_prev8, x_cur, s), axis=0, keepdims=True))
        return (jnp.concatenate([dxb.astype(BF16), dgb], axis=1), dcw[3], dcw[2], dcw[1], dcw[0],
                jnp.sum(d_cur, axis=0, keepdims=True))

    return _rowwise("lru_conv_bwd", nt, fn,
                    [_rows(dxc, tr), _halo_next(dxc, tr, C), _rows(xg, tr, C), _halo_prev(xg, tr, C), _rows(dgb, tr)] + [_whole(w) for w in cw],
                    [((T, 2 * C), BF16, (tr, 2 * C), lambda i: (i, 0))], accs=[(1, C)] * (CONV_W + 1))


def _lru_gate_fwd(xc, xcb, w_rarx, layer, b_ra, b_rx, lam):
    T, C = xc.shape
    W = C // RNN_BLOCKS
    bm = _tile(T, ROW_TILE)

    def epi(acc, xc_blk, bra, brx, lm):
        a, b = _lru_gates(acc, xc_blk, bra, brx, lm)
        return acc, a, b

    blk = lambda i, j: (i, j)
    par = lambda i, j: (0, j)
    return _matmul(
        "lru_gates", (T // bm, RNN_BLOCKS, 1),
        (xcb, (bm, W), lambda i, j, k: (i, j)), (w_rarx, (None, None, W, 2 * W), lambda i, j, k: (layer, j, 0, 0)),
        [((T, 2 * C), F32, (bm, 2 * W), blk), ((T, C), F32, (bm, W), blk), ((T, C), F32, (bm, W), blk)],
        (bm, 2 * W),
        extras=[(xc, (bm, W), blk), (b_ra, (1, W), par), (b_rx, (1, W), par), (lam, (1, W), par)], epilogue=epi)


def _lru_gate_bwd(pre, xc, lam_adj, h, b_ra, b_rx, lam):
    T, C = xc.shape
    W = C // RNN_BLOCKS
    tr = _tile(T, ROW_TILE // 2)

    def fn(i, pre, xc, adj, h_cur, h_prev8, bra, brx, lm):
        h_prev8 = jnp.where(i == 0, 0.0, h_prev8)
        da = adj * _shift_down(h_prev8, h_cur, 1)
        dpre, dxc, dbra, dbrx, dlam = [], [], [], [], []
        for n in range(RNN_BLOCKS):
            cs = slice(n * W, (n + 1) * W)
            _, vjp = jax.vjp(_lru_gates, pre[:, 2 * n * W:2 * (n + 1) * W], xc[:, cs], bra[:, cs], brx[:, cs], lm[:, cs])
            g = vjp((da[:, cs], adj[:, cs]))
            for lst, v in zip((dpre, dxc, dbra, dbrx, dlam), g):
                lst.append(v)
        cat = lambda l: jnp.concatenate(l, axis=1)
        return cat(dpre), cat(dxc), cat(dbra), cat(dbrx), cat(dlam)

    row = lambda i: (i, 0)
    return _rowwise("lru_gates_bwd", T // tr, fn,
                    [_rows(pre, tr), _rows(xc, tr), _rows(lam_adj, tr), _rows(h, tr), _halo_prev(h, tr, C),
                     _whole(b_ra), _whole(b_rx), _whole(lam)],
                    [((T, 2 * C), BF16, (tr, 2 * C), row), ((T, C), F32, (tr, C), row)], accs=[(1, C)] * 3)


def _scan_fwd(a, b):
    T, C = a.shape
    tt = _tile(T, ROW_TILE)

    def body(a_ref, b_ref, o_ref, c_ref):
        @pl.when(pl.program_id(0) == 0)
        def _():
            c_ref[...] = jnp.zeros_like(c_ref)

        row = lax.broadcasted_iota(jnp.int32, (8, C), 0)

        def step(j, carry):
            sl = pl.ds(pl.multiple_of(j * 8, 8), 8)
            A, B = a_ref[sl, :], b_ref[sl, :]
            for d in (1, 2, 4):
                ok = row >= d
                B = jnp.where(ok, A * pltpu.roll(B, d, 0) + B, B)
                A = jnp.where(ok, A * pltpu.roll(A, d, 0), A)
            h = A * carry + B
            o_ref[sl, :] = h
            return jnp.sum(jnp.where(row == 7, h, 0.0), axis=0, keepdims=True)

        c_ref[0:1, :] = lax.fori_loop(0, tt // 8, step, c_ref[0:1, :])

    spec = pl.BlockSpec((tt, C), lambda i: (i, 0))
    return pl.pallas_call(body, name="lru_scan", grid=(T // tt,), in_specs=[spec, spec], out_specs=spec,
                          out_shape=jax.ShapeDtypeStruct((T, C), F32), scratch_shapes=[pltpu.VMEM((8, C), F32)],
                          compiler_params=_cparams(("arbitrary",)))(a, b)


def _scan_bwd(a, dh):
    T, C = a.shape
    tt = _tile(T, ROW_TILE)
    nt = T // tt

    def body(a_ref, d_ref, o_ref, c_ref):
        @pl.when(pl.program_id(0) == 0)
        def _():
            c_ref[...] = jnp.zeros_like(c_ref)

        row = lax.broadcasted_iota(jnp.int32, (8, C), 0)

        def step(jj, carry):
            adj_next, a_next = carry
            j = tt // 8 - 1 - jj
            sl = pl.ds(pl.multiple_of(j * 8, 8), 8)
            a_blk = a_ref[sl, :]
            A = jnp.where(row < 7, pltpu.roll(a_blk, 7, 0), a_next)
            B = d_ref[sl, :]
            for d in (1, 2, 4):
                ok = row < 8 - d
                B = jnp.where(ok, A * pltpu.roll(B, 8 - d, 0) + B, B)
                A = jnp.where(ok, A * pltpu.roll(A, 8 - d, 0), A)
            adj = A * adj_next + B
            o_ref[sl, :] = adj
            first = lambda v: jnp.sum(jnp.where(row == 0, v, 0.0), axis=0, keepdims=True)
            return first(adj), first(a_blk)

        adj0, a0 = lax.fori_loop(0, tt // 8, step, (c_ref[0:1, :], c_ref[1:2, :]))
        c_ref[0:1, :] = adj0
        c_ref[1:2, :] = a0

    spec = pl.BlockSpec((tt, C), lambda i: (nt - 1 - i, 0))
    return pl.pallas_call(body, name="lru_scan_bwd", grid=(nt,), in_specs=[spec, spec], out_specs=spec,
                          out_shape=jax.ShapeDtypeStruct((T, C), F32), scratch_shapes=[pltpu.VMEM((8, C), F32)],
                          compiler_params=_cparams(("arbitrary",)))(a, dh)


def _lru_out_fwd(h, xg):
    T, C = h.shape
    tr = _tile(T, ROW_TILE)
    (y,) = _rowwise("lru_out", T // tr, lambda i, h, gb: (h * _gelu_tanh(gb),), [_rows(h, tr), _rows(xg, tr, C, 1)],
                    [((T, C), BF16, (tr, C), lambda i: (i, 0))])
    return y


def _lru_out_bwd(dy, h, xg):
    T, C = h.shape
    tr = _tile(T, ROW_TILE)

    def fn(i, dy, h, gb):
        _, vjp = jax.vjp(lambda h, gb: h * _gelu_tanh(gb), h, gb)
        return vjp(dy)

    row = lambda i: (i, 0)
    return _rowwise("lru_out_bwd", T // tr, fn, [_rows(dy, tr), _rows(h, tr), _rows(xg, tr, C, 1)],
                    [((T, C), F32, (tr, C), row), ((T, C), BF16, (tr, C), row)])


class _Sharded:
    def __init__(self, kind, size, off=0, width=None):
        self.kind, self.size, self.off, self.width = kind, size, off, width

    def slot(self, cx, cy):
        return (2 * cy + cx) if self.kind == "perm" else (2 * cx + cy)

    def at(self, ref, cx, cy, layers):
        s = self.slot(cx, cy)
        start = s * self.size
        if not isinstance(start, int):
            start = pl.multiple_of(start, 8 if self.kind in ("rows", "rarx") else 128)
        if self.kind in ("cols", "perm"):
            return ref.at[layers, :, pl.ds(start, self.size)]
        if self.kind == "rows":
            return ref.at[layers, pl.ds(start, self.size), :]
        if self.kind == "rarx":
            return ref.at[layers, :, pl.ds(start, self.size), pl.ds(self.off, self.width)]
        raise ValueError(self.kind)


def _mesh_pos():
    return lax.axis_index("x"), lax.axis_index("y"), lax.axis_index("c")


def _peer_chips(x, y):
    return [(1 - x, y), (x, 1 - y), (1 - x, 1 - y)]


def _place(shard, out_shape, out_dtype, sh, pos, prev=None):
    def body(p_ref, x_ref, *rest):
        rest[-1][...] = x_ref[...].astype(rest[-1].dtype)

    if sh.kind in ("cols", "perm"):
        L, R, Ns = shard.shape
        tr = _tile(R, ROW_TILE)
        k = 1 if sh.kind == "perm" else 0
        grid = (L, R // tr)
        ispec = pl.BlockSpec((None, tr, Ns), lambda l, i, p: (l, i, 0))
        ospec = pl.BlockSpec((None, tr, Ns), lambda l, i, p: (l, i, p[k]))
    elif sh.kind == "rows":
        L, Rs, D = shard.shape
        tr = _tile(Rs, ROW_TILE)
        nt = Rs // tr
        grid = (L, nt)
        ispec = pl.BlockSpec((None, tr, D), lambda l, i, p: (l, i, 0))
        ospec = pl.BlockSpec((None, tr, D), lambda l, i, p: (l, p[0] * nt + i, 0))
    else:
        L, nb, Rs, Wd = shard.shape
        cb = sh.off // Wd
        grid = (L, nb)
        ispec = pl.BlockSpec((None, None, Rs, Wd), lambda l, i, p: (l, i, 0, 0))
        ospec = pl.BlockSpec((None, None, Rs, Wd), lambda l, i, p: (l, i, p[0], cb))
    in_specs = [ispec]
    operands = [pos, shard]
    alias = {}
    if prev is not None:
        in_specs.append(pl.BlockSpec(memory_space=pl.ANY))
        operands.append(prev)
        alias = {2: 0}
    gs = pltpu.PrefetchScalarGridSpec(num_scalar_prefetch=1, grid=grid, in_specs=in_specs, out_specs=ospec)
    return pl.pallas_call(body, name="weight_place", grid_spec=gs, out_shape=jax.ShapeDtypeStruct(out_shape, out_dtype),
                          input_output_aliases=alias, compiler_params=_cparams(("arbitrary", "arbitrary")))(*operands)


def _all_gather(placed, shardings):
    n = len(placed)
    K = 3 * n

    def body(*refs):
        outs = refs[n:2 * n]
        send, recv = refs[2 * n:]
        x, y, c = _mesh_pos()
        sib = (x, y, 1 - c)
        peers = _peer_chips(x, y)
        pending = []
        for t, sh in enumerate(shardings):
            lh = placed[t].shape[0] // 2
            half = pl.ds(c * lh, lh)
            region = sh.at(outs[t], x, y, half)
            for j, (px, py) in enumerate(peers):
                cp = pltpu.make_async_remote_copy(region, region, send.at[3 * t + j], recv.at[3 * t + j],
                                                  device_id=(px, py, c), device_id_type=MESH)
                cp.start()
                pending.append(cp.wait_send)
        for t, sh in enumerate(shardings):
            lh = placed[t].shape[0] // 2
            half = pl.ds(c * lh, lh)
            for j, (px, py) in enumerate(peers):
                region = sh.at(outs[t], px, py, half)
                pltpu.make_async_remote_copy(region, region, send.at[3 * t + j], recv.at[3 * t + j],
                                             device_id=(px, py, c), device_id_type=MESH).wait_recv()
                fwd = pltpu.make_async_remote_copy(region, region, send.at[K + 3 * t + j], recv.at[K + 3 * t + j],
                                                   device_id=sib, device_id_type=MESH)
                fwd.start()
                pending.append(fwd.wait_send)
        for t, sh in enumerate(shardings):
            lh = placed[t].shape[0] // 2
            other = pl.ds((1 - c) * lh, lh)
            for j, (px, py) in enumerate(peers):
                region = sh.at(outs[t], px, py, other)
                pltpu.make_async_remote_copy(region, region, send.at[K + 3 * t + j], recv.at[K + 3 * t + j],
                                             device_id=sib, device_id_type=MESH).wait_recv()
        for w in pending:
            w()

    hbm = pl.BlockSpec(memory_space=pl.ANY)
    return pl.pallas_call(
        body, name="weights_all_gather", in_specs=[hbm] * n, out_specs=[hbm] * n,
        out_shape=[jax.ShapeDtypeStruct(p.shape, p.dtype) for p in placed],
        input_output_aliases={t: t for t in range(n)},
        scratch_shapes=[pltpu.SemaphoreType.DMA((2 * K,)), pltpu.SemaphoreType.DMA((2 * K,))],
    )(*placed)


def _sibling_swap(grads):
    n = len(grads)

    def body(*refs):
        ins, outs = refs[:n], refs[n:2 * n]
        send, recv = refs[2 * n:]
        x, y, c = _mesh_pos()
        cps = []
        for t in range(n):
            lh = grads[t].shape[0] // 2
            cp = pltpu.make_async_remote_copy(ins[t].at[pl.ds((1 - c) * lh, lh)], outs[t], send.at[t], recv.at[t],
                                              device_id=(x, y, 1 - c), device_id_type=MESH)
            cp.start()
            cps.append(cp)
        for cp in cps:
            cp.wait()

    hbm = pl.BlockSpec(memory_space=pl.ANY)
    return pl.pallas_call(
        body, name="grad_sibling_swap", in_specs=[hbm] * n, out_specs=[hbm] * n,
        out_shape=[jax.ShapeDtypeStruct((g.shape[0] // 2,) + g.shape[1:], g.dtype) for g in grads],
        scratch_shapes=[pltpu.SemaphoreType.DMA((n,)), pltpu.SemaphoreType.DMA((n,))],
    )(*grads)


def _shard_shape(full, sh):
    s = list(full)
    if sh.kind in ("cols", "perm"):
        s[-1] = sh.size
    else:
        s[-2] = sh.size
    return tuple(s)


def _chip_exchange(parts, shardings):
    n = len(parts)

    def body(*refs):
        ins, outs = refs[:n], refs[n:2 * n]
        send, recv, loc = refs[2 * n:]
        x, y, c = _mesh_pos()
        me = 2 * x + y
        pending = []
        for t, sh in enumerate(shardings):
            src_all = _Sharded("rarx", sh.size, 0, parts[t].shape[-1]) if sh.kind == "rarx" else sh
            mine = pltpu.make_async_copy(src_all.at(ins[t], x, y, slice(None)), outs[t].at[me], loc.at[t])
            mine.start()
            pending.append(mine.wait)
            for j, (px, py) in enumerate(_peer_chips(x, y)):
                cp = pltpu.make_async_remote_copy(src_all.at(ins[t], px, py, slice(None)), outs[t].at[me],
                                                  send.at[3 * t + j], recv.at[3 * t + j],
                                                  device_id=(px, py, c), device_id_type=MESH)
                cp.start()
                pending.append(cp.wait_send)
        for t, sh in enumerate(shardings):
            for j, (px, py) in enumerate(_peer_chips(x, y)):
                dst = outs[t].at[2 * px + py]
                pltpu.make_async_remote_copy(dst, dst, send.at[3 * t + j], recv.at[3 * t + j],
                                             device_id=(px, py, c), device_id_type=MESH).wait_recv()
        for w in pending:
            w()

    hbm = pl.BlockSpec(memory_space=pl.ANY)
    return pl.pallas_call(
        body, name="grad_chip_exchange", in_specs=[hbm] * n, out_specs=[hbm] * n,
        out_shape=[jax.ShapeDtypeStruct((N_CHIPS,) + _shard_shape(p.shape, sh), p.dtype) for p, sh in zip(parts, shardings)],
        scratch_shapes=[pltpu.SemaphoreType.DMA((3 * n,)), pltpu.SemaphoreType.DMA((3 * n,)), pltpu.SemaphoreType.DMA((n,))],
    )(*parts)


def _sibling_share(sums):
    n = len(sums)

    def body(*refs):
        outs = refs[n:2 * n]
        send, recv = refs[2 * n:]
        x, y, c = _mesh_pos()
        cps = []
        for t in range(n):
            lh = sums[t].shape[0] // 2
            mine = outs[t].at[pl.ds(c * lh, lh)]
            cp = pltpu.make_async_remote_copy(mine, mine, send.at[t], recv.at[t], device_id=(x, y, 1 - c), device_id_type=MESH)
            cp.start()
            cps.append(cp)
        for t, cp in enumerate(cps):
            lh = sums[t].shape[0] // 2
            theirs = outs[t].at[pl.ds((1 - c) * lh, lh)]
            cp.wait_send()
            pltpu.make_async_remote_copy(theirs, theirs, send.at[t], recv.at[t], device_id=(x, y, 1 - c),
                                         device_id_type=MESH).wait_recv()

    hbm = pl.BlockSpec(memory_space=pl.ANY)
    return pl.pallas_call(
        body, name="grad_sibling_share", in_specs=[hbm] * n, out_specs=[hbm] * n,
        out_shape=[jax.ShapeDtypeStruct(s.shape, s.dtype) for s in sums],
        input_output_aliases={t: t for t in range(n)},
        scratch_shapes=[pltpu.SemaphoreType.DMA((n,)), pltpu.SemaphoreType.DMA((n,))],
    )(*sums)


def _flat2(shape):
    return (math.prod(shape[:-1]), shape[-1])


def _pair_add(grad, got, c_arr):
    lh = got.shape[0]
    R, C = _flat2(got.shape)
    tr = _tile(R, 256 if C > 2048 else ROW_TILE)
    nt = R // tr
    (out,) = _rowwise(
        "grad_pair_add", nt, lambda i, a, b: (a.astype(F32) + b.astype(F32),),
        [(grad.reshape(_flat2(grad.shape)), (tr, C), lambda i, p: (p[0] * nt + i, 0)), (got.reshape(R, C), (tr, C), lambda i, p: (i, 0))],
        [((R, C), got.dtype, (tr, C), lambda i, p: (i, 0))], prefetch=c_arr)
    return out.reshape(got.shape)


def _chip_sum(recv, c_arr):
    shape = recv.shape[1:]
    R, C = _flat2(shape)
    tr = _tile(R, ROW_TILE)
    nt = R // tr
    r3 = recv.reshape((N_CHIPS, R, C))

    def fn(i, blk):
        v = blk.astype(F32)
        return (((v[0] + v[1]) + v[2]) + v[3],)

    (out,) = _rowwise("grad_chip_sum", nt, fn, [(r3, (N_CHIPS, tr, C), lambda i, p: (0, i, 0))],
                      [((2 * R, C), F32, (tr, C), lambda i, p: (p[0] * nt + i, 0))], prefetch=c_arr)
    return out.reshape((2 * shape[0],) + shape[1:])


def _adamw(w, g, m, v):
    shape = w.shape
    R, C = _flat2(shape)
    tr = _tile(R, 256 if R % 256 == 0 else R)

    def fn(i, w, g, m, v):
        m2 = ADAM_B1 * m + (1.0 - ADAM_B1) * g
        v2 = ADAM_B2 * v + (1.0 - ADAM_B2) * (g * g)
        m_hat = m2 / (1.0 - ADAM_B1 ** ADAM_STEP)
        v_hat = v2 / (1.0 - ADAM_B2 ** ADAM_STEP)
        delta = -ADAM_LR * (m_hat / (jnp.sqrt(v_hat) + ADAM_EPS) + ADAM_WD * w)
        return delta, m2, v2

    row = lambda i: (i, 0)
    f2 = lambda a: a.reshape(R, C)
    outs = _rowwise("adamw", R // tr, fn, [_rows(f2(a), tr) for a in (w, g, m, v)], [((R, C), F32, (tr, C), row)] * 3)
    return [o.reshape(shape) for o in outs]


def _loss_and_grad(h, target):
    T, D = h.shape
    tr = _tile(T, ROW_TILE)

    def fn(i, h, t):
        err = h - t
        return err * (1.0 / D), jnp.sum(err * err, axis=0, keepdims=True)

    dh, sq = _rowwise("loss", T // tr, fn, [_rows(h, tr), _rows(target, tr)], [((T, D), F32, (tr, D), lambda i: (i, 0))], accs=[(1, D)])
    return dh, (0.5 / D) * jnp.sum(sq)


def kernel(x, ffn1_w_in, ffn1_w_out, ffn2_w_in, ffn2_w_out, ln_g, ln_b, attn_w_qkv, attn_sinks, attn_w_o, lru_w_in, lru_conv_w, lru_conv_b, lru_w_ra, lru_b_ra, lru_w_rx, lru_b_rx, lru_lambda, lru_w_out, loss_target, m_ffn1_w_in, m_ffn1_w_out, m_ffn2_w_in, m_ffn2_w_out, m_ln_g, m_ln_b, m_attn_w_qkv, m_attn_sinks, m_attn_w_o, m_lru_w_in, m_lru_conv_w, m_lru_conv_b, m_lru_w_ra, m_lru_b_ra, m_lru_w_rx, m_lru_b_rx, m_lru_lambda, m_lru_w_out, v_ffn1_w_in, v_ffn1_w_out, v_ffn2_w_in, v_ffn2_w_out, v_ln_g, v_ln_b, v_attn_w_qkv, v_attn_sinks, v_attn_w_o, v_lru_w_in, v_lru_conv_w, v_lru_conv_b, v_lru_w_ra, v_lru_b_ra, v_lru_w_rx, v_lru_b_rx, v_lru_lambda, v_lru_w_out):
    names = ["ffn1_w_in", "ffn1_w_out", "ffn2_w_in", "ffn2_w_out", "ln_g", "ln_b", "attn_w_qkv", "attn_sinks", "attn_w_o",
             "lru_w_in", "lru_conv_w", "lru_conv_b", "lru_w_ra", "lru_b_ra", "lru_w_rx", "lru_b_rx", "lru_lambda", "lru_w_out"]
    W = dict(zip(names, [ffn1_w_in, ffn1_w_out, ffn2_w_in, ffn2_w_out, ln_g, ln_b, attn_w_qkv, attn_sinks, attn_w_o,
                         lru_w_in, lru_conv_w, lru_conv_b, lru_w_ra, lru_b_ra, lru_w_rx, lru_b_rx, lru_lambda, lru_w_out]))
    M = dict(zip(names, [m_ffn1_w_in, m_ffn1_w_out, m_ffn2_w_in, m_ffn2_w_out, m_ln_g, m_ln_b, m_attn_w_qkv, m_attn_sinks, m_attn_w_o,
                         m_lru_w_in, m_lru_conv_w, m_lru_conv_b, m_lru_w_ra, m_lru_b_ra, m_lru_w_rx, m_lru_b_rx, m_lru_lambda, m_lru_w_out]))
    V = dict(zip(names, [v_ffn1_w_in, v_ffn1_w_out, v_ffn2_w_in, v_ffn2_w_out, v_ln_g, v_ln_b, v_attn_w_qkv, v_attn_sinks, v_attn_w_o,
                         v_lru_w_in, v_lru_conv_w, v_lru_conv_b, v_lru_w_ra, v_lru_b_ra, v_lru_w_rx, v_lru_b_rx, v_lru_lambda, v_lru_w_out]))

    T, D = x.shape[1], x.shape[2]
    L = ffn1_w_in.shape[0]
    LA, LR = attn_w_qkv.shape[0], lru_w_in.shape[0]
    N2 = ffn1_w_in.shape[2] * N_CHIPS
    F = N2 // 2
    C = lru_lambda.shape[1] * N_CHIPS
    CW = C // N_CHIPS
    alpha = (2.0 * L) ** 0.25
    c_arr = lax.axis_index("c").astype(jnp.int32).reshape(1)

    n_sink = attn_sinks.size
    assert n_sink <= CW

    def pack_small(d):
        sink_row = jnp.zeros((1, CW), F32).at[0, :n_sink].set(d["attn_sinks"].reshape(-1))
        rows = [d["ln_g"].reshape(-1, CW), d["ln_b"].reshape(-1, CW), d["lru_conv_w"].reshape(-1, CW), d["lru_conv_b"],
                d["lru_b_ra"], d["lru_b_rx"], d["lru_lambda"], sink_row]
        p = jnp.concatenate(rows, axis=0)
        return jnp.concatenate([p, jnp.zeros((SMALL_ROWS - p.shape[0], CW), F32)], axis=0)

    o_g, o_b = 0, 3 * L
    o_cw = 6 * L
    o_cb = o_cw + LR * CONV_W
    o_ra, o_rx, o_lam = o_cb + LR, o_cb + 2 * LR, o_cb + 3 * LR
    o_sink = o_cb + 4 * LR
    assert o_sink < SMALL_ROWS

    cols = lambda a: _Sharded("cols", a.shape[-1])
    rows_ = lambda a: _Sharded("rows", a.shape[-2])
    RW = lru_w_ra.shape[2]
    sh_list = [_Sharded("perm", N2 // 4), rows_(ffn1_w_out), _Sharded("perm", N2 // 4), rows_(ffn2_w_out),
               cols(attn_w_qkv), rows_(attn_w_o), cols(lru_w_in), rows_(lru_w_out)]
    big = [ffn1_w_in, ffn1_w_out, ffn2_w_in, ffn2_w_out, attn_w_qkv, attn_w_o, lru_w_in, lru_w_out]
    blk_w = lru_w_ra.shape[3]
    small_sh = _Sharded("cols", CW)
    full = lambda a, sh: tuple(a.shape[:-1]) + (a.shape[-1] * N_CHIPS,) if sh.kind in ("cols", "perm") else \
        tuple(a.shape[:-2]) + (a.shape[-2] * N_CHIPS, a.shape[-1])
    mx, my = lax.axis_index("x"), lax.axis_index("y")
    pos = jnp.stack([2 * mx + my, 2 * my + mx]).astype(jnp.int32)
    placed = [_place(a, full(a, sh), BF16, sh, pos) for a, sh in zip(big, sh_list)]
    rarx_shape = (LR, RNN_BLOCKS, RW * N_CHIPS, 2 * blk_w)
    p_ra = _place(lru_w_ra, rarx_shape, BF16, _Sharded("rarx", RW, 0, blk_w), pos)
    placed.append(_place(lru_w_rx, rarx_shape, BF16, _Sharded("rarx", RW, blk_w, blk_w), pos, prev=p_ra))
    placed.append(_place(pack_small(W).reshape(2, SMALL_ROWS // 2, CW), (2, SMALL_ROWS // 2, C), F32, small_sh, pos))
    g_w_in1, g_w_out1, g_w_in2, g_w_out2, g_qkv, g_wo, g_lin, g_lout, g_rarx, g_small = _all_gather(
        placed, sh_list + [_Sharded("rarx", RW, 0, 2 * blk_w), small_sh])
    small = g_small.reshape(SMALL_ROWS, C)
    row_of = lambda r: small[r:r + 1]
    assert D == C, "packed small parameters assume d_model == d_rnn"

    c128, s128 = _rope_tables(T)
    sink_rows = [attn_sinks[j:j + 1] for j in range(LA)]

    h = x.reshape(T, D)
    hb = h.astype(BF16)
    saved = []
    for i in range(L):
        j = i // 2
        lay = {}
        lay["hb0"] = hb
        (z, h, hb), (lay["gu1"], lay["act1"]) = _ffn_fwd(h, hb, g_w_in1, g_w_out1, i, row_of(o_g + 3 * i), row_of(o_b + 3 * i), alpha)
        lay["z1"], lay["hb1"] = z, hb
        if i % 2 == 0:
            qkv = _qkv_rope(hb, g_qkv, j, c128, s128)
            o = _attn_fwd(qkv, sink_rows[j])
            lay["qkv"], lay["o"] = qkv, o
            z, h, hb = _proj_ln("attn_out_ln", o, g_wo, j, h, row_of(o_g + 3 * i + 1), row_of(o_b + 3 * i + 1), alpha, 1.0)
        else:
            bm = _tile(T, ROW_TILE)
            (xg,) = _matmul("lru_in", (T // bm, 1, 1), (hb, (bm, D), lambda p, q, k: (p, 0)),
                            (g_lin, (None, D, 2 * C), lambda p, q, k, j=j: (j, 0, 0)),
                            [((T, 2 * C), F32, (bm, 2 * C), lambda p, q: (p, 0))], (bm, 2 * C))
            cw = [row_of(o_cw + j * CONV_W + k) for k in range(CONV_W)]
            xc, xcb = _conv_fwd(xg, cw, row_of(o_cb + j))
            pre, a, b = _lru_gate_fwd(xc, xcb, g_rarx, j, row_of(o_ra + j), row_of(o_rx + j), row_of(o_lam + j))
            hs = _scan_fwd(a, b)
            y = _lru_out_fwd(hs, xg)
            lay.update(xg=xg, xc=xc, xcb=xcb, pre=pre, a=a, hs=hs, y=y, cw=cw)
            z, h, hb = _proj_ln("lru_out_ln", y, g_lout, j, h, row_of(o_g + 3 * i + 1), row_of(o_b + 3 * i + 1), alpha, 1.0)
        lay["z2"], lay["hb2"] = z, hb
        (z, h, hb), (lay["gu2"], lay["act2"]) = _ffn_fwd(h, hb, g_w_in2, g_w_out2, i, row_of(o_g + 3 * i + 2), row_of(o_b + 3 * i + 2), alpha)
        lay["z3"] = z
        saved.append(lay)

    dh, loss_local = _loss_and_grad(h, loss_target.reshape(T, D))
    loss = lax.psum(loss_local, ("x", "y", "c"))

    zeros = lambda shape: jnp.zeros(shape, BF16)
    d_w_in1, d_w_out1 = zeros(g_w_in1.shape), zeros(g_w_out1.shape)
    d_w_in2, d_w_out2 = zeros(g_w_in2.shape), zeros(g_w_out2.shape)
    d_qkv, d_wo, d_lin, d_lout, d_rarx = zeros(g_qkv.shape), zeros(g_wo.shape), zeros(g_lin.shape), zeros(g_lout.shape), zeros(g_rarx.shape)
    sg = [None] * SMALL_ROWS
    d_sinks = [None] * LA
    for i in reversed(range(L)):
        j = i // 2
        lay = saved[i]
        dh, sg[o_g + 3 * i + 2], sg[o_b + 3 * i + 2], d_w_in2, d_w_out2 = _ffn_bwd(
            dh, lay["z3"], row_of(o_g + 3 * i + 2), lay["hb2"], lay["gu2"], lay["act2"], g_w_in2, g_w_out2, i, d_w_in2, d_w_out2, alpha)
        if i % 2 == 0:
            dz, dmb, sg[o_g + 3 * i + 1], sg[o_b + 3 * i + 1] = _ln_bwd("attn_ln_bwd", dh, lay["z2"], row_of(o_g + 3 * i + 1), 1.0)
            d_wo = _grad_tn("attn_dwo", lay["o"], dmb, d_wo, j, _tile(Q_COLS, 1024), D)
            do = _back_proj("attn_do", dmb, g_wo, j, BF16)
            dq, dkv, dsk = _attn_bwd(lay["qkv"], do, sink_rows[j])
            d_sinks[j] = jnp.sum(dsk.reshape(N_HEADS, ATTN_BLOCK), axis=1)
            dqkv = _rope_bwd(dq, dkv, c128, s128)
            d_qkv = _grad_tn("attn_dwqkv", lay["hb1"], dqkv, d_qkv, j, D, dqkv.shape[1])
            dh = _input_grad("attn_dx", dqkv, g_qkv, j, dz, alpha)
        else:
            dz, dmb, sg[o_g + 3 * i + 1], sg[o_b + 3 * i + 1] = _ln_bwd("lru_ln_bwd", dh, lay["z2"], row_of(o_g + 3 * i + 1), 1.0)
            d_lout = _grad_tn("lru_dwout", lay["y"], dmb, d_lout, j, C, D)
            dy = _back_proj("lru_dy", dmb, g_lout, j, F32)
            dhs, dgb = _lru_out_bwd(dy, lay["hs"], lay["xg"])
            adj = _scan_bwd(lay["a"], dhs)
            dpre, dxc_direct, sg[o_ra + j], sg[o_rx + j], sg[o_lam + j] = _lru_gate_bwd(
                lay["pre"], lay["xc"], adj, lay["hs"], row_of(o_ra + j), row_of(o_rx + j), row_of(o_lam + j))
            blk = C // RNN_BLOCKS
            d_rarx = _grad_tn("lru_dwgates", lay["xcb"], dpre, d_rarx, j, blk, 2 * blk, a_cb=True)
            bm = _tile(T, ROW_TILE)
            (dxc,) = _matmul("lru_dxc", (T // bm, RNN_BLOCKS, 1), (dpre, (bm, 2 * blk), lambda p, q, k: (p, q)),
                             (g_rarx, (None, None, blk, 2 * blk), lambda p, q, k, j=j: (j, q, 0, 0)),
                             [((T, C), F32, (bm, blk), lambda p, q: (p, q))], (bm, blk), tb=True,
                             extras=[(dxc_direct, (bm, blk), lambda p, q: (p, q))], epilogue=lambda acc, d: (acc + d,))
            res = _conv_bwd(dxc, lay["xg"], dgb, lay["cw"])
            dxg = res[0]
            for k in range(CONV_W):
                sg[o_cw + j * CONV_W + k] = res[1 + k]
            sg[o_cb + j] = res[1 + CONV_W]
            d_lin = _grad_tn("lru_dwin", lay["hb1"], dxg, d_lin, j, D, _tile(2 * C, 1024))
            dh = _input_grad("lru_dx", dxg, g_lin, j, dz, alpha)
        dh, sg[o_g + 3 * i], sg[o_b + 3 * i], d_w_in1, d_w_out1 = _ffn_bwd(
            dh, lay["z1"], row_of(o_g + 3 * i), lay["hb0"], lay["gu1"], lay["act1"], g_w_in1, g_w_out1, i, d_w_in1, d_w_out1, alpha)
    grad_x = dh.reshape(x.shape)

    sink_vec = jnp.concatenate(d_sinks).reshape(1, n_sink)
    sg[o_sink] = jnp.tile(jnp.concatenate([sink_vec, jnp.zeros((1, CW - n_sink), F32)], axis=1), (1, N_CHIPS))
    zero_row = jnp.zeros((1, C), F32)
    d_small = jnp.concatenate([zero_row if r is None else r for r in sg], axis=0).reshape(2, SMALL_ROWS // 2, C)

    grads = [d_w_in1, d_w_out1, d_w_in2, d_w_out2, d_qkv, d_wo, d_lin, d_lout, d_rarx, d_small]
    gsh = sh_list + [_Sharded("rarx", RW, 0, 2 * blk_w), small_sh]
    got = _sibling_swap(grads)
    parts = [_pair_add(g, r, c_arr) for g, r in zip(grads, got)]
    recv = _chip_exchange(parts, gsh)
    sums = [_chip_sum(r, c_arr) for r in recv]
    tot = _sibling_share(sums)
    t_w_in1, t_w_out1, t_w_in2, t_w_out2, t_qkv, t_wo, t_lin, t_lout, t_rarx, t_small = tot
    t_small = t_small.reshape(SMALL_ROWS, CW)

    G = {"ffn1_w_in": t_w_in1, "ffn1_w_out": t_w_out1, "ffn2_w_in": t_w_in2, "ffn2_w_out": t_w_out2,
         "attn_w_qkv": t_qkv, "attn_w_o": t_wo, "lru_w_in": t_lin, "lru_w_out": t_lout,
         "lru_w_ra": t_rarx[..., :blk_w], "lru_w_rx": t_rarx[..., blk_w:]}

    def unpack_small(p):
        return {"ln_g": p[o_g:o_g + 3 * L].reshape(ln_g.shape), "ln_b": p[o_b:o_b + 3 * L].reshape(ln_b.shape),
                "lru_conv_w": p[o_cw:o_cw + LR * CONV_W].reshape(lru_conv_w.shape), "lru_conv_b": p[o_cb:o_cb + LR],
                "lru_b_ra": p[o_ra:o_ra + LR], "lru_b_rx": p[o_rx:o_rx + LR], "lru_lambda": p[o_lam:o_lam + LR],
                "attn_sinks": p[o_sink, :n_sink].reshape(attn_sinks.shape)}

    G.update(unpack_small(t_small))

    delta, new_m, new_v = {}, {}, {}
    small_names = ["ln_g", "ln_b", "lru_conv_w", "lru_conv_b", "lru_b_ra", "lru_b_rx", "lru_lambda", "attn_sinks"]
    for n in names:
        if n not in small_names:
            delta[n], new_m[n], new_v[n] = _adamw(W[n], G[n], M[n], V[n])
    ds, ms, vs = _adamw(pack_small(W), t_small, pack_small(M), pack_small(V))
    for d, p in ((delta, ds), (new_m, ms), (new_v, vs)):
        d.update(unpack_small(p))

    return (loss, grad_x, *[G[n] for n in names], *[delta[n] for n in names], *[new_m[n] for n in names], *[new_v[n] for n in names])
```

```python
import functools
import math

import jax
import jax.numpy as jnp
from jax import lax
from jax.experimental import pallas as pl
from jax.experimental.pallas import tpu as pltpu

F32 = jnp.float32
BF16 = jnp.bfloat16
MESH = pl.DeviceIdType.MESH

N_HEADS = 16
N_KV_HEADS = 4
HEAD_DIM = 64
GROUP = N_HEADS // N_KV_HEADS
ATTN_BLOCK = 128
ROPE_THETA = 10000.0
RNN_BLOCKS = 4
CONV_W = 4
LRU_C = 8.0
LN_EPS = 1e-5
ADAM_LR = 0.001
ADAM_B1 = 0.9
ADAM_B2 = 0.999
ADAM_EPS = 1e-08
ADAM_WD = 0.01
ADAM_STEP = 10
N_CHIPS = 4
NEG_BIG = -1e30
VMEM_LIMIT_MB = 56
ROW_TILE = 512
SMALL_ROWS = 48


def _cparams(sem):
    return pltpu.CompilerParams(dimension_semantics=sem, vmem_limit_bytes=VMEM_LIMIT_MB << 20)


def _tile(n, pref):
    if n <= pref:
        return n
    for t in range(pref - pref % 16, 0, -16):
        if n % t == 0:
            return t
    raise ValueError((n, pref))


def _matmul(name, grid, a, b, outs, acc_shape, *, ta=False, tb=False, extras=(), epilogue=None,
            n_outer=False, alias_in=None):
    gm, gn, gk = grid
    if n_outer:
        g = (gn, gm, gk)
        ijk = lambda p, q, k: (q, p, k)
    else:
        g = (gm, gn, gk)
        ijk = lambda p, q, k: (p, q, k)
    w3 = lambda f: (lambda p, q, k: f(*ijk(p, q, k)))
    w2 = lambda f: (lambda p, q, k: f(*ijk(p, q, k)[:2]))
    in_specs = [pl.BlockSpec(a[1], w3(a[2])), pl.BlockSpec(b[1], w3(b[2]))]
    in_specs += [pl.BlockSpec(e[1], w2(e[2])) for e in extras]
    operands = [a[0], b[0]] + [e[0] for e in extras]
    io_alias = {}
    n_alias = 0
    if alias_in is not None:
        in_specs.append(pl.BlockSpec(memory_space=pl.ANY))
        operands.append(alias_in)
        io_alias = {len(operands) - 1: 0}
        n_alias = 1
    ne, no = len(extras), len(outs)
    dims = (((0 if ta else 1,), (1 if tb else 0,)), ((), ()))

    def body(*refs):
        a_ref, b_ref = refs[0], refs[1]
        e_refs = refs[2:2 + ne]
        o_refs = refs[2 + ne + n_alias:2 + ne + n_alias + no]
        part = lax.dot_general(a_ref[...], b_ref[...], dims, preferred_element_type=F32)

        def finish(acc):
            res = epilogue(acc, *[r[...] for r in e_refs]) if epilogue is not None else (acc,)
            for r, v in zip(o_refs, res):
                r[...] = v.astype(r.dtype)

        if gk == 1:
            finish(part)
        else:
            acc_ref = refs[-1]
            k = pl.program_id(2)

            @pl.when(k == 0)
            def _():
                acc_ref[...] = part

            @pl.when(k > 0)
            def _():
                acc_ref[...] += part

            @pl.when(k == gk - 1)
            def _():
                finish(acc_ref[...])

    res = pl.pallas_call(
        body, name=name, grid=g, in_specs=in_specs,
        out_specs=[pl.BlockSpec(o[2], w2(o[3])) for o in outs],
        out_shape=[jax.ShapeDtypeStruct(o[0], o[1]) for o in outs],
        scratch_shapes=[pltpu.VMEM(acc_shape, F32)] if gk > 1 else [],
        input_output_aliases=io_alias,
        compiler_params=_cparams(("parallel", "parallel", "arbitrary")),
    )(*operands)
    return res


def _rowwise(name, nsteps, fn, ins, outs, accs=(), prefetch=None):
    n_in, n_out, n_acc = len(ins), len(outs), len(accs)
    n_pre = 0 if prefetch is None else 1

    def body(*refs):
        refs = refs[n_pre:]
        i = pl.program_id(0)
        res = fn(i, *[r[...] for r in refs[:n_in]])
        for r, v in zip(refs[n_in:n_in + n_out], res[:n_out]):
            r[...] = v.astype(r.dtype)
        acc_refs = refs[n_in + n_out:n_in + n_out + n_acc]
        if n_acc:
            @pl.when(i == 0)
            def _():
                for r in acc_refs:
                    r[...] = jnp.zeros_like(r)

            for r, v in zip(acc_refs, res[n_out:]):
                r[...] += v

    if prefetch is None:
        zero = lambda shape: (lambda i: (0,) * len(shape))
    else:
        zero = lambda shape: (lambda i, p: (0,) * len(shape))
    in_specs = [pl.BlockSpec(b, m) for _, b, m in ins]
    out_specs = [pl.BlockSpec(o[2], o[3]) for o in outs] + [pl.BlockSpec(s, zero(s)) for s in accs]
    out_shape = [jax.ShapeDtypeStruct(o[0], o[1]) for o in outs] + [jax.ShapeDtypeStruct(s, F32) for s in accs]
    cp = _cparams(("arbitrary",))
    if prefetch is None:
        call = pl.pallas_call(body, name=name, grid=(nsteps,), in_specs=in_specs, out_specs=out_specs,
                              out_shape=out_shape, compiler_params=cp)
        return call(*[x[0] for x in ins])
    gs = pltpu.PrefetchScalarGridSpec(num_scalar_prefetch=1, grid=(nsteps,), in_specs=in_specs, out_specs=out_specs)
    call = pl.pallas_call(body, name=name, grid_spec=gs, out_shape=out_shape, compiler_params=cp)
    return call(prefetch, *[x[0] for x in ins])


def _rows(arr, tr, cols=None, cb=0):
    cols = arr.shape[1] if cols is None else cols
    return (arr, (tr, cols), lambda i: (i, cb))


def _whole(arr):
    return (arr, arr.shape, lambda i: (0,) * arr.ndim)


def _layernorm_fwd(z, g, b):
    mu = jnp.mean(z, axis=-1, keepdims=True)
    xc = z - mu
    var = jnp.mean(xc * xc, axis=-1, keepdims=True)
    return xc * lax.rsqrt(var + LN_EPS) * g + b


def _gelu_tanh(x):
    c = math.sqrt(2.0 / math.pi)
    return x * (0.5 * (1.0 + jnp.tanh(c * (x + 0.044715 * (x * x * x)))))


@jax.custom_jvp
def _expm1(x):
    return jnp.where(jnp.abs(x) < 0.5, jnp.tanh(0.5 * x) * (jnp.exp(x) + 1.0), jnp.exp(x) - 1.0)


@_expm1.defjvp
def _expm1_jvp(primals, tangents):
    (x,), (t,) = primals, tangents
    return _expm1(x), jnp.exp(x) * t


def _log_sigmoid(x):
    return jnp.minimum(x, 0.0) - jnp.log1p(jnp.exp(-jnp.abs(x)))


def _lru_gates(pre, xc, b_ra, b_rx, lam):
    w = xc.shape[-1]
    r = jax.nn.sigmoid(pre[:, :w] + b_ra)
    ig = jax.nn.sigmoid(pre[:, w:] + b_rx)
    log_a = LRU_C * r * _log_sigmoid(lam)
    a = jnp.exp(log_a)
    b = jnp.sqrt(-_expm1(2.0 * log_a)) * (ig * xc)
    return a, b


def _swap_halves(x):
    n = x.shape[1]
    first = (lax.broadcasted_iota(jnp.int32, x.shape, 1) % HEAD_DIM) < (HEAD_DIM // 2)
    return jnp.where(first, pltpu.roll(x, n - HEAD_DIM // 2, 1), pltpu.roll(x, HEAD_DIM // 2, 1))


def _shift_down(prev8, cur, s):
    ext = jnp.concatenate([prev8, cur], axis=0)
    return pltpu.roll(ext, s, 0)[8:]


def _shift_up(cur, next8, s):
    ext = jnp.concatenate([cur, next8], axis=0)
    return pltpu.roll(ext, ext.shape[0] - s, 0)[:cur.shape[0]]


def _ln_epilogue(alpha, scale):
    def epi(acc, hprev, g, b):
        z = alpha * hprev + scale * acc
        h = _layernorm_fwd(z, g, b)
        return z, h, h
    return epi


def _proj_ln(name, act, w, layer, hprev, g, b, alpha, scale):
    T, K = act.shape
    D = w.shape[2]
    bm = _tile(T, ROW_TILE)
    row = lambda i, j: (i, 0)
    return _matmul(
        name, (T // bm, 1, 1),
        (act, (bm, K), lambda i, j, k: (i, 0)), (w, (None, K, D), lambda i, j, k: (layer, 0, 0)),
        [((T, D), F32, (bm, D), row), ((T, D), F32, (bm, D), row), ((T, D), BF16, (bm, D), row)],
        (bm, D),
        extras=[(hprev, (bm, D), row), (g, (1, D), lambda i, j: (0, 0)), (b, (1, D), lambda i, j: (0, 0))],
        epilogue=_ln_epilogue(alpha, scale))


def _ln_bwd(name, dh, z, g, scale):
    T, D = z.shape
    tr = _tile(T, ROW_TILE)

    def fn(i, dh, z, g):
        mu = jnp.mean(z, axis=-1, keepdims=True)
        xc = z - mu
        var = jnp.mean(xc * xc, axis=-1, keepdims=True)
        rstd = lax.rsqrt(var + LN_EPS)
        xhat = xc * rstd
        dxh = dh * g
        dz = rstd * (dxh - jnp.mean(dxh, axis=-1, keepdims=True) - xhat * jnp.mean(dxh * xhat, axis=-1, keepdims=True))
        return (dz, scale * dz, jnp.sum(dh * xhat, axis=0, keepdims=True), jnp.sum(dh, axis=0, keepdims=True))

    row = lambda i: (i, 0)
    return _rowwise(name, T // tr, fn, [_rows(dh, tr), _rows(z, tr), _whole(g)],
                    [((T, D), F32, (tr, D), row), ((T, D), BF16, (tr, D), row)], accs=[(1, D), (1, D)])


def _grad_tn(name, a, b, stack, layer, bm, bn, bk=4 * ROW_TILE, a_cb=None, b_cb=None):
    T, M = a.shape
    N = b.shape[1]
    bk = _tile(T, bk)
    lead = stack.ndim - 2
    if lead == 1:
        oblk, omap = (None, bm, bn), (lambda i, j: (layer, i, j))
    else:
        oblk, omap = (None, None, bm, bn), (lambda i, j: (layer, j, 0, 0))
    amap = (lambda i, j, k: (k, i)) if a_cb is None else (lambda i, j, k: (k, j))
    bmap = (lambda i, j, k: (k, j))
    gm = M // bm if a_cb is None else 1
    (out,) = _matmul(name, (gm, N // bn, T // bk), (a, (bk, bm), amap), (b, (bk, bn), bmap),
                     [(stack.shape, stack.dtype, oblk, omap)], (bm, bn), ta=True, alias_in=stack)
    return out


MXU_COLS = 256


def _col_chunks(width):
    return [(s, min(MXU_COLS, width - s)) for s in range(0, width, MXU_COLS)]


def _ffn_up(hb, w_in, layer):
    T, D = hb.shape
    N2 = w_in.shape[2]
    wd = N2 // 4
    bm = _tile(T, 2 * ROW_TILE)

    def body(a_ref, w_ref, gu_ref, act_ref):
        a = a_ref[...]
        for s, n in _col_chunks(wd):
            gg = jnp.dot(a, w_ref[:, s:s + n], preferred_element_type=F32)
            uu = jnp.dot(a, w_ref[:, wd + s:wd + s + n], preferred_element_type=F32)
            gu_ref[:, s:s + n] = gg.astype(gu_ref.dtype)
            gu_ref[:, wd + s:wd + s + n] = uu.astype(gu_ref.dtype)
            act_ref[:, s:s + n] = (gg * jax.nn.sigmoid(gg) * uu).astype(act_ref.dtype)

    return pl.pallas_call(
        body, name="ffn_up", grid=(2, T // bm),
        in_specs=[pl.BlockSpec((bm, D), lambda j, i: (i, 0)), pl.BlockSpec((None, D, 2 * wd), lambda j, i: (layer, 0, j))],
        out_specs=[pl.BlockSpec((bm, 2 * wd), lambda j, i: (i, j)), pl.BlockSpec((bm, wd), lambda j, i: (i, j))],
        out_shape=[jax.ShapeDtypeStruct((T, N2), BF16), jax.ShapeDtypeStruct((T, N2 // 2), BF16)],
        compiler_params=_cparams(("arbitrary", "arbitrary")))(hb, w_in)


def _ffn_dact(dyb, w_out, layer, gu):
    T, D = dyb.shape
    N2 = gu.shape[1]
    wd = N2 // 4
    bm = _tile(T, 2 * ROW_TILE)

    def body(dy_ref, w_ref, gu_ref, o_ref):
        dy = dy_ref[...]
        for s, n in _col_chunks(wd):
            dact = lax.dot_general(dy, w_ref[s:s + n, :], (((1,), (1,)), ((), ())), preferred_element_type=F32)
            gg = gu_ref[:, s:s + n].astype(F32)
            uu = gu_ref[:, wd + s:wd + s + n].astype(F32)
            sg = jax.nn.sigmoid(gg)
            silu = gg * sg
            o_ref[:, s:s + n] = (dact * uu * (sg + silu * (1.0 - sg))).astype(o_ref.dtype)
            o_ref[:, wd + s:wd + s + n] = (dact * silu).astype(o_ref.dtype)

    return pl.pallas_call(
        body, name="ffn_dact", grid=(2, T // bm),
        in_specs=[pl.BlockSpec((bm, D), lambda j, i: (i, 0)), pl.BlockSpec((None, wd, D), lambda j, i: (layer, j, 0)),
                  pl.BlockSpec((bm, 2 * wd), lambda j, i: (i, j))],
        out_specs=pl.BlockSpec((bm, 2 * wd), lambda j, i: (i, j)),
        out_shape=jax.ShapeDtypeStruct((T, N2), BF16),
        compiler_params=_cparams(("arbitrary", "arbitrary")))(dyb, w_out, gu)


def _ffn_fwd(hprev, hb, w_in, w_out, layer, g, b, alpha):
    gu, act = _ffn_up(hb, w_in, layer)
    z, h, hb2 = _proj_ln("ffn_down_ln", act, w_out, layer, hprev, g, b, alpha, 0.5)
    return (z, h, hb2), (gu, act)


def _ffn_bwd(dh, z, g, hb_in, gu, act, w_in, w_out, layer, dw_in, dw_out, alpha):
    T, D = z.shape
    N2 = w_in.shape[2]
    F = N2 // 2
    bm = _tile(T, ROW_TILE)
    dz, dyb, dg, db = _ln_bwd("ffn_ln_bwd", dh, z, g, 0.5)
    dgu = _ffn_dact(dyb, w_out, layer, gu)
    dw_out = _grad_tn("ffn_dwout", act, dyb, dw_out, layer, F // 2, D)
    dw_in = _grad_tn("ffn_dwin", hb_in, dgu, dw_in, layer, D, N2 // 4)
    (dh_prev,) = _matmul(
        "ffn_dx", (T // bm, 1, 1),
        (dgu, (bm, N2), lambda i, j, k: (i, 0)), (w_in, (None, D, N2), lambda i, j, k: (layer, 0, 0)),
        [((T, D), F32, (bm, D), lambda i, j: (i, 0))], (bm, D), tb=True,
        extras=[(dz, (bm, D), lambda i, j: (i, 0))], epilogue=lambda acc, dzb: (alpha * dzb + acc,))
    return dh_prev, dg, db, dw_in, dw_out


def _input_grad(name, dy, w, layer, dz, alpha):
    T, N = dy.shape
    D = w.shape[1]
    bm = _tile(T, ROW_TILE)
    (out,) = _matmul(
        name, (T // bm, 1, 1),
        (dy, (bm, N), lambda i, j, k: (i, 0)), (w, (None, D, N), lambda i, j, k: (layer, 0, 0)),
        [((T, D), F32, (bm, D), lambda i, j: (i, 0))], (bm, D), tb=True,
        extras=[(dz, (bm, D), lambda i, j: (i, 0))], epilogue=lambda acc, dzb: (alpha * dzb + acc,))
    return out


def _back_proj(name, dy, w, layer, dtype):
    T, D = dy.shape
    K = w.shape[1]
    bm = _tile(T, ROW_TILE)
    (out,) = _matmul(
        name, (T // bm, 1, 1),
        (dy, (bm, D), lambda i, j, k: (i, 0)), (w, (None, K, D), lambda i, j, k: (layer, 0, 0)),
        [((T, K), dtype, (bm, K), lambda i, j: (i, 0))], (bm, K), tb=True)
    return out


def _rope_tables(T):
    pos = jnp.arange(T, dtype=F32)
    inv_freq = ROPE_THETA ** (-jnp.arange(0, HEAD_DIM, 2, dtype=F32) / HEAD_DIM)
    ang = pos[:, None] * inv_freq[None, :]
    cos, sin = jnp.cos(ang), jnp.sin(ang)
    c128 = jnp.tile(cos, (1, 4))
    s128 = jnp.tile(jnp.concatenate([-sin, sin], axis=1), (1, 2))
    return c128, s128


QK_COLS = (N_HEADS + N_KV_HEADS) * HEAD_DIM
Q_COLS = N_HEADS * HEAD_DIM
KV_COLS = N_KV_HEADS * HEAD_DIM
Q_SCALE = HEAD_DIM ** -0.5


def _qkv_rope(hb, w_qkv, layer, c128, s128):
    T, D = hb.shape
    N = w_qkv.shape[2]
    bm = _tile(T, ROW_TILE)

    def epi(acc, c, s):
        x = acc[:, :QK_COLS]
        rep = QK_COLS // 128
        r = x * jnp.tile(c, (1, rep)) + _swap_halves(x) * jnp.tile(s, (1, rep))
        return (jnp.concatenate([r[:, :Q_COLS] * Q_SCALE, r[:, Q_COLS:], acc[:, QK_COLS:]], axis=1),)

    (qkv,) = _matmul(
        "qkv_rope", (T // bm, 1, 1),
        (hb, (bm, D), lambda i, j, k: (i, 0)), (w_qkv, (None, D, N), lambda i, j, k: (layer, 0, 0)),
        [((T, N), BF16, (bm, N), lambda i, j: (i, 0))], (bm, N),
        extras=[(c128, (bm, 128), lambda i, j: (i, 0)), (s128, (bm, 128), lambda i, j: (i, 0))], epilogue=epi)
    return qkv


def _rope_bwd(dq, dkv, c128, s128):
    T = dq.shape[0]
    tr = _tile(T, ROW_TILE)

    def fn(i, dq, dkv, c, s):
        dx = jnp.concatenate([dq * Q_SCALE, dkv[:, :KV_COLS]], axis=1)
        rep = QK_COLS // 128
        d = dx * jnp.tile(c, (1, rep)) + _swap_halves(dx * jnp.tile(s, (1, rep)))
        return (jnp.concatenate([d, dkv[:, KV_COLS:]], axis=1),)

    N = Q_COLS + 2 * KV_COLS
    (out,) = _rowwise("rope_bwd", T // tr, fn, [_rows(dq, tr), _rows(dkv, tr), _rows(c128, tr), _rows(s128, tr)],
                      [((T, N), BF16, (tr, N), lambda i: (i, 0))])
    return out


def _attn_mask(first_block):
    q_pos = lax.broadcasted_iota(jnp.int32, (GROUP * ATTN_BLOCK, 2 * ATTN_BLOCK), 0) & (ATTN_BLOCK - 1)
    col = lax.broadcasted_iota(jnp.int32, (GROUP * ATTN_BLOCK, 2 * ATTN_BLOCK), 1)
    dist = q_pos + ATTN_BLOCK - col
    return (dist >= 0) & (dist < ATTN_BLOCK) & ((col >= ATTN_BLOCK) | jnp.logical_not(first_block))


def _sink_column(sk_ref, kvh):
    rg = lax.broadcasted_iota(jnp.int32, (GROUP * ATTN_BLOCK, 1), 0) // ATTN_BLOCK
    col = jnp.full((GROUP * ATTN_BLOCK, 1), sk_ref[0, kvh * GROUP], F32)
    for gi in range(1, GROUP):
        col = jnp.where(rg == gi, sk_ref[0, kvh * GROUP + gi], col)
    return col


def _stack_heads(x, kvh):
    return jnp.concatenate([x[:, (kvh * GROUP + gi) * HEAD_DIM:(kvh * GROUP + gi + 1) * HEAD_DIM] for gi in range(GROUP)], axis=0)


def _unstack_heads(parts):
    cols = []
    for p in parts:
        cols += [p[gi * ATTN_BLOCK:(gi + 1) * ATTN_BLOCK] for gi in range(GROUP)]
    return jnp.concatenate(cols, axis=1)


def _attn_softmax(q4, kb, mask, sink):
    s = lax.dot_general(q4, kb, (((1,), (1,)), ((), ())), preferred_element_type=F32)
    s = jnp.where(mask, s, NEG_BIG)
    m = jnp.maximum(jnp.max(s, axis=1, keepdims=True), sink)
    p = jnp.exp(s - m)
    e_sink = jnp.exp(sink - m)
    den = jnp.sum(p, axis=1, keepdims=True) + e_sink
    return p / den, e_sink / den


def _attn_fwd(qkv, sinks):
    T = qkv.shape[0]
    nb = T // ATTN_BLOCK
    kcb, vcb = Q_COLS // KV_COLS, Q_COLS // KV_COLS + 1

    def body(q_ref, kc_ref, kp_ref, vc_ref, vp_ref, sk_ref, o_ref):
        i = pl.program_id(0)
        mask = _attn_mask(i == 0)
        q = q_ref[...]
        kband = jnp.concatenate([kp_ref[...], kc_ref[...]], axis=0)
        vband = jnp.concatenate([vp_ref[...], vc_ref[...]], axis=0)
        parts = []
        for kvh in range(N_KV_HEADS):
            hs = slice(kvh * HEAD_DIM, (kvh + 1) * HEAD_DIM)
            pn, _ = _attn_softmax(_stack_heads(q, kvh), kband[:, hs], mask, _sink_column(sk_ref, kvh))
            parts.append(jnp.dot(pn.astype(BF16), vband[:, hs], preferred_element_type=F32))
        o_ref[...] = _unstack_heads(parts).astype(o_ref.dtype)

    prev = lambda i: jnp.maximum(i - 1, 0)
    return pl.pallas_call(
        body, name="attn_fwd", grid=(nb,),
        in_specs=[pl.BlockSpec((ATTN_BLOCK, Q_COLS), lambda i: (i, 0)),
                  pl.BlockSpec((ATTN_BLOCK, KV_COLS), lambda i: (i, kcb)),
                  pl.BlockSpec((ATTN_BLOCK, KV_COLS), lambda i: (prev(i), kcb)),
                  pl.BlockSpec((ATTN_BLOCK, KV_COLS), lambda i: (i, vcb)),
                  pl.BlockSpec((ATTN_BLOCK, KV_COLS), lambda i: (prev(i), vcb)),
                  pl.BlockSpec(memory_space=pltpu.SMEM)],
        out_specs=pl.BlockSpec((ATTN_BLOCK, Q_COLS), lambda i: (i, 0)),
        out_shape=jax.ShapeDtypeStruct((T, Q_COLS), BF16),
        compiler_params=_cparams(("arbitrary",)),
    )(qkv, qkv, qkv, qkv, qkv, sinks)


def _attn_bwd(qkv, do, sinks):
    T = qkv.shape[0]
    nb = T // ATTN_BLOCK
    kcb, vcb = Q_COLS // KV_COLS, Q_COLS // KV_COLS + 1
    B = ATTN_BLOCK

    def body(q_ref, kc_ref, kp_ref, vc_ref, vp_ref, do_ref, sk_ref, dq_ref, dkv_ref, dsk_ref, carry_ref):
        i = pl.program_id(0)

        @pl.when(i == 0)
        def _():
            carry_ref[...] = jnp.zeros_like(carry_ref)
            dsk_ref[...] = jnp.zeros_like(dsk_ref)

        @pl.when(i < nb)
        def _():
            mask = _attn_mask(i == 0)
            q = q_ref[...]
            do_blk = do_ref[...]
            kband = jnp.concatenate([kp_ref[...], kc_ref[...]], axis=0)
            vband = jnp.concatenate([vp_ref[...], vc_ref[...]], axis=0)
            dq_parts, dk_parts, dv_parts = [], [], []
            for kvh in range(N_KV_HEADS):
                hs = slice(kvh * HEAD_DIM, (kvh + 1) * HEAD_DIM)
                q4 = _stack_heads(q, kvh)
                do4 = _stack_heads(do_blk, kvh)
                kb, vb = kband[:, hs], vband[:, hs]
                pn, p_sink = _attn_softmax(q4, kb, mask, _sink_column(sk_ref, kvh))
                dp = lax.dot_general(do4, vb, (((1,), (1,)), ((), ())), preferred_element_type=F32)
                delta = jnp.sum(pn * dp, axis=1, keepdims=True)
                ds = (pn * (dp - delta)).astype(BF16)
                dsk_ref[kvh] += -(p_sink * delta)
                dq_parts.append(jnp.dot(ds, kb, preferred_element_type=F32))
                dk_parts.append(lax.dot_general(ds, q4, (((0,), (0,)), ((), ())), preferred_element_type=F32))
                dv_parts.append(lax.dot_general(pn.astype(BF16), do4, (((0,), (0,)), ((), ())), preferred_element_type=F32))
            dq_ref[...] = _unstack_heads(dq_parts)
            dkv = jnp.concatenate(dk_parts + dv_parts, axis=1)
            dkv_ref[...] = carry_ref[...] + dkv[:B]
            carry_ref[...] = dkv[B:]

        @pl.when(i == nb)
        def _():
            dkv_ref[...] = carry_ref[...]

    cur = lambda i: jnp.minimum(i, nb - 1)
    prev = lambda i: jnp.maximum(cur(i) - 1, 0)
    lag = lambda i: jnp.maximum(i - 1, 0)
    return pl.pallas_call(
        body, name="attn_bwd", grid=(nb + 1,),
        in_specs=[pl.BlockSpec((B, Q_COLS), lambda i: (cur(i), 0)),
                  pl.BlockSpec((B, KV_COLS), lambda i: (cur(i), kcb)),
                  pl.BlockSpec((B, KV_COLS), lambda i: (prev(i), kcb)),
                  pl.BlockSpec((B, KV_COLS), lambda i: (cur(i), vcb)),
                  pl.BlockSpec((B, KV_COLS), lambda i: (prev(i), vcb)),
                  pl.BlockSpec((B, Q_COLS), lambda i: (cur(i), 0)),
                  pl.BlockSpec(memory_space=pltpu.SMEM)],
        out_specs=[pl.BlockSpec((B, Q_COLS), lambda i: (cur(i), 0)),
                   pl.BlockSpec((B, 2 * KV_COLS), lambda i: (lag(i), 0)),
                   pl.BlockSpec((N_KV_HEADS, GROUP * B, 1), lambda i: (0, 0, 0))],
        out_shape=[jax.ShapeDtypeStruct((T, Q_COLS), F32), jax.ShapeDtypeStruct((T, 2 * KV_COLS), F32),
                   jax.ShapeDtypeStruct((N_KV_HEADS, GROUP * B, 1), F32)],
        scratch_shapes=[pltpu.VMEM((B, 2 * KV_COLS), F32)],
        compiler_params=_cparams(("arbitrary",)),
    )(qkv, qkv, qkv, qkv, qkv, do, sinks)


def _halo_prev(arr, tr, cols, cb=0):
    per = tr // 8
    return (arr, (8, cols), lambda i: (jnp.maximum(i * per - 1, 0), cb))


def _halo_next(arr, tr, cols, cb=0):
    per = tr // 8
    last = arr.shape[0] // 8 - 1
    return (arr, (8, cols), lambda i: (jnp.minimum((i + 1) * per, last), cb))


def _conv_fwd(xg, cw, cb):
    T = xg.shape[0]
    C = cb.shape[1]
    tr = _tile(T, ROW_TILE)

    def fn(i, cur, prev8, cb, *cw):
        prev8 = jnp.where(i == 0, 0.0, prev8)
        xc = cb + cw[CONV_W - 1] * cur
        for s in range(1, CONV_W):
            xc = xc + cw[CONV_W - 1 - s] * _shift_down(prev8, cur, s)
        return xc, xc

    row = lambda i: (i, 0)
    return _rowwise("lru_conv", T // tr, fn, [_rows(xg, tr, C), _halo_prev(xg, tr, C), _whole(cb)] + [_whole(w) for w in cw],
                    [((T, C), F32, (tr, C), row), ((T, C), BF16, (tr, C), row)])


def _conv_bwd(dxc, xg, dgb, cw):
    T, C = dxc.shape
    tr = _tile(T, ROW_TILE)
    nt = T // tr

    def fn(i, d_cur, d_next8, x_cur, x_prev8, dgb, *cw):
        d_next8 = jnp.where(i == nt - 1, 0.0, d_next8)
        x_prev8 = jnp.where(i == 0, 0.0, x_prev8)
        dxb = cw[CONV_W - 1] * d_cur
        dcw = [jnp.sum(d_cur * x_cur, axis=0, keepdims=True)]
        for s in range(1, CONV_W):
            dxb = dxb + cw[CONV_W - 1 - s] * _shift_up(d_cur, d_next8, s)
            dcw.append(jnp.sum(d_cur * _shift_down(x_prev8, x_cur, s), axis=0, keepdims=True))
        return (jnp.concatenate([dxb.astype(BF16), dgb], axis=1), dcw[3], dcw[2], dcw[1], dcw[0],
                jnp.sum(d_cur, axis=0, keepdims=True))

    return _rowwise("lru_conv_bwd", nt, fn,
                    [_rows(dxc, tr), _halo_next(dxc, tr, C), _rows(xg, tr, C), _halo_prev(xg, tr, C), _rows(dgb, tr)] + [_whole(w) for w in cw],
                    [((T, 2 * C), BF16, (tr, 2 * C), lambda i: (i, 0))], accs=[(1, C)] * (CONV_W + 1))


def _lru_gate_fwd(xc, xcb, w_rarx, layer, b_ra, b_rx, lam):
    T, C = xc.shape
    W = C // RNN_BLOCKS
    bm = _tile(T, ROW_TILE)

    def epi(acc, xc_blk, bra, brx, lm):
        a, b = _lru_gates(acc, xc_blk, bra, brx, lm)
        return acc, a, b

    blk = lambda i, j: (i, j)
    par = lambda i, j: (0, j)
    return _matmul(
        "lru_gates", (T // bm, RNN_BLOCKS, 1),
        (xcb, (bm, W), lambda i, j, k: (i, j)), (w_rarx, (None, None, W, 2 * W), lambda i, j, k: (layer, j, 0, 0)),
        [((T, 2 * C), F32, (bm, 2 * W), blk), ((T, C), F32, (bm, W), blk), ((T, C), F32, (bm, W), blk)],
        (bm, 2 * W),
        extras=[(xc, (bm, W), blk), (b_ra, (1, W), par), (b_rx, (1, W), par), (lam, (1, W), par)], epilogue=epi)


def _lru_gate_bwd(pre, xc, lam_adj, h, b_ra, b_rx, lam):
    T, C = xc.shape
    W = C // RNN_BLOCKS
    tr = _tile(T, ROW_TILE // 2)

    def fn(i, pre, xc, adj, h_cur, h_prev8, bra, brx, lm):
        h_prev8 = jnp.where(i == 0, 0.0, h_prev8)
        da = adj * _shift_down(h_prev8, h_cur, 1)
        dpre, dxc, dbra, dbrx, dlam = [], [], [], [], []
        for n in range(RNN_BLOCKS):
            cs = slice(n * W, (n + 1) * W)
            _, vjp = jax.vjp(_lru_gates, pre[:, 2 * n * W:2 * (n + 1) * W], xc[:, cs], bra[:, cs], brx[:, cs], lm[:, cs])
            g = vjp((da[:, cs], adj[:, cs]))
            for lst, v in zip((dpre, dxc, dbra, dbrx, dlam), g):
                lst.append(v)
        cat = lambda l: jnp.concatenate(l, axis=1)
        return cat(dpre), cat(dxc), cat(dbra), cat(dbrx), cat(dlam)

    row = lambda i: (i, 0)
    return _rowwise("lru_gates_bwd", T // tr, fn,
                    [_rows(pre, tr), _rows(xc, tr), _rows(lam_adj, tr), _rows(h, tr), _halo_prev(h, tr, C),
                     _whole(b_ra), _whole(b_rx), _whole(lam)],
                    [((T, 2 * C), BF16, (tr, 2 * C), row), ((T, C), F32, (tr, C), row)], accs=[(1, C)] * 3)


def _scan_fwd(a, b):
    T, C = a.shape
    tt = _tile(T, ROW_TILE)

    def body(a_ref, b_ref, o_ref, c_ref):
        @pl.when(pl.program_id(0) == 0)
        def _():
            c_ref[...] = jnp.zeros_like(c_ref)

        row = lax.broadcasted_iota(jnp.int32, (8, C), 0)

        def step(j, carry):
            sl = pl.ds(pl.multiple_of(j * 8, 8), 8)
            A, B = a_ref[sl, :], b_ref[sl, :]
            for d in (1, 2, 4):
                ok = row >= d
                B = jnp.where(ok, A * pltpu.roll(B, d, 0) + B, B)
                A = jnp.where(ok, A * pltpu.roll(A, d, 0), A)
            h = A * carry + B
            o_ref[sl, :] = h
            return jnp.sum(jnp.where(row == 7, h, 0.0), axis=0, keepdims=True)

        c_ref[0:1, :] = lax.fori_loop(0, tt // 8, step, c_ref[0:1, :])

    spec = pl.BlockSpec((tt, C), lambda i: (i, 0))
    return pl.pallas_call(body, name="lru_scan", grid=(T // tt,), in_specs=[spec, spec], out_specs=spec,
                          out_shape=jax.ShapeDtypeStruct((T, C), F32), scratch_shapes=[pltpu.VMEM((8, C), F32)],
                          compiler_params=_cparams(("arbitrary",)))(a, b)


def _scan_bwd(a, dh):
    T, C = a.shape
    tt = _tile(T, ROW_TILE)
    nt = T // tt

    def body(a_ref, d_ref, o_ref, c_ref):
        @pl.when(pl.program_id(0) == 0)
        def _():
            c_ref[...] = jnp.zeros_like(c_ref)

        row = lax.broadcasted_iota(jnp.int32, (8, C), 0)

        def step(jj, carry):
            adj_next, a_next = carry
            j = tt // 8 - 1 - jj
            sl = pl.ds(pl.multiple_of(j * 8, 8), 8)
            a_blk = a_ref[sl, :]
            A = jnp.where(row < 7, pltpu.roll(a_blk, 7, 0), a_next)
            B = d_ref[sl, :]
            for d in (1, 2, 4):
                ok = row < 8 - d
                B = jnp.where(ok, A * pltpu.roll(B, 8 - d, 0) + B, B)
                A = jnp.where(ok, A * pltpu.roll(A, 8 - d, 0), A)
            adj = A * adj_next + B
            o_ref[sl, :] = adj
            first = lambda v: jnp.sum(jnp.where(row == 0, v, 0.0), axis=0, keepdims=True)
            return first(adj), first(a_blk)

        adj0, a0 = lax.fori_loop(0, tt // 8, step, (c_ref[0:1, :], c_ref[1:2, :]))
        c_ref[0:1, :] = adj0
        c_ref[1:2, :] = a0

    spec = pl.BlockSpec((tt, C), lambda i: (nt - 1 - i, 0))
    return pl.pallas_call(body, name="lru_scan_bwd", grid=(nt,), in_specs=[spec, spec], out_specs=spec,
                          out_shape=jax.ShapeDtypeStruct((T, C), F32), scratch_shapes=[pltpu.VMEM((8, C), F32)],
                          compiler_params=_cparams(("arbitrary",)))(a, dh)


def _lru_out_fwd(h, xg):
    T, C = h.shape
    tr = _tile(T, ROW_TILE)
    (y,) = _rowwise("lru_out", T // tr, lambda i, h, gb: (h * _gelu_tanh(gb),), [_rows(h, tr), _rows(xg, tr, C, 1)],
                    [((T, C), BF16, (tr, C), lambda i: (i, 0))])
    return y


def _lru_out_bwd(dy, h, xg):
    T, C = h.shape
    tr = _tile(T, ROW_TILE)

    def fn(i, dy, h, gb):
        _, vjp = jax.vjp(lambda h, gb: h * _gelu_tanh(gb), h, gb)
        return vjp(dy)

    row = lambda i: (i, 0)
    return _rowwise("lru_out_bwd", T // tr, fn, [_rows(dy, tr), _rows(h, tr), _rows(xg, tr, C, 1)],
                    [((T, C), F32, (tr, C), row), ((T, C), BF16, (tr, C), row)])


class _Sharded:
    def __init__(self, kind, size, off=0, width=None):
        self.kind, self.size, self.off, self.width = kind, size, off, width

    def slot(self, cx, cy):
        return (2 * cy + cx) if self.kind == "perm" else (2 * cx + cy)

    def at(self, ref, cx, cy, layers):
        s = self.slot(cx, cy)
        start = s * self.size
        if not isinstance(start, int):
            start = pl.multiple_of(start, 8 if self.kind in ("rows", "rarx") else 128)
        if self.kind in ("cols", "perm"):
            return ref.at[layers, :, pl.ds(start, self.size)]
        if self.kind == "rows":
            return ref.at[layers, pl.ds(start, self.size), :]
        if self.kind == "rarx":
            return ref.at[layers, :, pl.ds(start, self.size), pl.ds(self.off, self.width)]
        raise ValueError(self.kind)


def _mesh_pos():
    return lax.axis_index("x"), lax.axis_index("y"), lax.axis_index("c")


def _peer_chips(x, y):
    return [(1 - x, y), (x, 1 - y), (1 - x, 1 - y)]


def _place(shard, out_shape, out_dtype, sh, pos, prev=None):
    def body(p_ref, x_ref, *rest):
        rest[-1][...] = x_ref[...].astype(rest[-1].dtype)

    if sh.kind in ("cols", "perm"):
        L, R, Ns = shard.shape
        tr = _tile(R, ROW_TILE)
        k = 1 if sh.kind == "perm" else 0
        grid = (L, R // tr)
        ispec = pl.BlockSpec((None, tr, Ns), lambda l, i, p: (l, i, 0))
        ospec = pl.BlockSpec((None, tr, Ns), lambda l, i, p: (l, i, p[k]))
    elif sh.kind == "rows":
        L, Rs, D = shard.shape
        tr = _tile(Rs, ROW_TILE)
        nt = Rs // tr
        grid = (L, nt)
        ispec = pl.BlockSpec((None, tr, D), lambda l, i, p: (l, i, 0))
        ospec = pl.BlockSpec((None, tr, D), lambda l, i, p: (l, p[0] * nt + i, 0))
    else:
        L, nb, Rs, Wd = shard.shape
        cb = sh.off // Wd
        grid = (L, nb)
        ispec = pl.BlockSpec((None, None, Rs, Wd), lambda l, i, p: (l, i, 0, 0))
        ospec = pl.BlockSpec((None, None, Rs, Wd), lambda l, i, p: (l, i, p[0], cb))
    in_specs = [ispec]
    operands = [pos, shard]
    alias = {}
    if prev is not None:
        in_specs.append(pl.BlockSpec(memory_space=pl.ANY))
        operands.append(prev)
        alias = {2: 0}
    gs = pltpu.PrefetchScalarGridSpec(num_scalar_prefetch=1, grid=grid, in_specs=in_specs, out_specs=ospec)
    return pl.pallas_call(body, name="weight_place", grid_spec=gs, out_shape=jax.ShapeDtypeStruct(out_shape, out_dtype),
                          input_output_aliases=alias, compiler_params=_cparams(("arbitrary", "arbitrary")))(*operands)


def _all_gather(placed, shardings):
    n = len(placed)
    K = 3 * n

    def body(*refs):
        outs = refs[n:2 * n]
        send, recv = refs[2 * n:]
        x, y, c = _mesh_pos()
        sib = (x, y, 1 - c)
        peers = _peer_chips(x, y)
        pending = []
        for t, sh in enumerate(shardings):
            lh = placed[t].shape[0] // 2
            half = pl.ds(c * lh, lh)
            region = sh.at(outs[t], x, y, half)
            for j, (px, py) in enumerate(peers):
                cp = pltpu.make_async_remote_copy(region, region, send.at[3 * t + j], recv.at[3 * t + j],
                                                  device_id=(px, py, c), device_id_type=MESH)
                cp.start()
                pending.append(cp.wait_send)
        for t, sh in enumerate(shardings):
            lh = placed[t].shape[0] // 2
            half = pl.ds(c * lh, lh)
            for j, (px, py) in enumerate(peers):
                region = sh.at(outs[t], px, py, half)
                pltpu.make_async_remote_copy(region, region, send.at[3 * t + j], recv.at[3 * t + j],
                                             device_id=(px, py, c), device_id_type=MESH).wait_recv()
                fwd = pltpu.make_async_remote_copy(region, region, send.at[K + 3 * t + j], recv.at[K + 3 * t + j],
                                                   device_id=sib, device_id_type=MESH)
                fwd.start()
                pending.append(fwd.wait_send)
        for t, sh in enumerate(shardings):
            lh = placed[t].shape[0] // 2
            other = pl.ds((1 - c) * lh, lh)
            for j, (px, py) in enumerate(peers):
                region = sh.at(outs[t], px, py, other)
                pltpu.make_async_remote_copy(region, region, send.at[K + 3 * t + j], recv.at[K + 3 * t + j],
                                             device_id=sib, device_id_type=MESH).wait_recv()
        for w in pending:
            w()

    hbm = pl.BlockSpec(memory_space=pl.ANY)
    return pl.pallas_call(
        body, name="weights_all_gather", in_specs=[hbm] * n, out_specs=[hbm] * n,
        out_shape=[jax.ShapeDtypeStruct(p.shape, p.dtype) for p in placed],
        input_output_aliases={t: t for t in range(n)},
        scratch_shapes=[pltpu.SemaphoreType.DMA((2 * K,)), pltpu.SemaphoreType.DMA((2 * K,))],
    )(*placed)


def _sibling_swap(grads):
    n = len(grads)

    def body(*refs):
        ins, outs = refs[:n], refs[n:2 * n]
        send, recv = refs[2 * n:]
        x, y, c = _mesh_pos()
        cps = []
        for t in range(n):
            lh = grads[t].shape[0] // 2
            cp = pltpu.make_async_remote_copy(ins[t].at[pl.ds((1 - c) * lh, lh)], outs[t], send.at[t], recv.at[t],
                                              device_id=(x, y, 1 - c), device_id_type=MESH)
            cp.start()
            cps.append(cp)
        for cp in cps:
            cp.wait()

    hbm = pl.BlockSpec(memory_space=pl.ANY)
    return pl.pallas_call(
        body, name="grad_sibling_swap", in_specs=[hbm] * n, out_specs=[hbm] * n,
        out_shape=[jax.ShapeDtypeStruct((g.shape[0] // 2,) + g.shape[1:], g.dtype) for g in grads],
        scratch_shapes=[pltpu.SemaphoreType.DMA((n,)), pltpu.SemaphoreType.DMA((n,))],
    )(*grads)


def _shard_shape(full, sh):
    s = list(full)
    if sh.kind in ("cols", "perm"):
        s[-1] = sh.size
    else:
        s[-2] = sh.size
    return tuple(s)


def _chip_exchange(parts, shardings):
    n = len(parts)

    def body(*refs):
        ins, outs = refs[:n], refs[n:2 * n]
        send, recv, loc = refs[2 * n:]
        x, y, c = _mesh_pos()
        me = 2 * x + y
        pending = []
        for t, sh in enumerate(shardings):
            src_all = _Sharded("rarx", sh.size, 0, parts[t].shape[-1]) if sh.kind == "rarx" else sh
            mine = pltpu.make_async_copy(src_all.at(ins[t], x, y, slice(None)), outs[t].at[me], loc.at[t])
            mine.start()
            pending.append(mine.wait)
            for j, (px, py) in enumerate(_peer_chips(x, y)):
                cp = pltpu.make_async_remote_copy(src_all.at(ins[t], px, py, slice(None)), outs[t].at[me],
                                                  send.at[3 * t + j], recv.at[3 * t + j],
                                                  device_id=(px, py, c), device_id_type=MESH)
                cp.start()
                pending.append(cp.wait_send)
        for t, sh in enumerate(shardings):
            for j, (px, py) in enumerate(_peer_chips(x, y)):
                dst = outs[t].at[2 * px + py]
                pltpu.make_async_remote_copy(dst, dst, send.at[3 * t + j], recv.at[3 * t + j],
                                             device_id=(px, py, c), device_id_type=MESH).wait_recv()
        for w in pending:
            w()

    hbm = pl.BlockSpec(memory_space=pl.ANY)
    return pl.pallas_call(
        body, name="grad_chip_exchange", in_specs=[hbm] * n, out_specs=[hbm] * n,
        out_shape=[jax.ShapeDtypeStruct((N_CHIPS,) + _shard_shape(p.shape, sh), p.dtype) for p, sh in zip(parts, shardings)],
        scratch_shapes=[pltpu.SemaphoreType.DMA((3 * n,)), pltpu.SemaphoreType.DMA((3 * n,)), pltpu.SemaphoreType.DMA((n,))],
    )(*parts)


def _sibling_share(sums):
    n = len(sums)

    def body(*refs):
        outs = refs[n:2 * n]
        send, recv = refs[2 * n:]
        x, y, c = _mesh_pos()
        cps = []
        for t in range(n):
            lh = sums[t].shape[0] // 2
            mine = outs[t].at[pl.ds(c * lh, lh)]
            cp = pltpu.make_async_remote_copy(mine, mine, send.at[t], recv.at[t], device_id=(x, y, 1 - c), device_id_type=MESH)
            cp.start()
            cps.append(cp)
        for t, cp in enumerate(cps):
            lh = sums[t].shape[0] // 2
            theirs = outs[t].at[pl.ds((1 - c) * lh, lh)]
            cp.wait_send()
            pltpu.make_async_remote_copy(theirs, theirs, send.at[t], recv.at[t], device_id=(x, y, 1 - c),
                                         device_id_type=MESH).wait_recv()

    hbm = pl.BlockSpec(memory_space=pl.ANY)
    return pl.pallas_call(
        body, name="grad_sibling_share", in_specs=[hbm] * n, out_specs=[hbm] * n,
        out_shape=[jax.ShapeDtypeStruct(s.shape, s.dtype) for s in sums],
        input_output_aliases={t: t for t in range(n)},
        scratch_shapes=[pltpu.SemaphoreType.DMA((n,)), pltpu.SemaphoreType.DMA((n,))],
    )(*sums)


def _flat2(shape):
    return (math.prod(shape[:-1]), shape[-1])


def _pair_add(grad, got, c_arr):
    lh = got.shape[0]
    R, C = _flat2(got.shape)
    tr = _tile(R, 256 if C > 2048 else ROW_TILE)
    nt = R // tr
    (out,) = _rowwise(
        "grad_pair_add", nt, lambda i, a, b: (a.astype(F32) + b.astype(F32),),
        [(grad.reshape(_flat2(grad.shape)), (tr, C), lambda i, p: (p[0] * nt + i, 0)), (got.reshape(R, C), (tr, C), lambda i, p: (i, 0))],
        [((R, C), got.dtype, (tr, C), lambda i, p: (i, 0))], prefetch=c_arr)
    return out.reshape(got.shape)


def _chip_sum(recv, c_arr):
    shape = recv.shape[1:]
    R, C = _flat2(shape)
    tr = _tile(R, ROW_TILE)
    nt = R // tr
    r3 = recv.reshape((N_CHIPS, R, C))

    def fn(i, blk):
        v = blk.astype(F32)
        return (((v[0] + v[1]) + v[2]) + v[3],)

    (out,) = _rowwise("grad_chip_sum", nt, fn, [(r3, (N_CHIPS, tr, C), lambda i, p: (0, i, 0))],
                      [((2 * R, C), F32, (tr, C), lambda i, p: (p[0] * nt + i, 0))], prefetch=c_arr)
    return out.reshape((2 * shape[0],) + shape[1:])


def _adamw(w, g, m, v):
    shape = w.shape
    R, C = _flat2(shape)
    tr = _tile(R, 256 if R % 256 == 0 else R)

    def fn(i, w, g, m, v):
        m2 = ADAM_B1 * m + (1.0 - ADAM_B1) * g
        v2 = ADAM_B2 * v + (1.0 - ADAM_B2) * (g * g)
        m_hat = m2 / (1.0 - ADAM_B1 ** ADAM_STEP)
        v_hat = v2 / (1.0 - ADAM_B2 ** ADAM_STEP)
        delta = -ADAM_LR * (m_hat / (jnp.sqrt(v_hat) + ADAM_EPS) + ADAM_WD * w)
        return delta, m2, v2

    row = lambda i: (i, 0)
    f2 = lambda a: a.reshape(R, C)
    outs = _rowwise("adamw", R // tr, fn, [_rows(f2(a), tr) for a in (w, g, m, v)], [((R, C), F32, (tr, C), row)] * 3)
    return [o.reshape(shape) for o in outs]


def _loss_and_grad(h, target):
    T, D = h.shape
    tr = _tile(T, ROW_TILE)

    def fn(i, h, t):
        err = h - t
        return err * (1.0 / D), jnp.sum(err * err, axis=0, keepdims=True)

    dh, sq = _rowwise("loss", T // tr, fn, [_rows(h, tr), _rows(target, tr)], [((T, D), F32, (tr, D), lambda i: (i, 0))], accs=[(1, D)])
    return dh, (0.5 / D) * jnp.sum(sq)


def kernel(x, ffn1_w_in, ffn1_w_out, ffn2_w_in, ffn2_w_out, ln_g, ln_b, attn_w_qkv, attn_sinks, attn_w_o, lru_w_in, lru_conv_w, lru_conv_b, lru_w_ra, lru_b_ra, lru_w_rx, lru_b_rx, lru_lambda, lru_w_out, loss_target, m_ffn1_w_in, m_ffn1_w_out, m_ffn2_w_in, m_ffn2_w_out, m_ln_g, m_ln_b, m_attn_w_qkv, m_attn_sinks, m_attn_w_o, m_lru_w_in, m_lru_conv_w, m_lru_conv_b, m_lru_w_ra, m_lru_b_ra, m_lru_w_rx, m_lru_b_rx, m_lru_lambda, m_lru_w_out, v_ffn1_w_in, v_ffn1_w_out, v_ffn2_w_in, v_ffn2_w_out, v_ln_g, v_ln_b, v_attn_w_qkv, v_attn_sinks, v_attn_w_o, v_lru_w_in, v_lru_conv_w, v_lru_conv_b, v_lru_w_ra, v_lru_b_ra, v_lru_w_rx, v_lru_b_rx, v_lru_lambda, v_lru_w_out):
    names = ["ffn1_w_in", "ffn1_w_out", "ffn2_w_in", "ffn2_w_out", "ln_g", "ln_b", "attn_w_qkv", "attn_sinks", "attn_w_o",
             "lru_w_in", "lru_conv_w", "lru_conv_b", "lru_w_ra", "lru_b_ra", "lru_w_rx", "lru_b_rx", "lru_lambda", "lru_w_out"]
    W = dict(zip(names, [ffn1_w_in, ffn1_w_out, ffn2_w_in, ffn2_w_out, ln_g, ln_b, attn_w_qkv, attn_sinks, attn_w_o,
                         lru_w_in, lru_conv_w, lru_conv_b, lru_w_ra, lru_b_ra, lru_w_rx, lru_b_rx, lru_lambda, lru_w_out]))
    M = dict(zip(names, [m_ffn1_w_in, m_ffn1_w_out, m_ffn2_w_in, m_ffn2_w_out, m_ln_g, m_ln_b, m_attn_w_qkv, m_attn_sinks, m_attn_w_o,
                         m_lru_w_in, m_lru_conv_w, m_lru_conv_b, m_lru_w_ra, m_lru_b_ra, m_lru_w_rx, m_lru_b_rx, m_lru_lambda, m_lru_w_out]))
    V = dict(zip(names, [v_ffn1_w_in, v_ffn1_w_out, v_ffn2_w_in, v_ffn2_w_out, v_ln_g, v_ln_b, v_attn_w_qkv, v_attn_sinks, v_attn_w_o,
                         v_lru_w_in, v_lru_conv_w, v_lru_conv_b, v_lru_w_ra, v_lru_b_ra, v_lru_w_rx, v_lru_b_rx, v_lru_lambda, v_lru_w_out]))

    T, D = x.shape[1], x.shape[2]
    L = ffn1_w_in.shape[0]
    LA, LR = attn_w_qkv.shape[0], lru_w_in.shape[0]
    N2 = ffn1_w_in.shape[2] * N_CHIPS
    F = N2 // 2
    C = lru_lambda.shape[1] * N_CHIPS
    CW = C // N_CHIPS
    alpha = (2.0 * L) ** 0.25
    c_arr = lax.axis_index("c").astype(jnp.int32).reshape(1)

    n_sink = attn_sinks.size
    assert n_sink <= CW

    def pack_small(d):
        sink_row = jnp.zeros((1, CW), F32).at[0, :n_sink].set(d["attn_sinks"].reshape(-1))
        rows = [d["ln_g"].reshape(-1, CW), d["ln_b"].reshape(-1, CW), d["lru_conv_w"].reshape(-1, CW), d["lru_conv_b"],
                d["lru_b_ra"], d["lru_b_rx"], d["lru_lambda"], sink_row]
        p = jnp.concatenate(rows, axis=0)
        return jnp.concatenate([p, jnp.zeros((SMALL_ROWS - p.shape[0], CW), F32)], axis=0)

    o_g, o_b = 0, 3 * L
    o_cw = 6 * L
    o_cb = o_cw + LR * CONV_W
    o_ra, o_rx, o_lam = o_cb + LR, o_cb + 2 * LR, o_cb + 3 * LR
    o_sink = o_cb + 4 * LR
    assert o_sink < SMALL_ROWS

    cols = lambda a: _Sharded("cols", a.shape[-1])
    rows_ = lambda a: _Sharded("rows", a.shape[-2])
    RW = lru_w_ra.shape[2]
    sh_list = [_Sharded("perm", N2 // 4), rows_(ffn1_w_out), _Sharded("perm", N2 // 4), rows_(ffn2_w_out),
               cols(attn_w_qkv), rows_(attn_w_o), cols(lru_w_in), rows_(lru_w_out)]
    big = [ffn1_w_in, ffn1_w_out, ffn2_w_in, ffn2_w_out, attn_w_qkv, attn_w_o, lru_w_in, lru_w_out]
    blk_w = lru_w_ra.shape[3]
    small_sh = _Sharded("cols", CW)
    full = lambda a, sh: tuple(a.shape[:-1]) + (a.shape[-1] * N_CHIPS,) if sh.kind in ("cols", "perm") else \
        tuple(a.shape[:-2]) + (a.shape[-2] * N_CHIPS, a.shape[-1])
    mx, my = lax.axis_index("x"), lax.axis_index("y")
    pos = jnp.stack([2 * mx + my, 2 * my + mx]).astype(jnp.int32)
    placed = [_place(a, full(a, sh), BF16, sh, pos) for a, sh in zip(big, sh_list)]
    rarx_shape = (LR, RNN_BLOCKS, RW * N_CHIPS, 2 * blk_w)
    p_ra = _place(lru_w_ra, rarx_shape, BF16, _Sharded("rarx", RW, 0, blk_w), pos)
    placed.append(_place(lru_w_rx, rarx_shape, BF16, _Sharded("rarx", RW, blk_w, blk_w), pos, prev=p_ra))
    placed.append(_place(pack_small(W).reshape(2, SMALL_ROWS // 2, CW), (2, SMALL_ROWS // 2, C), F32, small_sh, pos))
    g_w_in1, g_w_out1, g_w_in2, g_w_out2, g_qkv, g_wo, g_lin, g_lout, g_rarx, g_small = _all_gather(
        placed, sh_list + [_Sharded("rarx", RW, 0, 2 * blk_w), small_sh])
    small = g_small.reshape(SMALL_ROWS, C)
    row_of = lambda r: small[r:r + 1]
    assert D == C, "packed small parameters assume d_model == d_rnn"

    c128, s128 = _rope_tables(T)
    sink_rows = [attn_sinks[j:j + 1] for j in range(LA)]

    h = x.reshape(T, D)
    hb = h.astype(BF16)
    saved = []
    for i in range(L):
        j = i // 2
        lay = {}
        lay["hb0"] = hb
        (z, h, hb), (lay["gu1"], lay["act1"]) = _ffn_fwd(h, hb, g_w_in1, g_w_out1, i, row_of(o_g + 3 * i), row_of(o_b + 3 * i), alpha)
        lay["z1"], lay["hb1"] = z, hb
        if i % 2 == 0:
            qkv = _qkv_rope(hb, g_qkv, j, c128, s128)
            o = _attn_fwd(qkv, sink_rows[j])
            lay["qkv"], lay["o"] = qkv, o
            z, h, hb = _proj_ln("attn_out_ln", o, g_wo, j, h, row_of(o_g + 3 * i + 1), row_of(o_b + 3 * i + 1), alpha, 1.0)
        else:
            bm = _tile(T, ROW_TILE)
            (xg,) = _matmul("lru_in", (T // bm, 1, 1), (hb, (bm, D), lambda p, q, k: (p, 0)),
                            (g_lin, (None, D, 2 * C), lambda p, q, k, j=j: (j, 0, 0)),
                            [((T, 2 * C), F32, (bm, 2 * C), lambda p, q: (p, 0))], (bm, 2 * C))
            cw = [row_of(o_cw + j * CONV_W + k) for k in range(CONV_W)]
            xc, xcb = _conv_fwd(xg, cw, row_of(o_cb + j))
            pre, a, b = _lru_gate_fwd(xc, xcb, g_rarx, j, row_of(o_ra + j), row_of(o_rx + j), row_of(o_lam + j))
            hs = _scan_fwd(a, b)
            y = _lru_out_fwd(hs, xg)
            lay.update(xg=xg, xc=xc, xcb=xcb, pre=pre, a=a, hs=hs, y=y, cw=cw)
            z, h, hb = _proj_ln("lru_out_ln", y, g_lout, j, h, row_of(o_g + 3 * i + 1), row_of(o_b + 3 * i + 1), alpha, 1.0)
        lay["z2"], lay["hb2"] = z, hb
        (z, h, hb), (lay["gu2"], lay["act2"]) = _ffn_fwd(h, hb, g_w_in2, g_w_out2, i, row_of(o_g + 3 * i + 2), row_of(o_b + 3 * i + 2), alpha)
        lay["z3"] = z
        saved.append(lay)

    dh, loss_local = _loss_and_grad(h, loss_target.reshape(T, D))
    loss = lax.psum(loss_local, ("x", "y", "c"))

    zeros = lambda shape: jnp.zeros(shape, BF16)
    d_w_in1, d_w_out1 = zeros(g_w_in1.shape), zeros(g_w_out1.shape)
    d_w_in2, d_w_out2 = zeros(g_w_in2.shape), zeros(g_w_out2.shape)
    d_qkv, d_wo, d_lin, d_lout, d_rarx = zeros(g_qkv.shape), zeros(g_wo.shape), zeros(g_lin.shape), zeros(g_lout.shape), zeros(g_rarx.shape)
    sg = [None] * SMALL_ROWS
    d_sinks = [None] * LA
    for i in reversed(range(L)):
        j = i // 2
        lay = saved[i]
        dh, sg[o_g + 3 * i + 2], sg[o_b + 3 * i + 2], d_w_in2, d_w_out2 = _ffn_bwd(
            dh, lay["z3"], row_of(o_g + 3 * i + 2), lay["hb2"], lay["gu2"], lay["act2"], g_w_in2, g_w_out2, i, d_w_in2, d_w_out2, alpha)
        if i % 2 == 0:
            dz, dmb, sg[o_g + 3 * i + 1], sg[o_b + 3 * i + 1] = _ln_bwd("attn_ln_bwd", dh, lay["z2"], row_of(o_g + 3 * i + 1), 1.0)
            d_wo = _grad_tn("attn_dwo", lay["o"], dmb, d_wo, j, _tile(Q_COLS, 1024), D)
            do = _back_proj("attn_do", dmb, g_wo, j, BF16)
            dq, dkv, dsk = _attn_bwd(lay["qkv"], do, sink_rows[j])
            d_sinks[j] = jnp.sum(dsk.reshape(N_HEADS, ATTN_BLOCK), axis=1)
            dqkv = _rope_bwd(dq, dkv, c128, s128)
            d_qkv = _grad_tn("attn_dwqkv", lay["hb1"], dqkv, d_qkv, j, D, dqkv.shape[1])
            dh = _input_grad("attn_dx", dqkv, g_qkv, j, dz, alpha)
        else:
            dz, dmb, sg[o_g + 3 * i + 1], sg[o_b + 3 * i + 1] = _ln_bwd("lru_ln_bwd", dh, lay["z2"], row_of(o_g + 3 * i + 1), 1.0)
            d_lout = _grad_tn("lru_dwout", lay["y"], dmb, d_lout, j, C, D)
            dy = _back_proj("lru_dy", dmb, g_lout, j, F32)
            dhs, dgb = _lru_out_bwd(dy, lay["hs"], lay["xg"])
            adj = _scan_bwd(lay["a"], dhs)
            dpre, dxc_direct, sg[o_ra + j], sg[o_rx + j], sg[o_lam + j] = _lru_gate_bwd(
                lay["pre"], lay["xc"], adj, lay["hs"], row_of(o_ra + j), row_of(o_rx + j), row_of(o_lam + j))
            blk = C // RNN_BLOCKS
            d_rarx = _grad_tn("lru_dwgates", lay["xcb"], dpre, d_rarx, j, blk, 2 * blk, a_cb=True)
            bm = _tile(T, ROW_TILE)
            (dxc,) = _matmul("lru_dxc", (T // bm, RNN_BLOCKS, 1), (dpre, (bm, 2 * blk), lambda p, q, k: (p, q)),
                             (g_rarx, (None, None, blk, 2 * blk), lambda p, q, k, j=j: (j, q, 0, 0)),
                             [((T, C), F32, (bm, blk), lambda p, q: (p, q))], (bm, blk), tb=True,
                             extras=[(dxc_direct, (bm, blk), lambda p, q: (p, q))], epilogue=lambda acc, d: (acc + d,))
            res = _conv_bwd(dxc, lay["xg"], dgb, lay["cw"])
            dxg = res[0]
            for k in range(CONV_W):
                sg[o_cw + j * CONV_W + k] = res[1 + k]
            sg[o_cb + j] = res[1 + CONV_W]
            d_lin = _grad_tn("lru_dwin", lay["hb1"], dxg, d_lin, j, D, _tile(2 * C, 1024))
            dh = _input_grad("lru_dx", dxg, g_lin, j, dz, alpha)
        dh, sg[o_g + 3 * i], sg[o_b + 3 * i], d_w_in1, d_w_out1 = _ffn_bwd(
            dh, lay["z1"], row_of(o_g + 3 * i), lay["hb0"], lay["gu1"], lay["act1"], g_w_in1, g_w_out1, i, d_w_in1, d_w_out1, alpha)
    grad_x = dh.reshape(x.shape)

    sink_vec = jnp.concatenate(d_sinks).reshape(1, n_sink)
    sg[o_sink] = jnp.tile(jnp.concatenate([sink_vec, jnp.zeros((1, CW - n_sink), F32)], axis=1), (1, N_CHIPS))
    zero_row = jnp.zeros((1, C), F32)
    d_small = jnp.concatenate([zero_row if r is None else r for r in sg], axis=0).reshape(2, SMALL_ROWS // 2, C)

    grads = [d_w_in1, d_w_out1, d_w_in2, d_w_out2, d_qkv, d_wo, d_lin, d_lout, d_rarx, d_small]
    gsh = sh_list + [_Sharded("rarx", RW, 0, 2 * blk_w), small_sh]
    got = _sibling_swap(grads)
    parts = [_pair_add(g, r, c_arr) for g, r in zip(grads, got)]
    recv = _chip_exchange(parts, gsh)
    sums = [_chip_sum(r, c_arr) for r in recv]
    tot = _sibling_share(sums)
    t_w_in1, t_w_out1, t_w_in2, t_w_out2, t_qkv, t_wo, t_lin, t_lout, t_rarx, t_small = tot
    t_small = t_small.reshape(SMALL_ROWS, CW)

    G = {"ffn1_w_in": t_w_in1, "ffn1_w_out": t_w_out1, "ffn2_w_in": t_w_in2, "ffn2_w_out": t_w_out2,
         "attn_w_qkv": t_qkv, "attn_w_o": t_wo, "lru_w_in": t_lin, "lru_w_out": t_lout,
         "lru_w_ra": t_rarx[..., :blk_w], "lru_w_rx": t_rarx[..., blk_w:]}

    def unpack_small(p):
        return {"ln_g": p[o_g:o_g + 3 * L].reshape(ln_g.shape), "ln_b": p[o_b:o_b + 3 * L].reshape(ln_b.shape),
                "lru_conv_w": p[o_cw:o_cw + LR * CONV_W].reshape(lru_conv_w.shape), "lru_conv_b": p[o_cb:o_cb + LR],
                "lru_b_ra": p[o_ra:o_ra + LR], "lru_b_rx": p[o_rx:o_rx + LR], "lru_lambda": p[o_lam:o_lam + LR],
                "attn_sinks": p[o_sink, :n_sink].reshape(attn_sinks.shape)}

    G.update(unpack_small(t_small))

    delta, new_m, new_v = {}, {}, {}
    small_names = ["ln_g", "ln_b", "lru_conv_w", "lru_conv_b", "lru_b_ra", "lru_b_rx", "lru_lambda", "attn_sinks"]
    for n in names:
        if n not in small_names:
            delta[n], new_m[n], new_v[n] = _adamw(W[n], G[n], M[n], V[n])
    ds, ms, vs = _adamw(pack_small(W), t_small, pack_small(M), pack_small(V))
    for d, p in ((delta, ds), (new_m, ms), (new_v, vs)):
        d.update(unpack_small(p))

    return (loss, grad_x, *[G[n] for n in names], *[delta[n] for n in names], *[new_m[n] for n in names], *[new_v[n] for n in names])
```

```python
import functools
import math

import jax
import jax.numpy as jnp
from jax import lax
from jax.experimental import pallas as pl
from jax.experimental.pallas import tpu as pltpu

F32 = jnp.float32
BF16 = jnp.bfloat16
MESH = pl.DeviceIdType.MESH

N_HEADS = 16
N_KV_HEADS = 4
HEAD_DIM = 64
GROUP = N_HEADS // N_KV_HEADS
ATTN_BLOCK = 128
ROPE_THETA = 10000.0
RNN_BLOCKS = 4
CONV_W = 4
LRU_C = 8.0
LN_EPS = 1e-5
ADAM_LR = 0.001
ADAM_B1 = 0.9
ADAM_B2 = 0.999
ADAM_EPS = 1e-08
ADAM_WD = 0.01
ADAM_STEP = 10
N_CHIPS = 4
NEG_BIG = -1e30
VMEM_LIMIT_MB = 56
ROW_TILE = 512
SMALL_ROWS = 96


def _cparams(sem):
    return pltpu.CompilerParams(dimension_semantics=sem, vmem_limit_bytes=VMEM_LIMIT_MB << 20)


def _tile(n, pref):
    if n <= pref:
        return n
    for t in range(pref - pref % 16, 0, -16):
        if n % t == 0:
            return t
    raise ValueError((n, pref))


def _matmul(name, grid, a, b, outs, acc_shape, *, ta=False, tb=False, extras=(), epilogue=None,
            n_outer=False, alias_in=None):
    gm, gn, gk = grid
    if n_outer:
        g = (gn, gm, gk)
        ijk = lambda p, q, k: (q, p, k)
    else:
        g = (gm, gn, gk)
        ijk = lambda p, q, k: (p, q, k)
    w3 = lambda f: (lambda p, q, k: f(*ijk(p, q, k)))
    w2 = lambda f: (lambda p, q, k: f(*ijk(p, q, k)[:2]))
    in_specs = [pl.BlockSpec(a[1], w3(a[2])), pl.BlockSpec(b[1], w3(b[2]))]
    in_specs += [pl.BlockSpec(e[1], w2(e[2])) for e in extras]
    operands = [a[0], b[0]] + [e[0] for e in extras]
    io_alias = {}
    n_alias = 0
    if alias_in is not None:
        in_specs.append(pl.BlockSpec(memory_space=pl.ANY))
        operands.append(alias_in)
        io_alias = {len(operands) - 1: 0}
        n_alias = 1
    ne, no = len(extras), len(outs)
    dims = (((0 if ta else 1,), (1 if tb else 0,)), ((), ()))

    def body(*refs):
        a_ref, b_ref = refs[0], refs[1]
        e_refs = refs[2:2 + ne]
        o_refs = refs[2 + ne + n_alias:2 + ne + n_alias + no]
        part = lax.dot_general(a_ref[...], b_ref[...], dims, preferred_element_type=F32)

        def finish(acc):
            res = epilogue(acc, *[r[...] for r in e_refs]) if epilogue is not None else (acc,)
            for r, v in zip(o_refs, res):
                r[...] = v.astype(r.dtype)

        if gk == 1:
            finish(part)
        else:
            acc_ref = refs[-1]
            k = pl.program_id(2)

            @pl.when(k == 0)
            def _():
                acc_ref[...] = part

            @pl.when(k > 0)
            def _():
                acc_ref[...] += part

            @pl.when(k == gk - 1)
            def _():
                finish(acc_ref[...])

    res = pl.pallas_call(
        body, name=name, grid=g, in_specs=in_specs,
        out_specs=[pl.BlockSpec(o[2], w2(o[3])) for o in outs],
        out_shape=[jax.ShapeDtypeStruct(o[0], o[1]) for o in outs],
        scratch_shapes=[pltpu.VMEM(acc_shape, F32)] if gk > 1 else [],
        input_output_aliases=io_alias,
        compiler_params=_cparams(("parallel", "parallel", "arbitrary")),
    )(*operands)
    return res


def _rowwise(name, nsteps, fn, ins, outs, accs=(), prefetch=None):
    n_in, n_out, n_acc = len(ins), len(outs), len(accs)
    n_pre = 0 if prefetch is None else 1

    def body(*refs):
        refs = refs[n_pre:]
        i = pl.program_id(0)
        res = fn(i, *[r[...] for r in refs[:n_in]])
        for r, v in zip(refs[n_in:n_in + n_out], res[:n_out]):
            r[...] = v.astype(r.dtype)
        acc_refs = refs[n_in + n_out:n_in + n_out + n_acc]
        if n_acc:
            @pl.when(i == 0)
            def _():
                for r in acc_refs:
                    r[...] = jnp.zeros_like(r)

            for r, v in zip(acc_refs, res[n_out:]):
                r[...] += v

    if prefetch is None:
        zero = lambda shape: (lambda i: (0,) * len(shape))
    else:
        zero = lambda shape: (lambda i, p: (0,) * len(shape))
    in_specs = [pl.BlockSpec(b, m) for _, b, m in ins]
    out_specs = [pl.BlockSpec(o[2], o[3]) for o in outs] + [pl.BlockSpec(s, zero(s)) for s in accs]
    out_shape = [jax.ShapeDtypeStruct(o[0], o[1]) for o in outs] + [jax.ShapeDtypeStruct(s, F32) for s in accs]
    cp = _cparams(("arbitrary",))
    if prefetch is None:
        call = pl.pallas_call(body, name=name, grid=(nsteps,), in_specs=in_specs, out_specs=out_specs,
                              out_shape=out_shape, compiler_params=cp)
        return call(*[x[0] for x in ins])
    gs = pltpu.PrefetchScalarGridSpec(num_scalar_prefetch=1, grid=(nsteps,), in_specs=in_specs, out_specs=out_specs)
    call = pl.pallas_call(body, name=name, grid_spec=gs, out_shape=out_shape, compiler_params=cp)
    return call(prefetch, *[x[0] for x in ins])


def _rows(arr, tr, cols=None, cb=0):
    cols = arr.shape[1] if cols is None else cols
    return (arr, (tr, cols), lambda i: (i, cb))


def _whole(arr):
    return (arr, arr.shape, lambda i: (0,) * arr.ndim)


def _layernorm_fwd(z, g, b):
    mu = jnp.mean(z, axis=-1, keepdims=True)
    xc = z - mu
    var = jnp.mean(xc * xc, axis=-1, keepdims=True)
    return xc * lax.rsqrt(var + LN_EPS) * g + b


def _gelu_tanh(x):
    c = math.sqrt(2.0 / math.pi)
    return x * (0.5 * (1.0 + jnp.tanh(c * (x + 0.044715 * (x * x * x)))))


@jax.custom_jvp
def _expm1(x):
    return jnp.where(jnp.abs(x) < 0.5, jnp.tanh(0.5 * x) * (jnp.exp(x) + 1.0), jnp.exp(x) - 1.0)


@_expm1.defjvp
def _expm1_jvp(primals, tangents):
    (x,), (t,) = primals, tangents
    return _expm1(x), jnp.exp(x) * t


def _log_sigmoid(x):
    return jnp.minimum(x, 0.0) - jnp.log1p(jnp.exp(-jnp.abs(x)))


def _lru_gates(pre, xc, b_ra, b_rx, lam):
    w = xc.shape[-1]
    r = jax.nn.sigmoid(pre[:, :w] + b_ra)
    ig = jax.nn.sigmoid(pre[:, w:] + b_rx)
    log_a = LRU_C * r * _log_sigmoid(lam)
    a = jnp.exp(log_a)
    b = jnp.sqrt(-_expm1(2.0 * log_a)) * (ig * xc)
    return a, b


def _swap_halves(x):
    n = x.shape[1]
    first = (lax.broadcasted_iota(jnp.int32, x.shape, 1) % HEAD_DIM) < (HEAD_DIM // 2)
    return jnp.where(first, pltpu.roll(x, n - HEAD_DIM // 2, 1), pltpu.roll(x, HEAD_DIM // 2, 1))


def _shift_down(prev8, cur, s):
    ext = jnp.concatenate([prev8, cur], axis=0)
    return pltpu.roll(ext, s, 0)[8:]


def _shift_up(cur, next8, s):
    ext = jnp.concatenate([cur, next8], axis=0)
    return pltpu.roll(ext, ext.shape[0] - s, 0)[:cur.shape[0]]


def _ln_epilogue(alpha, scale):
    def epi(acc, hprev, g, b):
        z = alpha * hprev + scale * acc
        h = _layernorm_fwd(z, g, b)
        return z, h, h
    return epi


def _proj_ln(name, act, w, layer, hprev, g, b, alpha, scale):
    T, K = act.shape
    D = w.shape[2]
    bm = _tile(T, ROW_TILE)
    row = lambda i, j: (i, 0)
    return _matmul(
        name, (T // bm, 1, 1),
        (act, (bm, K), lambda i, j, k: (i, 0)), (w, (None, K, D), lambda i, j, k: (layer, 0, 0)),
        [((T, D), F32, (bm, D), row), ((T, D), F32, (bm, D), row), ((T, D), BF16, (bm, D), row)],
        (bm, D),
        extras=[(hprev, (bm, D), row), (g, (1, D), lambda i, j: (0, 0)), (b, (1, D), lambda i, j: (0, 0))],
        epilogue=_ln_epilogue(alpha, scale))


def _ln_bwd(name, dh, z, g, scale):
    T, D = z.shape
    tr = _tile(T, ROW_TILE)

    def fn(i, dh, z, g):
        mu = jnp.mean(z, axis=-1, keepdims=True)
        xc = z - mu
        var = jnp.mean(xc * xc, axis=-1, keepdims=True)
        rstd = lax.rsqrt(var + LN_EPS)
        xhat = xc * rstd
        dxh = dh * g
        dz = rstd * (dxh - jnp.mean(dxh, axis=-1, keepdims=True) - xhat * jnp.mean(dxh * xhat, axis=-1, keepdims=True))
        return (dz, scale * dz, jnp.sum(dh * xhat, axis=0, keepdims=True), jnp.sum(dh, axis=0, keepdims=True))

    row = lambda i: (i, 0)
    return _rowwise(name, T // tr, fn, [_rows(dh, tr), _rows(z, tr), _whole(g)],
                    [((T, D), F32, (tr, D), row), ((T, D), BF16, (tr, D), row)], accs=[(1, D), (1, D)])


def _grad_tn(name, a, b, stack, layer, bm, bn, bk=4 * ROW_TILE, a_cb=None, b_cb=None):
    T, M = a.shape
    N = b.shape[1]
    bk = _tile(T, bk)
    lead = stack.ndim - 2
    if lead == 1:
        oblk, omap = (None, bm, bn), (lambda i, j: (layer, i, j))
    else:
        oblk, omap = (None, None, bm, bn), (lambda i, j: (layer, j, 0, 0))
    amap = (lambda i, j, k: (k, i)) if a_cb is None else (lambda i, j, k: (k, j))
    bmap = (lambda i, j, k: (k, j))
    gm = M // bm if a_cb is None else 1
    (out,) = _matmul(name, (gm, N // bn, T // bk), (a, (bk, bm), amap), (b, (bk, bn), bmap),
                     [(stack.shape, stack.dtype, oblk, omap)], (bm, bn), ta=True, alias_in=stack)
    return out


MXU_COLS = 256


def _col_chunks(width):
    return [(s, min(MXU_COLS, width - s)) for s in range(0, width, MXU_COLS)]


def _ffn_up(hb, w_in, layer):
    T, D = hb.shape
    N2 = w_in.shape[2]
    wd = N2 // 4
    bm = _tile(T, 2 * ROW_TILE)

    def body(a_ref, w_ref, gu_ref, act_ref):
        a = a_ref[...]
        for s, n in _col_chunks(wd):
            gg = jnp.dot(a, w_ref[:, s:s + n], preferred_element_type=F32)
            uu = jnp.dot(a, w_ref[:, wd + s:wd + s + n], preferred_element_type=F32)
            sg = jax.nn.sigmoid(gg)
            silu = gg * sg
            gu_ref[:, s:s + n] = (uu * (sg + silu * (1.0 - sg))).astype(gu_ref.dtype)
            gu_ref[:, wd + s:wd + s + n] = silu.astype(gu_ref.dtype)
            act_ref[:, s:s + n] = (silu * uu).astype(act_ref.dtype)

    return pl.pallas_call(
        body, name="ffn_up", grid=(2, T // bm),
        in_specs=[pl.BlockSpec((bm, D), lambda j, i: (i, 0)), pl.BlockSpec((None, D, 2 * wd), lambda j, i: (layer, 0, j))],
        out_specs=[pl.BlockSpec((bm, 2 * wd), lambda j, i: (i, j)), pl.BlockSpec((bm, wd), lambda j, i: (i, j))],
        out_shape=[jax.ShapeDtypeStruct((T, N2), BF16), jax.ShapeDtypeStruct((T, N2 // 2), BF16)],
        compiler_params=_cparams(("arbitrary", "arbitrary")))(hb, w_in)


def _ffn_dact(dyb, w_out, layer, gu):
    T, D = dyb.shape
    N2 = gu.shape[1]
    wd = N2 // 4
    bm = _tile(T, 2 * ROW_TILE)

    def body(dy_ref, w_ref, gu_ref, o_ref):
        dy = dy_ref[...]
        for s, n in _col_chunks(wd):
            dact = lax.dot_general(dy, w_ref[s:s + n, :], (((1,), (1,)), ((), ())), preferred_element_type=F32)
            o_ref[:, s:s + n] = (dact * gu_ref[:, s:s + n].astype(F32)).astype(o_ref.dtype)
            o_ref[:, wd + s:wd + s + n] = (dact * gu_ref[:, wd + s:wd + s + n].astype(F32)).astype(o_ref.dtype)

    return pl.pallas_call(
        body, name="ffn_dact", grid=(2, T // bm),
        in_specs=[pl.BlockSpec((bm, D), lambda j, i: (i, 0)), pl.BlockSpec((None, wd, D), lambda j, i: (layer, j, 0)),
                  pl.BlockSpec((bm, 2 * wd), lambda j, i: (i, j))],
        out_specs=pl.BlockSpec((bm, 2 * wd), lambda j, i: (i, j)),
        out_shape=jax.ShapeDtypeStruct((T, N2), BF16),
        compiler_params=_cparams(("arbitrary", "arbitrary")))(dyb, w_out, gu)


def _ffn_fwd(hprev, hb, w_in, w_out, layer, g, b, alpha):
    gu, act = _ffn_up(hb, w_in, layer)
    z, h, hb2 = _proj_ln("ffn_down_ln", act, w_out, layer, hprev, g, b, alpha, 0.5)
    return (z, h, hb2), (gu, act)


def _ffn_bwd(dh, z, g, hb_in, gu, act, w_in, w_out, layer, dw_in, dw_out, alpha):
    T, D = z.shape
    N2 = w_in.shape[2]
    F = N2 // 2
    bm = _tile(T, ROW_TILE)
    dz, dyb, dg, db = _ln_bwd("ffn_ln_bwd", dh, z, g, 0.5)
    dgu = _ffn_dact(dyb, w_out, layer, gu)
    dw_out = _grad_tn("ffn_dwout", act, dyb, dw_out, layer, F // 2, D)
    dw_in = _grad_tn("ffn_dwin", hb_in, dgu, dw_in, layer, D, N2 // 4)
    (dh_prev,) = _matmul(
        "ffn_dx", (T // bm, 1, 1),
        (dgu, (bm, N2), lambda i, j, k: (i, 0)), (w_in, (None, D, N2), lambda i, j, k: (layer, 0, 0)),
        [((T, D), F32, (bm, D), lambda i, j: (i, 0))], (bm, D), tb=True,
        extras=[(dz, (bm, D), lambda i, j: (i, 0))], epilogue=lambda acc, dzb: (alpha * dzb + acc,))
    return dh_prev, dg, db, dw_in, dw_out


def _input_grad(name, dy, w, layer, dz, alpha):
    T, N = dy.shape
    D = w.shape[1]
    bm = _tile(T, ROW_TILE)
    (out,) = _matmul(
        name, (T // bm, 1, 1),
        (dy, (bm, N), lambda i, j, k: (i, 0)), (w, (None, D, N), lambda i, j, k: (layer, 0, 0)),
        [((T, D), F32, (bm, D), lambda i, j: (i, 0))], (bm, D), tb=True,
        extras=[(dz, (bm, D), lambda i, j: (i, 0))], epilogue=lambda acc, dzb: (alpha * dzb + acc,))
    return out


def _back_proj(name, dy, w, layer, dtype):
    T, D = dy.shape
    K = w.shape[1]
    bm = _tile(T, ROW_TILE)
    (out,) = _matmul(
        name, (T // bm, 1, 1),
        (dy, (bm, D), lambda i, j, k: (i, 0)), (w, (None, K, D), lambda i, j, k: (layer, 0, 0)),
        [((T, K), dtype, (bm, K), lambda i, j: (i, 0))], (bm, K), tb=True)
    return out


def _rope_tables(T):
    pos = jnp.arange(T, dtype=F32)
    inv_freq = ROPE_THETA ** (-jnp.arange(0, HEAD_DIM, 2, dtype=F32) / HEAD_DIM)
    ang = pos[:, None] * inv_freq[None, :]
    cos, sin = jnp.cos(ang), jnp.sin(ang)
    c128 = jnp.tile(cos, (1, 4))
    s128 = jnp.tile(jnp.concatenate([-sin, sin], axis=1), (1, 2))
    return c128, s128


QK_COLS = (N_HEADS + N_KV_HEADS) * HEAD_DIM
Q_COLS = N_HEADS * HEAD_DIM
KV_COLS = N_KV_HEADS * HEAD_DIM
Q_SCALE = HEAD_DIM ** -0.5


def _qkv_rope(hb, w_qkv, layer, c128, s128):
    T, D = hb.shape
    N = w_qkv.shape[2]
    bm = _tile(T, ROW_TILE)

    def epi(acc, c, s):
        x = acc[:, :QK_COLS]
        rep = QK_COLS // 128
        r = x * jnp.tile(c, (1, rep)) + _swap_halves(x) * jnp.tile(s, (1, rep))
        return (jnp.concatenate([r[:, :Q_COLS] * Q_SCALE, r[:, Q_COLS:], acc[:, QK_COLS:]], axis=1),)

    (qkv,) = _matmul(
        "qkv_rope", (T // bm, 1, 1),
        (hb, (bm, D), lambda i, j, k: (i, 0)), (w_qkv, (None, D, N), lambda i, j, k: (layer, 0, 0)),
        [((T, N), BF16, (bm, N), lambda i, j: (i, 0))], (bm, N),
        extras=[(c128, (bm, 128), lambda i, j: (i, 0)), (s128, (bm, 128), lambda i, j: (i, 0))], epilogue=epi)
    return qkv


def _rope_bwd(dq, dkv, c128, s128):
    T = dq.shape[0]
    tr = _tile(T, ROW_TILE)

    def fn(i, dq, dkv, c, s):
        dx = jnp.concatenate([dq * Q_SCALE, dkv[:, :KV_COLS]], axis=1)
        rep = QK_COLS // 128
        d = dx * jnp.tile(c, (1, rep)) + _swap_halves(dx * jnp.tile(s, (1, rep)))
        return (jnp.concatenate([d, dkv[:, KV_COLS:]], axis=1),)

    N = Q_COLS + 2 * KV_COLS
    (out,) = _rowwise("rope_bwd", T // tr, fn, [_rows(dq, tr), _rows(dkv, tr), _rows(c128, tr), _rows(s128, tr)],
                      [((T, N), BF16, (tr, N), lambda i: (i, 0))])
    return out


def _attn_mask(first_block):
    q_pos = lax.broadcasted_iota(jnp.int32, (GROUP * ATTN_BLOCK, 2 * ATTN_BLOCK), 0) & (ATTN_BLOCK - 1)
    col = lax.broadcasted_iota(jnp.int32, (GROUP * ATTN_BLOCK, 2 * ATTN_BLOCK), 1)
    dist = q_pos + ATTN_BLOCK - col
    return (dist >= 0) & (dist < ATTN_BLOCK) & ((col >= ATTN_BLOCK) | jnp.logical_not(first_block))


def _sink_column(sk_ref, kvh):
    rg = lax.broadcasted_iota(jnp.int32, (GROUP * ATTN_BLOCK, 1), 0) // ATTN_BLOCK
    col = jnp.full((GROUP * ATTN_BLOCK, 1), sk_ref[0, kvh * GROUP], F32)
    for gi in range(1, GROUP):
        col = jnp.where(rg == gi, sk_ref[0, kvh * GROUP + gi], col)
    return col


def _stack_heads(x, kvh):
    return jnp.concatenate([x[:, (kvh * GROUP + gi) * HEAD_DIM:(kvh * GROUP + gi + 1) * HEAD_DIM] for gi in range(GROUP)], axis=0)


def _unstack_heads(parts):
    cols = []
    for p in parts:
        cols += [p[gi * ATTN_BLOCK:(gi + 1) * ATTN_BLOCK] for gi in range(GROUP)]
    return jnp.concatenate(cols, axis=1)


def _attn_softmax(q4, kb, mask, sink):
    s = lax.dot_general(q4, kb, (((1,), (1,)), ((), ())), preferred_element_type=F32)
    s = jnp.where(mask, s, NEG_BIG)
    m = jnp.maximum(jnp.max(s, axis=1, keepdims=True), sink)
    p = jnp.exp(s - m)
    e_sink = jnp.exp(sink - m)
    den = jnp.sum(p, axis=1, keepdims=True) + e_sink
    return p / den, e_sink / den


def _attn_fwd(qkv, sinks):
    T = qkv.shape[0]
    nb = T // ATTN_BLOCK
    kcb, vcb = Q_COLS // KV_COLS, Q_COLS // KV_COLS + 1

    def body(q_ref, kc_ref, kp_ref, vc_ref, vp_ref, sk_ref, o_ref):
        i = pl.program_id(0)
        mask = _attn_mask(i == 0)
        q = q_ref[...]
        kband = jnp.concatenate([kp_ref[...], kc_ref[...]], axis=0)
        vband = jnp.concatenate([vp_ref[...], vc_ref[...]], axis=0)
        parts = []
        for kvh in range(N_KV_HEADS):
            hs = slice(kvh * HEAD_DIM, (kvh + 1) * HEAD_DIM)
            pn, _ = _attn_softmax(_stack_heads(q, kvh), kband[:, hs], mask, _sink_column(sk_ref, kvh))
            parts.append(jnp.dot(pn.astype(BF16), vband[:, hs], preferred_element_type=F32))
        o_ref[...] = _unstack_heads(parts).astype(o_ref.dtype)

    prev = lambda i: jnp.maximum(i - 1, 0)
    return pl.pallas_call(
        body, name="attn_fwd", grid=(nb,),
        in_specs=[pl.BlockSpec((ATTN_BLOCK, Q_COLS), lambda i: (i, 0)),
                  pl.BlockSpec((ATTN_BLOCK, KV_COLS), lambda i: (i, kcb)),
                  pl.BlockSpec((ATTN_BLOCK, KV_COLS), lambda i: (prev(i), kcb)),
                  pl.BlockSpec((ATTN_BLOCK, KV_COLS), lambda i: (i, vcb)),
                  pl.BlockSpec((ATTN_BLOCK, KV_COLS), lambda i: (prev(i), vcb)),
                  pl.BlockSpec(memory_space=pltpu.SMEM)],
        out_specs=pl.BlockSpec((ATTN_BLOCK, Q_COLS), lambda i: (i, 0)),
        out_shape=jax.ShapeDtypeStruct((T, Q_COLS), BF16),
        compiler_params=_cparams(("arbitrary",)),
    )(qkv, qkv, qkv, qkv, qkv, sinks)


def _attn_bwd(qkv, do, sinks):
    T = qkv.shape[0]
    nb = T // ATTN_BLOCK
    kcb, vcb = Q_COLS // KV_COLS, Q_COLS // KV_COLS + 1
    B = ATTN_BLOCK

    def body(q_ref, kc_ref, kp_ref, vc_ref, vp_ref, do_ref, sk_ref, dq_ref, dkv_ref, dsk_ref, carry_ref):
        i = pl.program_id(0)

        @pl.when(i == 0)
        def _():
            carry_ref[...] = jnp.zeros_like(carry_ref)
            dsk_ref[...] = jnp.zeros_like(dsk_ref)

        @pl.when(i < nb)
        def _():
            mask = _attn_mask(i == 0)
            q = q_ref[...]
            do_blk = do_ref[...]
            kband = jnp.concatenate([kp_ref[...], kc_ref[...]], axis=0)
            vband = jnp.concatenate([vp_ref[...], vc_ref[...]], axis=0)
            dq_parts, dk_parts, dv_parts = [], [], []
            for kvh in range(N_KV_HEADS):
                hs = slice(kvh * HEAD_DIM, (kvh + 1) * HEAD_DIM)
                q4 = _stack_heads(q, kvh)
                do4 = _stack_heads(do_blk, kvh)
                kb, vb = kband[:, hs], vband[:, hs]
                pn, p_sink = _attn_softmax(q4, kb, mask, _sink_column(sk_ref, kvh))
                dp = lax.dot_general(do4, vb, (((1,), (1,)), ((), ())), preferred_element_type=F32)
                delta = jnp.sum(pn * dp, axis=1, keepdims=True)
                ds = (pn * (dp - delta)).astype(BF16)
                dsk_ref[kvh] += -(p_sink * delta)
                dq_parts.append(jnp.dot(ds, kb, preferred_element_type=F32))
                dk_parts.append(lax.dot_general(ds, q4, (((0,), (0,)), ((), ())), preferred_element_type=F32))
                dv_parts.append(lax.dot_general(pn.astype(BF16), do4, (((0,), (0,)), ((), ())), preferred_element_type=F32))
            dq_ref[...] = _unstack_heads(dq_parts)
            dkv = jnp.concatenate(dk_parts + dv_parts, axis=1)
            dkv_ref[...] = carry_ref[...] + dkv[:B]
            carry_ref[...] = dkv[B:]

        @pl.when(i == nb)
        def _():
            dkv_ref[...] = carry_ref[...]

    cur = lambda i: jnp.minimum(i, nb - 1)
    prev = lambda i: jnp.maximum(cur(i) - 1, 0)
    lag = lambda i: jnp.maximum(i - 1, 0)
    return pl.pallas_call(
        body, name="attn_bwd", grid=(nb + 1,),
        in_specs=[pl.BlockSpec((B, Q_COLS), lambda i: (cur(i), 0)),
                  pl.BlockSpec((B, KV_COLS), lambda i: (cur(i), kcb)),
                  pl.BlockSpec((B, KV_COLS), lambda i: (prev(i), kcb)),
                  pl.BlockSpec((B, KV_COLS), lambda i: (cur(i), vcb)),
                  pl.BlockSpec((B, KV_COLS), lambda i: (prev(i), vcb)),
                  pl.BlockSpec((B, Q_COLS), lambda i: (cur(i), 0)),
                  pl.BlockSpec(memory_space=pltpu.SMEM)],
        out_specs=[pl.BlockSpec((B, Q_COLS), lambda i: (cur(i), 0)),
                   pl.BlockSpec((B, 2 * KV_COLS), lambda i: (lag(i), 0)),
                   pl.BlockSpec((N_KV_HEADS, GROUP * B, 1), lambda i: (0, 0, 0))],
        out_shape=[jax.ShapeDtypeStruct((T, Q_COLS), F32), jax.ShapeDtypeStruct((T, 2 * KV_COLS), F32),
                   jax.ShapeDtypeStruct((N_KV_HEADS, GROUP * B, 1), F32)],
        scratch_shapes=[pltpu.VMEM((B, 2 * KV_COLS), F32)],
        compiler_params=_cparams(("arbitrary",)),
    )(qkv, qkv, qkv, qkv, qkv, do, sinks)


def _halo_prev(arr, tr, cols, cb=0):
    per = tr // 8
    return (arr, (8, cols), lambda i: (jnp.maximum(i * per - 1, 0), cb))


def _halo_next(arr, tr, cols, cb=0):
    per = tr // 8
    last = arr.shape[0] // 8 - 1
    return (arr, (8, cols), lambda i: (jnp.minimum((i + 1) * per, last), cb))


def _conv_fwd(xg, cw, cb):
    T = xg.shape[0]
    C = cb.shape[1]
    tr = _tile(T, ROW_TILE)

    def fn(i, cur, prev8, cb, *cw):
        prev8 = jnp.where(i == 0, 0.0, prev8)
        xc = cb + cw[CONV_W - 1] * cur
        for s in range(1, CONV_W):
            xc = xc + cw[CONV_W - 1 - s] * _shift_down(prev8, cur, s)
        return xc, xc

    row = lambda i: (i, 0)
    return _rowwise("lru_conv", T // tr, fn, [_rows(xg, tr, C), _halo_prev(xg, tr, C), _whole(cb)] + [_whole(w) for w in cw],
                    [((T, C), F32, (tr, C), row), ((T, C), BF16, (tr, C), row)])


def _conv_bwd(dxc, xg, dgb, cw):
    T, C = dxc.shape
    tr = _tile(T, ROW_TILE)
    nt = T // tr

    def fn(i, d_cur, d_next8, x_cur, x_prev8, dgb, *cw):
        d_next8 = jnp.where(i == nt - 1, 0.0, d_next8)
        x_prev8 = jnp.where(i == 0, 0.0, x_prev8)
        dxb = cw[CONV_W - 1] * d_cur
        dcw = [jnp.sum(d_cur * x_cur, axis=0, keepdims=True)]
        for s in range(1, CONV_W):
            dxb = dxb + cw[CONV_W - 1 - s] * _shift_up(d_cur, d_next8, s)
            dcw.append(jnp.sum(d_cur * _shift_down(x_prev8, x_cur, s), axis=0, keepdims=True))
        return (jnp.concatenate([dxb.astype(BF16), dgb], axis=1), dcw[3], dcw[2], dcw[1], dcw[0],
                jnp.sum(d_cur, axis=0, keepdims=True))

    return _rowwise("lru_conv_bwd", nt, fn,
                    [_rows(dxc, tr), _halo_next(dxc, tr, C), _rows(xg, tr, C), _halo_prev(xg, tr, C), _rows(dgb, tr)] + [_whole(w) for w in cw],
                    [((T, 2 * C), BF16, (tr, 2 * C), lambda i: (i, 0))], accs=[(1, C)] * (CONV_W + 1))


def _lru_gate_fwd(xc, xcb, w_rarx, layer, b_ra, b_rx, lam):
    T, C = xc.shape
    W = C // RNN_BLOCKS
    bm = _tile(T, ROW_TILE)

    def epi(acc, xc_blk, bra, brx, lm):
        a, b = _lru_gates(acc, xc_blk, bra, brx, lm)
        return acc, a, b

    blk = lambda i, j: (i, j)
    par = lambda i, j: (0, j)
    return _matmul(
        "lru_gates", (T // bm, RNN_BLOCKS, 1),
        (xcb, (bm, W), lambda i, j, k: (i, j)), (w_rarx, (None, None, W, 2 * W), lambda i, j, k: (layer, j, 0, 0)),
        [((T, 2 * C), F32, (bm, 2 * W), blk), ((T, C), F32, (bm, W), blk), ((T, C), F32, (bm, W), blk)],
        (bm, 2 * W),
        extras=[(xc, (bm, W), blk), (b_ra, (1, W), par), (b_rx, (1, W), par), (lam, (1, W), par)], epilogue=epi)


def _lru_gate_bwd(pre, xc, lam_adj, h, b_ra, b_rx, lam):
    T, C = xc.shape
    W = C // RNN_BLOCKS
    tr = _tile(T, ROW_TILE // 2)

    def fn(i, pre, xc, adj, h_cur, h_prev8, bra, brx, lm):
        h_prev8 = jnp.where(i == 0, 0.0, h_prev8)
        da = adj * _shift_down(h_prev8, h_cur, 1)
        dpre, dxc, dbra, dbrx, dlam = [], [], [], [], []
        for n in range(RNN_BLOCKS):
            cs = slice(n * W, (n + 1) * W)
            _, vjp = jax.vjp(_lru_gates, pre[:, 2 * n * W:2 * (n + 1) * W], xc[:, cs], bra[:, cs], brx[:, cs], lm[:, cs])
            g = vjp((da[:, cs], adj[:, cs]))
            for lst, v in zip((dpre, dxc, dbra, dbrx, dlam), g):
                lst.append(v)
        cat = lambda l: jnp.concatenate(l, axis=1)
        return cat(dpre), cat(dxc), cat(dbra), cat(dbrx), cat(dlam)

    row = lambda i: (i, 0)
    return _rowwise("lru_gates_bwd", T // tr, fn,
                    [_rows(pre, tr), _rows(xc, tr), _rows(lam_adj, tr), _rows(h, tr), _halo_prev(h, tr, C),
                     _whole(b_ra), _whole(b_rx), _whole(lam)],
                    [((T, 2 * C), BF16, (tr, 2 * C), row), ((T, C), F32, (tr, C), row)], accs=[(1, C)] * 3)


def _scan_fwd(a, b):
    T, C = a.shape
    tt = _tile(T, ROW_TILE)

    def body(a_ref, b_ref, o_ref, c_ref):
        @pl.when(pl.program_id(0) == 0)
        def _():
            c_ref[...] = jnp.zeros_like(c_ref)

        row = lax.broadcasted_iota(jnp.int32, (8, C), 0)

        def step(j, carry):
            sl = pl.ds(pl.multiple_of(j * 8, 8), 8)
            A, B = a_ref[sl, :], b_ref[sl, :]
            for d in (1, 2, 4):
                ok = row >= d
                B = jnp.where(ok, A * pltpu.roll(B, d, 0) + B, B)
                A = jnp.where(ok, A * pltpu.roll(A, d, 0), A)
            h = A * carry + B
            o_ref[sl, :] = h
            return jnp.sum(jnp.where(row == 7, h, 0.0), axis=0, keepdims=True)

        c_ref[0:1, :] = lax.fori_loop(0, tt // 8, step, c_ref[0:1, :])

    spec = pl.BlockSpec((tt, C), lambda i: (i, 0))
    return pl.pallas_call(body, name="lru_scan", grid=(T // tt,), in_specs=[spec, spec], out_specs=spec,
                          out_shape=jax.ShapeDtypeStruct((T, C), F32), scratch_shapes=[pltpu.VMEM((8, C), F32)],
                          compiler_params=_cparams(("arbitrary",)))(a, b)


def _scan_bwd(a, dh):
    T, C = a.shape
    tt = _tile(T, ROW_TILE)
    nt = T // tt

    def body(a_ref, d_ref, o_ref, c_ref):
        @pl.when(pl.program_id(0) == 0)
        def _():
            c_ref[...] = jnp.zeros_like(c_ref)

        row = lax.broadcasted_iota(jnp.int32, (8, C), 0)

        def step(jj, carry):
            adj_next, a_next = carry
            j = tt // 8 - 1 - jj
            sl = pl.ds(pl.multiple_of(j * 8, 8), 8)
            a_blk = a_ref[sl, :]
            A = jnp.where(row < 7, pltpu.roll(a_blk, 7, 0), a_next)
            B = d_ref[sl, :]
            for d in (1, 2, 4):
                ok = row < 8 - d
                B = jnp.where(ok, A * pltpu.roll(B, 8 - d, 0) + B, B)
                A = jnp.where(ok, A * pltpu.roll(A, 8 - d, 0), A)
            adj = A * adj_next + B
            o_ref[sl, :] = adj
            first = lambda v: jnp.sum(jnp.where(row == 0, v, 0.0), axis=0, keepdims=True)
            return first(adj), first(a_blk)

        adj0, a0 = lax.fori_loop(0, tt // 8, step, (c_ref[0:1, :], c_ref[1:2, :]))
        c_ref[0:1, :] = adj0
        c_ref[1:2, :] = a0

    spec = pl.BlockSpec((tt, C), lambda i: (nt - 1 - i, 0))
    return pl.pallas_call(body, name="lru_scan_bwd", grid=(nt,), in_specs=[spec, spec], out_specs=spec,
                          out_shape=jax.ShapeDtypeStruct((T, C), F32), scratch_shapes=[pltpu.VMEM((8, C), F32)],
                          compiler_params=_cparams(("arbitrary",)))(a, dh)


def _lru_out_fwd(h, xg):
    T, C = h.shape
    tr = _tile(T, ROW_TILE)
    (y,) = _rowwise("lru_out", T // tr, lambda i, h, gb: (h * _gelu_tanh(gb),), [_rows(h, tr), _rows(xg, tr, C, 1)],
                    [((T, C), BF16, (tr, C), lambda i: (i, 0))])
    return y


def _lru_out_bwd(dy, h, xg):
    T, C = h.shape
    tr = _tile(T, ROW_TILE)

    def fn(i, dy, h, gb):
        _, vjp = jax.vjp(lambda h, gb: h * _gelu_tanh(gb), h, gb)
        return vjp(dy)

    row = lambda i: (i, 0)
    return _rowwise("lru_out_bwd", T // tr, fn, [_rows(dy, tr), _rows(h, tr), _rows(xg, tr, C, 1)],
                    [((T, C), F32, (tr, C), row), ((T, C), BF16, (tr, C), row)])


class _Sharded:
    def __init__(self, kind, size, off=0, width=None):
        self.kind, self.size, self.off, self.width = kind, size, off, width

    def slot(self, cx, cy):
        return (2 * cy + cx) if self.kind == "perm" else (2 * cx + cy)

    def at(self, ref, cx, cy, layers):
        s = self.slot(cx, cy)
        start = s * self.size
        if not isinstance(start, int):
            start = pl.multiple_of(start, 8 if self.kind in ("rows", "rarx") else 128)
        if self.kind in ("cols", "perm"):
            return ref.at[layers, :, pl.ds(start, self.size)]
        if self.kind == "rows":
            return ref.at[layers, pl.ds(start, self.size), :]
        if self.kind == "rarx":
            return ref.at[layers, :, pl.ds(start, self.size), pl.ds(self.off, self.width)]
        raise ValueError(self.kind)


def _mesh_pos():
    return lax.axis_index("x"), lax.axis_index("y"), lax.axis_index("c")


def _peer_chips(x, y):
    return [(1 - x, y), (x, 1 - y), (1 - x, 1 - y)]


def _place(shard, out_shape, out_dtype, sh, pos, prev=None):
    def body(p_ref, x_ref, *rest):
        rest[-1][...] = x_ref[...].astype(rest[-1].dtype)

    if sh.kind in ("cols", "perm"):
        L, R, Ns = shard.shape
        tr = _tile(R, ROW_TILE)
        k = 1 if sh.kind == "perm" else 0
        grid = (L, R // tr)
        ispec = pl.BlockSpec((None, tr, Ns), lambda l, i, p: (l, i, 0))
        ospec = pl.BlockSpec((None, tr, Ns), lambda l, i, p: (l, i, p[k]))
    elif sh.kind == "rows":
        L, Rs, D = shard.shape
        tr = _tile(Rs, ROW_TILE)
        nt = Rs // tr
        grid = (L, nt)
        ispec = pl.BlockSpec((None, tr, D), lambda l, i, p: (l, i, 0))
        ospec = pl.BlockSpec((None, tr, D), lambda l, i, p: (l, p[0] * nt + i, 0))
    else:
        L, nb, Rs, Wd = shard.shape
        cb = sh.off // Wd
        grid = (L, nb)
        ispec = pl.BlockSpec((None, None, Rs, Wd), lambda l, i, p: (l, i, 0, 0))
        ospec = pl.BlockSpec((None, None, Rs, Wd), lambda l, i, p: (l, i, p[0], cb))
    in_specs = [ispec]
    operands = [pos, shard]
    alias = {}
    if prev is not None:
        in_specs.append(pl.BlockSpec(memory_space=pl.ANY))
        operands.append(prev)
        alias = {2: 0}
    gs = pltpu.PrefetchScalarGridSpec(num_scalar_prefetch=1, grid=grid, in_specs=in_specs, out_specs=ospec)
    return pl.pallas_call(body, name="weight_place", grid_spec=gs, out_shape=jax.ShapeDtypeStruct(out_shape, out_dtype),
                          input_output_aliases=alias, compiler_params=_cparams(("arbitrary", "arbitrary")))(*operands)


def _all_gather(placed, shardings):
    n = len(placed)
    K = 3 * n

    def body(*refs):
        outs = refs[n:2 * n]
        send, recv = refs[2 * n:]
        x, y, c = _mesh_pos()
        sib = (x, y, 1 - c)
        peers = _peer_chips(x, y)
        pending = []
        for t, sh in enumerate(shardings):
            lh = placed[t].shape[0] // 2
            half = pl.ds(c * lh, lh)
            region = sh.at(outs[t], x, y, half)
            for j, (px, py) in enumerate(peers):
                cp = pltpu.make_async_remote_copy(region, region, send.at[3 * t + j], recv.at[3 * t + j],
                                                  device_id=(px, py, c), device_id_type=MESH)
                cp.start()
                pending.append(cp.wait_send)
        for t, sh in enumerate(shardings):
            lh = placed[t].shape[0] // 2
            half = pl.ds(c * lh, lh)
            for j, (px, py) in enumerate(peers):
                region = sh.at(outs[t], px, py, half)
                pltpu.make_async_remote_copy(region, region, send.at[3 * t + j], recv.at[3 * t + j],
                                             device_id=(px, py, c), device_id_type=MESH).wait_recv()
                fwd = pltpu.make_async_remote_copy(region, region, send.at[K + 3 * t + j], recv.at[K + 3 * t + j],
                                                   device_id=sib, device_id_type=MESH)
                fwd.start()
                pending.append(fwd.wait_send)
        for t, sh in enumerate(shardings):
            lh = placed[t].shape[0] // 2
            other = pl.ds((1 - c) * lh, lh)
            for j, (px, py) in enumerate(peers):
                region = sh.at(outs[t], px, py, other)
                pltpu.make_async_remote_copy(region, region, send.at[K + 3 * t + j], recv.at[K + 3 * t + j],
                                             device_id=sib, device_id_type=MESH).wait_recv()
        for w in pending:
            w()

    hbm = pl.BlockSpec(memory_space=pl.ANY)
    return pl.pallas_call(
        body, name="weights_all_gather", in_specs=[hbm] * n, out_specs=[hbm] * n,
        out_shape=[jax.ShapeDtypeStruct(p.shape, p.dtype) for p in placed],
        input_output_aliases={t: t for t in range(n)},
        scratch_shapes=[pltpu.SemaphoreType.DMA((2 * K,)), pltpu.SemaphoreType.DMA((2 * K,))],
    )(*placed)


def _sibling_swap(grads):
    n = len(grads)

    def body(*refs):
        ins, outs = refs[:n], refs[n:2 * n]
        send, recv = refs[2 * n:]
        x, y, c = _mesh_pos()
        cps = []
        for t in range(n):
            lh = grads[t].shape[0] // 2
            cp = pltpu.make_async_remote_copy(ins[t].at[pl.ds((1 - c) * lh, lh)], outs[t], send.at[t], recv.at[t],
                                              device_id=(x, y, 1 - c), device_id_type=MESH)
            cp.start()
            cps.append(cp)
        for cp in cps:
            cp.wait()

    hbm = pl.BlockSpec(memory_space=pl.ANY)
    return pl.pallas_call(
        body, name="grad_sibling_swap", in_specs=[hbm] * n, out_specs=[hbm] * n,
        out_shape=[jax.ShapeDtypeStruct((g.shape[0] // 2,) + g.shape[1:], g.dtype) for g in grads],
        scratch_shapes=[pltpu.SemaphoreType.DMA((n,)), pltpu.SemaphoreType.DMA((n,))],
    )(*grads)


def _shard_shape(full, sh):
    s = list(full)
    if sh.kind in ("cols", "perm"):
        s[-1] = sh.size
    else:
        s[-2] = sh.size
    return tuple(s)


def _chip_exchange(parts, shardings):
    n = len(parts)

    def body(*refs):
        ins, outs = refs[:n], refs[n:2 * n]
        send, recv, loc = refs[2 * n:]
        x, y, c = _mesh_pos()
        me = 2 * x + y
        pending = []
        for t, sh in enumerate(shardings):
            src_all = _Sharded("rarx", sh.size, 0, parts[t].shape[-1]) if sh.kind == "rarx" else sh
            mine = pltpu.make_async_copy(src_all.at(ins[t], x, y, slice(None)), outs[t].at[me], loc.at[t])
            mine.start()
            pending.append(mine.wait)
            for j, (px, py) in enumerate(_peer_chips(x, y)):
                cp = pltpu.make_async_remote_copy(src_all.at(ins[t], px, py, slice(None)), outs[t].at[me],
                                                  send.at[3 * t + j], recv.at[3 * t + j],
                                                  device_id=(px, py, c), device_id_type=MESH)
                cp.start()
                pending.append(cp.wait_send)
        for t, sh in enumerate(shardings):
            for j, (px, py) in enumerate(_peer_chips(x, y)):
                dst = outs[t].at[2 * px + py]
                pltpu.make_async_remote_copy(dst, dst, send.at[3 * t + j], recv.at[3 * t + j],
                                             device_id=(px, py, c), device_id_type=MESH).wait_recv()
        for w in pending:
            w()

    hbm = pl.BlockSpec(memory_space=pl.ANY)
    return pl.pallas_call(
        body, name="grad_chip_exchange", in_specs=[hbm] * n, out_specs=[hbm] * n,
        out_shape=[jax.ShapeDtypeStruct((N_CHIPS,) + _shard_shape(p.shape, sh), p.dtype) for p, sh in zip(parts, shardings)],
        scratch_shapes=[pltpu.SemaphoreType.DMA((3 * n,)), pltpu.SemaphoreType.DMA((3 * n,)), pltpu.SemaphoreType.DMA((n,))],
    )(*parts)


def _sibling_share(sums):
    n = len(sums)

    def body(*refs):
        outs = refs[n:2 * n]
        send, recv = refs[2 * n:]
        x, y, c = _mesh_pos()
        cps = []
        for t in range(n):
            lh = sums[t].shape[0] // 2
            mine = outs[t].at[pl.ds(c * lh, lh)]
            cp = pltpu.make_async_remote_copy(mine, mine, send.at[t], recv.at[t], device_id=(x, y, 1 - c), device_id_type=MESH)
            cp.start()
            cps.append(cp)
        for t, cp in enumerate(cps):
            lh = sums[t].shape[0] // 2
            theirs = outs[t].at[pl.ds((1 - c) * lh, lh)]
            cp.wait_send()
            pltpu.make_async_remote_copy(theirs, theirs, send.at[t], recv.at[t], device_id=(x, y, 1 - c),
                                         device_id_type=MESH).wait_recv()

    hbm = pl.BlockSpec(memory_space=pl.ANY)
    return pl.pallas_call(
        body, name="grad_sibling_share", in_specs=[hbm] * n, out_specs=[hbm] * n,
        out_shape=[jax.ShapeDtypeStruct(s.shape, s.dtype) for s in sums],
        input_output_aliases={t: t for t in range(n)},
        scratch_shapes=[pltpu.SemaphoreType.DMA((n,)), pltpu.SemaphoreType.DMA((n,))],
    )(*sums)


def _flat2(shape):
    return (math.prod(shape[:-1]), shape[-1])


def _pair_add(grad, got, c_arr):
    lh = got.shape[0]
    R, C = _flat2(got.shape)
    tr = _tile(R, 256 if C > 2048 else ROW_TILE)
    nt = R // tr
    (out,) = _rowwise(
        "grad_pair_add", nt, lambda i, a, b: (a.astype(F32) + b.astype(F32),),
        [(grad.reshape(_flat2(grad.shape)), (tr, C), lambda i, p: (p[0] * nt + i, 0)), (got.reshape(R, C), (tr, C), lambda i, p: (i, 0))],
        [((R, C), got.dtype, (tr, C), lambda i, p: (i, 0))], prefetch=c_arr)
    return out.reshape(got.shape)


def _chip_sum(recv, c_arr):
    shape = recv.shape[1:]
    R, C = _flat2(shape)
    tr = _tile(R, ROW_TILE)
    nt = R // tr
    r3 = recv.reshape((N_CHIPS, R, C))

    def fn(i, blk):
        v = blk.astype(F32)
        return (((v[0] + v[1]) + v[2]) + v[3],)

    (out,) = _rowwise("grad_chip_sum", nt, fn, [(r3, (N_CHIPS, tr, C), lambda i, p: (0, i, 0))],
                      [((2 * R, C), F32, (tr, C), lambda i, p: (p[0] * nt + i, 0))], prefetch=c_arr)
    return out.reshape((2 * shape[0],) + shape[1:])


def _adamw(w, g, m, v):
    shape = w.shape
    R, C = _flat2(shape)
    tr = _tile(R, 256 if R % 256 == 0 else R)

    def fn(i, w, g, m, v):
        m2 = ADAM_B1 * m + (1.0 - ADAM_B1) * g
        v2 = ADAM_B2 * v + (1.0 - ADAM_B2) * (g * g)
        m_hat = m2 / (1.0 - ADAM_B1 ** ADAM_STEP)
        v_hat = v2 / (1.0 - ADAM_B2 ** ADAM_STEP)
        delta = -ADAM_LR * (m_hat / (jnp.sqrt(v_hat) + ADAM_EPS) + ADAM_WD * w)
        return delta, m2, v2

    row = lambda i: (i, 0)
    f2 = lambda a: a.reshape(R, C)
    outs = _rowwise("adamw", R // tr, fn, [_rows(f2(a), tr) for a in (w, g, m, v)], [((R, C), F32, (tr, C), row)] * 3)
    return [o.reshape(shape) for o in outs]


def _loss_and_grad(h, target):
    T, D = h.shape
    tr = _tile(T, ROW_TILE)

    def fn(i, h, t):
        err = h - t
        return err * (1.0 / D), jnp.sum(err * err, axis=0, keepdims=True)

    dh, sq = _rowwise("loss", T // tr, fn, [_rows(h, tr), _rows(target, tr)], [((T, D), F32, (tr, D), lambda i: (i, 0))], accs=[(1, D)])
    return dh, (0.5 / D) * jnp.sum(sq)


def kernel(x, ffn1_w_in, ffn1_w_out, ffn2_w_in, ffn2_w_out, ln_g, ln_b, attn_w_qkv, attn_sinks, attn_w_o, lru_w_in, lru_conv_w, lru_conv_b, lru_w_ra, lru_b_ra, lru_w_rx, lru_b_rx, lru_lambda, lru_w_out, loss_target, m_ffn1_w_in, m_ffn1_w_out, m_ffn2_w_in, m_ffn2_w_out, m_ln_g, m_ln_b, m_attn_w_qkv, m_attn_sinks, m_attn_w_o, m_lru_w_in, m_lru_conv_w, m_lru_conv_b, m_lru_w_ra, m_lru_b_ra, m_lru_w_rx, m_lru_b_rx, m_lru_lambda, m_lru_w_out, v_ffn1_w_in, v_ffn1_w_out, v_ffn2_w_in, v_ffn2_w_out, v_ln_g, v_ln_b, v_attn_w_qkv, v_attn_sinks, v_attn_w_o, v_lru_w_in, v_lru_conv_w, v_lru_conv_b, v_lru_w_ra, v_lru_b_ra, v_lru_w_rx, v_lru_b_rx, v_lru_lambda, v_lru_w_out):
    names = ["ffn1_w_in", "ffn1_w_out", "ffn2_w_in", "ffn2_w_out", "ln_g", "ln_b", "attn_w_qkv", "attn_sinks", "attn_w_o",
             "lru_w_in", "lru_conv_w", "lru_conv_b", "lru_w_ra", "lru_b_ra", "lru_w_rx", "lru_b_rx", "lru_lambda", "lru_w_out"]
    W = dict(zip(names, [ffn1_w_in, ffn1_w_out, ffn2_w_in, ffn2_w_out, ln_g, ln_b, attn_w_qkv, attn_sinks, attn_w_o,
                         lru_w_in, lru_conv_w, lru_conv_b, lru_w_ra, lru_b_ra, lru_w_rx, lru_b_rx, lru_lambda, lru_w_out]))
    M = dict(zip(names, [m_ffn1_w_in, m_ffn1_w_out, m_ffn2_w_in, m_ffn2_w_out, m_ln_g, m_ln_b, m_attn_w_qkv, m_attn_sinks, m_attn_w_o,
                         m_lru_w_in, m_lru_conv_w, m_lru_conv_b, m_lru_w_ra, m_lru_b_ra, m_lru_w_rx, m_lru_b_rx, m_lru_lambda, m_lru_w_out]))
    V = dict(zip(names, [v_ffn1_w_in, v_ffn1_w_out, v_ffn2_w_in, v_ffn2_w_out, v_ln_g, v_ln_b, v_attn_w_qkv, v_attn_sinks, v_attn_w_o,
                         v_lru_w_in, v_lru_conv_w, v_lru_conv_b, v_lru_w_ra, v_lru_b_ra, v_lru_w_rx, v_lru_b_rx, v_lru_lambda, v_lru_w_out]))

    T, D = x.shape[1], x.shape[2]
    L = ffn1_w_in.shape[0]
    LA, LR = attn_w_qkv.shape[0], lru_w_in.shape[0]
    N2 = ffn1_w_in.shape[2] * N_CHIPS
    F = N2 // 2
    C = lru_lambda.shape[1] * N_CHIPS
    CW = C // N_CHIPS
    alpha = (2.0 * L) ** 0.25
    c_arr = lax.axis_index("c").astype(jnp.int32).reshape(1)

    n_sink = attn_sinks.size
    assert n_sink <= CW

    up8 = lambda n: -(-n // 8) * 8
    o_g = 0
    o_b = o_g + up8(3 * L)
    o_cw = o_b + up8(3 * L)
    o_cb = o_cw + up8(LR * CONV_W)
    o_ra = o_cb + up8(LR)
    o_rx = o_ra + up8(LR)
    o_lam = o_rx + up8(LR)
    o_sink = o_lam + up8(LR)
    assert o_sink + 8 <= SMALL_ROWS

    def pack_small(d):
        parts = [d["ln_g"].reshape(-1, CW), d["ln_b"].reshape(-1, CW), d["lru_conv_w"].reshape(-1, CW), d["lru_conv_b"],
                 d["lru_b_ra"], d["lru_b_rx"], d["lru_lambda"],
                 jnp.pad(d["attn_sinks"].reshape(1, -1), ((0, 0), (0, CW - n_sink)))]
        parts = [jnp.pad(p, ((0, up8(p.shape[0]) - p.shape[0]), (0, 0))) for p in parts]
        used = sum(p.shape[0] for p in parts)
        return jnp.concatenate(parts + [jnp.zeros((SMALL_ROWS - used, CW), F32)], axis=0)

    cols = lambda a: _Sharded("cols", a.shape[-1])
    rows_ = lambda a: _Sharded("rows", a.shape[-2])
    RW = lru_w_ra.shape[2]
    sh_list = [_Sharded("perm", N2 // 4), rows_(ffn1_w_out), _Sharded("perm", N2 // 4), rows_(ffn2_w_out),
               cols(attn_w_qkv), rows_(attn_w_o), cols(lru_w_in), rows_(lru_w_out)]
    big = [ffn1_w_in, ffn1_w_out, ffn2_w_in, ffn2_w_out, attn_w_qkv, attn_w_o, lru_w_in, lru_w_out]
    blk_w = lru_w_ra.shape[3]
    small_sh = _Sharded("cols", CW)
    full = lambda a, sh: tuple(a.shape[:-1]) + (a.shape[-1] * N_CHIPS,) if sh.kind in ("cols", "perm") else \
        tuple(a.shape[:-2]) + (a.shape[-2] * N_CHIPS, a.shape[-1])
    mx, my = lax.axis_index("x"), lax.axis_index("y")
    pos = jnp.stack([2 * mx + my, 2 * my + mx]).astype(jnp.int32)
    placed = [_place(a, full(a, sh), BF16, sh, pos) for a, sh in zip(big, sh_list)]
    rarx_shape = (LR, RNN_BLOCKS, RW * N_CHIPS, 2 * blk_w)
    p_ra = _place(lru_w_ra, rarx_shape, BF16, _Sharded("rarx", RW, 0, blk_w), pos)
    placed.append(_place(lru_w_rx, rarx_shape, BF16, _Sharded("rarx", RW, blk_w, blk_w), pos, prev=p_ra))
    placed.append(_place(pack_small(W).reshape(2, SMALL_ROWS // 2, CW), (2, SMALL_ROWS // 2, C), F32, small_sh, pos))
    g_w_in1, g_w_out1, g_w_in2, g_w_out2, g_qkv, g_wo, g_lin, g_lout, g_rarx, g_small = _all_gather(
        placed, sh_list + [_Sharded("rarx", RW, 0, 2 * blk_w), small_sh])
    small = g_small.reshape(SMALL_ROWS, C)
    row_of = lambda r: small[r:r + 1]
    assert D == C, "packed small parameters assume d_model == d_rnn"

    c128, s128 = _rope_tables(T)
    sink_rows = [attn_sinks[j:j + 1] for j in range(LA)]

    h = x.reshape(T, D)
    hb = h.astype(BF16)
    saved = []
    for i in range(L):
        j = i // 2
        lay = {}
        lay["hb0"] = hb
        (z, h, hb), (lay["gu1"], lay["act1"]) = _ffn_fwd(h, hb, g_w_in1, g_w_out1, i, row_of(o_g + 3 * i), row_of(o_b + 3 * i), alpha)
        lay["z1"], lay["hb1"] = z, hb
        if i % 2 == 0:
            qkv = _qkv_rope(hb, g_qkv, j, c128, s128)
            o = _attn_fwd(qkv, sink_rows[j])
            lay["qkv"], lay["o"] = qkv, o
            z, h, hb = _proj_ln("attn_out_ln", o, g_wo, j, h, row_of(o_g + 3 * i + 1), row_of(o_b + 3 * i + 1), alpha, 1.0)
        else:
            bm = _tile(T, ROW_TILE)
            (xg,) = _matmul("lru_in", (T // bm, 1, 1), (hb, (bm, D), lambda p, q, k: (p, 0)),
                            (g_lin, (None, D, 2 * C), lambda p, q, k, j=j: (j, 0, 0)),
                            [((T, 2 * C), F32, (bm, 2 * C), lambda p, q: (p, 0))], (bm, 2 * C))
            cw = [row_of(o_cw + j * CONV_W + k) for k in range(CONV_W)]
            xc, xcb = _conv_fwd(xg, cw, row_of(o_cb + j))
            pre, a, b = _lru_gate_fwd(xc, xcb, g_rarx, j, row_of(o_ra + j), row_of(o_rx + j), row_of(o_lam + j))
            hs = _scan_fwd(a, b)
            y = _lru_out_fwd(hs, xg)
            lay.update(xg=xg, xc=xc, xcb=xcb, pre=pre, a=a, hs=hs, y=y, cw=cw)
            z, h, hb = _proj_ln("lru_out_ln", y, g_lout, j, h, row_of(o_g + 3 * i + 1), row_of(o_b + 3 * i + 1), alpha, 1.0)
        lay["z2"], lay["hb2"] = z, hb
        (z, h, hb), (lay["gu2"], lay["act2"]) = _ffn_fwd(h, hb, g_w_in2, g_w_out2, i, row_of(o_g + 3 * i + 2), row_of(o_b + 3 * i + 2), alpha)
        lay["z3"] = z
        saved.append(lay)

    dh, loss_local = _loss_and_grad(h, loss_target.reshape(T, D))
    loss = lax.psum(loss_local, ("x", "y", "c"))

    zeros = lambda shape: jnp.zeros(shape, BF16)
    d_w_in1, d_w_out1 = zeros(g_w_in1.shape), zeros(g_w_out1.shape)
    d_w_in2, d_w_out2 = zeros(g_w_in2.shape), zeros(g_w_out2.shape)
    d_qkv, d_wo, d_lin, d_lout, d_rarx = zeros(g_qkv.shape), zeros(g_wo.shape), zeros(g_lin.shape), zeros(g_lout.shape), zeros(g_rarx.shape)
    sg = [None] * SMALL_ROWS
    d_sinks = [None] * LA
    for i in reversed(range(L)):
        j = i // 2
        lay = saved[i]
        dh, sg[o_g + 3 * i + 2], sg[o_b + 3 * i + 2], d_w_in2, d_w_out2 = _ffn_bwd(
            dh, lay["z3"], row_of(o_g + 3 * i + 2), lay["hb2"], lay["gu2"], lay["act2"], g_w_in2, g_w_out2, i, d_w_in2, d_w_out2, alpha)
        if i % 2 == 0:
            dz, dmb, sg[o_g + 3 * i + 1], sg[o_b + 3 * i + 1] = _ln_bwd("attn_ln_bwd", dh, lay["z2"], row_of(o_g + 3 * i + 1), 1.0)
            d_wo = _grad_tn("attn_dwo", lay["o"], dmb, d_wo, j, _tile(Q_COLS, 1024), D)
            do = _back_proj("attn_do", dmb, g_wo, j, BF16)
            dq, dkv, dsk = _attn_bwd(lay["qkv"], do, sink_rows[j])
            d_sinks[j] = jnp.sum(dsk.reshape(N_HEADS, ATTN_BLOCK), axis=1)
            dqkv = _rope_bwd(dq, dkv, c128, s128)
            d_qkv = _grad_tn("attn_dwqkv", lay["hb1"], dqkv, d_qkv, j, D, dqkv.shape[1])
            dh = _input_grad("attn_dx", dqkv, g_qkv, j, dz, alpha)
        else:
            dz, dmb, sg[o_g + 3 * i + 1], sg[o_b + 3 * i + 1] = _ln_bwd("lru_ln_bwd", dh, lay["z2"], row_of(o_g + 3 * i + 1), 1.0)
            d_lout = _grad_tn("lru_dwout", lay["y"], dmb, d_lout, j, C, D)
            dy = _back_proj("lru_dy", dmb, g_lout, j, F32)
            dhs, dgb = _lru_out_bwd(dy, lay["hs"], lay["xg"])
            adj = _scan_bwd(lay["a"], dhs)
            dpre, dxc_direct, sg[o_ra + j], sg[o_rx + j], sg[o_lam + j] = _lru_gate_bwd(
                lay["pre"], lay["xc"], adj, lay["hs"], row_of(o_ra + j), row_of(o_rx + j), row_of(o_lam + j))
            blk = C // RNN_BLOCKS
            d_rarx = _grad_tn("lru_dwgates", lay["xcb"], dpre, d_rarx, j, blk, 2 * blk, a_cb=True)
            bm = _tile(T, ROW_TILE)
            (dxc,) = _matmul("lru_dxc", (T // bm, RNN_BLOCKS, 1), (dpre, (bm, 2 * blk), lambda p, q, k: (p, q)),
                             (g_rarx, (None, None, blk, 2 * blk), lambda p, q, k, j=j: (j, q, 0, 0)),
                             [((T, C), F32, (bm, blk), lambda p, q: (p, q))], (bm, blk), tb=True,
                             extras=[(dxc_direct, (bm, blk), lambda p, q: (p, q))], epilogue=lambda acc, d: (acc + d,))
            res = _conv_bwd(dxc, lay["xg"], dgb, lay["cw"])
            dxg = res[0]
            for k in range(CONV_W):
                sg[o_cw + j * CONV_W + k] = res[1 + k]
            sg[o_cb + j] = res[1 + CONV_W]
            d_lin = _grad_tn("lru_dwin", lay["hb1"], dxg, d_lin, j, D, _tile(2 * C, 1024))
            dh = _input_grad("lru_dx", dxg, g_lin, j, dz, alpha)
        dh, sg[o_g + 3 * i], sg[o_b + 3 * i], d_w_in1, d_w_out1 = _ffn_bwd(
            dh, lay["z1"], row_of(o_g + 3 * i), lay["hb0"], lay["gu1"], lay["act1"], g_w_in1, g_w_out1, i, d_w_in1, d_w_out1, alpha)
    grad_x = dh.reshape(x.shape)

    sink_vec = jnp.concatenate(d_sinks).reshape(1, n_sink)
    sg[o_sink] = jnp.tile(jnp.concatenate([sink_vec, jnp.zeros((1, CW - n_sink), F32)], axis=1), (1, N_CHIPS))
    zero_row = jnp.zeros((1, C), F32)
    d_small = jnp.concatenate([zero_row if r is None else r for r in sg], axis=0).reshape(2, SMALL_ROWS // 2, C)

    grads = [d_w_in1, d_w_out1, d_w_in2, d_w_out2, d_qkv, d_wo, d_lin, d_lout, d_rarx, d_small]
    gsh = sh_list + [_Sharded("rarx", RW, 0, 2 * blk_w), small_sh]
    got = _sibling_swap(grads)
    parts = [_pair_add(g, r, c_arr) for g, r in zip(grads, got)]
    recv = _chip_exchange(parts, gsh)
    sums = [_chip_sum(r, c_arr) for r in recv]
    tot = _sibling_share(sums)
    t_w_in1, t_w_out1, t_w_in2, t_w_out2, t_qkv, t_wo, t_lin, t_lout, t_rarx, t_small = tot
    t_small = t_small.reshape(SMALL_ROWS, CW)

    G = {"ffn1_w_in": t_w_in1, "ffn1_w_out": t_w_out1, "ffn2_w_in": t_w_in2, "ffn2_w_out": t_w_out2,
         "attn_w_qkv": t_qkv, "attn_w_o": t_wo, "lru_w_in": t_lin, "lru_w_out": t_lout,
         "lru_w_ra": t_rarx[..., :blk_w], "lru_w_rx": t_rarx[..., blk_w:]}

    def unpack_small(p):
        return {"ln_g": p[o_g:o_g + 3 * L].reshape(ln_g.shape), "ln_b": p[o_b:o_b + 3 * L].reshape(ln_b.shape),
                "lru_conv_w": p[o_cw:o_cw + LR * CONV_W].reshape(lru_conv_w.shape), "lru_conv_b": p[o_cb:o_cb + LR],
                "lru_b_ra": p[o_ra:o_ra + LR], "lru_b_rx": p[o_rx:o_rx + LR], "lru_lambda": p[o_lam:o_lam + LR],
                "attn_sinks": p[o_sink, :n_sink].reshape(attn_sinks.shape)}

    G.update(unpack_small(t_small))

    delta, new_m, new_v = {}, {}, {}
    small_names = ["ln_g", "ln_b", "lru_conv_w", "lru_conv_b", "lru_b_ra", "lru_b_rx", "lru_lambda", "attn_sinks"]
    for n in names:
        if n not in small_names:
            delta[n], new_m[n], new_v[n] = _adamw(W[n], G[n], M[n], V[n])
    ds, ms, vs = _adamw(pack_small(W), t_small, pack_small(M), pack_small(V))
    for d, p in ((delta, ds), (new_m, ms), (new_v, vs)):
        d.update(unpack_small(p))

    return (loss, grad_x, *[G[n] for n in names], *[delta[n] for n in names], *[new_m[n] for n in names], *[new_v[n] for n in names])
```

```python
import functools
import math

import jax
import jax.numpy as jnp
from jax import lax
from jax.experimental import pallas as pl
from jax.experimental.pallas import tpu as pltpu

F32 = jnp.float32
BF16 = jnp.bfloat16
MESH = pl.DeviceIdType.MESH

N_HEADS = 16
N_KV_HEADS = 4
HEAD_DIM = 64
GROUP = N_HEADS // N_KV_HEADS
ATTN_BLOCK = 128
ROPE_THETA = 10000.0
RNN_BLOCKS = 4
CONV_W = 4
LRU_C = 8.0
LN_EPS = 1e-5
ADAM_LR = 0.001
ADAM_B1 = 0.9
ADAM_B2 = 0.999
ADAM_EPS = 1e-08
ADAM_WD = 0.01
ADAM_STEP = 10
N_CHIPS = 4
NEG_BIG = -1e30
VMEM_LIMIT_MB = 56
ROW_TILE = 512
SMALL_ROWS = 96


def _cparams(sem):
    return pltpu.CompilerParams(dimension_semantics=sem, vmem_limit_bytes=VMEM_LIMIT_MB << 20)


def _tile(n, pref):
    if n <= pref:
        return n
    for t in range(pref - pref % 16, 0, -16):
        if n % t == 0:
            return t
    raise ValueError((n, pref))


class _SideComm:
    def __init__(self, arrays, n_sems, start, finish):
        self.arrays, self.n_sems, self.start, self.finish = list(arrays), n_sems, start, finish


def _hosted_call(body, comm, *, name, grid, in_specs, out_specs, out_shape, operands, scratch_shapes=(),
                 input_output_aliases=None, compiler_params=None):
    aliases = dict(input_output_aliases or {})
    in_specs, out_specs, out_shape = list(in_specs), list(out_specs), list(out_shape)
    operands, scratch_shapes = list(operands), list(scratch_shapes)
    n_in, n_out = len(in_specs), len(out_specs)
    if comm is None:
        res = pl.pallas_call(body, name=name, grid=grid, in_specs=in_specs, out_specs=out_specs, out_shape=out_shape,
                             scratch_shapes=scratch_shapes, input_output_aliases=aliases,
                             compiler_params=compiler_params)(*operands)
        return list(res), []
    m = len(comm.arrays)
    hbm = pl.BlockSpec(memory_space=pl.ANY)
    for t, arr in enumerate(comm.arrays):
        aliases[n_in + t] = n_out + t
    n_scr = len(scratch_shapes)

    def hosted(*refs):
        ins = refs[:n_in]
        outs = refs[n_in + m:n_in + m + n_out]
        carried = refs[n_in + m + n_out:n_in + 2 * m + n_out]
        scr = refs[n_in + 2 * m + n_out:n_in + 2 * m + n_out + n_scr]
        send, recv = refs[-2:]
        first = functools.reduce(jnp.logical_and, [pl.program_id(d) == 0 for d in range(len(grid))])
        last = functools.reduce(jnp.logical_and, [pl.program_id(d) == grid[d] - 1 for d in range(len(grid))])

        @pl.when(first)
        def _():
            comm.start(carried, send, recv)

        body(*ins, *outs, *scr)

        @pl.when(last)
        def _():
            comm.finish(carried, send, recv)

    res = pl.pallas_call(
        hosted, name=name, grid=grid, in_specs=in_specs + [hbm] * m, out_specs=out_specs + [hbm] * m,
        out_shape=out_shape + [jax.ShapeDtypeStruct(a.shape, a.dtype) for a in comm.arrays],
        scratch_shapes=scratch_shapes + [pltpu.SemaphoreType.DMA((comm.n_sems,)), pltpu.SemaphoreType.DMA((comm.n_sems,))],
        input_output_aliases=aliases, compiler_params=compiler_params)(*operands, *comm.arrays)
    return list(res[:n_out]), list(res[n_out:])


def _matmul(name, grid, a, b, outs, acc_shape, *, ta=False, tb=False, extras=(), epilogue=None,
            n_outer=False, alias_in=None, comm=None):
    gm, gn, gk = grid
    if n_outer:
        g = (gn, gm, gk)
        ijk = lambda p, q, k: (q, p, k)
    else:
        g = (gm, gn, gk)
        ijk = lambda p, q, k: (p, q, k)
    w3 = lambda f: (lambda p, q, k: f(*ijk(p, q, k)))
    w2 = lambda f: (lambda p, q, k: f(*ijk(p, q, k)[:2]))
    in_specs = [pl.BlockSpec(a[1], w3(a[2])), pl.BlockSpec(b[1], w3(b[2]))]
    in_specs += [pl.BlockSpec(e[1], w2(e[2])) for e in extras]
    operands = [a[0], b[0]] + [e[0] for e in extras]
    io_alias = {}
    n_alias = 0
    if alias_in is not None:
        in_specs.append(pl.BlockSpec(memory_space=pl.ANY))
        operands.append(alias_in)
        io_alias = {len(operands) - 1: 0}
        n_alias = 1
    ne, no = len(extras), len(outs)
    dims = (((0 if ta else 1,), (1 if tb else 0,)), ((), ()))

    def body(*refs):
        a_ref, b_ref = refs[0], refs[1]
        e_refs = refs[2:2 + ne]
        o_refs = refs[2 + ne + n_alias:2 + ne + n_alias + no]
        part = lax.dot_general(a_ref[...], b_ref[...], dims, preferred_element_type=F32)

        def finish(acc):
            res = epilogue(acc, *[r[...] for r in e_refs]) if epilogue is not None else (acc,)
            for r, v in zip(o_refs, res):
                r[...] = v.astype(r.dtype)

        if gk == 1:
            finish(part)
        else:
            acc_ref = refs[-1]
            k = pl.program_id(2)

            @pl.when(k == 0)
            def _():
                acc_ref[...] = part

            @pl.when(k > 0)
            def _():
                acc_ref[...] += part

            @pl.when(k == gk - 1)
            def _():
                finish(acc_ref[...])

    res, carried = _hosted_call(
        body, comm, name=name, grid=g, in_specs=in_specs,
        out_specs=[pl.BlockSpec(o[2], w2(o[3])) for o in outs],
        out_shape=[jax.ShapeDtypeStruct(o[0], o[1]) for o in outs],
        operands=operands,
        scratch_shapes=[pltpu.VMEM(acc_shape, F32)] if gk > 1 else [],
        input_output_aliases=io_alias,
        compiler_params=_cparams(("arbitrary", "arbitrary", "arbitrary")))
    return res if comm is None else (res, carried)


def _rowwise(name, nsteps, fn, ins, outs, accs=(), prefetch=None):
    n_in, n_out, n_acc = len(ins), len(outs), len(accs)
    n_pre = 0 if prefetch is None else 1

    def body(*refs):
        refs = refs[n_pre:]
        i = pl.program_id(0)
        res = fn(i, *[r[...] for r in refs[:n_in]])
        for r, v in zip(refs[n_in:n_in + n_out], res[:n_out]):
            r[...] = v.astype(r.dtype)
        acc_refs = refs[n_in + n_out:n_in + n_out + n_acc]
        if n_acc:
            @pl.when(i == 0)
            def _():
                for r in acc_refs:
                    r[...] = jnp.zeros_like(r)

            for r, v in zip(acc_refs, res[n_out:]):
                r[...] += v

    if prefetch is None:
        zero = lambda shape: (lambda i: (0,) * len(shape))
    else:
        zero = lambda shape: (lambda i, p: (0,) * len(shape))
    in_specs = [pl.BlockSpec(b, m) for _, b, m in ins]
    out_specs = [pl.BlockSpec(o[2], o[3]) for o in outs] + [pl.BlockSpec(s, zero(s)) for s in accs]
    out_shape = [jax.ShapeDtypeStruct(o[0], o[1]) for o in outs] + [jax.ShapeDtypeStruct(s, F32) for s in accs]
    cp = _cparams(("arbitrary",))
    if prefetch is None:
        call = pl.pallas_call(body, name=name, grid=(nsteps,), in_specs=in_specs, out_specs=out_specs,
                              out_shape=out_shape, compiler_params=cp)
        return call(*[x[0] for x in ins])
    gs = pltpu.PrefetchScalarGridSpec(num_scalar_prefetch=1, grid=(nsteps,), in_specs=in_specs, out_specs=out_specs)
    call = pl.pallas_call(body, name=name, grid_spec=gs, out_shape=out_shape, compiler_params=cp)
    return call(prefetch, *[x[0] for x in ins])


def _rows(arr, tr, cols=None, cb=0):
    cols = arr.shape[1] if cols is None else cols
    return (arr, (tr, cols), lambda i: (i, cb))


def _whole(arr):
    return (arr, arr.shape, lambda i: (0,) * arr.ndim)


def _layernorm_fwd(z, g, b):
    mu = jnp.mean(z, axis=-1, keepdims=True)
    xc = z - mu
    var = jnp.mean(xc * xc, axis=-1, keepdims=True)
    return xc * lax.rsqrt(var + LN_EPS) * g + b


def _gelu_tanh(x):
    c = math.sqrt(2.0 / math.pi)
    return x * (0.5 * (1.0 + jnp.tanh(c * (x + 0.044715 * (x * x * x)))))


@jax.custom_jvp
def _expm1(x):
    return jnp.where(jnp.abs(x) < 0.5, jnp.tanh(0.5 * x) * (jnp.exp(x) + 1.0), jnp.exp(x) - 1.0)


@_expm1.defjvp
def _expm1_jvp(primals, tangents):
    (x,), (t,) = primals, tangents
    return _expm1(x), jnp.exp(x) * t


def _log_sigmoid(x):
    return jnp.minimum(x, 0.0) - jnp.log1p(jnp.exp(-jnp.abs(x)))


def _lru_gates(pre, xc, b_ra, b_rx, lam):
    w = xc.shape[-1]
    r = jax.nn.sigmoid(pre[:, :w] + b_ra)
    ig = jax.nn.sigmoid(pre[:, w:] + b_rx)
    log_a = LRU_C * r * _log_sigmoid(lam)
    a = jnp.exp(log_a)
    b = jnp.sqrt(-_expm1(2.0 * log_a)) * (ig * xc)
    return a, b


def _swap_halves(x):
    n = x.shape[1]
    first = (lax.broadcasted_iota(jnp.int32, x.shape, 1) % HEAD_DIM) < (HEAD_DIM // 2)
    return jnp.where(first, pltpu.roll(x, n - HEAD_DIM // 2, 1), pltpu.roll(x, HEAD_DIM // 2, 1))


def _shift_down(prev8, cur, s):
    ext = jnp.concatenate([prev8, cur], axis=0)
    return pltpu.roll(ext, s, 0)[8:]


def _shift_up(cur, next8, s):
    ext = jnp.concatenate([cur, next8], axis=0)
    return pltpu.roll(ext, ext.shape[0] - s, 0)[:cur.shape[0]]


def _ln_epilogue(alpha, scale):
    def epi(acc, hprev, g, b):
        z = alpha * hprev + scale * acc
        h = _layernorm_fwd(z, g, b)
        return z, h, h
    return epi


def _proj_ln(name, act, w, layer, hprev, g, b, alpha, scale, comm=None):
    T, K = act.shape
    D = w.shape[2]
    bm = _tile(T, ROW_TILE)
    row = lambda i, j: (i, 0)
    return _matmul(
        name, (T // bm, 1, 1),
        (act, (bm, K), lambda i, j, k: (i, 0)), (w, (None, K, D), lambda i, j, k: (layer, 0, 0)),
        [((T, D), F32, (bm, D), row), ((T, D), F32, (bm, D), row), ((T, D), BF16, (bm, D), row)],
        (bm, D),
        extras=[(hprev, (bm, D), row), (g, (1, D), lambda i, j: (0, 0)), (b, (1, D), lambda i, j: (0, 0))],
        epilogue=_ln_epilogue(alpha, scale), comm=comm)


def _ln_bwd(name, dh, z, g, scale):
    T, D = z.shape
    tr = _tile(T, ROW_TILE)

    def fn(i, dh, z, g):
        mu = jnp.mean(z, axis=-1, keepdims=True)
        xc = z - mu
        var = jnp.mean(xc * xc, axis=-1, keepdims=True)
        rstd = lax.rsqrt(var + LN_EPS)
        xhat = xc * rstd
        dxh = dh * g
        dz = rstd * (dxh - jnp.mean(dxh, axis=-1, keepdims=True) - xhat * jnp.mean(dxh * xhat, axis=-1, keepdims=True))
        return (dz, scale * dz, jnp.sum(dh * xhat, axis=0, keepdims=True), jnp.sum(dh, axis=0, keepdims=True))

    row = lambda i: (i, 0)
    return _rowwise(name, T // tr, fn, [_rows(dh, tr), _rows(z, tr), _whole(g)],
                    [((T, D), F32, (tr, D), row), ((T, D), BF16, (tr, D), row)], accs=[(1, D), (1, D)])


def _grad_tn(name, a, b, stack, layer, bm, bn, bk=4 * ROW_TILE, a_cb=None, b_cb=None):
    T, M = a.shape
    N = b.shape[1]
    bk = _tile(T, bk)
    lead = stack.ndim - 2
    if lead == 1:
        oblk, omap = (None, bm, bn), (lambda i, j: (layer, i, j))
    else:
        oblk, omap = (None, None, bm, bn), (lambda i, j: (layer, j, 0, 0))
    amap = (lambda i, j, k: (k, i)) if a_cb is None else (lambda i, j, k: (k, j))
    bmap = (lambda i, j, k: (k, j))
    gm = M // bm if a_cb is None else 1
    (out,) = _matmul(name, (gm, N // bn, T // bk), (a, (bk, bm), amap), (b, (bk, bn), bmap),
                     [(stack.shape, stack.dtype, oblk, omap)], (bm, bn), ta=True, alias_in=stack)
    return out


MXU_COLS = 256


def _col_chunks(width):
    return [(s, min(MXU_COLS, width - s)) for s in range(0, width, MXU_COLS)]


def _ffn_up(hb, w_in, layer, comm=None):
    T, D = hb.shape
    N2 = w_in.shape[2]
    wd = N2 // 4
    bm = _tile(T, 2 * ROW_TILE)

    def body(a_ref, w_ref, gu_ref, act_ref):
        a = a_ref[...]
        for s, n in _col_chunks(wd):
            gg = jnp.dot(a, w_ref[:, s:s + n], preferred_element_type=F32)
            uu = jnp.dot(a, w_ref[:, wd + s:wd + s + n], preferred_element_type=F32)
            sg = jax.nn.sigmoid(gg)
            silu = gg * sg
            gu_ref[:, s:s + n] = (uu * (sg + silu * (1.0 - sg))).astype(gu_ref.dtype)
            gu_ref[:, wd + s:wd + s + n] = silu.astype(gu_ref.dtype)
            act_ref[:, s:s + n] = (silu * uu).astype(act_ref.dtype)

    (gu, act), carried = _hosted_call(
        body, comm, name="ffn_up", grid=(2, T // bm),
        in_specs=[pl.BlockSpec((bm, D), lambda j, i: (i, 0)), pl.BlockSpec((None, D, 2 * wd), lambda j, i: (layer, 0, j))],
        out_specs=[pl.BlockSpec((bm, 2 * wd), lambda j, i: (i, j)), pl.BlockSpec((bm, wd), lambda j, i: (i, j))],
        out_shape=[jax.ShapeDtypeStruct((T, N2), BF16), jax.ShapeDtypeStruct((T, N2 // 2), BF16)],
        operands=[hb, w_in], compiler_params=_cparams(("arbitrary", "arbitrary")))
    return gu, act, carried


def _ffn_dact(dyb, w_out, layer, gu):
    T, D = dyb.shape
    N2 = gu.shape[1]
    wd = N2 // 4
    bm = _tile(T, 2 * ROW_TILE)

    def body(dy_ref, w_ref, gu_ref, o_ref):
        dy = dy_ref[...]
        for s, n in _col_chunks(wd):
            dact = lax.dot_general(dy, w_ref[s:s + n, :], (((1,), (1,)), ((), ())), preferred_element_type=F32)
            o_ref[:, s:s + n] = (dact * gu_ref[:, s:s + n].astype(F32)).astype(o_ref.dtype)
            o_ref[:, wd + s:wd + s + n] = (dact * gu_ref[:, wd + s:wd + s + n].astype(F32)).astype(o_ref.dtype)

    return pl.pallas_call(
        body, name="ffn_dact", grid=(2, T // bm),
        in_specs=[pl.BlockSpec((bm, D), lambda j, i: (i, 0)), pl.BlockSpec((None, wd, D), lambda j, i: (layer, j, 0)),
                  pl.BlockSpec((bm, 2 * wd), lambda j, i: (i, j))],
        out_specs=pl.BlockSpec((bm, 2 * wd), lambda j, i: (i, j)),
        out_shape=jax.ShapeDtypeStruct((T, N2), BF16),
        compiler_params=_cparams(("arbitrary", "arbitrary")))(dyb, w_out, gu)


def _ffn_bwd(dh, z, g, hb_in, gu, act, w_in, w_out, layer, dw_in, dw_out, alpha):
    T, D = z.shape
    N2 = w_in.shape[2]
    F = N2 // 2
    bm = _tile(T, ROW_TILE)
    dz, dyb, dg, db = _ln_bwd("ffn_ln_bwd", dh, z, g, 0.5)
    dgu = _ffn_dact(dyb, w_out, layer, gu)
    dw_out = _grad_tn("ffn_dwout", act, dyb, dw_out, layer, F // 2, D)
    dw_in = _grad_tn("ffn_dwin", hb_in, dgu, dw_in, layer, D, N2 // 4)
    (dh_prev,) = _matmul(
        "ffn_dx", (T // bm, 1, 1),
        (dgu, (bm, N2), lambda i, j, k: (i, 0)), (w_in, (None, D, N2), lambda i, j, k: (layer, 0, 0)),
        [((T, D), F32, (bm, D), lambda i, j: (i, 0))], (bm, D), tb=True,
        extras=[(dz, (bm, D), lambda i, j: (i, 0))], epilogue=lambda acc, dzb: (alpha * dzb + acc,))
    return dh_prev, dg, db, dw_in, dw_out


def _input_grad(name, dy, w, layer, dz, alpha):
    T, N = dy.shape
    D = w.shape[1]
    bm = _tile(T, ROW_TILE)
    (out,) = _matmul(
        name, (T // bm, 1, 1),
        (dy, (bm, N), lambda i, j, k: (i, 0)), (w, (None, D, N), lambda i, j, k: (layer, 0, 0)),
        [((T, D), F32, (bm, D), lambda i, j: (i, 0))], (bm, D), tb=True,
        extras=[(dz, (bm, D), lambda i, j: (i, 0))], epilogue=lambda acc, dzb: (alpha * dzb + acc,))
    return out


def _back_proj(name, dy, w, layer, dtype):
    T, D = dy.shape
    K = w.shape[1]
    bm = _tile(T, ROW_TILE)
    (out,) = _matmul(
        name, (T // bm, 1, 1),
        (dy, (bm, D), lambda i, j, k: (i, 0)), (w, (None, K, D), lambda i, j, k: (layer, 0, 0)),
        [((T, K), dtype, (bm, K), lambda i, j: (i, 0))], (bm, K), tb=True)
    return out


def _rope_tables(T):
    pos = jnp.arange(T, dtype=F32)
    inv_freq = ROPE_THETA ** (-jnp.arange(0, HEAD_DIM, 2, dtype=F32) / HEAD_DIM)
    ang = pos[:, None] * inv_freq[None, :]
    cos, sin = jnp.cos(ang), jnp.sin(ang)
    c128 = jnp.tile(cos, (1, 4))
    s128 = jnp.tile(jnp.concatenate([-sin, sin], axis=1), (1, 2))
    return c128, s128


QK_COLS = (N_HEADS + N_KV_HEADS) * HEAD_DIM
Q_COLS = N_HEADS * HEAD_DIM
KV_COLS = N_KV_HEADS * HEAD_DIM
Q_SCALE = HEAD_DIM ** -0.5


def _qkv_rope(hb, w_qkv, layer, c128, s128):
    T, D = hb.shape
    N = w_qkv.shape[2]
    bm = _tile(T, ROW_TILE)

    def epi(acc, c, s):
        x = acc[:, :QK_COLS]
        rep = QK_COLS // 128
        r = x * jnp.tile(c, (1, rep)) + _swap_halves(x) * jnp.tile(s, (1, rep))
        return (jnp.concatenate([r[:, :Q_COLS] * Q_SCALE, r[:, Q_COLS:], acc[:, QK_COLS:]], axis=1),)

    (qkv,) = _matmul(
        "qkv_rope", (T // bm, 1, 1),
        (hb, (bm, D), lambda i, j, k: (i, 0)), (w_qkv, (None, D, N), lambda i, j, k: (layer, 0, 0)),
        [((T, N), BF16, (bm, N), lambda i, j: (i, 0))], (bm, N),
        extras=[(c128, (bm, 128), lambda i, j: (i, 0)), (s128, (bm, 128), lambda i, j: (i, 0))], epilogue=epi)
    return qkv


def _rope_bwd(dq, dkv, c128, s128):
    T = dq.shape[0]
    tr = _tile(T, ROW_TILE)

    def fn(i, dq, dkv, c, s):
        dx = jnp.concatenate([dq * Q_SCALE, dkv[:, :KV_COLS]], axis=1)
        rep = QK_COLS // 128
        d = dx * jnp.tile(c, (1, rep)) + _swap_halves(dx * jnp.tile(s, (1, rep)))
        return (jnp.concatenate([d, dkv[:, KV_COLS:]], axis=1),)

    N = Q_COLS + 2 * KV_COLS
    (out,) = _rowwise("rope_bwd", T // tr, fn, [_rows(dq, tr), _rows(dkv, tr), _rows(c128, tr), _rows(s128, tr)],
                      [((T, N), BF16, (tr, N), lambda i: (i, 0))])
    return out


def _attn_mask(first_block):
    q_pos = lax.broadcasted_iota(jnp.int32, (GROUP * ATTN_BLOCK, 2 * ATTN_BLOCK), 0) & (ATTN_BLOCK - 1)
    col = lax.broadcasted_iota(jnp.int32, (GROUP * ATTN_BLOCK, 2 * ATTN_BLOCK), 1)
    dist = q_pos + ATTN_BLOCK - col
    return (dist >= 0) & (dist < ATTN_BLOCK) & ((col >= ATTN_BLOCK) | jnp.logical_not(first_block))


def _sink_column(sk_ref, kvh):
    rg = lax.broadcasted_iota(jnp.int32, (GROUP * ATTN_BLOCK, 1), 0) // ATTN_BLOCK
    col = jnp.full((GROUP * ATTN_BLOCK, 1), sk_ref[0, kvh * GROUP], F32)
    for gi in range(1, GROUP):
        col = jnp.where(rg == gi, sk_ref[0, kvh * GROUP + gi], col)
    return col


def _stack_heads(x, kvh):
    return jnp.concatenate([x[:, (kvh * GROUP + gi) * HEAD_DIM:(kvh * GROUP + gi + 1) * HEAD_DIM] for gi in range(GROUP)], axis=0)


def _unstack_heads(parts):
    cols = []
    for p in parts:
        cols += [p[gi * ATTN_BLOCK:(gi + 1) * ATTN_BLOCK] for gi in range(GROUP)]
    return jnp.concatenate(cols, axis=1)


def _attn_softmax(q4, kb, mask, sink):
    s = lax.dot_general(q4, kb, (((1,), (1,)), ((), ())), preferred_element_type=F32)
    s = jnp.where(mask, s, NEG_BIG)
    m = jnp.maximum(jnp.max(s, axis=1, keepdims=True), sink)
    p = jnp.exp(s - m)
    e_sink = jnp.exp(sink - m)
    den = jnp.sum(p, axis=1, keepdims=True) + e_sink
    return p / den, e_sink / den


def _attn_fwd(qkv, sinks):
    T = qkv.shape[0]
    nb = T // ATTN_BLOCK
    kcb, vcb = Q_COLS // KV_COLS, Q_COLS // KV_COLS + 1

    def body(q_ref, kc_ref, kp_ref, vc_ref, vp_ref, sk_ref, o_ref):
        i = pl.program_id(0)
        mask = _attn_mask(i == 0)
        q = q_ref[...]
        kband = jnp.concatenate([kp_ref[...], kc_ref[...]], axis=0)
        vband = jnp.concatenate([vp_ref[...], vc_ref[...]], axis=0)
        parts = []
        for kvh in range(N_KV_HEADS):
            hs = slice(kvh * HEAD_DIM, (kvh + 1) * HEAD_DIM)
            pn, _ = _attn_softmax(_stack_heads(q, kvh), kband[:, hs], mask, _sink_column(sk_ref, kvh))
            parts.append(jnp.dot(pn.astype(BF16), vband[:, hs], preferred_element_type=F32))
        o_ref[...] = _unstack_heads(parts).astype(o_ref.dtype)

    prev = lambda i: jnp.maximum(i - 1, 0)
    return pl.pallas_call(
        body, name="attn_fwd", grid=(nb,),
        in_specs=[pl.BlockSpec((ATTN_BLOCK, Q_COLS), lambda i: (i, 0)),
                  pl.BlockSpec((ATTN_BLOCK, KV_COLS), lambda i: (i, kcb)),
                  pl.BlockSpec((ATTN_BLOCK, KV_COLS), lambda i: (prev(i), kcb)),
                  pl.BlockSpec((ATTN_BLOCK, KV_COLS), lambda i: (i, vcb)),
                  pl.BlockSpec((ATTN_BLOCK, KV_COLS), lambda i: (prev(i), vcb)),
                  pl.BlockSpec(memory_space=pltpu.SMEM)],
        out_specs=pl.BlockSpec((ATTN_BLOCK, Q_COLS), lambda i: (i, 0)),
        out_shape=jax.ShapeDtypeStruct((T, Q_COLS), BF16),
        compiler_params=_cparams(("arbitrary",)),
    )(qkv, qkv, qkv, qkv, qkv, sinks)


def _attn_bwd(qkv, do, sinks):
    T = qkv.shape[0]
    nb = T // ATTN_BLOCK
    kcb, vcb = Q_COLS // KV_COLS, Q_COLS // KV_COLS + 1
    B = ATTN_BLOCK

    def body(q_ref, kc_ref, kp_ref, vc_ref, vp_ref, do_ref, sk_ref, dq_ref, dkv_ref, dsk_ref, carry_ref):
        i = pl.program_id(0)

        @pl.when(i == 0)
        def _():
            carry_ref[...] = jnp.zeros_like(carry_ref)
            dsk_ref[...] = jnp.zeros_like(dsk_ref)

        @pl.when(i < nb)
        def _():
            mask = _attn_mask(i == 0)
            q = q_ref[...]
            do_blk = do_ref[...]
            kband = jnp.concatenate([kp_ref[...], kc_ref[...]], axis=0)
            vband = jnp.concatenate([vp_ref[...], vc_ref[...]], axis=0)
            dq_parts, dk_parts, dv_parts = [], [], []
            for kvh in range(N_KV_HEADS):
                hs = slice(kvh * HEAD_DIM, (kvh + 1) * HEAD_DIM)
                q4 = _stack_heads(q, kvh)
                do4 = _stack_heads(do_blk, kvh)
                kb, vb = kband[:, hs], vband[:, hs]
                pn, p_sink = _attn_softmax(q4, kb, mask, _sink_column(sk_ref, kvh))
                dp = lax.dot_general(do4, vb, (((1,), (1,)), ((), ())), preferred_element_type=F32)
                delta = jnp.sum(pn * dp, axis=1, keepdims=True)
                ds = (pn * (dp - delta)).astype(BF16)
                dsk_ref[kvh] += -(p_sink * delta)
                dq_parts.append(jnp.dot(ds, kb, preferred_element_type=F32))
                dk_parts.append(lax.dot_general(ds, q4, (((0,), (0,)), ((), ())), preferred_element_type=F32))
                dv_parts.append(lax.dot_general(pn.astype(BF16), do4, (((0,), (0,)), ((), ())), preferred_element_type=F32))
            dq_ref[...] = _unstack_heads(dq_parts)
            dkv = jnp.concatenate(dk_parts + dv_parts, axis=1)
            dkv_ref[...] = carry_ref[...] + dkv[:B]
            carry_ref[...] = dkv[B:]

        @pl.when(i == nb)
        def _():
            dkv_ref[...] = carry_ref[...]

    cur = lambda i: jnp.minimum(i, nb - 1)
    prev = lambda i: jnp.maximum(cur(i) - 1, 0)
    lag = lambda i: jnp.maximum(i - 1, 0)
    return pl.pallas_call(
        body, name="attn_bwd", grid=(nb + 1,),
        in_specs=[pl.BlockSpec((B, Q_COLS), lambda i: (cur(i), 0)),
                  pl.BlockSpec((B, KV_COLS), lambda i: (cur(i), kcb)),
                  pl.BlockSpec((B, KV_COLS), lambda i: (prev(i), kcb)),
                  pl.BlockSpec((B, KV_COLS), lambda i: (cur(i), vcb)),
                  pl.BlockSpec((B, KV_COLS), lambda i: (prev(i), vcb)),
                  pl.BlockSpec((B, Q_COLS), lambda i: (cur(i), 0)),
                  pl.BlockSpec(memory_space=pltpu.SMEM)],
        out_specs=[pl.BlockSpec((B, Q_COLS), lambda i: (cur(i), 0)),
                   pl.BlockSpec((B, 2 * KV_COLS), lambda i: (lag(i), 0)),
                   pl.BlockSpec((N_KV_HEADS, GROUP * B, 1), lambda i: (0, 0, 0))],
        out_shape=[jax.ShapeDtypeStruct((T, Q_COLS), F32), jax.ShapeDtypeStruct((T, 2 * KV_COLS), F32),
                   jax.ShapeDtypeStruct((N_KV_HEADS, GROUP * B, 1), F32)],
        scratch_shapes=[pltpu.VMEM((B, 2 * KV_COLS), F32)],
        compiler_params=_cparams(("arbitrary",)),
    )(qkv, qkv, qkv, qkv, qkv, do, sinks)


def _halo_prev(arr, tr, cols, cb=0):
    per = tr // 8
    return (arr, (8, cols), lambda i: (jnp.maximum(i * per - 1, 0), cb))


def _halo_next(arr, tr, cols, cb=0):
    per = tr // 8
    last = arr.shape[0] // 8 - 1
    return (arr, (8, cols), lambda i: (jnp.minimum((i + 1) * per, last), cb))


def _conv_fwd(xg, cw, cb):
    T = xg.shape[0]
    C = cb.shape[1]
    tr = _tile(T, ROW_TILE)

    def fn(i, cur, prev8, cb, *cw):
        prev8 = jnp.where(i == 0, 0.0, prev8)
        xc = cb + cw[CONV_W - 1] * cur
        for s in range(1, CONV_W):
            xc = xc + cw[CONV_W - 1 - s] * _shift_down(prev8, cur, s)
        return xc, xc

    row = lambda i: (i, 0)
    return _rowwise("lru_conv", T // tr, fn, [_rows(xg, tr, C), _halo_prev(xg, tr, C), _whole(cb)] + [_whole(w) for w in cw],
                    [((T, C), F32, (tr, C), row), ((T, C), BF16, (tr, C), row)])


def _conv_bwd(dxc, xg, dgb, cw):
    T, C = dxc.shape
    tr = _tile(T, ROW_TILE)
    nt = T // tr

    def fn(i, d_cur, d_next8, x_cur, x_prev8, dgb, *cw):
        d_next8 = jnp.where(i == nt - 1, 0.0, d_next8)
        x_prev8 = jnp.where(i == 0, 0.0, x_prev8)
        dxb = cw[CONV_W - 1] * d_cur
        dcw = [jnp.sum(d_cur * x_cur, axis=0, keepdims=True)]
        for s in range(1, CONV_W):
            dxb = dxb + cw[CONV_W - 1 - s] * _shift_up(d_cur, d_next8, s)
            dcw.append(jnp.sum(d_cur * _shift_down(x_prev8, x_cur, s), axis=0, keepdims=True))
        return (jnp.concatenate([dxb.astype(BF16), dgb], axis=1), dcw[3], dcw[2], dcw[1], dcw[0],
                jnp.sum(d_cur, axis=0, keepdims=True))

    return _rowwise("lru_conv_bwd", nt, fn,
                    [_rows(dxc, tr), _halo_next(dxc, tr, C), _rows(xg, tr, C), _halo_prev(xg, tr, C), _rows(dgb, tr)] + [_whole(w) for w in cw],
                    [((T, 2 * C), BF16, (tr, 2 * C), lambda i: (i, 0))], accs=[(1, C)] * (CONV_W + 1))


def _lru_gate_fwd(xc, xcb, w_rarx, layer, b_ra, b_rx, lam):
    T, C = xc.shape
    W = C // RNN_BLOCKS
    bm = _tile(T, ROW_TILE)

    def epi(acc, xc_blk, bra, brx, lm):
        a, b = _lru_gates(acc, xc_blk, bra, brx, lm)
        return acc, a, b

    blk = lambda i, j: (i, j)
    par = lambda i, j: (0, j)
    return _matmul(
        "lru_gates", (T // bm, RNN_BLOCKS, 1),
        (xcb, (bm, W), lambda i, j, k: (i, j)), (w_rarx, (None, None, W, 2 * W), lambda i, j, k: (layer, j, 0, 0)),
        [((T, 2 * C), F32, (bm, 2 * W), blk), ((T, C), F32, (bm, W), blk), ((T, C), F32, (bm, W), blk)],
        (bm, 2 * W),
        extras=[(xc, (bm, W), blk), (b_ra, (1, W), par), (b_rx, (1, W), par), (lam, (1, W), par)], epilogue=epi)


def _lru_gate_bwd(pre, xc, lam_adj, h, b_ra, b_rx, lam):
    T, C = xc.shape
    W = C // RNN_BLOCKS
    tr = _tile(T, ROW_TILE // 2)

    def fn(i, pre, xc, adj, h_cur, h_prev8, bra, brx, lm):
        h_prev8 = jnp.where(i == 0, 0.0, h_prev8)
        da = adj * _shift_down(h_prev8, h_cur, 1)
        dpre, dxc, dbra, dbrx, dlam = [], [], [], [], []
        for n in range(RNN_BLOCKS):
            cs = slice(n * W, (n + 1) * W)
            _, vjp = jax.vjp(_lru_gates, pre[:, 2 * n * W:2 * (n + 1) * W], xc[:, cs], bra[:, cs], brx[:, cs], lm[:, cs])
            g = vjp((da[:, cs], adj[:, cs]))
            for lst, v in zip((dpre, dxc, dbra, dbrx, dlam), g):
                lst.append(v)
        cat = lambda l: jnp.concatenate(l, axis=1)
        return cat(dpre), cat(dxc), cat(dbra), cat(dbrx), cat(dlam)

    row = lambda i: (i, 0)
    return _rowwise("lru_gates_bwd", T // tr, fn,
                    [_rows(pre, tr), _rows(xc, tr), _rows(lam_adj, tr), _rows(h, tr), _halo_prev(h, tr, C),
                     _whole(b_ra), _whole(b_rx), _whole(lam)],
                    [((T, 2 * C), BF16, (tr, 2 * C), row), ((T, C), F32, (tr, C), row)], accs=[(1, C)] * 3)


def _scan_fwd(a, b):
    T, C = a.shape
    tt = _tile(T, ROW_TILE)

    def body(a_ref, b_ref, o_ref, c_ref):
        @pl.when(pl.program_id(0) == 0)
        def _():
            c_ref[...] = jnp.zeros_like(c_ref)

        row = lax.broadcasted_iota(jnp.int32, (8, C), 0)

        def step(j, carry):
            sl = pl.ds(pl.multiple_of(j * 8, 8), 8)
            A, B = a_ref[sl, :], b_ref[sl, :]
            for d in (1, 2, 4):
                ok = row >= d
                B = jnp.where(ok, A * pltpu.roll(B, d, 0) + B, B)
                A = jnp.where(ok, A * pltpu.roll(A, d, 0), A)
            h = A * carry + B
            o_ref[sl, :] = h
            return jnp.sum(jnp.where(row == 7, h, 0.0), axis=0, keepdims=True)

        c_ref[0:1, :] = lax.fori_loop(0, tt // 8, step, c_ref[0:1, :])

    spec = pl.BlockSpec((tt, C), lambda i: (i, 0))
    return pl.pallas_call(body, name="lru_scan", grid=(T // tt,), in_specs=[spec, spec], out_specs=spec,
                          out_shape=jax.ShapeDtypeStruct((T, C), F32), scratch_shapes=[pltpu.VMEM((8, C), F32)],
                          compiler_params=_cparams(("arbitrary",)))(a, b)


def _scan_bwd(a, dh):
    T, C = a.shape
    tt = _tile(T, ROW_TILE)
    nt = T // tt

    def body(a_ref, d_ref, o_ref, c_ref):
        @pl.when(pl.program_id(0) == 0)
        def _():
            c_ref[...] = jnp.zeros_like(c_ref)

        row = lax.broadcasted_iota(jnp.int32, (8, C), 0)

        def step(jj, carry):
            adj_next, a_next = carry
            j = tt // 8 - 1 - jj
            sl = pl.ds(pl.multiple_of(j * 8, 8), 8)
            a_blk = a_ref[sl, :]
            A = jnp.where(row < 7, pltpu.roll(a_blk, 7, 0), a_next)
            B = d_ref[sl, :]
            for d in (1, 2, 4):
                ok = row < 8 - d
                B = jnp.where(ok, A * pltpu.roll(B, 8 - d, 0) + B, B)
                A = jnp.where(ok, A * pltpu.roll(A, 8 - d, 0), A)
            adj = A * adj_next + B
            o_ref[sl, :] = adj
            first = lambda v: jnp.sum(jnp.where(row == 0, v, 0.0), axis=0, keepdims=True)
            return first(adj), first(a_blk)

        adj0, a0 = lax.fori_loop(0, tt // 8, step, (c_ref[0:1, :], c_ref[1:2, :]))
        c_ref[0:1, :] = adj0
        c_ref[1:2, :] = a0

    spec = pl.BlockSpec((tt, C), lambda i: (nt - 1 - i, 0))
    return pl.pallas_call(body, name="lru_scan_bwd", grid=(nt,), in_specs=[spec, spec], out_specs=spec,
                          out_shape=jax.ShapeDtypeStruct((T, C), F32), scratch_shapes=[pltpu.VMEM((8, C), F32)],
                          compiler_params=_cparams(("arbitrary",)))(a, dh)


def _lru_out_fwd(h, xg):
    T, C = h.shape
    tr = _tile(T, ROW_TILE)
    (y,) = _rowwise("lru_out", T // tr, lambda i, h, gb: (h * _gelu_tanh(gb),), [_rows(h, tr), _rows(xg, tr, C, 1)],
                    [((T, C), BF16, (tr, C), lambda i: (i, 0))])
    return y


def _lru_out_bwd(dy, h, xg):
    T, C = h.shape
    tr = _tile(T, ROW_TILE)

    def fn(i, dy, h, gb):
        _, vjp = jax.vjp(lambda h, gb: h * _gelu_tanh(gb), h, gb)
        return vjp(dy)

    row = lambda i: (i, 0)
    return _rowwise("lru_out_bwd", T // tr, fn, [_rows(dy, tr), _rows(h, tr), _rows(xg, tr, C, 1)],
                    [((T, C), F32, (tr, C), row), ((T, C), BF16, (tr, C), row)])


class _Sharded:
    def __init__(self, kind, size, off=0, width=None):
        self.kind, self.size, self.off, self.width = kind, size, off, width

    def slot(self, cx, cy):
        return (2 * cy + cx) if self.kind == "perm" else (2 * cx + cy)

    def at(self, ref, cx, cy, layers, half=None):
        s = self.slot(cx, cy)
        start = s * self.size
        if not isinstance(start, int):
            start = pl.multiple_of(start, 8 if self.kind in ("rows", "rarx") else 128)
        if self.kind in ("cols", "perm"):
            n = ref.shape[1]
            rows = slice(None) if half is None else pl.ds(pl.multiple_of(half * (n // 2), 8), n // 2)
            return ref.at[layers, rows, pl.ds(start, self.size)]
        if self.kind == "rows":
            if half is None:
                return ref.at[layers, pl.ds(start, self.size), :]
            return ref.at[layers, pl.ds(pl.multiple_of(start + half * (self.size // 2), 8), self.size // 2), :]
        if self.kind == "rarx":
            n = ref.shape[1]
            blocks = slice(None) if half is None else pl.ds(half * (n // 2), n // 2)
            return ref.at[layers, blocks, pl.ds(start, self.size), pl.ds(self.off, self.width)]
        raise ValueError(self.kind)


def _mesh_pos():
    return lax.axis_index("x"), lax.axis_index("y"), lax.axis_index("c")


def _peer_chips(x, y):
    return [(1 - x, y), (x, 1 - y), (1 - x, 1 - y)]


def _place(shard, out_shape, out_dtype, sh, pos, prev=None):
    def body(p_ref, x_ref, *rest):
        rest[-1][...] = x_ref[...].astype(rest[-1].dtype)

    if sh.kind in ("cols", "perm"):
        L, R, Ns = shard.shape
        tr = _tile(R, ROW_TILE)
        k = 1 if sh.kind == "perm" else 0
        grid = (L, R // tr)
        ispec = pl.BlockSpec((None, tr, Ns), lambda l, i, p: (l, i, 0))
        ospec = pl.BlockSpec((None, tr, Ns), lambda l, i, p: (l, i, p[k]))
    elif sh.kind == "rows":
        L, Rs, D = shard.shape
        tr = _tile(Rs, ROW_TILE)
        nt = Rs // tr
        grid = (L, nt)
        ispec = pl.BlockSpec((None, tr, D), lambda l, i, p: (l, i, 0))
        ospec = pl.BlockSpec((None, tr, D), lambda l, i, p: (l, p[0] * nt + i, 0))
    else:
        L, nb, Rs, Wd = shard.shape
        cb = sh.off // Wd
        grid = (L, nb)
        ispec = pl.BlockSpec((None, None, Rs, Wd), lambda l, i, p: (l, i, 0, 0))
        ospec = pl.BlockSpec((None, None, Rs, Wd), lambda l, i, p: (l, i, p[0], cb))
    in_specs = [ispec]
    operands = [pos, shard]
    alias = {}
    if prev is not None:
        in_specs.append(pl.BlockSpec(memory_space=pl.ANY))
        operands.append(prev)
        alias = {2: 0}
    gs = pltpu.PrefetchScalarGridSpec(num_scalar_prefetch=1, grid=grid, in_specs=in_specs, out_specs=ospec)
    return pl.pallas_call(body, name="weight_place", grid_spec=gs, out_shape=jax.ShapeDtypeStruct(out_shape, out_dtype),
                          input_output_aliases=alias, compiler_params=_cparams(("arbitrary", "arbitrary")))(*operands)


def _gather_over_ici(jobs, base=0):
    assert len({id(j[0]) for j in jobs}) == len(jobs)

    def copies(refs, send, recv):
        x, y, c = _mesh_pos()
        out = []
        for t, (_, sh, layer) in enumerate(jobs):
            lay = pl.ds(layer, 1)
            mine = sh.at(refs[t], x, y, lay, c)
            for j, (px, py) in enumerate(_peer_chips(x, y)):
                theirs = sh.at(refs[t], px, py, lay, c)
                k = base + 3 * t + j
                out.append((pltpu.make_async_remote_copy(mine, mine, send.at[k], recv.at[k],
                                                         device_id=(px, py, c), device_id_type=MESH),
                            pltpu.make_async_remote_copy(theirs, theirs, send.at[k], recv.at[k],
                                                         device_id=(px, py, c), device_id_type=MESH)))
        return out

    def start(refs, send, recv):
        for out_cp, _ in copies(refs, send, recv):
            out_cp.start()

    def finish(refs, send, recv):
        for out_cp, in_cp in copies(refs, send, recv):
            in_cp.wait_recv()
            out_cp.wait_send()

    return _SideComm([j[0] for j in jobs], 3 * len(jobs), start, finish)


def _pass_to_sibling(jobs, base=0):
    assert len({id(j[0]) for j in jobs}) == len(jobs)

    def copies(refs, send, recv):
        x, y, c = _mesh_pos()
        out = []
        for t, (_, sh, layer) in enumerate(jobs):
            lay = pl.ds(layer, 1)
            for j, (px, py) in enumerate(_peer_chips(x, y)):
                got = sh.at(refs[t], px, py, lay, c)
                coming = sh.at(refs[t], px, py, lay, 1 - c)
                k = base + 3 * t + j
                out.append((pltpu.make_async_remote_copy(got, got, send.at[k], recv.at[k],
                                                         device_id=(x, y, 1 - c), device_id_type=MESH),
                            pltpu.make_async_remote_copy(coming, coming, send.at[k], recv.at[k],
                                                         device_id=(x, y, 1 - c), device_id_type=MESH)))
        return out

    def start(refs, send, recv):
        for out_cp, _ in copies(refs, send, recv):
            out_cp.start()

    def finish(refs, send, recv):
        for out_cp, in_cp in copies(refs, send, recv):
            in_cp.wait_recv()
            out_cp.wait_send()

    return _SideComm([j[0] for j in jobs], 3 * len(jobs), start, finish)


def _all_gather(jobs):
    n = len(jobs)
    ici, d2d = _gather_over_ici(jobs), _pass_to_sibling(jobs, 3 * n)

    def body(*refs):
        outs = refs[n:2 * n]
        send, recv = refs[2 * n:]
        ici.start(outs, send, recv)
        ici.finish(outs, send, recv)
        d2d.start(outs, send, recv)
        d2d.finish(outs, send, recv)

    hbm = pl.BlockSpec(memory_space=pl.ANY)
    return pl.pallas_call(
        body, name="weights_all_gather", in_specs=[hbm] * n, out_specs=[hbm] * n,
        out_shape=[jax.ShapeDtypeStruct(j[0].shape, j[0].dtype) for j in jobs],
        input_output_aliases={t: t for t in range(n)},
        scratch_shapes=[pltpu.SemaphoreType.DMA((6 * n,)), pltpu.SemaphoreType.DMA((6 * n,))],
    )(*[j[0] for j in jobs])


def _sibling_swap(grads):
    n = len(grads)

    def body(*refs):
        ins, outs = refs[:n], refs[n:2 * n]
        send, recv = refs[2 * n:]
        x, y, c = _mesh_pos()
        cps = []
        for t in range(n):
            lh = grads[t].shape[0] // 2
            cp = pltpu.make_async_remote_copy(ins[t].at[pl.ds((1 - c) * lh, lh)], outs[t], send.at[t], recv.at[t],
                                              device_id=(x, y, 1 - c), device_id_type=MESH)
            cp.start()
            cps.append(cp)
        for cp in cps:
            cp.wait()

    hbm = pl.BlockSpec(memory_space=pl.ANY)
    return pl.pallas_call(
        body, name="grad_sibling_swap", in_specs=[hbm] * n, out_specs=[hbm] * n,
        out_shape=[jax.ShapeDtypeStruct((g.shape[0] // 2,) + g.shape[1:], g.dtype) for g in grads],
        scratch_shapes=[pltpu.SemaphoreType.DMA((n,)), pltpu.SemaphoreType.DMA((n,))],
    )(*grads)


def _shard_shape(full, sh):
    s = list(full)
    if sh.kind in ("cols", "perm"):
        s[-1] = sh.size
    else:
        s[-2] = sh.size
    return tuple(s)


def _chip_exchange(parts, shardings):
    n = len(parts)

    def body(*refs):
        ins, outs = refs[:n], refs[n:2 * n]
        send, recv, loc = refs[2 * n:]
        x, y, c = _mesh_pos()
        me = 2 * x + y
        pending = []
        for t, sh in enumerate(shardings):
            src_all = _Sharded("rarx", sh.size, 0, parts[t].shape[-1]) if sh.kind == "rarx" else sh
            mine = pltpu.make_async_copy(src_all.at(ins[t], x, y, slice(None)), outs[t].at[me], loc.at[t])
            mine.start()
            pending.append(mine.wait)
            for j, (px, py) in enumerate(_peer_chips(x, y)):
                cp = pltpu.make_async_remote_copy(src_all.at(ins[t], px, py, slice(None)), outs[t].at[me],
                                                  send.at[3 * t + j], recv.at[3 * t + j],
                                                  device_id=(px, py, c), device_id_type=MESH)
                cp.start()
                pending.append(cp.wait_send)
        for t, sh in enumerate(shardings):
            for j, (px, py) in enumerate(_peer_chips(x, y)):
                dst = outs[t].at[2 * px + py]
                pltpu.make_async_remote_copy(dst, dst, send.at[3 * t + j], recv.at[3 * t + j],
                                             device_id=(px, py, c), device_id_type=MESH).wait_recv()
        for w in pending:
            w()

    hbm = pl.BlockSpec(memory_space=pl.ANY)
    return pl.pallas_call(
        body, name="grad_chip_exchange", in_specs=[hbm] * n, out_specs=[hbm] * n,
        out_shape=[jax.ShapeDtypeStruct((N_CHIPS,) + _shard_shape(p.shape, sh), p.dtype) for p, sh in zip(parts, shardings)],
        scratch_shapes=[pltpu.SemaphoreType.DMA((3 * n,)), pltpu.SemaphoreType.DMA((3 * n,)), pltpu.SemaphoreType.DMA((n,))],
    )(*parts)


def _sibling_share(sums):
    n = len(sums)

    def body(*refs):
        outs = refs[n:2 * n]
        send, recv = refs[2 * n:]
        x, y, c = _mesh_pos()
        cps = []
        for t in range(n):
            lh = sums[t].shape[0] // 2
            mine = outs[t].at[pl.ds(c * lh, lh)]
            cp = pltpu.make_async_remote_copy(mine, mine, send.at[t], recv.at[t], device_id=(x, y, 1 - c), device_id_type=MESH)
            cp.start()
            cps.append(cp)
        for t, cp in enumerate(cps):
            lh = sums[t].shape[0] // 2
            theirs = outs[t].at[pl.ds((1 - c) * lh, lh)]
            cp.wait_send()
            pltpu.make_async_remote_copy(theirs, theirs, send.at[t], recv.at[t], device_id=(x, y, 1 - c),
                                         device_id_type=MESH).wait_recv()

    hbm = pl.BlockSpec(memory_space=pl.ANY)
    return pl.pallas_call(
        body, name="grad_sibling_share", in_specs=[hbm] * n, out_specs=[hbm] * n,
        out_shape=[jax.ShapeDtypeStruct(s.shape, s.dtype) for s in sums],
        input_output_aliases={t: t for t in range(n)},
        scratch_shapes=[pltpu.SemaphoreType.DMA((n,)), pltpu.SemaphoreType.DMA((n,))],
    )(*sums)


def _flat2(shape):
    return (math.prod(shape[:-1]), shape[-1])


def _pair_add(grad, got, c_arr):
    lh = got.shape[0]
    R, C = _flat2(got.shape)
    tr = _tile(R, 256 if C > 2048 else ROW_TILE)
    nt = R // tr
    (out,) = _rowwise(
        "grad_pair_add", nt, lambda i, a, b: (a.astype(F32) + b.astype(F32),),
        [(grad.reshape(_flat2(grad.shape)), (tr, C), lambda i, p: (p[0] * nt + i, 0)), (got.reshape(R, C), (tr, C), lambda i, p: (i, 0))],
        [((R, C), got.dtype, (tr, C), lambda i, p: (i, 0))], prefetch=c_arr)
    return out.reshape(got.shape)


def _chip_sum(recv, c_arr):
    shape = recv.shape[1:]
    R, C = _flat2(shape)
    tr = _tile(R, ROW_TILE)
    nt = R // tr
    r3 = recv.reshape((N_CHIPS, R, C))

    def fn(i, blk):
        v = blk.astype(F32)
        return (((v[0] + v[1]) + v[2]) + v[3],)

    (out,) = _rowwise("grad_chip_sum", nt, fn, [(r3, (N_CHIPS, tr, C), lambda i, p: (0, i, 0))],
                      [((2 * R, C), F32, (tr, C), lambda i, p: (p[0] * nt + i, 0))], prefetch=c_arr)
    return out.reshape((2 * shape[0],) + shape[1:])


def _adamw(w, g, m, v):
    shape = w.shape
    R, C = _flat2(shape)
    tr = _tile(R, 256 if R % 256 == 0 else R)

    def fn(i, w, g, m, v):
        m2 = ADAM_B1 * m + (1.0 - ADAM_B1) * g
        v2 = ADAM_B2 * v + (1.0 - ADAM_B2) * (g * g)
        m_hat = m2 / (1.0 - ADAM_B1 ** ADAM_STEP)
        v_hat = v2 / (1.0 - ADAM_B2 ** ADAM_STEP)
        delta = -ADAM_LR * (m_hat / (jnp.sqrt(v_hat) + ADAM_EPS) + ADAM_WD * w)
        return delta, m2, v2

    row = lambda i: (i, 0)
    f2 = lambda a: a.reshape(R, C)
    outs = _rowwise("adamw", R // tr, fn, [_rows(f2(a), tr) for a in (w, g, m, v)], [((R, C), F32, (tr, C), row)] * 3)
    return [o.reshape(shape) for o in outs]


def _loss_and_grad(h, target):
    T, D = h.shape
    tr = _tile(T, ROW_TILE)

    def fn(i, h, t):
        err = h - t
        return err * (1.0 / D), jnp.sum(err * err, axis=0, keepdims=True)

    dh, sq = _rowwise("loss", T // tr, fn, [_rows(h, tr), _rows(target, tr)], [((T, D), F32, (tr, D), lambda i: (i, 0))], accs=[(1, D)])
    return dh, (0.5 / D) * jnp.sum(sq)


def kernel(x, ffn1_w_in, ffn1_w_out, ffn2_w_in, ffn2_w_out, ln_g, ln_b, attn_w_qkv, attn_sinks, attn_w_o, lru_w_in, lru_conv_w, lru_conv_b, lru_w_ra, lru_b_ra, lru_w_rx, lru_b_rx, lru_lambda, lru_w_out, loss_target, m_ffn1_w_in, m_ffn1_w_out, m_ffn2_w_in, m_ffn2_w_out, m_ln_g, m_ln_b, m_attn_w_qkv, m_attn_sinks, m_attn_w_o, m_lru_w_in, m_lru_conv_w, m_lru_conv_b, m_lru_w_ra, m_lru_b_ra, m_lru_w_rx, m_lru_b_rx, m_lru_lambda, m_lru_w_out, v_ffn1_w_in, v_ffn1_w_out, v_ffn2_w_in, v_ffn2_w_out, v_ln_g, v_ln_b, v_attn_w_qkv, v_attn_sinks, v_attn_w_o, v_lru_w_in, v_lru_conv_w, v_lru_conv_b, v_lru_w_ra, v_lru_b_ra, v_lru_w_rx, v_lru_b_rx, v_lru_lambda, v_lru_w_out):
    names = ["ffn1_w_in", "ffn1_w_out", "ffn2_w_in", "ffn2_w_out", "ln_g", "ln_b", "attn_w_qkv", "attn_sinks", "attn_w_o",
             "lru_w_in", "lru_conv_w", "lru_conv_b", "lru_w_ra", "lru_b_ra", "lru_w_rx", "lru_b_rx", "lru_lambda", "lru_w_out"]
    W = dict(zip(names, [ffn1_w_in, ffn1_w_out, ffn2_w_in, ffn2_w_out, ln_g, ln_b, attn_w_qkv, attn_sinks, attn_w_o,
                         lru_w_in, lru_conv_w, lru_conv_b, lru_w_ra, lru_b_ra, lru_w_rx, lru_b_rx, lru_lambda, lru_w_out]))
    M = dict(zip(names, [m_ffn1_w_in, m_ffn1_w_out, m_ffn2_w_in, m_ffn2_w_out, m_ln_g, m_ln_b, m_attn_w_qkv, m_attn_sinks, m_attn_w_o,
                         m_lru_w_in, m_lru_conv_w, m_lru_conv_b, m_lru_w_ra, m_lru_b_ra, m_lru_w_rx, m_lru_b_rx, m_lru_lambda, m_lru_w_out]))
    V = dict(zip(names, [v_ffn1_w_in, v_ffn1_w_out, v_ffn2_w_in, v_ffn2_w_out, v_ln_g, v_ln_b, v_attn_w_qkv, v_attn_sinks, v_attn_w_o,
                         v_lru_w_in, v_lru_conv_w, v_lru_conv_b, v_lru_w_ra, v_lru_b_ra, v_lru_w_rx, v_lru_b_rx, v_lru_lambda, v_lru_w_out]))

    T, D = x.shape[1], x.shape[2]
    L = ffn1_w_in.shape[0]
    LA, LR = attn_w_qkv.shape[0], lru_w_in.shape[0]
    N2 = ffn1_w_in.shape[2] * N_CHIPS
    F = N2 // 2
    C = lru_lambda.shape[1] * N_CHIPS
    CW = C // N_CHIPS
    alpha = (2.0 * L) ** 0.25
    c_arr = lax.axis_index("c").astype(jnp.int32).reshape(1)

    n_sink = attn_sinks.size
    assert n_sink <= CW

    up8 = lambda n: -(-n // 8) * 8
    o_g = 0
    o_b = o_g + up8(3 * L)
    o_cw = o_b + up8(3 * L)
    o_cb = o_cw + up8(LR * CONV_W)
    o_ra = o_cb + up8(LR)
    o_rx = o_ra + up8(LR)
    o_lam = o_rx + up8(LR)
    o_sink = o_lam + up8(LR)
    assert o_sink + 8 <= SMALL_ROWS

    def pack_small(d):
        parts = [d["ln_g"].reshape(-1, CW), d["ln_b"].reshape(-1, CW), d["lru_conv_w"].reshape(-1, CW), d["lru_conv_b"],
                 d["lru_b_ra"], d["lru_b_rx"], d["lru_lambda"],
                 jnp.pad(d["attn_sinks"].reshape(1, -1), ((0, 0), (0, CW - n_sink)))]
        parts = [jnp.pad(p, ((0, up8(p.shape[0]) - p.shape[0]), (0, 0))) for p in parts]
        used = sum(p.shape[0] for p in parts)
        return jnp.concatenate(parts + [jnp.zeros((SMALL_ROWS - used, CW), F32)], axis=0)

    cols = lambda a: _Sharded("cols", a.shape[-1])
    rows_ = lambda a: _Sharded("rows", a.shape[-2])
    RW = lru_w_ra.shape[2]
    sh_list = [_Sharded("perm", N2 // 4), rows_(ffn1_w_out), _Sharded("perm", N2 // 4), rows_(ffn2_w_out),
               cols(attn_w_qkv), rows_(attn_w_o), cols(lru_w_in), rows_(lru_w_out)]
    big = [ffn1_w_in, ffn1_w_out, ffn2_w_in, ffn2_w_out, attn_w_qkv, attn_w_o, lru_w_in, lru_w_out]
    blk_w = lru_w_ra.shape[3]
    small_sh = _Sharded("cols", CW)
    full = lambda a, sh: tuple(a.shape[:-1]) + (a.shape[-1] * N_CHIPS,) if sh.kind in ("cols", "perm") else \
        tuple(a.shape[:-2]) + (a.shape[-2] * N_CHIPS, a.shape[-1])
    mx, my = lax.axis_index("x"), lax.axis_index("y")
    pos = jnp.stack([2 * mx + my, 2 * my + mx]).astype(jnp.int32)
    placed = [_place(a, full(a, sh), BF16, sh, pos) for a, sh in zip(big, sh_list)]
    rarx_shape = (LR, RNN_BLOCKS, RW * N_CHIPS, 2 * blk_w)
    p_ra = _place(lru_w_ra, rarx_shape, BF16, _Sharded("rarx", RW, 0, blk_w), pos)
    placed.append(_place(lru_w_rx, rarx_shape, BF16, _Sharded("rarx", RW, blk_w, blk_w), pos, prev=p_ra))
    gnames = ["w_in1", "w_out1", "w_in2", "w_out2", "qkv", "wo", "lin", "lout", "rarx"]
    GW = dict(zip(gnames, placed))
    SH = dict(zip(gnames, sh_list + [_Sharded("rarx", RW, 0, 2 * blk_w)]))
    p_small = _place(pack_small(W).reshape(1, SMALL_ROWS, CW), (1, SMALL_ROWS, C), F32, small_sh, pos)

    def comm_for(stage, jobs):
        return stage([(GW[n], SH[n], l) for n, l in jobs]) if jobs else None

    def rebind(jobs, arrays):
        for (n, _), a in zip(jobs, arrays):
            GW[n] = a

    jobs0 = [(n, 0) for n in ("w_in1", "w_out1", "w_in2", "w_out2", "qkv", "wo")]
    got0 = _all_gather([(GW[n], SH[n], l) for n, l in jobs0] + [(p_small, small_sh, 0)])
    rebind(jobs0, got0)
    small = got0[-1].reshape(SMALL_ROWS, C)
    row_of = lambda r: small[r:r + 1]
    mixer_jobs = lambda l: [("qkv", l // 2), ("wo", l // 2)] if l % 2 == 0 else [("lin", l // 2), ("lout", l // 2), ("rarx", l // 2)]

    def ffn_forward(h, hb, n_in, n_out, layer, g, b, jobs):
        gu, act, carried = _ffn_up(hb, GW[n_in], layer, comm_for(_gather_over_ici, jobs))
        rebind(jobs, carried)
        res = _proj_ln("ffn_down_ln", act, GW[n_out], layer, h, g, b, alpha, 0.5, comm_for(_pass_to_sibling, jobs))
        if jobs:
            res, carried = res
            rebind(jobs, carried)
        return res, (gu, act)
    assert D == C, "packed small parameters assume d_model == d_rnn"

    c128, s128 = _rope_tables(T)
    sink_rows = [attn_sinks[j:j + 1] for j in range(LA)]

    h = x.reshape(T, D)
    hb = h.astype(BF16)
    saved = []
    for i in range(L):
        j = i // 2
        lay = {}
        lay["hb0"] = hb
        nxt = i + 1
        jobs1 = [("w_in2", nxt), ("w_out2", nxt)] if nxt < L else []
        jobs2 = [("w_in1", nxt), ("w_out1", nxt)] + mixer_jobs(nxt) if nxt < L else []
        (z, h, hb), (lay["gu1"], lay["act1"]) = ffn_forward(h, hb, "w_in1", "w_out1", i, row_of(o_g + 3 * i), row_of(o_b + 3 * i), jobs1)
        lay["z1"], lay["hb1"] = z, hb
        if i % 2 == 0:
            qkv = _qkv_rope(hb, GW["qkv"], j, c128, s128)
            o = _attn_fwd(qkv, sink_rows[j])
            lay["qkv"], lay["o"] = qkv, o
            z, h, hb = _proj_ln("attn_out_ln", o, GW["wo"], j, h, row_of(o_g + 3 * i + 1), row_of(o_b + 3 * i + 1), alpha, 1.0)
        else:
            bm = _tile(T, ROW_TILE)
            (xg,) = _matmul("lru_in", (T // bm, 1, 1), (hb, (bm, D), lambda p, q, k: (p, 0)),
                            (GW["lin"], (None, D, 2 * C), lambda p, q, k, j=j: (j, 0, 0)),
                            [((T, 2 * C), F32, (bm, 2 * C), lambda p, q: (p, 0))], (bm, 2 * C))
            cw = [row_of(o_cw + j * CONV_W + k) for k in range(CONV_W)]
            xc, xcb = _conv_fwd(xg, cw, row_of(o_cb + j))
            pre, a, b = _lru_gate_fwd(xc, xcb, GW["rarx"], j, row_of(o_ra + j), row_of(o_rx + j), row_of(o_lam + j))
            hs = _scan_fwd(a, b)
            y = _lru_out_fwd(hs, xg)
            lay.update(xg=xg, xc=xc, xcb=xcb, pre=pre, a=a, hs=hs, y=y, cw=cw)
            z, h, hb = _proj_ln("lru_out_ln", y, GW["lout"], j, h, row_of(o_g + 3 * i + 1), row_of(o_b + 3 * i + 1), alpha, 1.0)
        lay["z2"], lay["hb2"] = z, hb
        (z, h, hb), (lay["gu2"], lay["act2"]) = ffn_forward(h, hb, "w_in2", "w_out2", i, row_of(o_g + 3 * i + 2), row_of(o_b + 3 * i + 2), jobs2)
        lay["z3"] = z
        saved.append(lay)
    g_w_in1, g_w_out1, g_w_in2, g_w_out2, g_qkv, g_wo, g_lin, g_lout, g_rarx = [GW[n] for n in gnames]

    dh, loss_local = _loss_and_grad(h, loss_target.reshape(T, D))
    loss = lax.psum(loss_local, ("x", "y", "c"))

    zeros = lambda shape: jnp.zeros(shape, BF16)
    d_w_in1, d_w_out1 = zeros(g_w_in1.shape), zeros(g_w_out1.shape)
    d_w_in2, d_w_out2 = zeros(g_w_in2.shape), zeros(g_w_out2.shape)
    d_qkv, d_wo, d_lin, d_lout, d_rarx = zeros(g_qkv.shape), zeros(g_wo.shape), zeros(g_lin.shape), zeros(g_lout.shape), zeros(g_rarx.shape)
    sg = [None] * SMALL_ROWS
    d_sinks = [None] * LA
    for i in reversed(range(L)):
        j = i // 2
        lay = saved[i]
        dh, sg[o_g + 3 * i + 2], sg[o_b + 3 * i + 2], d_w_in2, d_w_out2 = _ffn_bwd(
            dh, lay["z3"], row_of(o_g + 3 * i + 2), lay["hb2"], lay["gu2"], lay["act2"], g_w_in2, g_w_out2, i, d_w_in2, d_w_out2, alpha)
        if i % 2 == 0:
            dz, dmb, sg[o_g + 3 * i + 1], sg[o_b + 3 * i + 1] = _ln_bwd("attn_ln_bwd", dh, lay["z2"], row_of(o_g + 3 * i + 1), 1.0)
            d_wo = _grad_tn("attn_dwo", lay["o"], dmb, d_wo, j, _tile(Q_COLS, 1024), D)
            do = _back_proj("attn_do", dmb, g_wo, j, BF16)
            dq, dkv, dsk = _attn_bwd(lay["qkv"], do, sink_rows[j])
            d_sinks[j] = jnp.sum(dsk.reshape(N_HEADS, ATTN_BLOCK), axis=1)
            dqkv = _rope_bwd(dq, dkv, c128, s128)
            d_qkv = _grad_tn("attn_dwqkv", lay["hb1"], dqkv, d_qkv, j, D, dqkv.shape[1])
            dh = _input_grad("attn_dx", dqkv, g_qkv, j, dz, alpha)
        else:
            dz, dmb, sg[o_g + 3 * i + 1], sg[o_b + 3 * i + 1] = _ln_bwd("lru_ln_bwd", dh, lay["z2"], row_of(o_g + 3 * i + 1), 1.0)
            d_lout = _grad_tn("lru_dwout", lay["y"], dmb, d_lout, j, C, D)
            dy = _back_proj("lru_dy", dmb, g_lout, j, F32)
            dhs, dgb = _lru_out_bwd(dy, lay["hs"], lay["xg"])
            adj = _scan_bwd(lay["a"], dhs)
            dpre, dxc_direct, sg[o_ra + j], sg[o_rx + j], sg[o_lam + j] = _lru_gate_bwd(
                lay["pre"], lay["xc"], adj, lay["hs"], row_of(o_ra + j), row_of(o_rx + j), row_of(o_lam + j))
            blk = C // RNN_BLOCKS
            d_rarx = _grad_tn("lru_dwgates", lay["xcb"], dpre, d_rarx, j, blk, 2 * blk, a_cb=True)
            bm = _tile(T, ROW_TILE)
            (dxc,) = _matmul("lru_dxc", (T // bm, RNN_BLOCKS, 1), (dpre, (bm, 2 * blk), lambda p, q, k: (p, q)),
                             (g_rarx, (None, None, blk, 2 * blk), lambda p, q, k, j=j: (j, q, 0, 0)),
                             [((T, C), F32, (bm, blk), lambda p, q: (p, q))], (bm, blk), tb=True,
                             extras=[(dxc_direct, (bm, blk), lambda p, q: (p, q))], epilogue=lambda acc, d: (acc + d,))
            res = _conv_bwd(dxc, lay["xg"], dgb, lay["cw"])
            dxg = res[0]
            for k in range(CONV_W):
                sg[o_cw + j * CONV_W + k] = res[1 + k]
            sg[o_cb + j] = res[1 + CONV_W]
            d_lin = _grad_tn("lru_dwin", lay["hb1"], dxg, d_lin, j, D, _tile(2 * C, 1024))
            dh = _input_grad("lru_dx", dxg, g_lin, j, dz, alpha)
        dh, sg[o_g + 3 * i], sg[o_b + 3 * i], d_w_in1, d_w_out1 = _ffn_bwd(
            dh, lay["z1"], row_of(o_g + 3 * i), lay["hb0"], lay["gu1"], lay["act1"], g_w_in1, g_w_out1, i, d_w_in1, d_w_out1, alpha)
    grad_x = dh.reshape(x.shape)

    sink_vec = jnp.concatenate(d_sinks).reshape(1, n_sink)
    sg[o_sink] = jnp.tile(jnp.concatenate([sink_vec, jnp.zeros((1, CW - n_sink), F32)], axis=1), (1, N_CHIPS))
    zero_row = jnp.zeros((1, C), F32)
    d_small = jnp.concatenate([zero_row if r is None else r for r in sg], axis=0).reshape(2, SMALL_ROWS // 2, C)

    grads = [d_w_in1, d_w_out1, d_w_in2, d_w_out2, d_qkv, d_wo, d_lin, d_lout, d_rarx, d_small]
    gsh = sh_list + [_Sharded("rarx", RW, 0, 2 * blk_w), small_sh]
    got = _sibling_swap(grads)
    parts = [_pair_add(g, r, c_arr) for g, r in zip(grads, got)]
    recv = _chip_exchange(parts, gsh)
    sums = [_chip_sum(r, c_arr) for r in recv]
    tot = _sibling_share(sums)
    t_w_in1, t_w_out1, t_w_in2, t_w_out2, t_qkv, t_wo, t_lin, t_lout, t_rarx, t_small = tot
    t_small = t_small.reshape(SMALL_ROWS, CW)

    G = {"ffn1_w_in": t_w_in1, "ffn1_w_out": t_w_out1, "ffn2_w_in": t_w_in2, "ffn2_w_out": t_w_out2,
         "attn_w_qkv": t_qkv, "attn_w_o": t_wo, "lru_w_in": t_lin, "lru_w_out": t_lout,
         "lru_w_ra": t_rarx[..., :blk_w], "lru_w_rx": t_rarx[..., blk_w:]}

    def unpack_small(p):
        return {"ln_g": p[o_g:o_g + 3 * L].reshape(ln_g.shape), "ln_b": p[o_b:o_b + 3 * L].reshape(ln_b.shape),
                "lru_conv_w": p[o_cw:o_cw + LR * CONV_W].reshape(lru_conv_w.shape), "lru_conv_b": p[o_cb:o_cb + LR],
                "lru_b_ra": p[o_ra:o_ra + LR], "lru_b_rx": p[o_rx:o_rx + LR], "lru_lambda": p[o_lam:o_lam + LR],
                "attn_sinks": p[o_sink, :n_sink].reshape(attn_sinks.shape)}

    G.update(unpack_small(t_small))

    delta, new_m, new_v = {}, {}, {}
    small_names = ["ln_g", "ln_b", "lru_conv_w", "lru_conv_b", "lru_b_ra", "lru_b_rx", "lru_lambda", "attn_sinks"]
    for n in names:
        if n not in small_names:
            delta[n], new_m[n], new_v[n] = _adamw(W[n], G[n], M[n], V[n])
    ds, ms, vs = _adamw(pack_small(W), t_small, pack_small(M), pack_small(V))
    for d, p in ((delta, ds), (new_m, ms), (new_v, vs)):
        d.update(unpack_small(p))

    return (loss, grad_x, *[G[n] for n in names], *[delta[n] for n in names], *[new_m[n] for n in names], *[new_v[n] for n in names])
```

```python
import functools
import math

import jax
import jax.numpy as jnp
from jax import lax
from jax.experimental import pallas as pl
from jax.experimental.pallas import tpu as pltpu

F32 = jnp.float32
BF16 = jnp.bfloat16
MESH = pl.DeviceIdType.MESH

N_HEADS = 16
N_KV_HEADS = 4
HEAD_DIM = 64
GROUP = N_HEADS // N_KV_HEADS
ATTN_BLOCK = 128
ROPE_THETA = 10000.0
RNN_BLOCKS = 4
CONV_W = 4
LRU_C = 8.0
LN_EPS = 1e-5
ADAM_LR = 0.001
ADAM_B1 = 0.9
ADAM_B2 = 0.999
ADAM_EPS = 1e-08
ADAM_WD = 0.01
ADAM_STEP = 10
N_CHIPS = 4
NEG_BIG = -1e30
VMEM_LIMIT_MB = 56
ROW_TILE = 512
SMALL_ROWS = 96


def _cparams(sem):
    return pltpu.CompilerParams(dimension_semantics=sem, vmem_limit_bytes=VMEM_LIMIT_MB << 20)


def _tile(n, pref):
    if n <= pref:
        return n
    for t in range(pref - pref % 16, 0, -16):
        if n % t == 0:
            return t
    raise ValueError((n, pref))


class _SideComm:
    def __init__(self, arrays, n_sems, start, finish):
        self.arrays, self.n_sems, self.start, self.finish = list(arrays), n_sems, start, finish


def _hosted_call(body, comm, *, name, grid, in_specs, out_specs, out_shape, operands, scratch_shapes=(),
                 input_output_aliases=None, compiler_params=None):
    aliases = dict(input_output_aliases or {})
    in_specs, out_specs, out_shape = list(in_specs), list(out_specs), list(out_shape)
    operands, scratch_shapes = list(operands), list(scratch_shapes)
    n_in, n_out = len(in_specs), len(out_specs)
    if comm is None:
        res = pl.pallas_call(body, name=name, grid=grid, in_specs=in_specs, out_specs=out_specs, out_shape=out_shape,
                             scratch_shapes=scratch_shapes, input_output_aliases=aliases,
                             compiler_params=compiler_params)(*operands)
        return list(res), []
    m = len(comm.arrays)
    hbm = pl.BlockSpec(memory_space=pl.ANY)
    real = [t for t, arr in enumerate(comm.arrays) if not isinstance(arr, jax.ShapeDtypeStruct)]
    mi = len(real)
    for ti, t in enumerate(real):
        aliases[n_in + ti] = n_out + t
    n_scr = len(scratch_shapes)

    def hosted(*refs):
        ins = refs[:n_in]
        outs = refs[n_in + mi:n_in + mi + n_out]
        carried = refs[n_in + mi + n_out:n_in + mi + n_out + m]
        scr = refs[n_in + mi + n_out + m:n_in + mi + n_out + m + n_scr]
        send, recv = refs[-2:]
        first = functools.reduce(jnp.logical_and, [pl.program_id(d) == 0 for d in range(len(grid))])
        last = functools.reduce(jnp.logical_and, [pl.program_id(d) == grid[d] - 1 for d in range(len(grid))])

        @pl.when(first)
        def _():
            comm.start(carried, send, recv)

        body(*ins, *outs, *scr)

        @pl.when(last)
        def _():
            comm.finish(carried, send, recv)

    res = pl.pallas_call(
        hosted, name=name, grid=grid, in_specs=in_specs + [hbm] * mi, out_specs=out_specs + [hbm] * m,
        out_shape=out_shape + [jax.ShapeDtypeStruct(a.shape, a.dtype) for a in comm.arrays],
        scratch_shapes=scratch_shapes + [pltpu.SemaphoreType.DMA((comm.n_sems,)), pltpu.SemaphoreType.DMA((comm.n_sems,))],
        input_output_aliases=aliases, compiler_params=compiler_params)(*operands, *[comm.arrays[t] for t in real])
    return list(res[:n_out]), list(res[n_out:])


def _combine(makers):
    comms, base = [], 0
    for mk in makers:
        comms.append(mk(base))
        base += comms[-1].n_sems
    offs = [0]
    for cm in comms:
        offs.append(offs[-1] + len(cm.arrays))

    def start(refs, send, recv):
        for cm, o in zip(comms, offs):
            cm.start(refs[o:o + len(cm.arrays)], send, recv)

    def finish(refs, send, recv):
        for cm, o in zip(comms, offs):
            cm.finish(refs[o:o + len(cm.arrays)], send, recv)

    return _SideComm(sum((cm.arrays for cm in comms), []), base, start, finish), [len(cm.arrays) for cm in comms]


def _comm_call(name, comm):
    def body():
        pass

    _, carried = _hosted_call(body, comm, name=name, grid=(1,), in_specs=[], out_specs=[], out_shape=[], operands=[])
    return carried


def _matmul(name, grid, a, b, outs, acc_shape, *, ta=False, tb=False, extras=(), epilogue=None,
            n_outer=False, alias_in=None, comm=None):
    gm, gn, gk = grid
    if n_outer:
        g = (gn, gm, gk)
        ijk = lambda p, q, k: (q, p, k)
    else:
        g = (gm, gn, gk)
        ijk = lambda p, q, k: (p, q, k)
    w3 = lambda f: (lambda p, q, k: f(*ijk(p, q, k)))
    w2 = lambda f: (lambda p, q, k: f(*ijk(p, q, k)[:2]))
    in_specs = [pl.BlockSpec(a[1], w3(a[2])), pl.BlockSpec(b[1], w3(b[2]))]
    in_specs += [pl.BlockSpec(e[1], w2(e[2])) for e in extras]
    operands = [a[0], b[0]] + [e[0] for e in extras]
    io_alias = {}
    n_alias = 0
    if alias_in is not None:
        in_specs.append(pl.BlockSpec(memory_space=pl.ANY))
        operands.append(alias_in)
        io_alias = {len(operands) - 1: 0}
        n_alias = 1
    ne, no = len(extras), len(outs)
    dims = (((0 if ta else 1,), (1 if tb else 0,)), ((), ()))

    def body(*refs):
        a_ref, b_ref = refs[0], refs[1]
        e_refs = refs[2:2 + ne]
        o_refs = refs[2 + ne + n_alias:2 + ne + n_alias + no]
        part = lax.dot_general(a_ref[...], b_ref[...], dims, preferred_element_type=F32)

        def finish(acc):
            res = epilogue(acc, *[r[...] for r in e_refs]) if epilogue is not None else (acc,)
            for r, v in zip(o_refs, res):
                r[...] = v.astype(r.dtype)

        if gk == 1:
            finish(part)
        else:
            acc_ref = refs[-1]
            k = pl.program_id(2)

            @pl.when(k == 0)
            def _():
                acc_ref[...] = part

            @pl.when(k > 0)
            def _():
                acc_ref[...] += part

            @pl.when(k == gk - 1)
            def _():
                finish(acc_ref[...])

    res, carried = _hosted_call(
        body, comm, name=name, grid=g, in_specs=in_specs,
        out_specs=[pl.BlockSpec(o[2], w2(o[3])) for o in outs],
        out_shape=[jax.ShapeDtypeStruct(o[0], o[1]) for o in outs],
        operands=operands,
        scratch_shapes=[pltpu.VMEM(acc_shape, F32)] if gk > 1 else [],
        input_output_aliases=io_alias,
        compiler_params=_cparams(("arbitrary", "arbitrary", "arbitrary")))
    return res if comm is None else (res, carried)


def _rowwise(name, nsteps, fn, ins, outs, accs=(), prefetch=None, into=None):
    n_in, n_out, n_acc = len(ins), len(outs), len(accs)
    n_pre = 0 if prefetch is None else 1
    n_into = 0 if into is None else 1

    def body(*refs):
        refs = refs[n_pre:]
        i = pl.program_id(0)
        res = fn(i, *[r[...] for r in refs[:n_in]])
        refs = refs[:n_in] + refs[n_in + n_into:]
        for r, v in zip(refs[n_in:n_in + n_out], res[:n_out]):
            r[...] = v.astype(r.dtype)
        acc_refs = refs[n_in + n_out:n_in + n_out + n_acc]
        if n_acc:
            @pl.when(i == 0)
            def _():
                for r in acc_refs:
                    r[...] = jnp.zeros_like(r)

            for r, v in zip(acc_refs, res[n_out:]):
                r[...] += v

    if prefetch is None:
        zero = lambda shape: (lambda i: (0,) * len(shape))
    else:
        zero = lambda shape: (lambda i, p: (0,) * len(shape))
    in_specs = [pl.BlockSpec(b, m) for _, b, m in ins]
    operands = [x[0] for x in ins]
    alias = {}
    if into is not None:
        in_specs.append(pl.BlockSpec(memory_space=pl.ANY))
        operands.append(into)
        alias = {n_pre + n_in: 0}
    out_specs = [pl.BlockSpec(o[2], o[3]) for o in outs] + [pl.BlockSpec(s, zero(s)) for s in accs]
    out_shape = [jax.ShapeDtypeStruct(o[0], o[1]) for o in outs] + [jax.ShapeDtypeStruct(s, F32) for s in accs]
    cp = _cparams(("arbitrary",))
    if prefetch is None:
        call = pl.pallas_call(body, name=name, grid=(nsteps,), in_specs=in_specs, out_specs=out_specs,
                              out_shape=out_shape, input_output_aliases=alias, compiler_params=cp)
        return call(*operands)
    gs = pltpu.PrefetchScalarGridSpec(num_scalar_prefetch=1, grid=(nsteps,), in_specs=in_specs, out_specs=out_specs)
    call = pl.pallas_call(body, name=name, grid_spec=gs, out_shape=out_shape, input_output_aliases=alias, compiler_params=cp)
    return call(prefetch, *operands)


def _rows(arr, tr, cols=None, cb=0):
    cols = arr.shape[1] if cols is None else cols
    return (arr, (tr, cols), lambda i: (i, cb))


def _whole(arr):
    return (arr, arr.shape, lambda i: (0,) * arr.ndim)


def _layernorm_fwd(z, g, b):
    mu = jnp.mean(z, axis=-1, keepdims=True)
    xc = z - mu
    var = jnp.mean(xc * xc, axis=-1, keepdims=True)
    return xc * lax.rsqrt(var + LN_EPS) * g + b


def _gelu_tanh(x):
    c = math.sqrt(2.0 / math.pi)
    return x * (0.5 * (1.0 + jnp.tanh(c * (x + 0.044715 * (x * x * x)))))


@jax.custom_jvp
def _expm1(x):
    return jnp.where(jnp.abs(x) < 0.5, jnp.tanh(0.5 * x) * (jnp.exp(x) + 1.0), jnp.exp(x) - 1.0)


@_expm1.defjvp
def _expm1_jvp(primals, tangents):
    (x,), (t,) = primals, tangents
    return _expm1(x), jnp.exp(x) * t


def _log_sigmoid(x):
    return jnp.minimum(x, 0.0) - jnp.log1p(jnp.exp(-jnp.abs(x)))


def _lru_gates(pre, xc, b_ra, b_rx, lam):
    w = xc.shape[-1]
    r = jax.nn.sigmoid(pre[:, :w] + b_ra)
    ig = jax.nn.sigmoid(pre[:, w:] + b_rx)
    log_a = LRU_C * r * _log_sigmoid(lam)
    a = jnp.exp(log_a)
    b = jnp.sqrt(-_expm1(2.0 * log_a)) * (ig * xc)
    return a, b


def _swap_halves(x):
    n = x.shape[1]
    first = (lax.broadcasted_iota(jnp.int32, x.shape, 1) % HEAD_DIM) < (HEAD_DIM // 2)
    return jnp.where(first, pltpu.roll(x, n - HEAD_DIM // 2, 1), pltpu.roll(x, HEAD_DIM // 2, 1))


def _shift_down(prev8, cur, s):
    ext = jnp.concatenate([prev8, cur], axis=0)
    return pltpu.roll(ext, s, 0)[8:]


def _shift_up(cur, next8, s):
    ext = jnp.concatenate([cur, next8], axis=0)
    return pltpu.roll(ext, ext.shape[0] - s, 0)[:cur.shape[0]]


def _ln_epilogue(alpha, scale):
    def epi(acc, hprev, g, b):
        z = alpha * hprev + scale * acc
        h = _layernorm_fwd(z, g, b)
        return z, h, h
    return epi


def _proj_ln(name, act, w, layer, hprev, g, b, alpha, scale, comm=None):
    T, K = act.shape
    D = w.shape[2]
    bm = _tile(T, ROW_TILE)
    row = lambda i, j: (i, 0)
    return _matmul(
        name, (T // bm, 1, 1),
        (act, (bm, K), lambda i, j, k: (i, 0)), (w, (None, K, D), lambda i, j, k: (layer, 0, 0)),
        [((T, D), F32, (bm, D), row), ((T, D), F32, (bm, D), row), ((T, D), BF16, (bm, D), row)],
        (bm, D),
        extras=[(hprev, (bm, D), row), (g, (1, D), lambda i, j: (0, 0)), (b, (1, D), lambda i, j: (0, 0))],
        epilogue=_ln_epilogue(alpha, scale), comm=comm)


def _ln_bwd(name, dh, z, g, scale):
    T, D = z.shape
    tr = _tile(T, ROW_TILE)

    def fn(i, dh, z, g):
        mu = jnp.mean(z, axis=-1, keepdims=True)
        xc = z - mu
        var = jnp.mean(xc * xc, axis=-1, keepdims=True)
        rstd = lax.rsqrt(var + LN_EPS)
        xhat = xc * rstd
        dxh = dh * g
        dz = rstd * (dxh - jnp.mean(dxh, axis=-1, keepdims=True) - xhat * jnp.mean(dxh * xhat, axis=-1, keepdims=True))
        return (dz, scale * dz, jnp.sum(dh * xhat, axis=0, keepdims=True), jnp.sum(dh, axis=0, keepdims=True))

    row = lambda i: (i, 0)
    return _rowwise(name, T // tr, fn, [_rows(dh, tr), _rows(z, tr), _whole(g)],
                    [((T, D), F32, (tr, D), row), ((T, D), BF16, (tr, D), row)], accs=[(1, D), (1, D)])


def _grad_tn(name, a, b, bm, bn, bk=4 * ROW_TILE, block_diag=False, comm=None):
    T, M = a.shape
    N = b.shape[1]
    bk = _tile(T, bk)
    if not block_diag:
        shape, oblk, omap = (1, M, N), (None, bm, bn), (lambda i, j: (0, i, j))
        amap, gm = (lambda i, j, k: (k, i)), M // bm
    else:
        shape, oblk, omap = (1, N // bn, bm, bn), (None, None, bm, bn), (lambda i, j: (0, j, 0, 0))
        amap, gm = (lambda i, j, k: (k, j)), 1
    res = _matmul(name, (gm, N // bn, T // bk), (a, (bk, bm), amap), (b, (bk, bn), lambda i, j, k: (k, j)),
                  [(shape, BF16, oblk, omap)], (bm, bn), ta=True, comm=comm)
    return res[0] if comm is None else (res[0][0], res[1])


MXU_COLS = 256


def _col_chunks(width):
    return [(s, min(MXU_COLS, width - s)) for s in range(0, width, MXU_COLS)]


def _ffn_up(hb, w_in, layer, comm=None):
    T, D = hb.shape
    N2 = w_in.shape[2]
    wd = N2 // 4
    bm = _tile(T, 2 * ROW_TILE)

    def body(a_ref, w_ref, gu_ref, act_ref):
        a = a_ref[...]
        for s, n in _col_chunks(wd):
            gg = jnp.dot(a, w_ref[:, s:s + n], preferred_element_type=F32)
            uu = jnp.dot(a, w_ref[:, wd + s:wd + s + n], preferred_element_type=F32)
            sg = jax.nn.sigmoid(gg)
            silu = gg * sg
            gu_ref[:, s:s + n] = (uu * (sg + silu * (1.0 - sg))).astype(gu_ref.dtype)
            gu_ref[:, wd + s:wd + s + n] = silu.astype(gu_ref.dtype)
            act_ref[:, s:s + n] = (silu * uu).astype(act_ref.dtype)

    (gu, act), carried = _hosted_call(
        body, comm, name="ffn_up", grid=(2, T // bm),
        in_specs=[pl.BlockSpec((bm, D), lambda j, i: (i, 0)), pl.BlockSpec((None, D, 2 * wd), lambda j, i: (layer, 0, j))],
        out_specs=[pl.BlockSpec((bm, 2 * wd), lambda j, i: (i, j)), pl.BlockSpec((bm, wd), lambda j, i: (i, j))],
        out_shape=[jax.ShapeDtypeStruct((T, N2), BF16), jax.ShapeDtypeStruct((T, N2 // 2), BF16)],
        operands=[hb, w_in], compiler_params=_cparams(("arbitrary", "arbitrary")))
    return gu, act, carried


def _ffn_dact(dyb, w_out, layer, gu):
    T, D = dyb.shape
    N2 = gu.shape[1]
    wd = N2 // 4
    bm = _tile(T, 2 * ROW_TILE)

    def body(dy_ref, w_ref, gu_ref, o_ref):
        dy = dy_ref[...]
        for s, n in _col_chunks(wd):
            dact = lax.dot_general(dy, w_ref[s:s + n, :], (((1,), (1,)), ((), ())), preferred_element_type=F32)
            o_ref[:, s:s + n] = (dact * gu_ref[:, s:s + n].astype(F32)).astype(o_ref.dtype)
            o_ref[:, wd + s:wd + s + n] = (dact * gu_ref[:, wd + s:wd + s + n].astype(F32)).astype(o_ref.dtype)

    return pl.pallas_call(
        body, name="ffn_dact", grid=(2, T // bm),
        in_specs=[pl.BlockSpec((bm, D), lambda j, i: (i, 0)), pl.BlockSpec((None, wd, D), lambda j, i: (layer, j, 0)),
                  pl.BlockSpec((bm, 2 * wd), lambda j, i: (i, j))],
        out_specs=pl.BlockSpec((bm, 2 * wd), lambda j, i: (i, j)),
        out_shape=jax.ShapeDtypeStruct((T, N2), BF16),
        compiler_params=_cparams(("arbitrary", "arbitrary")))(dyb, w_out, gu)


def _ffn_dx(dgu, w_in, layer, dz, alpha, comm=None):
    T, N2 = dgu.shape
    D = w_in.shape[1]
    bm = _tile(T, ROW_TILE)
    res = _matmul(
        "ffn_dx", (T // bm, 1, 1),
        (dgu, (bm, N2), lambda i, j, k: (i, 0)), (w_in, (None, D, N2), lambda i, j, k: (layer, 0, 0)),
        [((T, D), F32, (bm, D), lambda i, j: (i, 0))], (bm, D), tb=True,
        extras=[(dz, (bm, D), lambda i, j: (i, 0))], epilogue=lambda acc, dzb: (alpha * dzb + acc,), comm=comm)
    return res[0] if comm is None else (res[0][0], res[1])


def _input_grad(name, dy, w, layer, dz, alpha):
    T, N = dy.shape
    D = w.shape[1]
    bm = _tile(T, ROW_TILE)
    (out,) = _matmul(
        name, (T // bm, 1, 1),
        (dy, (bm, N), lambda i, j, k: (i, 0)), (w, (None, D, N), lambda i, j, k: (layer, 0, 0)),
        [((T, D), F32, (bm, D), lambda i, j: (i, 0))], (bm, D), tb=True,
        extras=[(dz, (bm, D), lambda i, j: (i, 0))], epilogue=lambda acc, dzb: (alpha * dzb + acc,))
    return out


def _back_proj(name, dy, w, layer, dtype):
    T, D = dy.shape
    K = w.shape[1]
    bm = _tile(T, ROW_TILE)
    (out,) = _matmul(
        name, (T // bm, 1, 1),
        (dy, (bm, D), lambda i, j, k: (i, 0)), (w, (None, K, D), lambda i, j, k: (layer, 0, 0)),
        [((T, K), dtype, (bm, K), lambda i, j: (i, 0))], (bm, K), tb=True)
    return out


def _rope_tables(T):
    pos = jnp.arange(T, dtype=F32)
    inv_freq = ROPE_THETA ** (-jnp.arange(0, HEAD_DIM, 2, dtype=F32) / HEAD_DIM)
    ang = pos[:, None] * inv_freq[None, :]
    cos, sin = jnp.cos(ang), jnp.sin(ang)
    c128 = jnp.tile(cos, (1, 4))
    s128 = jnp.tile(jnp.concatenate([-sin, sin], axis=1), (1, 2))
    return c128, s128


QK_COLS = (N_HEADS + N_KV_HEADS) * HEAD_DIM
Q_COLS = N_HEADS * HEAD_DIM
KV_COLS = N_KV_HEADS * HEAD_DIM
Q_SCALE = HEAD_DIM ** -0.5


def _qkv_rope(hb, w_qkv, layer, c128, s128):
    T, D = hb.shape
    N = w_qkv.shape[2]
    bm = _tile(T, ROW_TILE)

    def epi(acc, c, s):
        x = acc[:, :QK_COLS]
        rep = QK_COLS // 128
        r = x * jnp.tile(c, (1, rep)) + _swap_halves(x) * jnp.tile(s, (1, rep))
        return (jnp.concatenate([r[:, :Q_COLS] * Q_SCALE, r[:, Q_COLS:], acc[:, QK_COLS:]], axis=1),)

    (qkv,) = _matmul(
        "qkv_rope", (T // bm, 1, 1),
        (hb, (bm, D), lambda i, j, k: (i, 0)), (w_qkv, (None, D, N), lambda i, j, k: (layer, 0, 0)),
        [((T, N), BF16, (bm, N), lambda i, j: (i, 0))], (bm, N),
        extras=[(c128, (bm, 128), lambda i, j: (i, 0)), (s128, (bm, 128), lambda i, j: (i, 0))], epilogue=epi)
    return qkv


def _rope_bwd(dq, dkv, c128, s128):
    T = dq.shape[0]
    tr = _tile(T, ROW_TILE)

    def fn(i, dq, dkv, c, s):
        dx = jnp.concatenate([dq * Q_SCALE, dkv[:, :KV_COLS]], axis=1)
        rep = QK_COLS // 128
        d = dx * jnp.tile(c, (1, rep)) + _swap_halves(dx * jnp.tile(s, (1, rep)))
        return (jnp.concatenate([d, dkv[:, KV_COLS:]], axis=1),)

    N = Q_COLS + 2 * KV_COLS
    (out,) = _rowwise("rope_bwd", T // tr, fn, [_rows(dq, tr), _rows(dkv, tr), _rows(c128, tr), _rows(s128, tr)],
                      [((T, N), BF16, (tr, N), lambda i: (i, 0))])
    return out


def _attn_mask(first_block):
    q_pos = lax.broadcasted_iota(jnp.int32, (GROUP * ATTN_BLOCK, 2 * ATTN_BLOCK), 0) & (ATTN_BLOCK - 1)
    col = lax.broadcasted_iota(jnp.int32, (GROUP * ATTN_BLOCK, 2 * ATTN_BLOCK), 1)
    dist = q_pos + ATTN_BLOCK - col
    return (dist >= 0) & (dist < ATTN_BLOCK) & ((col >= ATTN_BLOCK) | jnp.logical_not(first_block))


def _sink_column(sk_ref, kvh):
    rg = lax.broadcasted_iota(jnp.int32, (GROUP * ATTN_BLOCK, 1), 0) // ATTN_BLOCK
    col = jnp.full((GROUP * ATTN_BLOCK, 1), sk_ref[0, kvh * GROUP], F32)
    for gi in range(1, GROUP):
        col = jnp.where(rg == gi, sk_ref[0, kvh * GROUP + gi], col)
    return col


def _stack_heads(x, kvh):
    return jnp.concatenate([x[:, (kvh * GROUP + gi) * HEAD_DIM:(kvh * GROUP + gi + 1) * HEAD_DIM] for gi in range(GROUP)], axis=0)


def _unstack_heads(parts):
    cols = []
    for p in parts:
        cols += [p[gi * ATTN_BLOCK:(gi + 1) * ATTN_BLOCK] for gi in range(GROUP)]
    return jnp.concatenate(cols, axis=1)


def _attn_softmax(q4, kb, mask, sink):
    s = lax.dot_general(q4, kb, (((1,), (1,)), ((), ())), preferred_element_type=F32)
    s = jnp.where(mask, s, NEG_BIG)
    m = jnp.maximum(jnp.max(s, axis=1, keepdims=True), sink)
    p = jnp.exp(s - m)
    e_sink = jnp.exp(sink - m)
    den = jnp.sum(p, axis=1, keepdims=True) + e_sink
    return p / den, e_sink / den


def _attn_fwd(qkv, sinks):
    T = qkv.shape[0]
    nb = T // ATTN_BLOCK
    kcb, vcb = Q_COLS // KV_COLS, Q_COLS // KV_COLS + 1

    def body(q_ref, kc_ref, kp_ref, vc_ref, vp_ref, sk_ref, o_ref):
        i = pl.program_id(0)
        mask = _attn_mask(i == 0)
        q = q_ref[...]
        kband = jnp.concatenate([kp_ref[...], kc_ref[...]], axis=0)
        vband = jnp.concatenate([vp_ref[...], vc_ref[...]], axis=0)
        parts = []
        for kvh in range(N_KV_HEADS):
            hs = slice(kvh * HEAD_DIM, (kvh + 1) * HEAD_DIM)
            pn, _ = _attn_softmax(_stack_heads(q, kvh), kband[:, hs], mask, _sink_column(sk_ref, kvh))
            parts.append(jnp.dot(pn.astype(BF16), vband[:, hs], preferred_element_type=F32))
        o_ref[...] = _unstack_heads(parts).astype(o_ref.dtype)

    prev = lambda i: jnp.maximum(i - 1, 0)
    return pl.pallas_call(
        body, name="attn_fwd", grid=(nb,),
        in_specs=[pl.BlockSpec((ATTN_BLOCK, Q_COLS), lambda i: (i, 0)),
                  pl.BlockSpec((ATTN_BLOCK, KV_COLS), lambda i: (i, kcb)),
                  pl.BlockSpec((ATTN_BLOCK, KV_COLS), lambda i: (prev(i), kcb)),
                  pl.BlockSpec((ATTN_BLOCK, KV_COLS), lambda i: (i, vcb)),
                  pl.BlockSpec((ATTN_BLOCK, KV_COLS), lambda i: (prev(i), vcb)),
                  pl.BlockSpec(memory_space=pltpu.SMEM)],
        out_specs=pl.BlockSpec((ATTN_BLOCK, Q_COLS), lambda i: (i, 0)),
        out_shape=jax.ShapeDtypeStruct((T, Q_COLS), BF16),
        compiler_params=_cparams(("arbitrary",)),
    )(qkv, qkv, qkv, qkv, qkv, sinks)


def _attn_bwd(qkv, do, sinks):
    T = qkv.shape[0]
    nb = T // ATTN_BLOCK
    kcb, vcb = Q_COLS // KV_COLS, Q_COLS // KV_COLS + 1
    B = ATTN_BLOCK

    def body(q_ref, kc_ref, kp_ref, vc_ref, vp_ref, do_ref, sk_ref, dq_ref, dkv_ref, dsk_ref, carry_ref):
        i = pl.program_id(0)

        @pl.when(i == 0)
        def _():
            carry_ref[...] = jnp.zeros_like(carry_ref)
            dsk_ref[...] = jnp.zeros_like(dsk_ref)

        @pl.when(i < nb)
        def _():
            mask = _attn_mask(i == 0)
            q = q_ref[...]
            do_blk = do_ref[...]
            kband = jnp.concatenate([kp_ref[...], kc_ref[...]], axis=0)
            vband = jnp.concatenate([vp_ref[...], vc_ref[...]], axis=0)
            dq_parts, dk_parts, dv_parts = [], [], []
            for kvh in range(N_KV_HEADS):
                hs = slice(kvh * HEAD_DIM, (kvh + 1) * HEAD_DIM)
                q4 = _stack_heads(q, kvh)
                do4 = _stack_heads(do_blk, kvh)
                kb, vb = kband[:, hs], vband[:, hs]
                pn, p_sink = _attn_softmax(q4, kb, mask, _sink_column(sk_ref, kvh))
                dp = lax.dot_general(do4, vb, (((1,), (1,)), ((), ())), preferred_element_type=F32)
                delta = jnp.sum(pn * dp, axis=1, keepdims=True)
                ds = (pn * (dp - delta)).astype(BF16)
                dsk_ref[kvh] += -(p_sink * delta)
                dq_parts.append(jnp.dot(ds, kb, preferred_element_type=F32))
                dk_parts.append(lax.dot_general(ds, q4, (((0,), (0,)), ((), ())), preferred_element_type=F32))
                dv_parts.append(lax.dot_general(pn.astype(BF16), do4, (((0,), (0,)), ((), ())), preferred_element_type=F32))
            dq_ref[...] = _unstack_heads(dq_parts)
            dkv = jnp.concatenate(dk_parts + dv_parts, axis=1)
            dkv_ref[...] = carry_ref[...] + dkv[:B]
            carry_ref[...] = dkv[B:]

        @pl.when(i == nb)
        def _():
            dkv_ref[...] = carry_ref[...]

    cur = lambda i: jnp.minimum(i, nb - 1)
    prev = lambda i: jnp.maximum(cur(i) - 1, 0)
    lag = lambda i: jnp.maximum(i - 1, 0)
    return pl.pallas_call(
        body, name="attn_bwd", grid=(nb + 1,),
        in_specs=[pl.BlockSpec((B, Q_COLS), lambda i: (cur(i), 0)),
                  pl.BlockSpec((B, KV_COLS), lambda i: (cur(i), kcb)),
                  pl.BlockSpec((B, KV_COLS), lambda i: (prev(i), kcb)),
                  pl.BlockSpec((B, KV_COLS), lambda i: (cur(i), vcb)),
                  pl.BlockSpec((B, KV_COLS), lambda i: (prev(i), vcb)),
                  pl.BlockSpec((B, Q_COLS), lambda i: (cur(i), 0)),
                  pl.BlockSpec(memory_space=pltpu.SMEM)],
        out_specs=[pl.BlockSpec((B, Q_COLS), lambda i: (cur(i), 0)),
                   pl.BlockSpec((B, 2 * KV_COLS), lambda i: (lag(i), 0)),
                   pl.BlockSpec((N_KV_HEADS, GROUP * B, 1), lambda i: (0, 0, 0))],
        out_shape=[jax.ShapeDtypeStruct((T, Q_COLS), F32), jax.ShapeDtypeStruct((T, 2 * KV_COLS), F32),
                   jax.ShapeDtypeStruct((N_KV_HEADS, GROUP * B, 1), F32)],
        scratch_shapes=[pltpu.VMEM((B, 2 * KV_COLS), F32)],
        compiler_params=_cparams(("arbitrary",)),
    )(qkv, qkv, qkv, qkv, qkv, do, sinks)


def _halo_prev(arr, tr, cols, cb=0):
    per = tr // 8
    return (arr, (8, cols), lambda i: (jnp.maximum(i * per - 1, 0), cb))


def _halo_next(arr, tr, cols, cb=0):
    per = tr // 8
    last = arr.shape[0] // 8 - 1
    return (arr, (8, cols), lambda i: (jnp.minimum((i + 1) * per, last), cb))


def _conv_fwd(xg, cw, cb):
    T = xg.shape[0]
    C = cb.shape[1]
    tr = _tile(T, ROW_TILE)

    def fn(i, cur, prev8, cb, *cw):
        prev8 = jnp.where(i == 0, 0.0, prev8)
        xc = cb + cw[CONV_W - 1] * cur
        for s in range(1, CONV_W):
            xc = xc + cw[CONV_W - 1 - s] * _shift_down(prev8, cur, s)
        return xc, xc

    row = lambda i: (i, 0)
    return _rowwise("lru_conv", T // tr, fn, [_rows(xg, tr, C), _halo_prev(xg, tr, C), _whole(cb)] + [_whole(w) for w in cw],
                    [((T, C), F32, (tr, C), row), ((T, C), BF16, (tr, C), row)])


def _conv_bwd(dxc, xg, dgb, cw):
    T, C = dxc.shape
    tr = _tile(T, ROW_TILE)
    nt = T // tr

    def fn(i, d_cur, d_next8, x_cur, x_prev8, dgb, *cw):
        d_next8 = jnp.where(i == nt - 1, 0.0, d_next8)
        x_prev8 = jnp.where(i == 0, 0.0, x_prev8)
        dxb = cw[CONV_W - 1] * d_cur
        dcw = [jnp.sum(d_cur * x_cur, axis=0, keepdims=True)]
        for s in range(1, CONV_W):
            dxb = dxb + cw[CONV_W - 1 - s] * _shift_up(d_cur, d_next8, s)
            dcw.append(jnp.sum(d_cur * _shift_down(x_prev8, x_cur, s), axis=0, keepdims=True))
        return (jnp.concatenate([dxb.astype(BF16), dgb], axis=1), dcw[3], dcw[2], dcw[1], dcw[0],
                jnp.sum(d_cur, axis=0, keepdims=True))

    return _rowwise("lru_conv_bwd", nt, fn,
                    [_rows(dxc, tr), _halo_next(dxc, tr, C), _rows(xg, tr, C), _halo_prev(xg, tr, C), _rows(dgb, tr)] + [_whole(w) for w in cw],
                    [((T, 2 * C), BF16, (tr, 2 * C), lambda i: (i, 0))], accs=[(1, C)] * (CONV_W + 1))


def _lru_gate_fwd(xc, xcb, w_rarx, layer, b_ra, b_rx, lam):
    T, C = xc.shape
    W = C // RNN_BLOCKS
    bm = _tile(T, ROW_TILE)

    def epi(acc, xc_blk, bra, brx, lm):
        a, b = _lru_gates(acc, xc_blk, bra, brx, lm)
        return acc, a, b

    blk = lambda i, j: (i, j)
    par = lambda i, j: (0, j)
    return _matmul(
        "lru_gates", (T // bm, RNN_BLOCKS, 1),
        (xcb, (bm, W), lambda i, j, k: (i, j)), (w_rarx, (None, None, W, 2 * W), lambda i, j, k: (layer, j, 0, 0)),
        [((T, 2 * C), F32, (bm, 2 * W), blk), ((T, C), F32, (bm, W), blk), ((T, C), F32, (bm, W), blk)],
        (bm, 2 * W),
        extras=[(xc, (bm, W), blk), (b_ra, (1, W), par), (b_rx, (1, W), par), (lam, (1, W), par)], epilogue=epi)


def _lru_gate_bwd(pre, xc, lam_adj, h, b_ra, b_rx, lam):
    T, C = xc.shape
    W = C // RNN_BLOCKS
    tr = _tile(T, ROW_TILE // 2)

    def fn(i, pre, xc, adj, h_cur, h_prev8, bra, brx, lm):
        h_prev8 = jnp.where(i == 0, 0.0, h_prev8)
        da = adj * _shift_down(h_prev8, h_cur, 1)
        dpre, dxc, dbra, dbrx, dlam = [], [], [], [], []
        for n in range(RNN_BLOCKS):
            cs = slice(n * W, (n + 1) * W)
            _, vjp = jax.vjp(_lru_gates, pre[:, 2 * n * W:2 * (n + 1) * W], xc[:, cs], bra[:, cs], brx[:, cs], lm[:, cs])
            g = vjp((da[:, cs], adj[:, cs]))
            for lst, v in zip((dpre, dxc, dbra, dbrx, dlam), g):
                lst.append(v)
        cat = lambda l: jnp.concatenate(l, axis=1)
        return cat(dpre), cat(dxc), cat(dbra), cat(dbrx), cat(dlam)

    row = lambda i: (i, 0)
    return _rowwise("lru_gates_bwd", T // tr, fn,
                    [_rows(pre, tr), _rows(xc, tr), _rows(lam_adj, tr), _rows(h, tr), _halo_prev(h, tr, C),
                     _whole(b_ra), _whole(b_rx), _whole(lam)],
                    [((T, 2 * C), BF16, (tr, 2 * C), row), ((T, C), F32, (tr, C), row)], accs=[(1, C)] * 3)


def _scan_fwd(a, b):
    T, C = a.shape
    tt = _tile(T, ROW_TILE)

    def body(a_ref, b_ref, o_ref, c_ref):
        @pl.when(pl.program_id(0) == 0)
        def _():
            c_ref[...] = jnp.zeros_like(c_ref)

        row = lax.broadcasted_iota(jnp.int32, (8, C), 0)

        def step(j, carry):
            sl = pl.ds(pl.multiple_of(j * 8, 8), 8)
            A, B = a_ref[sl, :], b_ref[sl, :]
            for d in (1, 2, 4):
                ok = row >= d
                B = jnp.where(ok, A * pltpu.roll(B, d, 0) + B, B)
                A = jnp.where(ok, A * pltpu.roll(A, d, 0), A)
            h = A * carry + B
            o_ref[sl, :] = h
            return jnp.sum(jnp.where(row == 7, h, 0.0), axis=0, keepdims=True)

        c_ref[0:1, :] = lax.fori_loop(0, tt // 8, step, c_ref[0:1, :])

    spec = pl.BlockSpec((tt, C), lambda i: (i, 0))
    return pl.pallas_call(body, name="lru_scan", grid=(T // tt,), in_specs=[spec, spec], out_specs=spec,
                          out_shape=jax.ShapeDtypeStruct((T, C), F32), scratch_shapes=[pltpu.VMEM((8, C), F32)],
                          compiler_params=_cparams(("arbitrary",)))(a, b)


def _scan_bwd(a, dh):
    T, C = a.shape
    tt = _tile(T, ROW_TILE)
    nt = T // tt

    def body(a_ref, d_ref, o_ref, c_ref):
        @pl.when(pl.program_id(0) == 0)
        def _():
            c_ref[...] = jnp.zeros_like(c_ref)

        row = lax.broadcasted_iota(jnp.int32, (8, C), 0)

        def step(jj, carry):
            adj_next, a_next = carry
            j = tt // 8 - 1 - jj
            sl = pl.ds(pl.multiple_of(j * 8, 8), 8)
            a_blk = a_ref[sl, :]
            A = jnp.where(row < 7, pltpu.roll(a_blk, 7, 0), a_next)
            B = d_ref[sl, :]
            for d in (1, 2, 4):
                ok = row < 8 - d
                B = jnp.where(ok, A * pltpu.roll(B, 8 - d, 0) + B, B)
                A = jnp.where(ok, A * pltpu.roll(A, 8 - d, 0), A)
            adj = A * adj_next + B
            o_ref[sl, :] = adj
            first = lambda v: jnp.sum(jnp.where(row == 0, v, 0.0), axis=0, keepdims=True)
            return first(adj), first(a_blk)

        adj0, a0 = lax.fori_loop(0, tt // 8, step, (c_ref[0:1, :], c_ref[1:2, :]))
        c_ref[0:1, :] = adj0
        c_ref[1:2, :] = a0

    spec = pl.BlockSpec((tt, C), lambda i: (nt - 1 - i, 0))
    return pl.pallas_call(body, name="lru_scan_bwd", grid=(nt,), in_specs=[spec, spec], out_specs=spec,
                          out_shape=jax.ShapeDtypeStruct((T, C), F32), scratch_shapes=[pltpu.VMEM((8, C), F32)],
                          compiler_params=_cparams(("arbitrary",)))(a, dh)


def _lru_out_fwd(h, xg):
    T, C = h.shape
    tr = _tile(T, ROW_TILE)
    (y,) = _rowwise("lru_out", T // tr, lambda i, h, gb: (h * _gelu_tanh(gb),), [_rows(h, tr), _rows(xg, tr, C, 1)],
                    [((T, C), BF16, (tr, C), lambda i: (i, 0))])
    return y


def _lru_out_bwd(dy, h, xg):
    T, C = h.shape
    tr = _tile(T, ROW_TILE)

    def fn(i, dy, h, gb):
        _, vjp = jax.vjp(lambda h, gb: h * _gelu_tanh(gb), h, gb)
        return vjp(dy)

    row = lambda i: (i, 0)
    return _rowwise("lru_out_bwd", T // tr, fn, [_rows(dy, tr), _rows(h, tr), _rows(xg, tr, C, 1)],
                    [((T, C), F32, (tr, C), row), ((T, C), BF16, (tr, C), row)])


class _Sharded:
    def __init__(self, kind, size, off=0, width=None):
        self.kind, self.size, self.off, self.width = kind, size, off, width

    def slot(self, cx, cy):
        return (2 * cy + cx) if self.kind == "perm" else (2 * cx + cy)

    def at(self, ref, cx, cy, layers, half=None):
        s = self.slot(cx, cy)
        start = s * self.size
        if not isinstance(start, int):
            start = pl.multiple_of(start, 8 if self.kind in ("rows", "rarx") else 128)
        if self.kind in ("cols", "perm"):
            n = ref.shape[1]
            rows = slice(None) if half is None else pl.ds(pl.multiple_of(half * (n // 2), 8), n // 2)
            return ref.at[layers, rows, pl.ds(start, self.size)]
        if self.kind == "rows":
            if half is None:
                return ref.at[layers, pl.ds(start, self.size), :]
            return ref.at[layers, pl.ds(pl.multiple_of(start + half * (self.size // 2), 8), self.size // 2), :]
        if self.kind == "rarx":
            n = ref.shape[1]
            blocks = slice(None) if half is None else pl.ds(half * (n // 2), n // 2)
            return ref.at[layers, blocks, pl.ds(start, self.size), pl.ds(self.off, self.width)]
        raise ValueError(self.kind)


def _mesh_pos():
    return lax.axis_index("x"), lax.axis_index("y"), lax.axis_index("c")


def _peer_chips(x, y):
    return [(1 - x, y), (x, 1 - y), (1 - x, 1 - y)]


def _place(shard, out_shape, out_dtype, sh, pos, prev=None):
    def body(p_ref, x_ref, *rest):
        rest[-1][...] = x_ref[...].astype(rest[-1].dtype)

    if sh.kind in ("cols", "perm"):
        L, R, Ns = shard.shape
        tr = _tile(R, ROW_TILE)
        k = 1 if sh.kind == "perm" else 0
        grid = (L, R // tr)
        ispec = pl.BlockSpec((None, tr, Ns), lambda l, i, p: (l, i, 0))
        ospec = pl.BlockSpec((None, tr, Ns), lambda l, i, p: (l, i, p[k]))
    elif sh.kind == "rows":
        L, Rs, D = shard.shape
        tr = _tile(Rs, ROW_TILE)
        nt = Rs // tr
        grid = (L, nt)
        ispec = pl.BlockSpec((None, tr, D), lambda l, i, p: (l, i, 0))
        ospec = pl.BlockSpec((None, tr, D), lambda l, i, p: (l, p[0] * nt + i, 0))
    else:
        L, nb, Rs, Wd = shard.shape
        cb = sh.off // Wd
        grid = (L, nb)
        ispec = pl.BlockSpec((None, None, Rs, Wd), lambda l, i, p: (l, i, 0, 0))
        ospec = pl.BlockSpec((None, None, Rs, Wd), lambda l, i, p: (l, i, p[0], cb))
    in_specs = [ispec]
    operands = [pos, shard]
    alias = {}
    if prev is not None:
        in_specs.append(pl.BlockSpec(memory_space=pl.ANY))
        operands.append(prev)
        alias = {2: 0}
    gs = pltpu.PrefetchScalarGridSpec(num_scalar_prefetch=1, grid=grid, in_specs=in_specs, out_specs=ospec)
    return pl.pallas_call(body, name="weight_place", grid_spec=gs, out_shape=jax.ShapeDtypeStruct(out_shape, out_dtype),
                          input_output_aliases=alias, compiler_params=_cparams(("arbitrary", "arbitrary")))(*operands)


def _gather_over_ici(jobs, base=0):
    assert len({id(j[0]) for j in jobs}) == len(jobs)

    def copies(refs, send, recv):
        x, y, c = _mesh_pos()
        out = []
        for t, (_, sh, layer) in enumerate(jobs):
            lay = pl.ds(layer, 1)
            mine = sh.at(refs[t], x, y, lay, c)
            for j, (px, py) in enumerate(_peer_chips(x, y)):
                theirs = sh.at(refs[t], px, py, lay, c)
                k = base + 3 * t + j
                out.append((pltpu.make_async_remote_copy(mine, mine, send.at[k], recv.at[k],
                                                         device_id=(px, py, c), device_id_type=MESH),
                            pltpu.make_async_remote_copy(theirs, theirs, send.at[k], recv.at[k],
                                                         device_id=(px, py, c), device_id_type=MESH)))
        return out

    def start(refs, send, recv):
        for out_cp, _ in copies(refs, send, recv):
            out_cp.start()

    def finish(refs, send, recv):
        for out_cp, in_cp in copies(refs, send, recv):
            in_cp.wait_recv()
            out_cp.wait_send()

    return _SideComm([j[0] for j in jobs], 3 * len(jobs), start, finish)


def _pass_to_sibling(jobs, base=0):
    assert len({id(j[0]) for j in jobs}) == len(jobs)

    def copies(refs, send, recv):
        x, y, c = _mesh_pos()
        out = []
        for t, (_, sh, layer) in enumerate(jobs):
            lay = pl.ds(layer, 1)
            for j, (px, py) in enumerate(_peer_chips(x, y)):
                got = sh.at(refs[t], px, py, lay, c)
                coming = sh.at(refs[t], px, py, lay, 1 - c)
                k = base + 3 * t + j
                out.append((pltpu.make_async_remote_copy(got, got, send.at[k], recv.at[k],
                                                         device_id=(x, y, 1 - c), device_id_type=MESH),
                            pltpu.make_async_remote_copy(coming, coming, send.at[k], recv.at[k],
                                                         device_id=(x, y, 1 - c), device_id_type=MESH)))
        return out

    def start(refs, send, recv):
        for out_cp, _ in copies(refs, send, recv):
            out_cp.start()

    def finish(refs, send, recv):
        for out_cp, in_cp in copies(refs, send, recv):
            in_cp.wait_recv()
            out_cp.wait_send()

    return _SideComm([j[0] for j in jobs], 3 * len(jobs), start, finish)


def _all_gather(jobs):
    n = len(jobs)
    ici, d2d = _gather_over_ici(jobs), _pass_to_sibling(jobs, 3 * n)

    def body(*refs):
        outs = refs[n:2 * n]
        send, recv = refs[2 * n:]
        ici.start(outs, send, recv)
        ici.finish(outs, send, recv)
        d2d.start(outs, send, recv)
        d2d.finish(outs, send, recv)

    hbm = pl.BlockSpec(memory_space=pl.ANY)
    return pl.pallas_call(
        body, name="weights_all_gather", in_specs=[hbm] * n, out_specs=[hbm] * n,
        out_shape=[jax.ShapeDtypeStruct(j[0].shape, j[0].dtype) for j in jobs],
        input_output_aliases={t: t for t in range(n)},
        scratch_shapes=[pltpu.SemaphoreType.DMA((6 * n,)), pltpu.SemaphoreType.DMA((6 * n,))],
    )(*[j[0] for j in jobs])


def _half_view(sh, ref, h):
    if sh.kind == "rows":
        n = ref.shape[-1]
        return ref.at[:, pl.ds(pl.multiple_of(h * (n // 2), 128), n // 2)]
    n = ref.shape[0]
    return ref.at[pl.ds(h * (n // 2), n // 2)]


def _half_shape(sh, layer_shape):
    s = list(layer_shape)
    s[len(s) - 1 if sh.kind == "rows" else 0] //= 2
    return tuple(s)


def _chip_part(sh, ref, cx, cy):
    start = sh.slot(cx, cy) * sh.size
    if sh.kind in ("cols", "perm"):
        return ref.at[:, pl.ds(pl.multiple_of(start, 128), sh.size)]
    if sh.kind == "rows":
        return ref.at[pl.ds(pl.multiple_of(start, 8), sh.size), :]
    return ref.at[:, pl.ds(pl.multiple_of(start, 8), sh.size), :]


def _chip_part_shape(sh, half_shape):
    s = list(half_shape)
    s[{"cols": len(s) - 1, "perm": len(s) - 1, "rows": 0, "rarx": 1}[sh.kind]] = sh.size
    return tuple(s)


def _swap_stage(jobs):
    assert len({id(j[0]) for j in jobs}) == len(jobs)

    def make(base):
        arrays = []
        for g, sh, _ in jobs:
            arrays += [g, jax.ShapeDtypeStruct(_half_shape(sh, g.shape[1:]), g.dtype)]

        def copies(refs, send, recv):
            x, y, c = _mesh_pos()
            return [pltpu.make_async_remote_copy(_half_view(sh, refs[2 * t].at[layer], 1 - c), refs[2 * t + 1],
                                                 send.at[base + t], recv.at[base + t],
                                                 device_id=(x, y, 1 - c), device_id_type=MESH)
                    for t, (_, sh, layer) in enumerate(jobs)]

        def start(refs, send, recv):
            for cp in copies(refs, send, recv):
                cp.start()

        def finish(refs, send, recv):
            for cp in copies(refs, send, recv):
                cp.wait()

        return _SideComm(arrays, len(jobs), start, finish)
    return make


def _exchange_stage(jobs):
    n = len(jobs)

    def make(base):
        arrays = []
        for p, sh in jobs:
            arrays += [p, jax.ShapeDtypeStruct((N_CHIPS,) + _chip_part_shape(sh, p.shape), p.dtype)]

        def copies(refs, send, recv):
            x, y, c = _mesh_pos()
            me = 2 * x + y
            out = []
            for t, (_, sh) in enumerate(jobs):
                part, land = refs[2 * t], refs[2 * t + 1]
                local = pltpu.make_async_copy(_chip_part(sh, part, x, y), land.at[me], send.at[base + 3 * n + t])
                remote = []
                for j, (px, py) in enumerate(_peer_chips(x, y)):
                    k = base + 3 * t + j
                    src = land.at[2 * px + py]
                    remote.append((pltpu.make_async_remote_copy(_chip_part(sh, part, px, py), land.at[me], send.at[k],
                                                                recv.at[k], device_id=(px, py, c), device_id_type=MESH),
                                   pltpu.make_async_remote_copy(src, src, send.at[k], recv.at[k],
                                                                device_id=(px, py, c), device_id_type=MESH)))
                out.append((local, remote))
            return out

        def start(refs, send, recv):
            for local, remote in copies(refs, send, recv):
                local.start()
                for out_cp, _ in remote:
                    out_cp.start()

        def finish(refs, send, recv):
            for local, remote in copies(refs, send, recv):
                for out_cp, in_cp in remote:
                    in_cp.wait_recv()
                    out_cp.wait_send()
                local.wait()

        return _SideComm(arrays, 4 * n, start, finish)
    return make


def _share_stage(jobs):
    assert len({id(j[0]) for j in jobs}) == len(jobs)

    def make(base):
        def copies(refs, send, recv):
            x, y, c = _mesh_pos()
            out = []
            for t, (_, sh, layer) in enumerate(jobs):
                mine = _half_view(sh, refs[t].at[layer], c)
                theirs = _half_view(sh, refs[t].at[layer], 1 - c)
                out.append((pltpu.make_async_remote_copy(mine, mine, send.at[base + t], recv.at[base + t],
                                                         device_id=(x, y, 1 - c), device_id_type=MESH),
                            pltpu.make_async_remote_copy(theirs, theirs, send.at[base + t], recv.at[base + t],
                                                         device_id=(x, y, 1 - c), device_id_type=MESH)))
            return out

        def start(refs, send, recv):
            for out_cp, _ in copies(refs, send, recv):
                out_cp.start()

        def finish(refs, send, recv):
            for out_cp, in_cp in copies(refs, send, recv):
                in_cp.wait_recv()
                out_cp.wait_send()

        return _SideComm([j[0] for j in jobs], len(jobs), start, finish)
    return make


def _pair_add_layer(g, sh, layer, got, c_arr):
    if sh.kind == "rows":
        R, Ch = got.shape
        tr = _tile(R, ROW_TILE)
        ins = [(g, (None, tr, Ch), lambda i, p: (layer, i, p[0])), (got, (tr, Ch), lambda i, p: (i, 0))]
        nt, g2 = R // tr, got
    else:
        C = got.shape[-1]
        g2 = got.reshape(-1, C)
        R = g2.shape[0]
        tr = _tile(R, 256 if C > 2048 else ROW_TILE)
        nt = R // tr
        ins = [(g.reshape(g.shape[0], -1, C), (None, tr, C), lambda i, p: (layer, p[0] * nt + i, 0)),
               (g2, (tr, C), lambda i, p: (i, 0))]
    (out,) = _rowwise("grad_pair_add", nt, lambda i, a, b: (a.astype(F32) + b.astype(F32),), ins,
                      [(g2.shape, got.dtype, ins[1][1], lambda i, p: (i, 0))], prefetch=c_arr)
    return out.reshape(got.shape)


def _chip_sum_layer(land, sh, layer, tot, c_arr):
    def fn(i, blk):
        v = blk.astype(F32)
        return (((v[0] + v[1]) + v[2]) + v[3],)

    if sh.kind == "rows":
        _, R, Ch = land.shape
        tr = _tile(R, ROW_TILE)
        ins = [(land, (N_CHIPS, tr, Ch), lambda i, p: (0, i, 0))]
        outs = [(tot.shape, F32, (None, tr, Ch), lambda i, p: (layer, i, p[0]))]
        (out,) = _rowwise("grad_chip_sum", R // tr, fn, ins, outs, prefetch=c_arr, into=tot)
        return out
    C = land.shape[-1]
    l3 = land.reshape(N_CHIPS, -1, C)
    R = l3.shape[1]
    tr = _tile(R, ROW_TILE)
    nt = R // tr
    t3 = tot.reshape(tot.shape[0], -1, C)
    (out,) = _rowwise("grad_chip_sum", nt, fn, [(l3, (N_CHIPS, tr, C), lambda i, p: (0, i, 0))],
                      [(t3.shape, F32, (None, tr, C), lambda i, p: (layer, p[0] * nt + i, 0))], prefetch=c_arr, into=t3)
    return out.reshape(tot.shape)


def _flat2(shape):
    return (math.prod(shape[:-1]), shape[-1])


def _adamw(w, g, m, v):
    shape = w.shape
    R, C = _flat2(shape)
    tr = _tile(R, 256 if R % 256 == 0 else R)

    def fn(i, w, g, m, v):
        m2 = ADAM_B1 * m + (1.0 - ADAM_B1) * g
        v2 = ADAM_B2 * v + (1.0 - ADAM_B2) * (g * g)
        m_hat = m2 / (1.0 - ADAM_B1 ** ADAM_STEP)
        v_hat = v2 / (1.0 - ADAM_B2 ** ADAM_STEP)
        delta = -ADAM_LR * (m_hat / (jnp.sqrt(v_hat) + ADAM_EPS) + ADAM_WD * w)
        return delta, m2, v2

    row = lambda i: (i, 0)
    f2 = lambda a: a.reshape(R, C)
    outs = _rowwise("adamw", R // tr, fn, [_rows(f2(a), tr) for a in (w, g, m, v)], [((R, C), F32, (tr, C), row)] * 3)
    return [o.reshape(shape) for o in outs]


def _loss_and_grad(h, target):
    T, D = h.shape
    tr = _tile(T, ROW_TILE)

    def fn(i, h, t):
        err = h - t
        return err * (1.0 / D), jnp.sum(err * err, axis=0, keepdims=True)

    dh, sq = _rowwise("loss", T // tr, fn, [_rows(h, tr), _rows(target, tr)], [((T, D), F32, (tr, D), lambda i: (i, 0))], accs=[(1, D)])
    return dh, (0.5 / D) * jnp.sum(sq)


def kernel(x, ffn1_w_in, ffn1_w_out, ffn2_w_in, ffn2_w_out, ln_g, ln_b, attn_w_qkv, attn_sinks, attn_w_o, lru_w_in, lru_conv_w, lru_conv_b, lru_w_ra, lru_b_ra, lru_w_rx, lru_b_rx, lru_lambda, lru_w_out, loss_target, m_ffn1_w_in, m_ffn1_w_out, m_ffn2_w_in, m_ffn2_w_out, m_ln_g, m_ln_b, m_attn_w_qkv, m_attn_sinks, m_attn_w_o, m_lru_w_in, m_lru_conv_w, m_lru_conv_b, m_lru_w_ra, m_lru_b_ra, m_lru_w_rx, m_lru_b_rx, m_lru_lambda, m_lru_w_out, v_ffn1_w_in, v_ffn1_w_out, v_ffn2_w_in, v_ffn2_w_out, v_ln_g, v_ln_b, v_attn_w_qkv, v_attn_sinks, v_attn_w_o, v_lru_w_in, v_lru_conv_w, v_lru_conv_b, v_lru_w_ra, v_lru_b_ra, v_lru_w_rx, v_lru_b_rx, v_lru_lambda, v_lru_w_out):
    names = ["ffn1_w_in", "ffn1_w_out", "ffn2_w_in", "ffn2_w_out", "ln_g", "ln_b", "attn_w_qkv", "attn_sinks", "attn_w_o",
             "lru_w_in", "lru_conv_w", "lru_conv_b", "lru_w_ra", "lru_b_ra", "lru_w_rx", "lru_b_rx", "lru_lambda", "lru_w_out"]
    W = dict(zip(names, [ffn1_w_in, ffn1_w_out, ffn2_w_in, ffn2_w_out, ln_g, ln_b, attn_w_qkv, attn_sinks, attn_w_o,
                         lru_w_in, lru_conv_w, lru_conv_b, lru_w_ra, lru_b_ra, lru_w_rx, lru_b_rx, lru_lambda, lru_w_out]))
    M = dict(zip(names, [m_ffn1_w_in, m_ffn1_w_out, m_ffn2_w_in, m_ffn2_w_out, m_ln_g, m_ln_b, m_attn_w_qkv, m_attn_sinks, m_attn_w_o,
                         m_lru_w_in, m_lru_conv_w, m_lru_conv_b, m_lru_w_ra, m_lru_b_ra, m_lru_w_rx, m_lru_b_rx, m_lru_lambda, m_lru_w_out]))
    V = dict(zip(names, [v_ffn1_w_in, v_ffn1_w_out, v_ffn2_w_in, v_ffn2_w_out, v_ln_g, v_ln_b, v_attn_w_qkv, v_attn_sinks, v_attn_w_o,
                         v_lru_w_in, v_lru_conv_w, v_lru_conv_b, v_lru_w_ra, v_lru_b_ra, v_lru_w_rx, v_lru_b_rx, v_lru_lambda, v_lru_w_out]))

    T, D = x.shape[1], x.shape[2]
    L = ffn1_w_in.shape[0]
    LA, LR = attn_w_qkv.shape[0], lru_w_in.shape[0]
    N2 = ffn1_w_in.shape[2] * N_CHIPS
    F = N2 // 2
    C = lru_lambda.shape[1] * N_CHIPS
    CW = C // N_CHIPS
    alpha = (2.0 * L) ** 0.25
    c_arr = lax.axis_index("c").astype(jnp.int32).reshape(1)

    n_sink = attn_sinks.size
    assert n_sink <= CW

    up8 = lambda n: -(-n // 8) * 8
    o_g = 0
    o_b = o_g + up8(3 * L)
    o_cw = o_b + up8(3 * L)
    o_cb = o_cw + up8(LR * CONV_W)
    o_ra = o_cb + up8(LR)
    o_rx = o_ra + up8(LR)
    o_lam = o_rx + up8(LR)
    o_sink = o_lam + up8(LR)
    assert o_sink + 8 <= SMALL_ROWS

    def pack_small(d):
        parts = [d["ln_g"].reshape(-1, CW), d["ln_b"].reshape(-1, CW), d["lru_conv_w"].reshape(-1, CW), d["lru_conv_b"],
                 d["lru_b_ra"], d["lru_b_rx"], d["lru_lambda"],
                 jnp.pad(d["attn_sinks"].reshape(1, -1), ((0, 0), (0, CW - n_sink)))]
        parts = [jnp.pad(p, ((0, up8(p.shape[0]) - p.shape[0]), (0, 0))) for p in parts]
        used = sum(p.shape[0] for p in parts)
        return jnp.concatenate(parts + [jnp.zeros((SMALL_ROWS - used, CW), F32)], axis=0)

    cols = lambda a: _Sharded("cols", a.shape[-1])
    rows_ = lambda a: _Sharded("rows", a.shape[-2])
    RW = lru_w_ra.shape[2]
    sh_list = [_Sharded("perm", N2 // 4), rows_(ffn1_w_out), _Sharded("perm", N2 // 4), rows_(ffn2_w_out),
               cols(attn_w_qkv), rows_(attn_w_o), cols(lru_w_in), rows_(lru_w_out)]
    big = [ffn1_w_in, ffn1_w_out, ffn2_w_in, ffn2_w_out, attn_w_qkv, attn_w_o, lru_w_in, lru_w_out]
    blk_w = lru_w_ra.shape[3]
    small_sh = _Sharded("cols", CW)
    full = lambda a, sh: tuple(a.shape[:-1]) + (a.shape[-1] * N_CHIPS,) if sh.kind in ("cols", "perm") else \
        tuple(a.shape[:-2]) + (a.shape[-2] * N_CHIPS, a.shape[-1])
    mx, my = lax.axis_index("x"), lax.axis_index("y")
    pos = jnp.stack([2 * mx + my, 2 * my + mx]).astype(jnp.int32)
    placed = [_place(a, full(a, sh), BF16, sh, pos) for a, sh in zip(big, sh_list)]
    rarx_shape = (LR, RNN_BLOCKS, RW * N_CHIPS, 2 * blk_w)
    p_ra = _place(lru_w_ra, rarx_shape, BF16, _Sharded("rarx", RW, 0, blk_w), pos)
    placed.append(_place(lru_w_rx, rarx_shape, BF16, _Sharded("rarx", RW, blk_w, blk_w), pos, prev=p_ra))
    gnames = ["w_in1", "w_out1", "w_in2", "w_out2", "qkv", "wo", "lin", "lout", "rarx"]
    GW = dict(zip(gnames, placed))
    SH = dict(zip(gnames, sh_list + [_Sharded("rarx", RW, 0, 2 * blk_w)]))
    p_small = _place(pack_small(W).reshape(1, SMALL_ROWS, CW), (1, SMALL_ROWS, C), F32, small_sh, pos)

    def comm_for(stage, jobs):
        return stage([(GW[n], SH[n], l) for n, l in jobs]) if jobs else None

    def rebind(jobs, arrays):
        for (n, _), a in zip(jobs, arrays):
            GW[n] = a

    jobs0 = [(n, 0) for n in ("w_in1", "w_out1", "w_in2", "w_out2", "qkv", "wo")]
    got0 = _all_gather([(GW[n], SH[n], l) for n, l in jobs0] + [(p_small, small_sh, 0)])
    rebind(jobs0, got0)
    small = got0[-1].reshape(SMALL_ROWS, C)
    row_of = lambda r: small[r:r + 1]
    mixer_jobs = lambda l: [("qkv", l // 2), ("wo", l // 2)] if l % 2 == 0 else [("lin", l // 2), ("lout", l // 2), ("rarx", l // 2)]

    def ffn_forward(h, hb, n_in, n_out, layer, g, b, jobs):
        gu, act, carried = _ffn_up(hb, GW[n_in], layer, comm_for(_gather_over_ici, jobs))
        rebind(jobs, carried)
        res = _proj_ln("ffn_down_ln", act, GW[n_out], layer, h, g, b, alpha, 0.5, comm_for(_pass_to_sibling, jobs))
        if jobs:
            res, carried = res
            rebind(jobs, carried)
        return res, (gu, act)
    assert D == C, "packed small parameters assume d_model == d_rnn"

    c128, s128 = _rope_tables(T)
    sink_rows = [attn_sinks[j:j + 1] for j in range(LA)]

    h = x.reshape(T, D)
    hb = h.astype(BF16)
    saved = []
    for i in range(L):
        j = i // 2
        lay = {}
        lay["hb0"] = hb
        nxt = i + 1
        jobs1 = [("w_in2", nxt), ("w_out2", nxt)] if nxt < L else []
        jobs2 = [("w_in1", nxt), ("w_out1", nxt)] + mixer_jobs(nxt) if nxt < L else []
        (z, h, hb), (lay["gu1"], lay["act1"]) = ffn_forward(h, hb, "w_in1", "w_out1", i, row_of(o_g + 3 * i), row_of(o_b + 3 * i), jobs1)
        lay["z1"], lay["hb1"] = z, hb
        if i % 2 == 0:
            qkv = _qkv_rope(hb, GW["qkv"], j, c128, s128)
            o = _attn_fwd(qkv, sink_rows[j])
            lay["qkv"], lay["o"] = qkv, o
            z, h, hb = _proj_ln("attn_out_ln", o, GW["wo"], j, h, row_of(o_g + 3 * i + 1), row_of(o_b + 3 * i + 1), alpha, 1.0)
        else:
            bm = _tile(T, ROW_TILE)
            (xg,) = _matmul("lru_in", (T // bm, 1, 1), (hb, (bm, D), lambda p, q, k: (p, 0)),
                            (GW["lin"], (None, D, 2 * C), lambda p, q, k, j=j: (j, 0, 0)),
                            [((T, 2 * C), F32, (bm, 2 * C), lambda p, q: (p, 0))], (bm, 2 * C))
            cw = [row_of(o_cw + j * CONV_W + k) for k in range(CONV_W)]
            xc, xcb = _conv_fwd(xg, cw, row_of(o_cb + j))
            pre, a, b = _lru_gate_fwd(xc, xcb, GW["rarx"], j, row_of(o_ra + j), row_of(o_rx + j), row_of(o_lam + j))
            hs = _scan_fwd(a, b)
            y = _lru_out_fwd(hs, xg)
            lay.update(xg=xg, xc=xc, xcb=xcb, pre=pre, a=a, hs=hs, y=y, cw=cw)
            z, h, hb = _proj_ln("lru_out_ln", y, GW["lout"], j, h, row_of(o_g + 3 * i + 1), row_of(o_b + 3 * i + 1), alpha, 1.0)
        lay["z2"], lay["hb2"] = z, hb
        (z, h, hb), (lay["gu2"], lay["act2"]) = ffn_forward(h, hb, "w_in2", "w_out2", i, row_of(o_g + 3 * i + 2), row_of(o_b + 3 * i + 2), jobs2)
        lay["z3"] = z
        saved.append(lay)
    g_w_in1, g_w_out1, g_w_in2, g_w_out2, g_qkv, g_wo, g_lin, g_lout, g_rarx = [GW[n] for n in gnames]

    dh, loss_local = _loss_and_grad(h, loss_target.reshape(T, D))
    loss = lax.psum(loss_local, ("x", "y", "c"))

    SH2 = dict(SH, small=small_sh)
    shard_stack = {"w_in1": ffn1_w_in.shape, "w_out1": ffn1_w_out.shape, "w_in2": ffn2_w_in.shape, "w_out2": ffn2_w_out.shape,
                   "qkv": attn_w_qkv.shape, "wo": attn_w_o.shape, "lin": lru_w_in.shape, "lout": lru_w_out.shape,
                   "rarx": (LR, RNN_BLOCKS, RW, 2 * blk_w), "small": (1, SMALL_ROWS, CW)}
    TOT = {n: jnp.zeros(s, F32) for n, s in shard_stack.items()}
    pend_exchange, pend_share = [], []

    def swap_share_comm(swap_jobs):
        share_jobs = list(pend_share)
        pend_share.clear()
        makers = []
        if swap_jobs:
            makers.append(_swap_stage([(g, SH2[n], 0) for n, _, g in swap_jobs]))
        if share_jobs:
            makers.append(_share_stage([(TOT[n], SH2[n], l) for n, l in share_jobs]))
        comm, _ = _combine(makers)

        def after(carried):
            for t, (n, l, _) in enumerate(swap_jobs):
                part = _pair_add_layer(carried[2 * t], SH2[n], 0, carried[2 * t + 1], c_arr)
                pend_exchange.append((n, l, part))
            for t, (n, _) in enumerate(share_jobs):
                TOT[n] = carried[2 * len(swap_jobs) + t]
        return comm, after

    def exchange_comm():
        jobs = list(pend_exchange)
        pend_exchange.clear()
        if not jobs:
            return None, None
        comm, _ = _combine([_exchange_stage([(part, SH2[n]) for n, _, part in jobs])])

        def after(carried):
            for t, (n, l, _) in enumerate(jobs):
                TOT[n] = _chip_sum_layer(carried[2 * t + 1], SH2[n], l, TOT[n], c_arr)
                pend_share.append((n, l))
        return comm, after

    def ffn_backward(dh, z, g, hb_in, gu, act, n_in, n_out, layer, also_swap):
        dz, dyb, dg, db = _ln_bwd("ffn_ln_bwd", dh, z, g, 0.5)
        dgu = _ffn_dact(dyb, GW[n_out], layer, gu)
        g_out = _grad_tn("ffn_dwout", act, dyb, N2 // 4, D)
        comm, after = exchange_comm()
        g_in = _grad_tn("ffn_dwin", hb_in, dgu, D, N2 // 4, comm=comm)
        if comm is not None:
            g_in, carried = g_in
            after(carried)
        comm, after = swap_share_comm([(n_in, layer, g_in), (n_out, layer, g_out)] + also_swap)
        dh_prev, carried = _ffn_dx(dgu, GW[n_in], layer, dz, alpha, comm)
        after(carried)
        return dh_prev, dg, db

    sg = [None] * SMALL_ROWS
    d_sinks = [None] * LA
    for i in reversed(range(L)):
        j = i // 2
        lay = saved[i]
        dh, sg[o_g + 3 * i + 2], sg[o_b + 3 * i + 2] = ffn_backward(
            dh, lay["z3"], row_of(o_g + 3 * i + 2), lay["hb2"], lay["gu2"], lay["act2"], "w_in2", "w_out2", i, [])
        if i % 2 == 0:
            dz, dmb, sg[o_g + 3 * i + 1], sg[o_b + 3 * i + 1] = _ln_bwd("attn_ln_bwd", dh, lay["z2"], row_of(o_g + 3 * i + 1), 1.0)
            d_wo = _grad_tn("attn_dwo", lay["o"], dmb, _tile(Q_COLS, 1024), D)
            do = _back_proj("attn_do", dmb, g_wo, j, BF16)
            dq, dkv, dsk = _attn_bwd(lay["qkv"], do, sink_rows[j])
            d_sinks[j] = jnp.sum(dsk.reshape(N_HEADS, ATTN_BLOCK), axis=1)
            dqkv = _rope_bwd(dq, dkv, c128, s128)
            d_qkv = _grad_tn("attn_dwqkv", lay["hb1"], dqkv, D, dqkv.shape[1])
            dh = _input_grad("attn_dx", dqkv, g_qkv, j, dz, alpha)
            mixer_grads = [("wo", j, d_wo), ("qkv", j, d_qkv)]
        else:
            dz, dmb, sg[o_g + 3 * i + 1], sg[o_b + 3 * i + 1] = _ln_bwd("lru_ln_bwd", dh, lay["z2"], row_of(o_g + 3 * i + 1), 1.0)
            d_lout = _grad_tn("lru_dwout", lay["y"], dmb, C, D)
            dy = _back_proj("lru_dy", dmb, g_lout, j, F32)
            dhs, dgb = _lru_out_bwd(dy, lay["hs"], lay["xg"])
            adj = _scan_bwd(lay["a"], dhs)
            dpre, dxc_direct, sg[o_ra + j], sg[o_rx + j], sg[o_lam + j] = _lru_gate_bwd(
                lay["pre"], lay["xc"], adj, lay["hs"], row_of(o_ra + j), row_of(o_rx + j), row_of(o_lam + j))
            blk = C // RNN_BLOCKS
            d_rarx = _grad_tn("lru_dwgates", lay["xcb"], dpre, blk, 2 * blk, block_diag=True)
            bm = _tile(T, ROW_TILE)
            (dxc,) = _matmul("lru_dxc", (T // bm, RNN_BLOCKS, 1), (dpre, (bm, 2 * blk), lambda p, q, k: (p, q)),
                             (g_rarx, (None, None, blk, 2 * blk), lambda p, q, k, j=j: (j, q, 0, 0)),
                             [((T, C), F32, (bm, blk), lambda p, q: (p, q))], (bm, blk), tb=True,
                             extras=[(dxc_direct, (bm, blk), lambda p, q: (p, q))], epilogue=lambda acc, d: (acc + d,))
            res = _conv_bwd(dxc, lay["xg"], dgb, lay["cw"])
            dxg = res[0]
            for k in range(CONV_W):
                sg[o_cw + j * CONV_W + k] = res[1 + k]
            sg[o_cb + j] = res[1 + CONV_W]
            d_lin = _grad_tn("lru_dwin", lay["hb1"], dxg, D, _tile(2 * C, 1024))
            dh = _input_grad("lru_dx", dxg, g_lin, j, dz, alpha)
            mixer_grads = [("lout", j, d_lout), ("rarx", j, d_rarx), ("lin", j, d_lin)]
        dh, sg[o_g + 3 * i], sg[o_b + 3 * i] = ffn_backward(
            dh, lay["z1"], row_of(o_g + 3 * i), lay["hb0"], lay["gu1"], lay["act1"], "w_in1", "w_out1", i, mixer_grads)
    grad_x = dh.reshape(x.shape)

    sink_vec = jnp.concatenate(d_sinks).reshape(1, n_sink)
    sg[o_sink] = jnp.tile(jnp.concatenate([sink_vec, jnp.zeros((1, CW - n_sink), F32)], axis=1), (1, N_CHIPS))
    zero_row = jnp.zeros((1, C), F32)
    d_small = jnp.concatenate([zero_row if r is None else r for r in sg], axis=0).reshape(1, SMALL_ROWS, C)

    comm, after = swap_share_comm([("small", 0, d_small)])
    after(_comm_call("grad_tail_swap", comm))
    comm, after = exchange_comm()
    after(_comm_call("grad_tail_exchange", comm))
    comm, after = swap_share_comm([])
    after(_comm_call("grad_tail_share", comm))
    t_w_in1, t_w_out1, t_w_in2, t_w_out2, t_qkv, t_wo, t_lin, t_lout, t_rarx = [TOT[n] for n in gnames]
    t_small = TOT["small"].reshape(SMALL_ROWS, CW)

    G = {"ffn1_w_in": t_w_in1, "ffn1_w_out": t_w_out1, "ffn2_w_in": t_w_in2, "ffn2_w_out": t_w_out2,
         "attn_w_qkv": t_qkv, "attn_w_o": t_wo, "lru_w_in": t_lin, "lru_w_out": t_lout,
         "lru_w_ra": t_rarx[..., :blk_w], "lru_w_rx": t_rarx[..., blk_w:]}

    def unpack_small(p):
        return {"ln_g": p[o_g:o_g + 3 * L].reshape(ln_g.shape), "ln_b": p[o_b:o_b + 3 * L].reshape(ln_b.shape),
                "lru_conv_w": p[o_cw:o_cw + LR * CONV_W].reshape(lru_conv_w.shape), "lru_conv_b": p[o_cb:o_cb + LR],
                "lru_b_ra": p[o_ra:o_ra + LR], "lru_b_rx": p[o_rx:o_rx + LR], "lru_lambda": p[o_lam:o_lam + LR],
                "attn_sinks": p[o_sink, :n_sink].reshape(attn_sinks.shape)}

    G.update(unpack_small(t_small))

    delta, new_m, new_v = {}, {}, {}
    small_names = ["ln_g", "ln_b", "lru_conv_w", "lru_conv_b", "lru_b_ra", "lru_b_rx", "lru_lambda", "attn_sinks"]
    for n in names:
        if n not in small_names:
            delta[n], new_m[n], new_v[n] = _adamw(W[n], G[n], M[n], V[n])
    ds, ms, vs = _adamw(pack_small(W), t_small, pack_small(M), pack_small(V))
    for d, p in ((delta, ds), (new_m, ms), (new_v, vs)):
        d.update(unpack_small(p))

    return (loss, grad_x, *[G[n] for n in names], *[delta[n] for n in names], *[new_m[n] for n in names], *[new_v[n] for n in names])
```

```python
import functools
import math

import jax
import jax.numpy as jnp
from jax import lax
from jax.experimental import pallas as pl
from jax.experimental.pallas import tpu as pltpu

F32 = jnp.float32
BF16 = jnp.bfloat16
MESH = pl.DeviceIdType.MESH

N_HEADS = 16
N_KV_HEADS = 4
HEAD_DIM = 64
GROUP = N_HEADS // N_KV_HEADS
ATTN_BLOCK = 128
ROPE_THETA = 10000.0
RNN_BLOCKS = 4
CONV_W = 4
LRU_C = 8.0
LN_EPS = 1e-5
ADAM_LR = 0.001
ADAM_B1 = 0.9
ADAM_B2 = 0.999
ADAM_EPS = 1e-08
ADAM_WD = 0.01
ADAM_STEP = 10
N_CHIPS = 4
NEG_BIG = -1e30
VMEM_LIMIT_MB = 56
ROW_TILE = 512
SMALL_ROWS = 96


def _cparams(sem):
    return pltpu.CompilerParams(dimension_semantics=sem, vmem_limit_bytes=VMEM_LIMIT_MB << 20)


def _tile(n, pref):
    if n <= pref:
        return n
    for t in range(pref - pref % 16, 0, -16):
        if n % t == 0:
            return t
    raise ValueError((n, pref))


class _SideComm:
    def __init__(self, arrays, n_sems, start, finish):
        self.arrays, self.n_sems, self.start, self.finish = list(arrays), n_sems, start, finish


def _hosted_call(body, comm, *, name, grid, in_specs, out_specs, out_shape, operands, scratch_shapes=(),
                 input_output_aliases=None, compiler_params=None):
    aliases = dict(input_output_aliases or {})
    in_specs, out_specs, out_shape = list(in_specs), list(out_specs), list(out_shape)
    operands, scratch_shapes = list(operands), list(scratch_shapes)
    n_in, n_out = len(in_specs), len(out_specs)
    if comm is None:
        res = pl.pallas_call(body, name=name, grid=grid, in_specs=in_specs, out_specs=out_specs, out_shape=out_shape,
                             scratch_shapes=scratch_shapes, input_output_aliases=aliases,
                             compiler_params=compiler_params)(*operands)
        return list(res), []
    m = len(comm.arrays)
    hbm = pl.BlockSpec(memory_space=pl.ANY)
    real = [t for t, arr in enumerate(comm.arrays) if not isinstance(arr, jax.ShapeDtypeStruct)]
    mi = len(real)
    for ti, t in enumerate(real):
        aliases[n_in + ti] = n_out + t
    n_scr = len(scratch_shapes)

    def hosted(*refs):
        ins = refs[:n_in]
        outs = refs[n_in + mi:n_in + mi + n_out]
        carried = refs[n_in + mi + n_out:n_in + mi + n_out + m]
        scr = refs[n_in + mi + n_out + m:n_in + mi + n_out + m + n_scr]
        send, recv = refs[-2:]
        first = functools.reduce(jnp.logical_and, [pl.program_id(d) == 0 for d in range(len(grid))])
        last = functools.reduce(jnp.logical_and, [pl.program_id(d) == grid[d] - 1 for d in range(len(grid))])

        @pl.when(first)
        def _():
            comm.start(carried, send, recv)

        body(*ins, *outs, *scr)

        @pl.when(last)
        def _():
            comm.finish(carried, send, recv)

    res = pl.pallas_call(
        hosted, name=name, grid=grid, in_specs=in_specs + [hbm] * mi, out_specs=out_specs + [hbm] * m,
        out_shape=out_shape + [jax.ShapeDtypeStruct(a.shape, a.dtype) for a in comm.arrays],
        scratch_shapes=scratch_shapes + [pltpu.SemaphoreType.DMA((comm.n_sems,)), pltpu.SemaphoreType.DMA((comm.n_sems,))],
        input_output_aliases=aliases, compiler_params=compiler_params)(*operands, *[comm.arrays[t] for t in real])
    return list(res[:n_out]), list(res[n_out:])


def _combine(makers):
    comms, base = [], 0
    for mk in makers:
        comms.append(mk(base))
        base += comms[-1].n_sems
    offs = [0]
    for cm in comms:
        offs.append(offs[-1] + len(cm.arrays))

    def start(refs, send, recv):
        for cm, o in zip(comms, offs):
            cm.start(refs[o:o + len(cm.arrays)], send, recv)

    def finish(refs, send, recv):
        for cm, o in zip(comms, offs):
            cm.finish(refs[o:o + len(cm.arrays)], send, recv)

    return _SideComm(sum((cm.arrays for cm in comms), []), base, start, finish), [len(cm.arrays) for cm in comms]


def _comm_call(name, comm):
    def body():
        pass

    _, carried = _hosted_call(body, comm, name=name, grid=(1,), in_specs=[], out_specs=[], out_shape=[], operands=[])
    return carried


def _matmul(name, grid, a, b, outs, acc_shape, *, ta=False, tb=False, extras=(), epilogue=None,
            n_outer=False, alias_in=None, comm=None):
    gm, gn, gk = grid
    if n_outer:
        g = (gn, gm, gk)
        ijk = lambda p, q, k: (q, p, k)
    else:
        g = (gm, gn, gk)
        ijk = lambda p, q, k: (p, q, k)
    w3 = lambda f: (lambda p, q, k: f(*ijk(p, q, k)))
    w2 = lambda f: (lambda p, q, k: f(*ijk(p, q, k)[:2]))
    in_specs = [pl.BlockSpec(a[1], w3(a[2])), pl.BlockSpec(b[1], w3(b[2]))]
    in_specs += [pl.BlockSpec(e[1], w2(e[2])) for e in extras]
    operands = [a[0], b[0]] + [e[0] for e in extras]
    io_alias = {}
    n_alias = 0
    if alias_in is not None:
        in_specs.append(pl.BlockSpec(memory_space=pl.ANY))
        operands.append(alias_in)
        io_alias = {len(operands) - 1: 0}
        n_alias = 1
    ne, no = len(extras), len(outs)
    dims = (((0 if ta else 1,), (1 if tb else 0,)), ((), ()))

    def body(*refs):
        a_ref, b_ref = refs[0], refs[1]
        e_refs = refs[2:2 + ne]
        o_refs = refs[2 + ne + n_alias:2 + ne + n_alias + no]
        part = lax.dot_general(a_ref[...], b_ref[...], dims, preferred_element_type=F32)

        def finish(acc):
            res = epilogue(acc, *[r[...] for r in e_refs]) if epilogue is not None else (acc,)
            for r, v in zip(o_refs, res):
                r[...] = v.astype(r.dtype)

        if gk == 1:
            finish(part)
        else:
            acc_ref = refs[-1]
            k = pl.program_id(2)

            @pl.when(k == 0)
            def _():
                acc_ref[...] = part

            @pl.when(k > 0)
            def _():
                acc_ref[...] += part

            @pl.when(k == gk - 1)
            def _():
                finish(acc_ref[...])

    res, carried = _hosted_call(
        body, comm, name=name, grid=g, in_specs=in_specs,
        out_specs=[pl.BlockSpec(o[2], w2(o[3])) for o in outs],
        out_shape=[jax.ShapeDtypeStruct(o[0], o[1]) for o in outs],
        operands=operands,
        scratch_shapes=[pltpu.VMEM(acc_shape, F32)] if gk > 1 else [],
        input_output_aliases=io_alias,
        compiler_params=_cparams(("arbitrary", "arbitrary", "arbitrary")))
    return res if comm is None else (res, carried)


def _rowwise(name, nsteps, fn, ins, outs, accs=(), prefetch=None, into=None):
    n_in, n_out, n_acc = len(ins), len(outs), len(accs)
    n_pre = 0 if prefetch is None else 1
    n_into = 0 if into is None else 1

    def body(*refs):
        refs = refs[n_pre:]
        i = pl.program_id(0)
        res = fn(i, *[r[...] for r in refs[:n_in]])
        refs = refs[:n_in] + refs[n_in + n_into:]
        for r, v in zip(refs[n_in:n_in + n_out], res[:n_out]):
            r[...] = v.astype(r.dtype)
        acc_refs = refs[n_in + n_out:n_in + n_out + n_acc]
        if n_acc:
            @pl.when(i == 0)
            def _():
                for r in acc_refs:
                    r[...] = jnp.zeros_like(r)

            for r, v in zip(acc_refs, res[n_out:]):
                r[...] += v

    if prefetch is None:
        zero = lambda shape: (lambda i: (0,) * len(shape))
    else:
        zero = lambda shape: (lambda i, p: (0,) * len(shape))
    in_specs = [pl.BlockSpec(b, m) for _, b, m in ins]
    operands = [x[0] for x in ins]
    alias = {}
    if into is not None:
        in_specs.append(pl.BlockSpec(memory_space=pl.ANY))
        operands.append(into)
        alias = {n_pre + n_in: 0}
    out_specs = [pl.BlockSpec(o[2], o[3]) for o in outs] + [pl.BlockSpec(s, zero(s)) for s in accs]
    out_shape = [jax.ShapeDtypeStruct(o[0], o[1]) for o in outs] + [jax.ShapeDtypeStruct(s, F32) for s in accs]
    cp = _cparams(("arbitrary",))
    if prefetch is None:
        call = pl.pallas_call(body, name=name, grid=(nsteps,), in_specs=in_specs, out_specs=out_specs,
                              out_shape=out_shape, input_output_aliases=alias, compiler_params=cp)
        return call(*operands)
    gs = pltpu.PrefetchScalarGridSpec(num_scalar_prefetch=1, grid=(nsteps,), in_specs=in_specs, out_specs=out_specs)
    call = pl.pallas_call(body, name=name, grid_spec=gs, out_shape=out_shape, input_output_aliases=alias, compiler_params=cp)
    return call(prefetch, *operands)


def _rows(arr, tr, cols=None, cb=0):
    cols = arr.shape[1] if cols is None else cols
    return (arr, (tr, cols), lambda i: (i, cb))


def _whole(arr):
    return (arr, arr.shape, lambda i: (0,) * arr.ndim)


def _gelu_tanh(x):
    c = math.sqrt(2.0 / math.pi)
    return x * (0.5 * (1.0 + jnp.tanh(c * (x + 0.044715 * (x * x * x)))))


@jax.custom_jvp
def _expm1(x):
    return jnp.where(jnp.abs(x) < 0.5, jnp.tanh(0.5 * x) * (jnp.exp(x) + 1.0), jnp.exp(x) - 1.0)


@_expm1.defjvp
def _expm1_jvp(primals, tangents):
    (x,), (t,) = primals, tangents
    return _expm1(x), jnp.exp(x) * t


def _log_sigmoid(x):
    return jnp.minimum(x, 0.0) - jnp.log1p(jnp.exp(-jnp.abs(x)))


def _lru_gates(pre, xc, b_ra, b_rx, lam):
    w = xc.shape[-1]
    r = jax.nn.sigmoid(pre[:, :w] + b_ra)
    ig = jax.nn.sigmoid(pre[:, w:] + b_rx)
    log_a = LRU_C * r * _log_sigmoid(lam)
    a = jnp.exp(log_a)
    b = jnp.sqrt(-_expm1(2.0 * log_a)) * (ig * xc)
    return a, b


def _swap_halves(x):
    n = x.shape[1]
    first = (lax.broadcasted_iota(jnp.int32, x.shape, 1) % HEAD_DIM) < (HEAD_DIM // 2)
    return jnp.where(first, pltpu.roll(x, n - HEAD_DIM // 2, 1), pltpu.roll(x, HEAD_DIM // 2, 1))


def _shift_down(prev8, cur, s):
    ext = jnp.concatenate([prev8, cur], axis=0)
    return pltpu.roll(ext, s, 0)[8:]


def _shift_up(cur, next8, s):
    ext = jnp.concatenate([cur, next8], axis=0)
    return pltpu.roll(ext, ext.shape[0] - s, 0)[:cur.shape[0]]


LANES = 128


def _ln_epilogue(alpha, scale):
    def epi(acc, prev, gp, bp, g, b):
        z = alpha * (prev * gp + bp) + scale * acc
        mu = jnp.mean(z, axis=-1, keepdims=True)
        xc = z - mu
        var = jnp.mean(xc * xc, axis=-1, keepdims=True)
        rstd = lax.rsqrt(var + LN_EPS)
        xhat = xc * rstd
        return xhat, xhat * g + b, jnp.broadcast_to(rstd, (rstd.shape[0], LANES))
    return epi


def _proj_ln(name, act, w, layer, prev, g, b, alpha, scale, comm=None):
    T, K = act.shape
    D = w.shape[2]
    bm = _tile(T, ROW_TILE)
    row = lambda i, j: (i, 0)
    par = lambda i, j: (0, 0)
    return _matmul(
        name, (T // bm, 1, 1),
        (act, (bm, K), lambda i, j, k: (i, 0)), (w, (None, K, D), lambda i, j, k: (layer, 0, 0)),
        [((T, D), F32, (bm, D), row), ((T, D), BF16, (bm, D), row), ((T, LANES), F32, (bm, LANES), row)],
        (bm, D),
        extras=[(prev[0], (bm, D), row), (prev[1], (1, D), par), (prev[2], (1, D), par), (g, (1, D), par), (b, (1, D), par)],
        epilogue=_ln_epilogue(alpha, scale), comm=comm)


def _ln_bwd(name, dh, xhat, rstd, g, scale):
    T, D = xhat.shape
    tr = _tile(T, ROW_TILE)

    def fn(i, dh, xhat, rstd, g):
        dxh = dh * g
        rs = jnp.tile(rstd, (1, D // LANES))
        dz = rs * (dxh - jnp.mean(dxh, axis=-1, keepdims=True) - xhat * jnp.mean(dxh * xhat, axis=-1, keepdims=True))
        return (dz, scale * dz, jnp.sum(dh * xhat, axis=0, keepdims=True), jnp.sum(dh, axis=0, keepdims=True))

    row = lambda i: (i, 0)
    return _rowwise(name, T // tr, fn, [_rows(dh, tr), _rows(xhat, tr), _rows(rstd, tr), _whole(g)],
                    [((T, D), F32, (tr, D), row), ((T, D), BF16, (tr, D), row)], accs=[(1, D), (1, D)])


def _grad_tn(name, a, b, bm, bn, bk=4 * ROW_TILE, block_diag=False, comm=None):
    T, M = a.shape
    N = b.shape[1]
    bk = _tile(T, bk)
    if not block_diag:
        shape, oblk, omap = (1, M, N), (None, bm, bn), (lambda i, j: (0, i, j))
        amap, gm = (lambda i, j, k: (k, i)), M // bm
    else:
        shape, oblk, omap = (1, N // bn, bm, bn), (None, None, bm, bn), (lambda i, j: (0, j, 0, 0))
        amap, gm = (lambda i, j, k: (k, j)), 1
    res = _matmul(name, (gm, N // bn, T // bk), (a, (bk, bm), amap), (b, (bk, bn), lambda i, j, k: (k, j)),
                  [(shape, BF16, oblk, omap)], (bm, bn), ta=True, comm=comm)
    return res[0] if comm is None else (res[0][0], res[1])


MXU_COLS = 256


def _col_chunks(width):
    return [(s, min(MXU_COLS, width - s)) for s in range(0, width, MXU_COLS)]


def _ffn_up(hb, w_in, layer, comm=None):
    T, D = hb.shape
    N2 = w_in.shape[2]
    wd = N2 // 4
    bm = _tile(T, 2 * ROW_TILE)

    def body(a_ref, w_ref, gu_ref, act_ref):
        a = a_ref[...]
        for s, n in _col_chunks(wd):
            gg = jnp.dot(a, w_ref[:, s:s + n], preferred_element_type=F32)
            uu = jnp.dot(a, w_ref[:, wd + s:wd + s + n], preferred_element_type=F32)
            sg = jax.nn.sigmoid(gg)
            silu = gg * sg
            gu_ref[:, s:s + n] = (uu * (sg + silu * (1.0 - sg))).astype(gu_ref.dtype)
            gu_ref[:, wd + s:wd + s + n] = silu.astype(gu_ref.dtype)
            act_ref[:, s:s + n] = (silu * uu).astype(act_ref.dtype)

    (gu, act), carried = _hosted_call(
        body, comm, name="ffn_up", grid=(2, T // bm),
        in_specs=[pl.BlockSpec((bm, D), lambda j, i: (i, 0)), pl.BlockSpec((None, D, 2 * wd), lambda j, i: (layer, 0, j))],
        out_specs=[pl.BlockSpec((bm, 2 * wd), lambda j, i: (i, j)), pl.BlockSpec((bm, wd), lambda j, i: (i, j))],
        out_shape=[jax.ShapeDtypeStruct((T, N2), BF16), jax.ShapeDtypeStruct((T, N2 // 2), BF16)],
        operands=[hb, w_in], compiler_params=_cparams(("arbitrary", "arbitrary")))
    return gu, act, carried


def _ffn_dact(dyb, w_out, layer, gu):
    T, D = dyb.shape
    N2 = gu.shape[1]
    wd = N2 // 4
    bm = _tile(T, 2 * ROW_TILE)

    def body(dy_ref, w_ref, gu_ref, o_ref):
        dy = dy_ref[...]
        for s, n in _col_chunks(wd):
            dact = lax.dot_general(dy, w_ref[s:s + n, :], (((1,), (1,)), ((), ())), preferred_element_type=F32)
            o_ref[:, s:s + n] = (dact * gu_ref[:, s:s + n].astype(F32)).astype(o_ref.dtype)
            o_ref[:, wd + s:wd + s + n] = (dact * gu_ref[:, wd + s:wd + s + n].astype(F32)).astype(o_ref.dtype)

    return pl.pallas_call(
        body, name="ffn_dact", grid=(2, T // bm),
        in_specs=[pl.BlockSpec((bm, D), lambda j, i: (i, 0)), pl.BlockSpec((None, wd, D), lambda j, i: (layer, j, 0)),
                  pl.BlockSpec((bm, 2 * wd), lambda j, i: (i, j))],
        out_specs=pl.BlockSpec((bm, 2 * wd), lambda j, i: (i, j)),
        out_shape=jax.ShapeDtypeStruct((T, N2), BF16),
        compiler_params=_cparams(("arbitrary", "arbitrary")))(dyb, w_out, gu)


def _ffn_dx(dgu, w_in, layer, dz, alpha, comm=None):
    T, N2 = dgu.shape
    D = w_in.shape[1]
    bm = _tile(T, ROW_TILE)
    res = _matmul(
        "ffn_dx", (T // bm, 1, 1),
        (dgu, (bm, N2), lambda i, j, k: (i, 0)), (w_in, (None, D, N2), lambda i, j, k: (layer, 0, 0)),
        [((T, D), F32, (bm, D), lambda i, j: (i, 0))], (bm, D), tb=True,
        extras=[(dz, (bm, D), lambda i, j: (i, 0))], epilogue=lambda acc, dzb: (alpha * dzb + acc,), comm=comm)
    return res[0] if comm is None else (res[0][0], res[1])


def _input_grad(name, dy, w, layer, dz, alpha, comm):
    T, N = dy.shape
    D = w.shape[1]
    bm = _tile(T, ROW_TILE)
    (out,), carried = _matmul(
        name, (T // bm, 1, 1),
        (dy, (bm, N), lambda i, j, k: (i, 0)), (w, (None, D, N), lambda i, j, k: (layer, 0, 0)),
        [((T, D), F32, (bm, D), lambda i, j: (i, 0))], (bm, D), tb=True,
        extras=[(dz, (bm, D), lambda i, j: (i, 0))], epilogue=lambda acc, dzb: (alpha * dzb + acc,), comm=comm)
    return out, carried


def _back_proj(name, dy, w, layer, dtype):
    T, D = dy.shape
    K = w.shape[1]
    bm = _tile(T, ROW_TILE)
    (out,) = _matmul(
        name, (T // bm, 1, 1),
        (dy, (bm, D), lambda i, j, k: (i, 0)), (w, (None, K, D), lambda i, j, k: (layer, 0, 0)),
        [((T, K), dtype, (bm, K), lambda i, j: (i, 0))], (bm, K), tb=True)
    return out


def _rope_tables(T):
    pos = jnp.arange(T, dtype=F32)
    inv_freq = ROPE_THETA ** (-jnp.arange(0, HEAD_DIM, 2, dtype=F32) / HEAD_DIM)
    ang = pos[:, None] * inv_freq[None, :]
    cos, sin = jnp.cos(ang), jnp.sin(ang)
    c128 = jnp.tile(cos, (1, 4))
    s128 = jnp.tile(jnp.concatenate([-sin, sin], axis=1), (1, 2))
    return c128, s128


QK_COLS = (N_HEADS + N_KV_HEADS) * HEAD_DIM
Q_COLS = N_HEADS * HEAD_DIM
KV_COLS = N_KV_HEADS * HEAD_DIM
Q_SCALE = HEAD_DIM ** -0.5


def _qkv_rope(hb, w_qkv, layer, c128, s128):
    T, D = hb.shape
    N = w_qkv.shape[2]
    bm = _tile(T, ROW_TILE)

    def epi(acc, c, s):
        x = acc[:, :QK_COLS]
        rep = QK_COLS // 128
        r = x * jnp.tile(c, (1, rep)) + _swap_halves(x) * jnp.tile(s, (1, rep))
        return (jnp.concatenate([r[:, :Q_COLS] * Q_SCALE, r[:, Q_COLS:], acc[:, QK_COLS:]], axis=1),)

    (qkv,) = _matmul(
        "qkv_rope", (T // bm, 1, 1),
        (hb, (bm, D), lambda i, j, k: (i, 0)), (w_qkv, (None, D, N), lambda i, j, k: (layer, 0, 0)),
        [((T, N), BF16, (bm, N), lambda i, j: (i, 0))], (bm, N),
        extras=[(c128, (bm, 128), lambda i, j: (i, 0)), (s128, (bm, 128), lambda i, j: (i, 0))], epilogue=epi)
    return qkv


def _rope_bwd(dq, dkv, c128, s128):
    T = dq.shape[0]
    tr = _tile(T, ROW_TILE)

    def fn(i, dq, dkv, c, s):
        dx = jnp.concatenate([dq * Q_SCALE, dkv[:, :KV_COLS]], axis=1)
        rep = QK_COLS // 128
        d = dx * jnp.tile(c, (1, rep)) + _swap_halves(dx * jnp.tile(s, (1, rep)))
        return (jnp.concatenate([d, dkv[:, KV_COLS:]], axis=1),)

    N = Q_COLS + 2 * KV_COLS
    (out,) = _rowwise("rope_bwd", T // tr, fn, [_rows(dq, tr), _rows(dkv, tr), _rows(c128, tr), _rows(s128, tr)],
                      [((T, N), BF16, (tr, N), lambda i: (i, 0))])
    return out


def _attn_mask(first_block):
    q_pos = lax.broadcasted_iota(jnp.int32, (GROUP * ATTN_BLOCK, 2 * ATTN_BLOCK), 0) & (ATTN_BLOCK - 1)
    col = lax.broadcasted_iota(jnp.int32, (GROUP * ATTN_BLOCK, 2 * ATTN_BLOCK), 1)
    dist = q_pos + ATTN_BLOCK - col
    return (dist >= 0) & (dist < ATTN_BLOCK) & ((col >= ATTN_BLOCK) | jnp.logical_not(first_block))


def _sink_column(sk_ref, kvh):
    rg = lax.broadcasted_iota(jnp.int32, (GROUP * ATTN_BLOCK, 1), 0) // ATTN_BLOCK
    col = jnp.full((GROUP * ATTN_BLOCK, 1), sk_ref[0, kvh * GROUP], F32)
    for gi in range(1, GROUP):
        col = jnp.where(rg == gi, sk_ref[0, kvh * GROUP + gi], col)
    return col


def _stack_heads(x, kvh):
    return jnp.concatenate([x[:, (kvh * GROUP + gi) * HEAD_DIM:(kvh * GROUP + gi + 1) * HEAD_DIM] for gi in range(GROUP)], axis=0)


def _unstack_heads(parts):
    cols = []
    for p in parts:
        cols += [p[gi * ATTN_BLOCK:(gi + 1) * ATTN_BLOCK] for gi in range(GROUP)]
    return jnp.concatenate(cols, axis=1)


def _attn_softmax(q4, kb, mask, sink):
    s = lax.dot_general(q4, kb, (((1,), (1,)), ((), ())), preferred_element_type=F32)
    s = jnp.where(mask, s, NEG_BIG)
    m = jnp.maximum(jnp.max(s, axis=1, keepdims=True), sink)
    p = jnp.exp(s - m)
    e_sink = jnp.exp(sink - m)
    den = jnp.sum(p, axis=1, keepdims=True) + e_sink
    return p / den, e_sink / den


def _attn_fwd(qkv, sinks):
    T = qkv.shape[0]
    nb = T // ATTN_BLOCK
    kcb, vcb = Q_COLS // KV_COLS, Q_COLS // KV_COLS + 1

    def body(q_ref, kc_ref, kp_ref, vc_ref, vp_ref, sk_ref, o_ref):
        i = pl.program_id(0)
        mask = _attn_mask(i == 0)
        q = q_ref[...]
        kband = jnp.concatenate([kp_ref[...], kc_ref[...]], axis=0)
        vband = jnp.concatenate([vp_ref[...], vc_ref[...]], axis=0)
        parts = []
        for kvh in range(N_KV_HEADS):
            hs = slice(kvh * HEAD_DIM, (kvh + 1) * HEAD_DIM)
            pn, _ = _attn_softmax(_stack_heads(q, kvh), kband[:, hs], mask, _sink_column(sk_ref, kvh))
            parts.append(jnp.dot(pn.astype(BF16), vband[:, hs], preferred_element_type=F32))
        o_ref[...] = _unstack_heads(parts).astype(o_ref.dtype)

    prev = lambda i: jnp.maximum(i - 1, 0)
    return pl.pallas_call(
        body, name="attn_fwd", grid=(nb,),
        in_specs=[pl.BlockSpec((ATTN_BLOCK, Q_COLS), lambda i: (i, 0)),
                  pl.BlockSpec((ATTN_BLOCK, KV_COLS), lambda i: (i, kcb)),
                  pl.BlockSpec((ATTN_BLOCK, KV_COLS), lambda i: (prev(i), kcb)),
                  pl.BlockSpec((ATTN_BLOCK, KV_COLS), lambda i: (i, vcb)),
                  pl.BlockSpec((ATTN_BLOCK, KV_COLS), lambda i: (prev(i), vcb)),
                  pl.BlockSpec(memory_space=pltpu.SMEM)],
        out_specs=pl.BlockSpec((ATTN_BLOCK, Q_COLS), lambda i: (i, 0)),
        out_shape=jax.ShapeDtypeStruct((T, Q_COLS), BF16),
        compiler_params=_cparams(("arbitrary",)),
    )(qkv, qkv, qkv, qkv, qkv, sinks)


def _attn_bwd(qkv, do, sinks):
    T = qkv.shape[0]
    nb = T // ATTN_BLOCK
    kcb, vcb = Q_COLS // KV_COLS, Q_COLS // KV_COLS + 1
    B = ATTN_BLOCK

    def body(q_ref, kc_ref, kp_ref, vc_ref, vp_ref, do_ref, sk_ref, dq_ref, dkv_ref, dsk_ref, carry_ref):
        i = pl.program_id(0)

        @pl.when(i == 0)
        def _():
            carry_ref[...] = jnp.zeros_like(carry_ref)
            dsk_ref[...] = jnp.zeros_like(dsk_ref)

        @pl.when(i < nb)
        def _():
            mask = _attn_mask(i == 0)
            q = q_ref[...]
            do_blk = do_ref[...]
            kband = jnp.concatenate([kp_ref[...], kc_ref[...]], axis=0)
            vband = jnp.concatenate([vp_ref[...], vc_ref[...]], axis=0)
            dq_parts, dk_parts, dv_parts = [], [], []
            for kvh in range(N_KV_HEADS):
                hs = slice(kvh * HEAD_DIM, (kvh + 1) * HEAD_DIM)
                q4 = _stack_heads(q, kvh)
                do4 = _stack_heads(do_blk, kvh)
                kb, vb = kband[:, hs], vband[:, hs]
                pn, p_sink = _attn_softmax(q4, kb, mask, _sink_column(sk_ref, kvh))
                dp = lax.dot_general(do4, vb, (((1,), (1,)), ((), ())), preferred_element_type=F32)
                delta = jnp.sum(pn * dp, axis=1, keepdims=True)
                ds = (pn * (dp - delta)).astype(BF16)
                dsk_ref[kvh] += -(p_sink * delta)
                dq_parts.append(jnp.dot(ds, kb, preferred_element_type=F32))
                dk_parts.append(lax.dot_general(ds, q4, (((0,), (0,)), ((), ())), preferred_element_type=F32))
                dv_parts.append(lax.dot_general(pn.astype(BF16), do4, (((0,), (0,)), ((), ())), preferred_element_type=F32))
            dq_ref[...] = _unstack_heads(dq_parts)
            dkv = jnp.concatenate(dk_parts + dv_parts, axis=1)
            dkv_ref[...] = carry_ref[...] + dkv[:B]
            carry_ref[...] = dkv[B:]

        @pl.when(i == nb)
        def _():
            dkv_ref[...] = carry_ref[...]

    cur = lambda i: jnp.minimum(i, nb - 1)
    prev = lambda i: jnp.maximum(cur(i) - 1, 0)
    lag = lambda i: jnp.maximum(i - 1, 0)
    return pl.pallas_call(
        body, name="attn_bwd", grid=(nb + 1,),
        in_specs=[pl.BlockSpec((B, Q_COLS), lambda i: (cur(i), 0)),
                  pl.BlockSpec((B, KV_COLS), lambda i: (cur(i), kcb)),
                  pl.BlockSpec((B, KV_COLS), lambda i: (prev(i), kcb)),
                  pl.BlockSpec((B, KV_COLS), lambda i: (cur(i), vcb)),
                  pl.BlockSpec((B, KV_COLS), lambda i: (prev(i), vcb)),
                  pl.BlockSpec((B, Q_COLS), lambda i: (cur(i), 0)),
                  pl.BlockSpec(memory_space=pltpu.SMEM)],
        out_specs=[pl.BlockSpec((B, Q_COLS), lambda i: (cur(i), 0)),
                   pl.BlockSpec((B, 2 * KV_COLS), lambda i: (lag(i), 0)),
                   pl.BlockSpec((N_KV_HEADS, GROUP * B, 1), lambda i: (0, 0, 0))],
        out_shape=[jax.ShapeDtypeStruct((T, Q_COLS), F32), jax.ShapeDtypeStruct((T, 2 * KV_COLS), F32),
                   jax.ShapeDtypeStruct((N_KV_HEADS, GROUP * B, 1), F32)],
        scratch_shapes=[pltpu.VMEM((B, 2 * KV_COLS), F32)],
        compiler_params=_cparams(("arbitrary",)),
    )(qkv, qkv, qkv, qkv, qkv, do, sinks)


def _halo_prev(arr, tr, cols, cb=0):
    per = tr // 8
    return (arr, (8, cols), lambda i: (jnp.maximum(i * per - 1, 0), cb))


def _halo_next(arr, tr, cols, cb=0):
    per = tr // 8
    last = arr.shape[0] // 8 - 1
    return (arr, (8, cols), lambda i: (jnp.minimum((i + 1) * per, last), cb))


def _conv_fwd(xg, cw, cb):
    T = xg.shape[0]
    C = cb.shape[1]
    tr = _tile(T, ROW_TILE)

    def fn(i, cur, prev8, cb, *cw):
        prev8 = jnp.where(i == 0, 0.0, prev8)
        xc = cb + cw[CONV_W - 1] * cur
        for s in range(1, CONV_W):
            xc = xc + cw[CONV_W - 1 - s] * _shift_down(prev8, cur, s)
        return xc, xc

    row = lambda i: (i, 0)
    return _rowwise("lru_conv", T // tr, fn, [_rows(xg, tr, C), _halo_prev(xg, tr, C), _whole(cb)] + [_whole(w) for w in cw],
                    [((T, C), F32, (tr, C), row), ((T, C), BF16, (tr, C), row)])


def _conv_bwd(dxc, xg, dgb, cw):
    T, C = dxc.shape
    tr = _tile(T, ROW_TILE)
    nt = T // tr

    def fn(i, d_cur, d_next8, x_cur, x_prev8, dgb, *cw):
        d_next8 = jnp.where(i == nt - 1, 0.0, d_next8)
        x_prev8 = jnp.where(i == 0, 0.0, x_prev8)
        dxb = cw[CONV_W - 1] * d_cur
        dcw = [jnp.sum(d_cur * x_cur, axis=0, keepdims=True)]
        for s in range(1, CONV_W):
            dxb = dxb + cw[CONV_W - 1 - s] * _shift_up(d_cur, d_next8, s)
            dcw.append(jnp.sum(d_cur * _shift_down(x_prev8, x_cur, s), axis=0, keepdims=True))
        return (jnp.concatenate([dxb.astype(BF16), dgb], axis=1), dcw[3], dcw[2], dcw[1], dcw[0],
                jnp.sum(d_cur, axis=0, keepdims=True))

    return _rowwise("lru_conv_bwd", nt, fn,
                    [_rows(dxc, tr), _halo_next(dxc, tr, C), _rows(xg, tr, C), _halo_prev(xg, tr, C), _rows(dgb, tr)] + [_whole(w) for w in cw],
                    [((T, 2 * C), BF16, (tr, 2 * C), lambda i: (i, 0))], accs=[(1, C)] * (CONV_W + 1))


def _lru_gate_fwd(xc, xcb, w_rarx, layer, b_ra, b_rx, lam):
    T, C = xc.shape
    W = C // RNN_BLOCKS
    bm = _tile(T, ROW_TILE)

    def epi(acc, xc_blk, bra, brx, lm):
        a, b = _lru_gates(acc, xc_blk, bra, brx, lm)
        return acc, a, b

    blk = lambda i, j: (i, j)
    par = lambda i, j: (0, j)
    return _matmul(
        "lru_gates", (T // bm, RNN_BLOCKS, 1),
        (xcb, (bm, W), lambda i, j, k: (i, j)), (w_rarx, (None, None, W, 2 * W), lambda i, j, k: (layer, j, 0, 0)),
        [((T, 2 * C), F32, (bm, 2 * W), blk), ((T, C), F32, (bm, W), blk), ((T, C), F32, (bm, W), blk)],
        (bm, 2 * W),
        extras=[(xc, (bm, W), blk), (b_ra, (1, W), par), (b_rx, (1, W), par), (lam, (1, W), par)], epilogue=epi)


def _lru_gate_bwd(pre, xc, lam_adj, h, b_ra, b_rx, lam):
    T, C = xc.shape
    W = C // RNN_BLOCKS
    tr = _tile(T, ROW_TILE // 2)

    def fn(i, pre, xc, adj, h_cur, h_prev8, bra, brx, lm):
        h_prev8 = jnp.where(i == 0, 0.0, h_prev8)
        da = adj * _shift_down(h_prev8, h_cur, 1)
        dpre, dxc, dbra, dbrx, dlam = [], [], [], [], []
        for n in range(RNN_BLOCKS):
            cs = slice(n * W, (n + 1) * W)
            _, vjp = jax.vjp(_lru_gates, pre[:, 2 * n * W:2 * (n + 1) * W], xc[:, cs], bra[:, cs], brx[:, cs], lm[:, cs])
            g = vjp((da[:, cs], adj[:, cs]))
            for lst, v in zip((dpre, dxc, dbra, dbrx, dlam), g):
                lst.append(v)
        cat = lambda l: jnp.concatenate(l, axis=1)
        return cat(dpre), cat(dxc), cat(dbra), cat(dbrx), cat(dlam)

    row = lambda i: (i, 0)
    return _rowwise("lru_gates_bwd", T // tr, fn,
                    [_rows(pre, tr), _rows(xc, tr), _rows(lam_adj, tr), _rows(h, tr), _halo_prev(h, tr, C),
                     _whole(b_ra), _whole(b_rx), _whole(lam)],
                    [((T, 2 * C), BF16, (tr, 2 * C), row), ((T, C), F32, (tr, C), row)], accs=[(1, C)] * 3)


def _scan_fwd(a, b):
    T, C = a.shape
    tt = _tile(T, ROW_TILE)

    def body(a_ref, b_ref, o_ref, c_ref):
        @pl.when(pl.program_id(0) == 0)
        def _():
            c_ref[...] = jnp.zeros_like(c_ref)

        row = lax.broadcasted_iota(jnp.int32, (8, C), 0)

        def step(j, carry):
            sl = pl.ds(pl.multiple_of(j * 8, 8), 8)
            A, B = a_ref[sl, :], b_ref[sl, :]
            for d in (1, 2, 4):
                ok = row >= d
                B = jnp.where(ok, A * pltpu.roll(B, d, 0) + B, B)
                A = jnp.where(ok, A * pltpu.roll(A, d, 0), A)
            h = A * carry + B
            o_ref[sl, :] = h
            return jnp.sum(jnp.where(row == 7, h, 0.0), axis=0, keepdims=True)

        c_ref[0:1, :] = lax.fori_loop(0, tt // 8, step, c_ref[0:1, :])

    spec = pl.BlockSpec((tt, C), lambda i: (i, 0))
    return pl.pallas_call(body, name="lru_scan", grid=(T // tt,), in_specs=[spec, spec], out_specs=spec,
                          out_shape=jax.ShapeDtypeStruct((T, C), F32), scratch_shapes=[pltpu.VMEM((8, C), F32)],
                          compiler_params=_cparams(("arbitrary",)))(a, b)


def _scan_bwd(a, dh):
    T, C = a.shape
    tt = _tile(T, ROW_TILE)
    nt = T // tt

    def body(a_ref, d_ref, o_ref, c_ref):
        @pl.when(pl.program_id(0) == 0)
        def _():
            c_ref[...] = jnp.zeros_like(c_ref)

        row = lax.broadcasted_iota(jnp.int32, (8, C), 0)

        def step(jj, carry):
            adj_next, a_next = carry
            j = tt // 8 - 1 - jj
            sl = pl.ds(pl.multiple_of(j * 8, 8), 8)
            a_blk = a_ref[sl, :]
            A = jnp.where(row < 7, pltpu.roll(a_blk, 7, 0), a_next)
            B = d_ref[sl, :]
            for d in (1, 2, 4):
                ok = row < 8 - d
                B = jnp.where(ok, A * pltpu.roll(B, 8 - d, 0) + B, B)
                A = jnp.where(ok, A * pltpu.roll(A, 8 - d, 0), A)
            adj = A * adj_next + B
            o_ref[sl, :] = adj
            first = lambda v: jnp.sum(jnp.where(row == 0, v, 0.0), axis=0, keepdims=True)
            return first(adj), first(a_blk)

        adj0, a0 = lax.fori_loop(0, tt // 8, step, (c_ref[0:1, :], c_ref[1:2, :]))
        c_ref[0:1, :] = adj0
        c_ref[1:2, :] = a0

    spec = pl.BlockSpec((tt, C), lambda i: (nt - 1 - i, 0))
    return pl.pallas_call(body, name="lru_scan_bwd", grid=(nt,), in_specs=[spec, spec], out_specs=spec,
                          out_shape=jax.ShapeDtypeStruct((T, C), F32), scratch_shapes=[pltpu.VMEM((8, C), F32)],
                          compiler_params=_cparams(("arbitrary",)))(a, dh)


def _lru_out_fwd(h, xg):
    T, C = h.shape
    tr = _tile(T, ROW_TILE)
    (y,) = _rowwise("lru_out", T // tr, lambda i, h, gb: (h * _gelu_tanh(gb),), [_rows(h, tr), _rows(xg, tr, C, 1)],
                    [((T, C), BF16, (tr, C), lambda i: (i, 0))])
    return y


def _lru_out_bwd(dy, h, xg):
    T, C = h.shape
    tr = _tile(T, ROW_TILE)

    def fn(i, dy, h, gb):
        _, vjp = jax.vjp(lambda h, gb: h * _gelu_tanh(gb), h, gb)
        return vjp(dy)

    row = lambda i: (i, 0)
    return _rowwise("lru_out_bwd", T // tr, fn, [_rows(dy, tr), _rows(h, tr), _rows(xg, tr, C, 1)],
                    [((T, C), F32, (tr, C), row), ((T, C), BF16, (tr, C), row)])


class _Sharded:
    def __init__(self, kind, size, off=0, width=None):
        self.kind, self.size, self.off, self.width = kind, size, off, width

    def slot(self, cx, cy):
        return (2 * cy + cx) if self.kind == "perm" else (2 * cx + cy)

    def at(self, ref, cx, cy, layers, half=None):
        s = self.slot(cx, cy)
        start = s * self.size
        if not isinstance(start, int):
            start = pl.multiple_of(start, 8 if self.kind in ("rows", "rarx") else 128)
        if self.kind in ("cols", "perm"):
            n = ref.shape[1]
            rows = slice(None) if half is None else pl.ds(pl.multiple_of(half * (n // 2), 8), n // 2)
            return ref.at[layers, rows, pl.ds(start, self.size)]
        if self.kind == "rows":
            if half is None:
                return ref.at[layers, pl.ds(start, self.size), :]
            return ref.at[layers, pl.ds(pl.multiple_of(start + half * (self.size // 2), 8), self.size // 2), :]
        if self.kind == "rarx":
            n = ref.shape[1]
            blocks = slice(None) if half is None else pl.ds(half * (n // 2), n // 2)
            return ref.at[layers, blocks, pl.ds(start, self.size), pl.ds(self.off, self.width)]
        raise ValueError(self.kind)


def _mesh_pos():
    return lax.axis_index("x"), lax.axis_index("y"), lax.axis_index("c")


def _peer_chips(x, y):
    return [(1 - x, y), (x, 1 - y), (1 - x, 1 - y)]


def _place(shard, out_shape, out_dtype, sh, pos, prev=None):
    def body(p_ref, x_ref, *rest):
        rest[-1][...] = x_ref[...].astype(rest[-1].dtype)

    if sh.kind in ("cols", "perm"):
        L, R, Ns = shard.shape
        tr = _tile(R, ROW_TILE)
        k = 1 if sh.kind == "perm" else 0
        grid = (L, R // tr)
        ispec = pl.BlockSpec((None, tr, Ns), lambda l, i, p: (l, i, 0))
        ospec = pl.BlockSpec((None, tr, Ns), lambda l, i, p: (l, i, p[k]))
    elif sh.kind == "rows":
        L, Rs, D = shard.shape
        tr = _tile(Rs, ROW_TILE)
        nt = Rs // tr
        grid = (L, nt)
        ispec = pl.BlockSpec((None, tr, D), lambda l, i, p: (l, i, 0))
        ospec = pl.BlockSpec((None, tr, D), lambda l, i, p: (l, p[0] * nt + i, 0))
    else:
        L, nb, Rs, Wd = shard.shape
        cb = sh.off // Wd
        grid = (L, nb)
        ispec = pl.BlockSpec((None, None, Rs, Wd), lambda l, i, p: (l, i, 0, 0))
        ospec = pl.BlockSpec((None, None, Rs, Wd), lambda l, i, p: (l, i, p[0], cb))
    in_specs = [ispec]
    operands = [pos, shard]
    alias = {}
    if prev is not None:
        in_specs.append(pl.BlockSpec(memory_space=pl.ANY))
        operands.append(prev)
        alias = {2: 0}
    gs = pltpu.PrefetchScalarGridSpec(num_scalar_prefetch=1, grid=grid, in_specs=in_specs, out_specs=ospec)
    return pl.pallas_call(body, name="weight_place", grid_spec=gs, out_shape=jax.ShapeDtypeStruct(out_shape, out_dtype),
                          input_output_aliases=alias, compiler_params=_cparams(("arbitrary", "arbitrary")))(*operands)


def _gather_over_ici(jobs, base=0):
    assert len({id(j[0]) for j in jobs}) == len(jobs)

    def copies(refs, send, recv):
        x, y, c = _mesh_pos()
        out = []
        for t, (_, sh, layer) in enumerate(jobs):
            lay = pl.ds(layer, 1)
            mine = sh.at(refs[t], x, y, lay, c)
            for j, (px, py) in enumerate(_peer_chips(x, y)):
                theirs = sh.at(refs[t], px, py, lay, c)
                k = base + 3 * t + j
                out.append((pltpu.make_async_remote_copy(mine, mine, send.at[k], recv.at[k],
                                                         device_id=(px, py, c), device_id_type=MESH),
                            pltpu.make_async_remote_copy(theirs, theirs, send.at[k], recv.at[k],
                                                         device_id=(px, py, c), device_id_type=MESH)))
        return out

    def start(refs, send, recv):
        for out_cp, _ in copies(refs, send, recv):
            out_cp.start()

    def finish(refs, send, recv):
        for out_cp, in_cp in copies(refs, send, recv):
            in_cp.wait_recv()
            out_cp.wait_send()

    return _SideComm([j[0] for j in jobs], 3 * len(jobs), start, finish)


def _pass_to_sibling(jobs, base=0):
    assert len({id(j[0]) for j in jobs}) == len(jobs)

    def copies(refs, send, recv):
        x, y, c = _mesh_pos()
        out = []
        for t, (_, sh, layer) in enumerate(jobs):
            lay = pl.ds(layer, 1)
            for j, (px, py) in enumerate(_peer_chips(x, y)):
                got = sh.at(refs[t], px, py, lay, c)
                coming = sh.at(refs[t], px, py, lay, 1 - c)
                k = base + 3 * t + j
                out.append((pltpu.make_async_remote_copy(got, got, send.at[k], recv.at[k],
                                                         device_id=(x, y, 1 - c), device_id_type=MESH),
                            pltpu.make_async_remote_copy(coming, coming, send.at[k], recv.at[k],
                                                         device_id=(x, y, 1 - c), device_id_type=MESH)))
        return out

    def start(refs, send, recv):
        for out_cp, _ in copies(refs, send, recv):
            out_cp.start()

    def finish(refs, send, recv):
        for out_cp, in_cp in copies(refs, send, recv):
            in_cp.wait_recv()
            out_cp.wait_send()

    return _SideComm([j[0] for j in jobs], 3 * len(jobs), start, finish)


def _all_gather(jobs):
    n = len(jobs)
    ici, d2d = _gather_over_ici(jobs), _pass_to_sibling(jobs, 3 * n)

    def body(*refs):
        outs = refs[n:2 * n]
        send, recv = refs[2 * n:]
        ici.start(outs, send, recv)
        ici.finish(outs, send, recv)
        d2d.start(outs, send, recv)
        d2d.finish(outs, send, recv)

    hbm = pl.BlockSpec(memory_space=pl.ANY)
    return pl.pallas_call(
        body, name="weights_all_gather", in_specs=[hbm] * n, out_specs=[hbm] * n,
        out_shape=[jax.ShapeDtypeStruct(j[0].shape, j[0].dtype) for j in jobs],
        input_output_aliases={t: t for t in range(n)},
        scratch_shapes=[pltpu.SemaphoreType.DMA((6 * n,)), pltpu.SemaphoreType.DMA((6 * n,))],
    )(*[j[0] for j in jobs])


def _half_view(sh, ref, h):
    if sh.kind == "rows":
        n = ref.shape[-1]
        return ref.at[:, pl.ds(pl.multiple_of(h * (n // 2), 128), n // 2)]
    n = ref.shape[0]
    return ref.at[pl.ds(h * (n // 2), n // 2)]


def _half_shape(sh, layer_shape):
    s = list(layer_shape)
    s[len(s) - 1 if sh.kind == "rows" else 0] //= 2
    return tuple(s)


def _chip_part(sh, ref, cx, cy):
    start = sh.slot(cx, cy) * sh.size
    if sh.kind in ("cols", "perm"):
        return ref.at[:, pl.ds(pl.multiple_of(start, 128), sh.size)]
    if sh.kind == "rows":
        return ref.at[pl.ds(pl.multiple_of(start, 8), sh.size), :]
    return ref.at[:, pl.ds(pl.multiple_of(start, 8), sh.size), :]


def _chip_part_shape(sh, half_shape):
    s = list(half_shape)
    s[{"cols": len(s) - 1, "perm": len(s) - 1, "rows": 0, "rarx": 1}[sh.kind]] = sh.size
    return tuple(s)


def _swap_stage(jobs):
    assert len({id(j[0]) for j in jobs}) == len(jobs)

    def make(base):
        arrays = []
        for g, sh, _ in jobs:
            arrays += [g, jax.ShapeDtypeStruct(_half_shape(sh, g.shape[1:]), g.dtype)]

        def copies(refs, send, recv):
            x, y, c = _mesh_pos()
            return [pltpu.make_async_remote_copy(_half_view(sh, refs[2 * t].at[layer], 1 - c), refs[2 * t + 1],
                                                 send.at[base + t], recv.at[base + t],
                                                 device_id=(x, y, 1 - c), device_id_type=MESH)
                    for t, (_, sh, layer) in enumerate(jobs)]

        def start(refs, send, recv):
            for cp in copies(refs, send, recv):
                cp.start()

        def finish(refs, send, recv):
            for cp in copies(refs, send, recv):
                cp.wait()

        return _SideComm(arrays, len(jobs), start, finish)
    return make


def _exchange_stage(jobs):
    n = len(jobs)

    def make(base):
        arrays = []
        for p, sh in jobs:
            arrays += [p, jax.ShapeDtypeStruct((N_CHIPS,) + _chip_part_shape(sh, p.shape), p.dtype)]

        def copies(refs, send, recv):
            x, y, c = _mesh_pos()
            me = 2 * x + y
            out = []
            for t, (_, sh) in enumerate(jobs):
                part, land = refs[2 * t], refs[2 * t + 1]
                local = pltpu.make_async_copy(_chip_part(sh, part, x, y), land.at[me], send.at[base + 3 * n + t])
                remote = []
                for j, (px, py) in enumerate(_peer_chips(x, y)):
                    k = base + 3 * t + j
                    src = land.at[2 * px + py]
                    remote.append((pltpu.make_async_remote_copy(_chip_part(sh, part, px, py), land.at[me], send.at[k],
                                                                recv.at[k], device_id=(px, py, c), device_id_type=MESH),
                                   pltpu.make_async_remote_copy(src, src, send.at[k], recv.at[k],
                                                                device_id=(px, py, c), device_id_type=MESH)))
                out.append((local, remote))
            return out

        def start(refs, send, recv):
            for local, remote in copies(refs, send, recv):
                local.start()
                for out_cp, _ in remote:
                    out_cp.start()

        def finish(refs, send, recv):
            for local, remote in copies(refs, send, recv):
                for out_cp, in_cp in remote:
                    in_cp.wait_recv()
                    out_cp.wait_send()
                local.wait()

        return _SideComm(arrays, 4 * n, start, finish)
    return make


def _share_stage(jobs):
    assert len({id(j[0]) for j in jobs}) == len(jobs)

    def make(base):
        def copies(refs, send, recv):
            x, y, c = _mesh_pos()
            out = []
            for t, (_, sh, layer) in enumerate(jobs):
                mine = _half_view(sh, refs[t].at[layer], c)
                theirs = _half_view(sh, refs[t].at[layer], 1 - c)
                out.append((pltpu.make_async_remote_copy(mine, mine, send.at[base + t], recv.at[base + t],
                                                         device_id=(x, y, 1 - c), device_id_type=MESH),
                            pltpu.make_async_remote_copy(theirs, theirs, send.at[base + t], recv.at[base + t],
                                                         device_id=(x, y, 1 - c), device_id_type=MESH)))
            return out

        def start(refs, send, recv):
            for out_cp, _ in copies(refs, send, recv):
                out_cp.start()

        def finish(refs, send, recv):
            for out_cp, in_cp in copies(refs, send, recv):
                in_cp.wait_recv()
                out_cp.wait_send()

        return _SideComm([j[0] for j in jobs], len(jobs), start, finish)
    return make


def _pair_add_layer(g, sh, layer, got, c_arr):
    if sh.kind == "rows":
        R, Ch = got.shape
        tr = _tile(R, ROW_TILE)
        ins = [(g, (None, tr, Ch), lambda i, p: (layer, i, p[0])), (got, (tr, Ch), lambda i, p: (i, 0))]
        nt, g2 = R // tr, got
    else:
        C = got.shape[-1]
        g2 = got.reshape(-1, C)
        R = g2.shape[0]
        tr = _tile(R, 256 if C > 2048 else ROW_TILE)
        nt = R // tr
        ins = [(g.reshape(g.shape[0], -1, C), (None, tr, C), lambda i, p: (layer, p[0] * nt + i, 0)),
               (g2, (tr, C), lambda i, p: (i, 0))]
    (out,) = _rowwise("grad_pair_add", nt, lambda i, a, b: (a.astype(F32) + b.astype(F32),), ins,
                      [(g2.shape, got.dtype, ins[1][1], lambda i, p: (i, 0))], prefetch=c_arr)
    return out.reshape(got.shape)


def _chip_sum_layer(land, sh, layer, tot, c_arr):
    def fn(i, blk):
        v = blk.astype(F32)
        return (((v[0] + v[1]) + v[2]) + v[3],)

    if sh.kind == "rows":
        _, R, Ch = land.shape
        tr = _tile(R, ROW_TILE)
        ins = [(land, (N_CHIPS, tr, Ch), lambda i, p: (0, i, 0))]
        outs = [(tot.shape, F32, (None, tr, Ch), lambda i, p: (layer, i, p[0]))]
        (out,) = _rowwise("grad_chip_sum", R // tr, fn, ins, outs, prefetch=c_arr, into=tot)
        return out
    C = land.shape[-1]
    l3 = land.reshape(N_CHIPS, -1, C)
    R = l3.shape[1]
    tr = _tile(R, ROW_TILE)
    nt = R // tr
    t3 = tot.reshape(tot.shape[0], -1, C)
    (out,) = _rowwise("grad_chip_sum", nt, fn, [(l3, (N_CHIPS, tr, C), lambda i, p: (0, i, 0))],
                      [(t3.shape, F32, (None, tr, C), lambda i, p: (layer, p[0] * nt + i, 0))], prefetch=c_arr, into=t3)
    return out.reshape(tot.shape)


def _flat2(shape):
    return (math.prod(shape[:-1]), shape[-1])


def _adamw(w, g, m, v):
    shape = w.shape
    R, C = _flat2(shape)
    tr = _tile(R, 256 if R % 256 == 0 else R)

    def fn(i, w, g, m, v):
        m2 = ADAM_B1 * m + (1.0 - ADAM_B1) * g
        v2 = ADAM_B2 * v + (1.0 - ADAM_B2) * (g * g)
        m_hat = m2 / (1.0 - ADAM_B1 ** ADAM_STEP)
        v_hat = v2 / (1.0 - ADAM_B2 ** ADAM_STEP)
        delta = -ADAM_LR * (m_hat / (jnp.sqrt(v_hat) + ADAM_EPS) + ADAM_WD * w)
        return g, delta, m2, v2

    row = lambda i: (i, 0)
    f2 = lambda a: a.reshape(R, C)
    outs = _rowwise("adamw", R // tr, fn, [_rows(f2(a), tr) for a in (w, g, m, v)], [((R, C), F32, (tr, C), row)] * 4)
    return [o.reshape(shape) for o in outs]


def _loss_and_grad(last, target):
    xhat, g, b = last
    T, D = xhat.shape
    tr = _tile(T, ROW_TILE)

    def fn(i, xhat, t, g, b):
        err = (xhat * g + b) - t
        return err * (1.0 / D), jnp.sum(err * err, axis=0, keepdims=True)

    dh, sq = _rowwise("loss", T // tr, fn, [_rows(xhat, tr), _rows(target, tr), _whole(g), _whole(b)],
                      [((T, D), F32, (tr, D), lambda i: (i, 0))], accs=[(1, D)])
    return dh, (0.5 / D) * jnp.sum(sq)


def kernel(x, ffn1_w_in, ffn1_w_out, ffn2_w_in, ffn2_w_out, ln_g, ln_b, attn_w_qkv, attn_sinks, attn_w_o, lru_w_in, lru_conv_w, lru_conv_b, lru_w_ra, lru_b_ra, lru_w_rx, lru_b_rx, lru_lambda, lru_w_out, loss_target, m_ffn1_w_in, m_ffn1_w_out, m_ffn2_w_in, m_ffn2_w_out, m_ln_g, m_ln_b, m_attn_w_qkv, m_attn_sinks, m_attn_w_o, m_lru_w_in, m_lru_conv_w, m_lru_conv_b, m_lru_w_ra, m_lru_b_ra, m_lru_w_rx, m_lru_b_rx, m_lru_lambda, m_lru_w_out, v_ffn1_w_in, v_ffn1_w_out, v_ffn2_w_in, v_ffn2_w_out, v_ln_g, v_ln_b, v_attn_w_qkv, v_attn_sinks, v_attn_w_o, v_lru_w_in, v_lru_conv_w, v_lru_conv_b, v_lru_w_ra, v_lru_b_ra, v_lru_w_rx, v_lru_b_rx, v_lru_lambda, v_lru_w_out):
    names = ["ffn1_w_in", "ffn1_w_out", "ffn2_w_in", "ffn2_w_out", "ln_g", "ln_b", "attn_w_qkv", "attn_sinks", "attn_w_o",
             "lru_w_in", "lru_conv_w", "lru_conv_b", "lru_w_ra", "lru_b_ra", "lru_w_rx", "lru_b_rx", "lru_lambda", "lru_w_out"]
    W = dict(zip(names, [ffn1_w_in, ffn1_w_out, ffn2_w_in, ffn2_w_out, ln_g, ln_b, attn_w_qkv, attn_sinks, attn_w_o,
                         lru_w_in, lru_conv_w, lru_conv_b, lru_w_ra, lru_b_ra, lru_w_rx, lru_b_rx, lru_lambda, lru_w_out]))
    M = dict(zip(names, [m_ffn1_w_in, m_ffn1_w_out, m_ffn2_w_in, m_ffn2_w_out, m_ln_g, m_ln_b, m_attn_w_qkv, m_attn_sinks, m_attn_w_o,
                         m_lru_w_in, m_lru_conv_w, m_lru_conv_b, m_lru_w_ra, m_lru_b_ra, m_lru_w_rx, m_lru_b_rx, m_lru_lambda, m_lru_w_out]))
    V = dict(zip(names, [v_ffn1_w_in, v_ffn1_w_out, v_ffn2_w_in, v_ffn2_w_out, v_ln_g, v_ln_b, v_attn_w_qkv, v_attn_sinks, v_attn_w_o,
                         v_lru_w_in, v_lru_conv_w, v_lru_conv_b, v_lru_w_ra, v_lru_b_ra, v_lru_w_rx, v_lru_b_rx, v_lru_lambda, v_lru_w_out]))

    T, D = x.shape[1], x.shape[2]
    L = ffn1_w_in.shape[0]
    LA, LR = attn_w_qkv.shape[0], lru_w_in.shape[0]
    N2 = ffn1_w_in.shape[2] * N_CHIPS
    F = N2 // 2
    C = lru_lambda.shape[1] * N_CHIPS
    CW = C // N_CHIPS
    alpha = (2.0 * L) ** 0.25
    c_arr = lax.axis_index("c").astype(jnp.int32).reshape(1)

    n_sink = attn_sinks.size
    assert n_sink <= CW

    up8 = lambda n: -(-n // 8) * 8
    o_g = 0
    o_b = o_g + up8(3 * L)
    o_cw = o_b + up8(3 * L)
    o_cb = o_cw + up8(LR * CONV_W)
    o_ra = o_cb + up8(LR)
    o_rx = o_ra + up8(LR)
    o_lam = o_rx + up8(LR)
    o_sink = o_lam + up8(LR)
    assert o_sink + 8 <= SMALL_ROWS

    def pack_small(d):
        parts = [d["ln_g"].reshape(-1, CW), d["ln_b"].reshape(-1, CW), d["lru_conv_w"].reshape(-1, CW), d["lru_conv_b"],
                 d["lru_b_ra"], d["lru_b_rx"], d["lru_lambda"],
                 jnp.pad(d["attn_sinks"].reshape(1, -1), ((0, 0), (0, CW - n_sink)))]
        parts = [jnp.pad(p, ((0, up8(p.shape[0]) - p.shape[0]), (0, 0))) for p in parts]
        used = sum(p.shape[0] for p in parts)
        return jnp.concatenate(parts + [jnp.zeros((SMALL_ROWS - used, CW), F32)], axis=0)

    cols = lambda a: _Sharded("cols", a.shape[-1])
    rows_ = lambda a: _Sharded("rows", a.shape[-2])
    RW = lru_w_ra.shape[2]
    sh_list = [_Sharded("perm", N2 // 4), rows_(ffn1_w_out), _Sharded("perm", N2 // 4), rows_(ffn2_w_out),
               cols(attn_w_qkv), rows_(attn_w_o), cols(lru_w_in), rows_(lru_w_out)]
    big = [ffn1_w_in, ffn1_w_out, ffn2_w_in, ffn2_w_out, attn_w_qkv, attn_w_o, lru_w_in, lru_w_out]
    blk_w = lru_w_ra.shape[3]
    small_sh = _Sharded("cols", CW)
    full = lambda a, sh: tuple(a.shape[:-1]) + (a.shape[-1] * N_CHIPS,) if sh.kind in ("cols", "perm") else \
        tuple(a.shape[:-2]) + (a.shape[-2] * N_CHIPS, a.shape[-1])
    mx, my = lax.axis_index("x"), lax.axis_index("y")
    pos = jnp.stack([2 * mx + my, 2 * my + mx]).astype(jnp.int32)
    placed = [_place(a, full(a, sh), BF16, sh, pos) for a, sh in zip(big, sh_list)]
    rarx_shape = (LR, RNN_BLOCKS, RW * N_CHIPS, 2 * blk_w)
    p_ra = _place(lru_w_ra, rarx_shape, BF16, _Sharded("rarx", RW, 0, blk_w), pos)
    placed.append(_place(lru_w_rx, rarx_shape, BF16, _Sharded("rarx", RW, blk_w, blk_w), pos, prev=p_ra))
    gnames = ["w_in1", "w_out1", "w_in2", "w_out2", "qkv", "wo", "lin", "lout", "rarx"]
    GW = dict(zip(gnames, placed))
    SH = dict(zip(gnames, sh_list + [_Sharded("rarx", RW, 0, 2 * blk_w)]))
    p_small = _place(pack_small(W).reshape(1, SMALL_ROWS, CW), (1, SMALL_ROWS, C), F32, small_sh, pos)

    def comm_for(stage, jobs):
        return stage([(GW[n], SH[n], l) for n, l in jobs]) if jobs else None

    def rebind(jobs, arrays):
        for (n, _), a in zip(jobs, arrays):
            GW[n] = a

    jobs0 = [(n, 0) for n in ("w_in1", "w_out1", "w_in2", "w_out2", "qkv", "wo")]
    got0 = _all_gather([(GW[n], SH[n], l) for n, l in jobs0] + [(p_small, small_sh, 0)])
    rebind(jobs0, got0)
    small = got0[-1].reshape(SMALL_ROWS, C)
    row_of = lambda r: small[r:r + 1]
    mixer_jobs = lambda l: [("qkv", l // 2), ("wo", l // 2)] if l % 2 == 0 else [("lin", l // 2), ("lout", l // 2), ("rarx", l // 2)]

    def ffn_forward(prev, hb, n_in, n_out, layer, g, b, jobs):
        gu, act, carried = _ffn_up(hb, GW[n_in], layer, comm_for(_gather_over_ici, jobs))
        rebind(jobs, carried)
        res = _proj_ln("ffn_down_ln", act, GW[n_out], layer, prev, g, b, alpha, 0.5, comm_for(_pass_to_sibling, jobs))
        if jobs:
            res, carried = res
            rebind(jobs, carried)
        return res, (gu, act)
    assert D == C, "packed small parameters assume d_model == d_rnn"

    c128, s128 = _rope_tables(T)
    sink_rows = [attn_sinks[j:j + 1] for j in range(LA)]

    x2 = x.reshape(T, D)
    prev = (x2, jnp.ones((1, D), F32), jnp.zeros((1, D), F32))
    hb = x2.astype(BF16)
    saved = []
    for i in range(L):
        j = i // 2
        lay = {}
        lay["hb0"] = hb
        nxt = i + 1
        gains = [row_of(o_g + 3 * i + k) for k in range(3)]
        biases = [row_of(o_b + 3 * i + k) for k in range(3)]
        jobs1 = [("w_in2", nxt), ("w_out2", nxt)] if nxt < L else []
        jobs2 = [("w_in1", nxt), ("w_out1", nxt)] + mixer_jobs(nxt) if nxt < L else []
        (xh, hb, rs), (lay["gu1"], lay["act1"]) = ffn_forward(prev, hb, "w_in1", "w_out1", i, gains[0], biases[0], jobs1)
        lay["ln1"], lay["hb1"], prev = (xh, rs), hb, (xh, gains[0], biases[0])
        if i % 2 == 0:
            qkv = _qkv_rope(hb, GW["qkv"], j, c128, s128)
            o = _attn_fwd(qkv, sink_rows[j])
            lay["qkv"], lay["o"] = qkv, o
            xh, hb, rs = _proj_ln("attn_out_ln", o, GW["wo"], j, prev, gains[1], biases[1], alpha, 1.0)
        else:
            bm = _tile(T, ROW_TILE)
            (xg,) = _matmul("lru_in", (T // bm, 1, 1), (hb, (bm, D), lambda p, q, k: (p, 0)),
                            (GW["lin"], (None, D, 2 * C), lambda p, q, k, j=j: (j, 0, 0)),
                            [((T, 2 * C), F32, (bm, 2 * C), lambda p, q: (p, 0))], (bm, 2 * C))
            cw = [row_of(o_cw + j * CONV_W + k) for k in range(CONV_W)]
            xc, xcb = _conv_fwd(xg, cw, row_of(o_cb + j))
            pre, a, b = _lru_gate_fwd(xc, xcb, GW["rarx"], j, row_of(o_ra + j), row_of(o_rx + j), row_of(o_lam + j))
            hs = _scan_fwd(a, b)
            y = _lru_out_fwd(hs, xg)
            lay.update(xg=xg, xc=xc, xcb=xcb, pre=pre, a=a, hs=hs, y=y, cw=cw)
            xh, hb, rs = _proj_ln("lru_out_ln", y, GW["lout"], j, prev, gains[1], biases[1], alpha, 1.0)
        lay["ln2"], lay["hb2"], prev = (xh, rs), hb, (xh, gains[1], biases[1])
        (xh, hb, rs), (lay["gu2"], lay["act2"]) = ffn_forward(prev, hb, "w_in2", "w_out2", i, gains[2], biases[2], jobs2)
        lay["ln3"], prev = (xh, rs), (xh, gains[2], biases[2])
        saved.append(lay)
    g_w_in1, g_w_out1, g_w_in2, g_w_out2, g_qkv, g_wo, g_lin, g_lout, g_rarx = [GW[n] for n in gnames]

    dh, loss_local = _loss_and_grad(prev, loss_target.reshape(T, D))
    loss = lax.psum(loss_local, ("x", "y", "c"))

    SH2 = dict(SH, small=small_sh)
    shard_stack = {"w_in1": ffn1_w_in.shape, "w_out1": ffn1_w_out.shape, "w_in2": ffn2_w_in.shape, "w_out2": ffn2_w_out.shape,
                   "qkv": attn_w_qkv.shape, "wo": attn_w_o.shape, "lin": lru_w_in.shape, "lout": lru_w_out.shape,
                   "rarx": (LR, RNN_BLOCKS, RW, 2 * blk_w), "small": (1, SMALL_ROWS, CW)}
    TOT = {n: lax.empty(s, F32) for n, s in shard_stack.items()}
    pend_exchange, pend_share = [], []

    def swap_share_comm(swap_jobs):
        share_jobs = list(pend_share)
        pend_share.clear()
        makers = []
        if swap_jobs:
            makers.append(_swap_stage([(g, SH2[n], 0) for n, _, g in swap_jobs]))
        if share_jobs:
            makers.append(_share_stage([(TOT[n], SH2[n], l) for n, l in share_jobs]))
        comm, _ = _combine(makers)

        def after(carried):
            for t, (n, l, _) in enumerate(swap_jobs):
                part = _pair_add_layer(carried[2 * t], SH2[n], 0, carried[2 * t + 1], c_arr)
                pend_exchange.append((n, l, part))
            for t, (n, _) in enumerate(share_jobs):
                TOT[n] = carried[2 * len(swap_jobs) + t]
        return comm, after

    def exchange_comm():
        jobs = list(pend_exchange)
        pend_exchange.clear()
        if not jobs:
            return None, None
        comm, _ = _combine([_exchange_stage([(part, SH2[n]) for n, _, part in jobs])])

        def after(carried):
            for t, (n, l, _) in enumerate(jobs):
                TOT[n] = _chip_sum_layer(carried[2 * t + 1], SH2[n], l, TOT[n], c_arr)
                pend_share.append((n, l))
        return comm, after

    def ffn_backward(dh, ln, g, hb_in, gu, act, n_in, n_out, layer, also_swap):
        dz, dyb, dg, db = _ln_bwd("ffn_ln_bwd", dh, ln[0], ln[1], g, 0.5)
        dgu = _ffn_dact(dyb, GW[n_out], layer, gu)
        g_out = _grad_tn("ffn_dwout", act, dyb, N2 // 4, D)
        comm, after = exchange_comm()
        g_in = _grad_tn("ffn_dwin", hb_in, dgu, D, N2 // 4, comm=comm)
        if comm is not None:
            g_in, carried = g_in
            after(carried)
        comm, after = swap_share_comm([(n_in, layer, g_in), (n_out, layer, g_out)] + also_swap)
        dh_prev, carried = _ffn_dx(dgu, GW[n_in], layer, dz, alpha, comm)
        after(carried)
        return dh_prev, dg, db

    sg = [None] * SMALL_ROWS
    d_sinks = [None] * LA
    for i in reversed(range(L)):
        j = i // 2
        lay = saved[i]
        dh, sg[o_g + 3 * i + 2], sg[o_b + 3 * i + 2] = ffn_backward(
            dh, lay["ln3"], row_of(o_g + 3 * i + 2), lay["hb2"], lay["gu2"], lay["act2"], "w_in2", "w_out2", i, [])
        if i % 2 == 0:
            dz, dmb, sg[o_g + 3 * i + 1], sg[o_b + 3 * i + 1] = _ln_bwd("attn_ln_bwd", dh, *lay["ln2"], row_of(o_g + 3 * i + 1), 1.0)
            d_wo = _grad_tn("attn_dwo", lay["o"], dmb, _tile(Q_COLS, 1024), D)
            do = _back_proj("attn_do", dmb, g_wo, j, BF16)
            dq, dkv, dsk = _attn_bwd(lay["qkv"], do, sink_rows[j])
            d_sinks[j] = jnp.sum(dsk.reshape(N_HEADS, ATTN_BLOCK), axis=1)
            dqkv = _rope_bwd(dq, dkv, c128, s128)
            d_qkv = _grad_tn("attn_dwqkv", lay["hb1"], dqkv, D, dqkv.shape[1])
            comm, after = swap_share_comm([("wo", j, d_wo), ("qkv", j, d_qkv)])
            dh, carried = _input_grad("attn_dx", dqkv, g_qkv, j, dz, alpha, comm)
            after(carried)
        else:
            dz, dmb, sg[o_g + 3 * i + 1], sg[o_b + 3 * i + 1] = _ln_bwd("lru_ln_bwd", dh, *lay["ln2"], row_of(o_g + 3 * i + 1), 1.0)
            d_lout = _grad_tn("lru_dwout", lay["y"], dmb, C, D)
            dy = _back_proj("lru_dy", dmb, g_lout, j, F32)
            dhs, dgb = _lru_out_bwd(dy, lay["hs"], lay["xg"])
            adj = _scan_bwd(lay["a"], dhs)
            dpre, dxc_direct, sg[o_ra + j], sg[o_rx + j], sg[o_lam + j] = _lru_gate_bwd(
                lay["pre"], lay["xc"], adj, lay["hs"], row_of(o_ra + j), row_of(o_rx + j), row_of(o_lam + j))
            blk = C // RNN_BLOCKS
            d_rarx = _grad_tn("lru_dwgates", lay["xcb"], dpre, blk, 2 * blk, block_diag=True)
            bm = _tile(T, ROW_TILE)
            (dxc,) = _matmul("lru_dxc", (T // bm, RNN_BLOCKS, 1), (dpre, (bm, 2 * blk), lambda p, q, k: (p, q)),
                             (g_rarx, (None, None, blk, 2 * blk), lambda p, q, k, j=j: (j, q, 0, 0)),
                             [((T, C), F32, (bm, blk), lambda p, q: (p, q))], (bm, blk), tb=True,
                             extras=[(dxc_direct, (bm, blk), lambda p, q: (p, q))], epilogue=lambda acc, d: (acc + d,))
            res = _conv_bwd(dxc, lay["xg"], dgb, lay["cw"])
            dxg = res[0]
            for k in range(CONV_W):
                sg[o_cw + j * CONV_W + k] = res[1 + k]
            sg[o_cb + j] = res[1 + CONV_W]
            d_lin = _grad_tn("lru_dwin", lay["hb1"], dxg, D, _tile(2 * C, 1024))
            comm, after = swap_share_comm([("lout", j, d_lout), ("rarx", j, d_rarx), ("lin", j, d_lin)])
            dh, carried = _input_grad("lru_dx", dxg, g_lin, j, dz, alpha, comm)
            after(carried)
        dh, sg[o_g + 3 * i], sg[o_b + 3 * i] = ffn_backward(
            dh, lay["ln1"], row_of(o_g + 3 * i), lay["hb0"], lay["gu1"], lay["act1"], "w_in1", "w_out1", i, [])
    grad_x = dh.reshape(x.shape)

    sink_vec = jnp.concatenate(d_sinks).reshape(1, n_sink)
    sg[o_sink] = jnp.tile(jnp.concatenate([sink_vec, jnp.zeros((1, CW - n_sink), F32)], axis=1), (1, N_CHIPS))
    zero_row = jnp.zeros((1, C), F32)
    d_small = jnp.concatenate([zero_row if r is None else r for r in sg], axis=0).reshape(1, SMALL_ROWS, C)

    comm, after = swap_share_comm([("small", 0, d_small)])
    after(_comm_call("grad_tail_swap", comm))
    comm, after = exchange_comm()
    after(_comm_call("grad_tail_exchange", comm))
    comm, after = swap_share_comm([])
    after(_comm_call("grad_tail_share", comm))
    t_w_in1, t_w_out1, t_w_in2, t_w_out2, t_qkv, t_wo, t_lin, t_lout, t_rarx = [TOT[n] for n in gnames]
    t_small = TOT["small"].reshape(SMALL_ROWS, CW)

    G = {"ffn1_w_in": t_w_in1, "ffn1_w_out": t_w_out1, "ffn2_w_in": t_w_in2, "ffn2_w_out": t_w_out2,
         "attn_w_qkv": t_qkv, "attn_w_o": t_wo, "lru_w_in": t_lin, "lru_w_out": t_lout,
         "lru_w_ra": t_rarx[..., :blk_w], "lru_w_rx": t_rarx[..., blk_w:]}

    def unpack_small(p):
        return {"ln_g": p[o_g:o_g + 3 * L].reshape(ln_g.shape), "ln_b": p[o_b:o_b + 3 * L].reshape(ln_b.shape),
                "lru_conv_w": p[o_cw:o_cw + LR * CONV_W].reshape(lru_conv_w.shape), "lru_conv_b": p[o_cb:o_cb + LR],
                "lru_b_ra": p[o_ra:o_ra + LR], "lru_b_rx": p[o_rx:o_rx + LR], "lru_lambda": p[o_lam:o_lam + LR],
                "attn_sinks": p[o_sink, :n_sink].reshape(attn_sinks.shape)}

    G.update(unpack_small(t_small))

    delta, new_m, new_v = {}, {}, {}
    small_names = ["ln_g", "ln_b", "lru_conv_w", "lru_conv_b", "lru_b_ra", "lru_b_rx", "lru_lambda", "attn_sinks"]
    for n in names:
        if n not in small_names:
            G[n], delta[n], new_m[n], new_v[n] = _adamw(W[n], G[n], M[n], V[n])
    _, ds, ms, vs = _adamw(pack_small(W), t_small, pack_small(M), pack_small(V))
    for d, p in ((delta, ds), (new_m, ms), (new_v, vs)):
        d.update(unpack_small(p))

    return (loss, grad_x, *[G[n] for n in names], *[delta[n] for n in names], *[new_m[n] for n in names], *[new_v[n] for n in names])
```

```python
import functools
import math

import jax
import jax.numpy as jnp
from jax import lax
from jax.experimental import pallas as pl
from jax.experimental.pallas import tpu as pltpu

F32 = jnp.float32
BF16 = jnp.bfloat16
MESH = pl.DeviceIdType.MESH

N_HEADS = 16
N_KV_HEADS = 4
HEAD_DIM = 64
GROUP = N_HEADS // N_KV_HEADS
ATTN_BLOCK = 128
ROPE_THETA = 10000.0
RNN_BLOCKS = 4
CONV_W = 4
LRU_C = 8.0
LN_EPS = 1e-5
ADAM_LR = 0.001
ADAM_B1 = 0.9
ADAM_B2 = 0.999
ADAM_EPS = 1e-08
ADAM_WD = 0.01
ADAM_STEP = 10
N_CHIPS = 4
NEG_BIG = -1e30
VMEM_LIMIT_MB = 56
ROW_TILE = 512
SMALL_ROWS = 96


def _cparams(sem):
    return pltpu.CompilerParams(dimension_semantics=sem, vmem_limit_bytes=VMEM_LIMIT_MB << 20)


def _tile(n, pref):
    if n <= pref:
        return n
    for t in range(pref - pref % 16, 0, -16):
        if n % t == 0:
            return t
    raise ValueError((n, pref))


class _SideComm:
    def __init__(self, arrays, n_sems, start, finish):
        self.arrays, self.n_sems, self.start, self.finish = list(arrays), n_sems, start, finish


def _hosted_call(body, comm, *, name, grid, in_specs, out_specs, out_shape, operands, scratch_shapes=(),
                 input_output_aliases=None, compiler_params=None):
    aliases = dict(input_output_aliases or {})
    in_specs, out_specs, out_shape = list(in_specs), list(out_specs), list(out_shape)
    operands, scratch_shapes = list(operands), list(scratch_shapes)
    n_in, n_out = len(in_specs), len(out_specs)
    if comm is None:
        res = pl.pallas_call(body, name=name, grid=grid, in_specs=in_specs, out_specs=out_specs, out_shape=out_shape,
                             scratch_shapes=scratch_shapes, input_output_aliases=aliases,
                             compiler_params=compiler_params)(*operands)
        return list(res), []
    m = len(comm.arrays)
    hbm = pl.BlockSpec(memory_space=pl.ANY)
    real = [t for t, arr in enumerate(comm.arrays) if not isinstance(arr, jax.ShapeDtypeStruct)]
    mi = len(real)
    for ti, t in enumerate(real):
        aliases[n_in + ti] = n_out + t
    n_scr = len(scratch_shapes)

    def hosted(*refs):
        ins = refs[:n_in]
        outs = refs[n_in + mi:n_in + mi + n_out]
        carried = refs[n_in + mi + n_out:n_in + mi + n_out + m]
        scr = refs[n_in + mi + n_out + m:n_in + mi + n_out + m + n_scr]
        send, recv = refs[-2:]
        first = functools.reduce(jnp.logical_and, [pl.program_id(d) == 0 for d in range(len(grid))])
        last = functools.reduce(jnp.logical_and, [pl.program_id(d) == grid[d] - 1 for d in range(len(grid))])

        @pl.when(first)
        def _():
            comm.start(carried, send, recv)

        body(*ins, *outs, *scr)

        @pl.when(last)
        def _():
            comm.finish(carried, send, recv)

    res = pl.pallas_call(
        hosted, name=name, grid=grid, in_specs=in_specs + [hbm] * mi, out_specs=out_specs + [hbm] * m,
        out_shape=out_shape + [jax.ShapeDtypeStruct(a.shape, a.dtype) for a in comm.arrays],
        scratch_shapes=scratch_shapes + [pltpu.SemaphoreType.DMA((comm.n_sems,)), pltpu.SemaphoreType.DMA((comm.n_sems,))],
        input_output_aliases=aliases, compiler_params=compiler_params)(*operands, *[comm.arrays[t] for t in real])
    return list(res[:n_out]), list(res[n_out:])


def _combine(makers):
    comms, base = [], 0
    for mk in makers:
        comms.append(mk(base))
        base += comms[-1].n_sems
    offs = [0]
    for cm in comms:
        offs.append(offs[-1] + len(cm.arrays))

    def start(refs, send, recv):
        for cm, o in zip(comms, offs):
            cm.start(refs[o:o + len(cm.arrays)], send, recv)

    def finish(refs, send, recv):
        for cm, o in zip(comms, offs):
            cm.finish(refs[o:o + len(cm.arrays)], send, recv)

    return _SideComm(sum((cm.arrays for cm in comms), []), base, start, finish), [len(cm.arrays) for cm in comms]


def _comm_call(name, comm):
    def body():
        pass

    _, carried = _hosted_call(body, comm, name=name, grid=(1,), in_specs=[], out_specs=[], out_shape=[], operands=[])
    return carried


def _matmul(name, grid, a, b, outs, acc_shape, *, ta=False, tb=False, extras=(), epilogue=None,
            n_outer=False, alias_in=None, comm=None):
    gm, gn, gk = grid
    if n_outer:
        g = (gn, gm, gk)
        ijk = lambda p, q, k: (q, p, k)
    else:
        g = (gm, gn, gk)
        ijk = lambda p, q, k: (p, q, k)
    w3 = lambda f: (lambda p, q, k: f(*ijk(p, q, k)))
    w2 = lambda f: (lambda p, q, k: f(*ijk(p, q, k)[:2]))
    in_specs = [pl.BlockSpec(a[1], w3(a[2])), pl.BlockSpec(b[1], w3(b[2]))]
    in_specs += [pl.BlockSpec(e[1], w2(e[2])) for e in extras]
    operands = [a[0], b[0]] + [e[0] for e in extras]
    io_alias = {}
    n_alias = 0
    if alias_in is not None:
        in_specs.append(pl.BlockSpec(memory_space=pl.ANY))
        operands.append(alias_in)
        io_alias = {len(operands) - 1: 0}
        n_alias = 1
    ne, no = len(extras), len(outs)
    dims = (((0 if ta else 1,), (1 if tb else 0,)), ((), ()))

    def body(*refs):
        a_ref, b_ref = refs[0], refs[1]
        e_refs = refs[2:2 + ne]
        o_refs = refs[2 + ne + n_alias:2 + ne + n_alias + no]
        part = lax.dot_general(a_ref[...], b_ref[...], dims, preferred_element_type=F32)

        def finish(acc):
            res = epilogue(acc, *[r[...] for r in e_refs]) if epilogue is not None else (acc,)
            for r, v in zip(o_refs, res):
                r[...] = v.astype(r.dtype)

        if gk == 1:
            finish(part)
        else:
            acc_ref = refs[-1]
            k = pl.program_id(2)

            @pl.when(k == 0)
            def _():
                acc_ref[...] = part

            @pl.when(k > 0)
            def _():
                acc_ref[...] += part

            @pl.when(k == gk - 1)
            def _():
                finish(acc_ref[...])

    res, carried = _hosted_call(
        body, comm, name=name, grid=g, in_specs=in_specs,
        out_specs=[pl.BlockSpec(o[2], w2(o[3])) for o in outs],
        out_shape=[jax.ShapeDtypeStruct(o[0], o[1]) for o in outs],
        operands=operands,
        scratch_shapes=[pltpu.VMEM(acc_shape, F32)] if gk > 1 else [],
        input_output_aliases=io_alias,
        compiler_params=_cparams(("arbitrary", "arbitrary", "arbitrary")))
    return res if comm is None else (res, carried)


def _rowwise(name, nsteps, fn, ins, outs, accs=(), prefetch=None, into=None):
    n_in, n_out, n_acc = len(ins), len(outs), len(accs)
    n_pre = 0 if prefetch is None else 1
    n_into = 0 if into is None else 1

    def body(*refs):
        refs = refs[n_pre:]
        i = pl.program_id(0)
        res = fn(i, *[r[...] for r in refs[:n_in]])
        refs = refs[:n_in] + refs[n_in + n_into:]
        for r, v in zip(refs[n_in:n_in + n_out], res[:n_out]):
            r[...] = v.astype(r.dtype)
        acc_refs = refs[n_in + n_out:n_in + n_out + n_acc]
        if n_acc:
            @pl.when(i == 0)
            def _():
                for r in acc_refs:
                    r[...] = jnp.zeros_like(r)

            for r, v in zip(acc_refs, res[n_out:]):
                r[...] += v

    if prefetch is None:
        zero = lambda shape: (lambda i: (0,) * len(shape))
    else:
        zero = lambda shape: (lambda i, p: (0,) * len(shape))
    in_specs = [pl.BlockSpec(b, m) for _, b, m in ins]
    operands = [x[0] for x in ins]
    alias = {}
    if into is not None:
        in_specs.append(pl.BlockSpec(memory_space=pl.ANY))
        operands.append(into)
        alias = {n_pre + n_in: 0}
    out_specs = [pl.BlockSpec(o[2], o[3]) for o in outs] + [pl.BlockSpec(s, zero(s)) for s in accs]
    out_shape = [jax.ShapeDtypeStruct(o[0], o[1]) for o in outs] + [jax.ShapeDtypeStruct(s, F32) for s in accs]
    cp = _cparams(("arbitrary",))
    if prefetch is None:
        call = pl.pallas_call(body, name=name, grid=(nsteps,), in_specs=in_specs, out_specs=out_specs,
                              out_shape=out_shape, input_output_aliases=alias, compiler_params=cp)
        return call(*operands)
    gs = pltpu.PrefetchScalarGridSpec(num_scalar_prefetch=1, grid=(nsteps,), in_specs=in_specs, out_specs=out_specs)
    call = pl.pallas_call(body, name=name, grid_spec=gs, out_shape=out_shape, input_output_aliases=alias, compiler_params=cp)
    return call(prefetch, *operands)


def _rows(arr, tr, cols=None, cb=0):
    cols = arr.shape[1] if cols is None else cols
    return (arr, (tr, cols), lambda i: (i, cb))


def _whole(arr):
    return (arr, arr.shape, lambda i: (0,) * arr.ndim)


def _gelu_tanh(x):
    c = math.sqrt(2.0 / math.pi)
    return x * (0.5 * (1.0 + jnp.tanh(c * (x + 0.044715 * (x * x * x)))))


@jax.custom_jvp
def _expm1(x):
    return jnp.where(jnp.abs(x) < 0.5, jnp.tanh(0.5 * x) * (jnp.exp(x) + 1.0), jnp.exp(x) - 1.0)


@_expm1.defjvp
def _expm1_jvp(primals, tangents):
    (x,), (t,) = primals, tangents
    return _expm1(x), jnp.exp(x) * t


def _log_sigmoid(x):
    return jnp.minimum(x, 0.0) - jnp.log1p(jnp.exp(-jnp.abs(x)))


def _lru_gates(pre, xc, b_ra, b_rx, lam):
    w = xc.shape[-1]
    r = jax.nn.sigmoid(pre[:, :w] + b_ra)
    ig = jax.nn.sigmoid(pre[:, w:] + b_rx)
    log_a = LRU_C * r * _log_sigmoid(lam)
    a = jnp.exp(log_a)
    b = jnp.sqrt(-_expm1(2.0 * log_a)) * (ig * xc)
    return a, b


def _swap_halves(x):
    n = x.shape[1]
    first = (lax.broadcasted_iota(jnp.int32, x.shape, 1) % HEAD_DIM) < (HEAD_DIM // 2)
    return jnp.where(first, pltpu.roll(x, n - HEAD_DIM // 2, 1), pltpu.roll(x, HEAD_DIM // 2, 1))


def _shift_down(prev8, cur, s):
    ext = jnp.concatenate([prev8, cur], axis=0)
    return pltpu.roll(ext, s, 0)[8:]


def _shift_up(cur, next8, s):
    ext = jnp.concatenate([cur, next8], axis=0)
    return pltpu.roll(ext, ext.shape[0] - s, 0)[:cur.shape[0]]


LANES = 128


def _ln_epilogue(alpha, scale):
    def epi(acc, prev, gp, bp, g, b):
        z = alpha * (prev * gp + bp) + scale * acc
        mu = jnp.mean(z, axis=-1, keepdims=True)
        xc = z - mu
        var = jnp.mean(xc * xc, axis=-1, keepdims=True)
        rstd = lax.rsqrt(var + LN_EPS)
        xhat = xc * rstd
        return xhat, xhat * g + b, jnp.broadcast_to(rstd, (rstd.shape[0], LANES))
    return epi


def _proj_ln(name, act, w, layer, prev, g, b, alpha, scale, comm=None):
    T, K = act.shape
    D = w.shape[2]
    bm = _tile(T, 2 * ROW_TILE)
    row = lambda i, j: (i, 0)
    par = lambda i, j: (0, 0)
    return _matmul(
        name, (T // bm, 1, 1),
        (act, (bm, K), lambda i, j, k: (i, 0)), (w, (None, K, D), lambda i, j, k: (layer, 0, 0)),
        [((T, D), F32, (bm, D), row), ((T, D), BF16, (bm, D), row), ((T, LANES), F32, (bm, LANES), row)],
        (bm, D),
        extras=[(prev[0], (bm, D), row), (prev[1], (1, D), par), (prev[2], (1, D), par), (g, (1, D), par), (b, (1, D), par)],
        epilogue=_ln_epilogue(alpha, scale), comm=comm)


def _ln_bwd(name, dh, xhat, rstd, g, scale):
    T, D = xhat.shape
    tr = _tile(T, ROW_TILE)

    def fn(i, dh, xhat, rstd, g):
        dxh = dh * g
        rs = jnp.tile(rstd, (1, D // LANES))
        dz = rs * (dxh - jnp.mean(dxh, axis=-1, keepdims=True) - xhat * jnp.mean(dxh * xhat, axis=-1, keepdims=True))
        return (dz, scale * dz, jnp.sum(dh * xhat, axis=0, keepdims=True), jnp.sum(dh, axis=0, keepdims=True))

    row = lambda i: (i, 0)
    return _rowwise(name, T // tr, fn, [_rows(dh, tr), _rows(xhat, tr), _rows(rstd, tr), _whole(g)],
                    [((T, D), F32, (tr, D), row), ((T, D), BF16, (tr, D), row)], accs=[(1, D), (1, D)])


def _grad_tn(name, a, b, bm, bn, bk=4 * ROW_TILE, block_diag=False, comm=None):
    T, M = a.shape
    N = b.shape[1]
    bk = _tile(T, bk)
    if not block_diag:
        shape, oblk, omap = (1, M, N), (None, bm, bn), (lambda i, j: (0, i, j))
        amap, gm = (lambda i, j, k: (k, i)), M // bm
    else:
        shape, oblk, omap = (1, N // bn, bm, bn), (None, None, bm, bn), (lambda i, j: (0, j, 0, 0))
        amap, gm = (lambda i, j, k: (k, j)), 1
    res = _matmul(name, (gm, N // bn, T // bk), (a, (bk, bm), amap), (b, (bk, bn), lambda i, j, k: (k, j)),
                  [(shape, BF16, oblk, omap)], (bm, bn), ta=True, comm=comm)
    return res[0] if comm is None else (res[0][0], res[1])


MXU_COLS = 256


def _col_chunks(width):
    return [(s, min(MXU_COLS, width - s)) for s in range(0, width, MXU_COLS)]


def _ffn_up(hb, w_in, layer, comm=None):
    T, D = hb.shape
    N2 = w_in.shape[2]
    wd = N2 // 4
    bm = _tile(T, 2 * ROW_TILE)

    def body(a_ref, w_ref, gu_ref, act_ref):
        a = a_ref[...]
        for s, n in _col_chunks(wd):
            gg = jnp.dot(a, w_ref[:, s:s + n], preferred_element_type=F32)
            uu = jnp.dot(a, w_ref[:, wd + s:wd + s + n], preferred_element_type=F32)
            sg = jax.nn.sigmoid(gg)
            silu = gg * sg
            gu_ref[:, s:s + n] = (uu * (sg + silu * (1.0 - sg))).astype(gu_ref.dtype)
            gu_ref[:, wd + s:wd + s + n] = silu.astype(gu_ref.dtype)
            act_ref[:, s:s + n] = (silu * uu).astype(act_ref.dtype)

    (gu, act), carried = _hosted_call(
        body, comm, name="ffn_up", grid=(2, T // bm),
        in_specs=[pl.BlockSpec((bm, D), lambda j, i: (i, 0)), pl.BlockSpec((None, D, 2 * wd), lambda j, i: (layer, 0, j))],
        out_specs=[pl.BlockSpec((bm, 2 * wd), lambda j, i: (i, j)), pl.BlockSpec((bm, wd), lambda j, i: (i, j))],
        out_shape=[jax.ShapeDtypeStruct((T, N2), BF16), jax.ShapeDtypeStruct((T, N2 // 2), BF16)],
        operands=[hb, w_in], compiler_params=_cparams(("arbitrary", "arbitrary")))
    return gu, act, carried


def _ffn_dact(dyb, w_out, layer, gu):
    T, D = dyb.shape
    N2 = gu.shape[1]
    wd = N2 // 4
    bm = _tile(T, 2 * ROW_TILE)

    def body(dy_ref, w_ref, gu_ref, o_ref):
        dy = dy_ref[...]
        for s, n in _col_chunks(wd):
            dact = lax.dot_general(dy, w_ref[s:s + n, :], (((1,), (1,)), ((), ())), preferred_element_type=F32)
            o_ref[:, s:s + n] = (dact * gu_ref[:, s:s + n].astype(F32)).astype(o_ref.dtype)
            o_ref[:, wd + s:wd + s + n] = (dact * gu_ref[:, wd + s:wd + s + n].astype(F32)).astype(o_ref.dtype)

    return pl.pallas_call(
        body, name="ffn_dact", grid=(2, T // bm),
        in_specs=[pl.BlockSpec((bm, D), lambda j, i: (i, 0)), pl.BlockSpec((None, wd, D), lambda j, i: (layer, j, 0)),
                  pl.BlockSpec((bm, 2 * wd), lambda j, i: (i, j))],
        out_specs=pl.BlockSpec((bm, 2 * wd), lambda j, i: (i, j)),
        out_shape=jax.ShapeDtypeStruct((T, N2), BF16),
        compiler_params=_cparams(("arbitrary", "arbitrary")))(dyb, w_out, gu)


def _ffn_dx(dgu, w_in, layer, dz, alpha, comm=None):
    T, N2 = dgu.shape
    D = w_in.shape[1]
    bm = _tile(T, ROW_TILE)
    res = _matmul(
        "ffn_dx", (T // bm, 1, 1),
        (dgu, (bm, N2), lambda i, j, k: (i, 0)), (w_in, (None, D, N2), lambda i, j, k: (layer, 0, 0)),
        [((T, D), F32, (bm, D), lambda i, j: (i, 0))], (bm, D), tb=True,
        extras=[(dz, (bm, D), lambda i, j: (i, 0))], epilogue=lambda acc, dzb: (alpha * dzb + acc,), comm=comm)
    return res[0] if comm is None else (res[0][0], res[1])


def _input_grad(name, dy, w, layer, dz, alpha, comm):
    T, N = dy.shape
    D = w.shape[1]
    bm = _tile(T, ROW_TILE)
    (out,), carried = _matmul(
        name, (T // bm, 1, 1),
        (dy, (bm, N), lambda i, j, k: (i, 0)), (w, (None, D, N), lambda i, j, k: (layer, 0, 0)),
        [((T, D), F32, (bm, D), lambda i, j: (i, 0))], (bm, D), tb=True,
        extras=[(dz, (bm, D), lambda i, j: (i, 0))], epilogue=lambda acc, dzb: (alpha * dzb + acc,), comm=comm)
    return out, carried


def _back_proj(name, dy, w, layer, dtype):
    T, D = dy.shape
    K = w.shape[1]
    bm = _tile(T, ROW_TILE)
    (out,) = _matmul(
        name, (T // bm, 1, 1),
        (dy, (bm, D), lambda i, j, k: (i, 0)), (w, (None, K, D), lambda i, j, k: (layer, 0, 0)),
        [((T, K), dtype, (bm, K), lambda i, j: (i, 0))], (bm, K), tb=True)
    return out


def _rope_tables(T):
    pos = jnp.arange(T, dtype=F32)
    inv_freq = ROPE_THETA ** (-jnp.arange(0, HEAD_DIM, 2, dtype=F32) / HEAD_DIM)
    ang = pos[:, None] * inv_freq[None, :]
    cos, sin = jnp.cos(ang), jnp.sin(ang)
    c128 = jnp.tile(cos, (1, 4))
    s128 = jnp.tile(jnp.concatenate([-sin, sin], axis=1), (1, 2))
    return c128, s128


QK_COLS = (N_HEADS + N_KV_HEADS) * HEAD_DIM
Q_COLS = N_HEADS * HEAD_DIM
KV_COLS = N_KV_HEADS * HEAD_DIM
Q_SCALE = HEAD_DIM ** -0.5


def _qkv_rope(hb, w_qkv, layer, c128, s128):
    T, D = hb.shape
    N = w_qkv.shape[2]
    bm = _tile(T, ROW_TILE)

    def epi(acc, c, s):
        x = acc[:, :QK_COLS]
        rep = QK_COLS // 128
        r = x * jnp.tile(c, (1, rep)) + _swap_halves(x) * jnp.tile(s, (1, rep))
        return (jnp.concatenate([r[:, :Q_COLS] * Q_SCALE, r[:, Q_COLS:], acc[:, QK_COLS:]], axis=1),)

    (qkv,) = _matmul(
        "qkv_rope", (T // bm, 1, 1),
        (hb, (bm, D), lambda i, j, k: (i, 0)), (w_qkv, (None, D, N), lambda i, j, k: (layer, 0, 0)),
        [((T, N), BF16, (bm, N), lambda i, j: (i, 0))], (bm, N),
        extras=[(c128, (bm, 128), lambda i, j: (i, 0)), (s128, (bm, 128), lambda i, j: (i, 0))], epilogue=epi)
    return qkv


def _rope_bwd(dq, dkv, c128, s128):
    T = dq.shape[0]
    tr = _tile(T, ROW_TILE)

    def fn(i, dq, dkv, c, s):
        dx = jnp.concatenate([dq * Q_SCALE, dkv[:, :KV_COLS]], axis=1)
        rep = QK_COLS // 128
        d = dx * jnp.tile(c, (1, rep)) + _swap_halves(dx * jnp.tile(s, (1, rep)))
        return (jnp.concatenate([d, dkv[:, KV_COLS:]], axis=1),)

    N = Q_COLS + 2 * KV_COLS
    (out,) = _rowwise("rope_bwd", T // tr, fn, [_rows(dq, tr), _rows(dkv, tr), _rows(c128, tr), _rows(s128, tr)],
                      [((T, N), BF16, (tr, N), lambda i: (i, 0))])
    return out


def _attn_mask(first_block):
    q_pos = lax.broadcasted_iota(jnp.int32, (GROUP * ATTN_BLOCK, 2 * ATTN_BLOCK), 0) & (ATTN_BLOCK - 1)
    col = lax.broadcasted_iota(jnp.int32, (GROUP * ATTN_BLOCK, 2 * ATTN_BLOCK), 1)
    dist = q_pos + ATTN_BLOCK - col
    return (dist >= 0) & (dist < ATTN_BLOCK) & ((col >= ATTN_BLOCK) | jnp.logical_not(first_block))


def _sink_column(sk_ref, kvh):
    rg = lax.broadcasted_iota(jnp.int32, (GROUP * ATTN_BLOCK, 1), 0) // ATTN_BLOCK
    col = jnp.full((GROUP * ATTN_BLOCK, 1), sk_ref[0, kvh * GROUP], F32)
    for gi in range(1, GROUP):
        col = jnp.where(rg == gi, sk_ref[0, kvh * GROUP + gi], col)
    return col


def _stack_heads(x, kvh):
    return jnp.concatenate([x[:, (kvh * GROUP + gi) * HEAD_DIM:(kvh * GROUP + gi + 1) * HEAD_DIM] for gi in range(GROUP)], axis=0)


def _unstack_heads(parts):
    cols = []
    for p in parts:
        cols += [p[gi * ATTN_BLOCK:(gi + 1) * ATTN_BLOCK] for gi in range(GROUP)]
    return jnp.concatenate(cols, axis=1)


def _attn_softmax(q4, kb, mask, sink):
    s = lax.dot_general(q4, kb, (((1,), (1,)), ((), ())), preferred_element_type=F32)
    s = jnp.where(mask, s, NEG_BIG)
    m = jnp.maximum(jnp.max(s, axis=1, keepdims=True), sink)
    p = jnp.exp(s - m)
    e_sink = jnp.exp(sink - m)
    den = jnp.sum(p, axis=1, keepdims=True) + e_sink
    return p / den, e_sink / den


def _attn_fwd(qkv, sinks):
    T = qkv.shape[0]
    nb = T // ATTN_BLOCK
    kcb, vcb = Q_COLS // KV_COLS, Q_COLS // KV_COLS + 1

    def body(q_ref, kc_ref, kp_ref, vc_ref, vp_ref, sk_ref, o_ref):
        i = pl.program_id(0)
        mask = _attn_mask(i == 0)
        q = q_ref[...]
        kband = jnp.concatenate([kp_ref[...], kc_ref[...]], axis=0)
        vband = jnp.concatenate([vp_ref[...], vc_ref[...]], axis=0)
        parts = []
        for kvh in range(N_KV_HEADS):
            hs = slice(kvh * HEAD_DIM, (kvh + 1) * HEAD_DIM)
            pn, _ = _attn_softmax(_stack_heads(q, kvh), kband[:, hs], mask, _sink_column(sk_ref, kvh))
            parts.append(jnp.dot(pn.astype(BF16), vband[:, hs], preferred_element_type=F32))
        o_ref[...] = _unstack_heads(parts).astype(o_ref.dtype)

    prev = lambda i: jnp.maximum(i - 1, 0)
    return pl.pallas_call(
        body, name="attn_fwd", grid=(nb,),
        in_specs=[pl.BlockSpec((ATTN_BLOCK, Q_COLS), lambda i: (i, 0)),
                  pl.BlockSpec((ATTN_BLOCK, KV_COLS), lambda i: (i, kcb)),
                  pl.BlockSpec((ATTN_BLOCK, KV_COLS), lambda i: (prev(i), kcb)),
                  pl.BlockSpec((ATTN_BLOCK, KV_COLS), lambda i: (i, vcb)),
                  pl.BlockSpec((ATTN_BLOCK, KV_COLS), lambda i: (prev(i), vcb)),
                  pl.BlockSpec(memory_space=pltpu.SMEM)],
        out_specs=pl.BlockSpec((ATTN_BLOCK, Q_COLS), lambda i: (i, 0)),
        out_shape=jax.ShapeDtypeStruct((T, Q_COLS), BF16),
        compiler_params=_cparams(("arbitrary",)),
    )(qkv, qkv, qkv, qkv, qkv, sinks)


def _attn_bwd(qkv, do, sinks):
    T = qkv.shape[0]
    nb = T // ATTN_BLOCK
    kcb, vcb = Q_COLS // KV_COLS, Q_COLS // KV_COLS + 1
    B = ATTN_BLOCK

    def body(q_ref, kc_ref, kp_ref, vc_ref, vp_ref, do_ref, sk_ref, dq_ref, dkv_ref, dsk_ref, carry_ref):
        i = pl.program_id(0)

        @pl.when(i == 0)
        def _():
            carry_ref[...] = jnp.zeros_like(carry_ref)
            dsk_ref[...] = jnp.zeros_like(dsk_ref)

        @pl.when(i < nb)
        def _():
            mask = _attn_mask(i == 0)
            q = q_ref[...]
            do_blk = do_ref[...]
            kband = jnp.concatenate([kp_ref[...], kc_ref[...]], axis=0)
            vband = jnp.concatenate([vp_ref[...], vc_ref[...]], axis=0)
            dq_parts, dk_parts, dv_parts = [], [], []
            for kvh in range(N_KV_HEADS):
                hs = slice(kvh * HEAD_DIM, (kvh + 1) * HEAD_DIM)
                q4 = _stack_heads(q, kvh)
                do4 = _stack_heads(do_blk, kvh)
                kb, vb = kband[:, hs], vband[:, hs]
                pn, p_sink = _attn_softmax(q4, kb, mask, _sink_column(sk_ref, kvh))
                dp = lax.dot_general(do4, vb, (((1,), (1,)), ((), ())), preferred_element_type=F32)
                delta = jnp.sum(pn * dp, axis=1, keepdims=True)
                ds = (pn * (dp - delta)).astype(BF16)
                dsk_ref[kvh] += -(p_sink * delta)
                dq_parts.append(jnp.dot(ds, kb, preferred_element_type=F32))
                dk_parts.append(lax.dot_general(ds, q4, (((0,), (0,)), ((), ())), preferred_element_type=F32))
                dv_parts.append(lax.dot_general(pn.astype(BF16), do4, (((0,), (0,)), ((), ())), preferred_element_type=F32))
            dq_ref[...] = _unstack_heads(dq_parts)
            dkv = jnp.concatenate(dk_parts + dv_parts, axis=1)
            dkv_ref[...] = carry_ref[...] + dkv[:B]
            carry_ref[...] = dkv[B:]

        @pl.when(i == nb)
        def _():
            dkv_ref[...] = carry_ref[...]

    cur = lambda i: jnp.minimum(i, nb - 1)
    prev = lambda i: jnp.maximum(cur(i) - 1, 0)
    lag = lambda i: jnp.maximum(i - 1, 0)
    return pl.pallas_call(
        body, name="attn_bwd", grid=(nb + 1,),
        in_specs=[pl.BlockSpec((B, Q_COLS), lambda i: (cur(i), 0)),
                  pl.BlockSpec((B, KV_COLS), lambda i: (cur(i), kcb)),
                  pl.BlockSpec((B, KV_COLS), lambda i: (prev(i), kcb)),
                  pl.BlockSpec((B, KV_COLS), lambda i: (cur(i), vcb)),
                  pl.BlockSpec((B, KV_COLS), lambda i: (prev(i), vcb)),
                  pl.BlockSpec((B, Q_COLS), lambda i: (cur(i), 0)),
                  pl.BlockSpec(memory_space=pltpu.SMEM)],
        out_specs=[pl.BlockSpec((B, Q_COLS), lambda i: (cur(i), 0)),
                   pl.BlockSpec((B, 2 * KV_COLS), lambda i: (lag(i), 0)),
                   pl.BlockSpec((N_KV_HEADS, GROUP * B, 1), lambda i: (0, 0, 0))],
        out_shape=[jax.ShapeDtypeStruct((T, Q_COLS), F32), jax.ShapeDtypeStruct((T, 2 * KV_COLS), F32),
                   jax.ShapeDtypeStruct((N_KV_HEADS, GROUP * B, 1), F32)],
        scratch_shapes=[pltpu.VMEM((B, 2 * KV_COLS), F32)],
        compiler_params=_cparams(("arbitrary",)),
    )(qkv, qkv, qkv, qkv, qkv, do, sinks)


def _halo_prev(arr, tr, cols, cb=0):
    per = tr // 8
    return (arr, (8, cols), lambda i: (jnp.maximum(i * per - 1, 0), cb))


def _halo_next(arr, tr, cols, cb=0):
    per = tr // 8
    last = arr.shape[0] // 8 - 1
    return (arr, (8, cols), lambda i: (jnp.minimum((i + 1) * per, last), cb))


def _conv_fwd(xg, cw, cb):
    T = xg.shape[0]
    C = cb.shape[1]
    tr = _tile(T, ROW_TILE)

    def fn(i, cur, prev8, cb, *cw):
        prev8 = jnp.where(i == 0, 0.0, prev8)
        xc = cb + cw[CONV_W - 1] * cur
        for s in range(1, CONV_W):
            xc = xc + cw[CONV_W - 1 - s] * _shift_down(prev8, cur, s)
        return xc, xc

    row = lambda i: (i, 0)
    return _rowwise("lru_conv", T // tr, fn, [_rows(xg, tr, C), _halo_prev(xg, tr, C), _whole(cb)] + [_whole(w) for w in cw],
                    [((T, C), F32, (tr, C), row), ((T, C), BF16, (tr, C), row)])


def _conv_bwd(dxc, xg, dgb, cw):
    T, C = dxc.shape
    tr = _tile(T, ROW_TILE)
    nt = T // tr

    def fn(i, d_cur, d_next8, x_cur, x_prev8, dgb, *cw):
        d_next8 = jnp.where(i == nt - 1, 0.0, d_next8)
        x_prev8 = jnp.where(i == 0, 0.0, x_prev8)
        dxb = cw[CONV_W - 1] * d_cur
        dcw = [jnp.sum(d_cur * x_cur, axis=0, keepdims=True)]
        for s in range(1, CONV_W):
            dxb = dxb + cw[CONV_W - 1 - s] * _shift_up(d_cur, d_next8, s)
            dcw.append(jnp.sum(d_cur * _shift_down(x_prev8, x_cur, s), axis=0, keepdims=True))
        return (jnp.concatenate([dxb.astype(BF16), dgb], axis=1), dcw[3], dcw[2], dcw[1], dcw[0],
                jnp.sum(d_cur, axis=0, keepdims=True))

    return _rowwise("lru_conv_bwd", nt, fn,
                    [_rows(dxc, tr), _halo_next(dxc, tr, C), _rows(xg, tr, C), _halo_prev(xg, tr, C), _rows(dgb, tr)] + [_whole(w) for w in cw],
                    [((T, 2 * C), BF16, (tr, 2 * C), lambda i: (i, 0))], accs=[(1, C)] * (CONV_W + 1))


def _lru_gate_fwd(xc, xcb, w_rarx, layer, b_ra, b_rx, lam):
    T, C = xc.shape
    W = C // RNN_BLOCKS
    bm = _tile(T, ROW_TILE)

    def epi(acc, xc_blk, bra, brx, lm):
        a, b = _lru_gates(acc, xc_blk, bra, brx, lm)
        return acc, a, b

    blk = lambda i, j: (i, j)
    par = lambda i, j: (0, j)
    return _matmul(
        "lru_gates", (T // bm, RNN_BLOCKS, 1),
        (xcb, (bm, W), lambda i, j, k: (i, j)), (w_rarx, (None, None, W, 2 * W), lambda i, j, k: (layer, j, 0, 0)),
        [((T, 2 * C), F32, (bm, 2 * W), blk), ((T, C), F32, (bm, W), blk), ((T, C), F32, (bm, W), blk)],
        (bm, 2 * W),
        extras=[(xc, (bm, W), blk), (b_ra, (1, W), par), (b_rx, (1, W), par), (lam, (1, W), par)], epilogue=epi)


def _lru_gate_bwd(pre, xc, lam_adj, h, b_ra, b_rx, lam):
    T, C = xc.shape
    W = C // RNN_BLOCKS
    tr = _tile(T, ROW_TILE // 2)

    def fn(i, pre, xc, adj, h_cur, h_prev8, bra, brx, lm):
        h_prev8 = jnp.where(i == 0, 0.0, h_prev8)
        da = adj * _shift_down(h_prev8, h_cur, 1)
        dpre, dxc, dbra, dbrx, dlam = [], [], [], [], []
        for n in range(RNN_BLOCKS):
            cs = slice(n * W, (n + 1) * W)
            _, vjp = jax.vjp(_lru_gates, pre[:, 2 * n * W:2 * (n + 1) * W], xc[:, cs], bra[:, cs], brx[:, cs], lm[:, cs])
            g = vjp((da[:, cs], adj[:, cs]))
            for lst, v in zip((dpre, dxc, dbra, dbrx, dlam), g):
                lst.append(v)
        cat = lambda l: jnp.concatenate(l, axis=1)
        return cat(dpre), cat(dxc), cat(dbra), cat(dbrx), cat(dlam)

    row = lambda i: (i, 0)
    return _rowwise("lru_gates_bwd", T // tr, fn,
                    [_rows(pre, tr), _rows(xc, tr), _rows(lam_adj, tr), _rows(h, tr), _halo_prev(h, tr, C),
                     _whole(b_ra), _whole(b_rx), _whole(lam)],
                    [((T, 2 * C), BF16, (tr, 2 * C), row), ((T, C), F32, (tr, C), row)], accs=[(1, C)] * 3)


def _scan_fwd(a, b):
    T, C = a.shape
    tt = _tile(T, ROW_TILE)

    def body(a_ref, b_ref, o_ref, c_ref):
        @pl.when(pl.program_id(0) == 0)
        def _():
            c_ref[...] = jnp.zeros_like(c_ref)

        row = lax.broadcasted_iota(jnp.int32, (8, C), 0)

        def step(j, carry):
            sl = pl.ds(pl.multiple_of(j * 8, 8), 8)
            A, B = a_ref[sl, :], b_ref[sl, :]
            for d in (1, 2, 4):
                ok = row >= d
                B = jnp.where(ok, A * pltpu.roll(B, d, 0) + B, B)
                A = jnp.where(ok, A * pltpu.roll(A, d, 0), A)
            h = A * carry + B
            o_ref[sl, :] = h
            return jnp.sum(jnp.where(row == 7, h, 0.0), axis=0, keepdims=True)

        c_ref[0:1, :] = lax.fori_loop(0, tt // 8, step, c_ref[0:1, :])

    spec = pl.BlockSpec((tt, C), lambda i: (i, 0))
    return pl.pallas_call(body, name="lru_scan", grid=(T // tt,), in_specs=[spec, spec], out_specs=spec,
                          out_shape=jax.ShapeDtypeStruct((T, C), F32), scratch_shapes=[pltpu.VMEM((8, C), F32)],
                          compiler_params=_cparams(("arbitrary",)))(a, b)


def _scan_bwd(a, dh):
    T, C = a.shape
    tt = _tile(T, ROW_TILE)
    nt = T // tt

    def body(a_ref, d_ref, o_ref, c_ref):
        @pl.when(pl.program_id(0) == 0)
        def _():
            c_ref[...] = jnp.zeros_like(c_ref)

        row = lax.broadcasted_iota(jnp.int32, (8, C), 0)

        def step(jj, carry):
            adj_next, a_next = carry
            j = tt // 8 - 1 - jj
            sl = pl.ds(pl.multiple_of(j * 8, 8), 8)
            a_blk = a_ref[sl, :]
            A = jnp.where(row < 7, pltpu.roll(a_blk, 7, 0), a_next)
            B = d_ref[sl, :]
            for d in (1, 2, 4):
                ok = row < 8 - d
                B = jnp.where(ok, A * pltpu.roll(B, 8 - d, 0) + B, B)
                A = jnp.where(ok, A * pltpu.roll(A, 8 - d, 0), A)
            adj = A * adj_next + B
            o_ref[sl, :] = adj
            first = lambda v: jnp.sum(jnp.where(row == 0, v, 0.0), axis=0, keepdims=True)
            return first(adj), first(a_blk)

        adj0, a0 = lax.fori_loop(0, tt // 8, step, (c_ref[0:1, :], c_ref[1:2, :]))
        c_ref[0:1, :] = adj0
        c_ref[1:2, :] = a0

    spec = pl.BlockSpec((tt, C), lambda i: (nt - 1 - i, 0))
    return pl.pallas_call(body, name="lru_scan_bwd", grid=(nt,), in_specs=[spec, spec], out_specs=spec,
                          out_shape=jax.ShapeDtypeStruct((T, C), F32), scratch_shapes=[pltpu.VMEM((8, C), F32)],
                          compiler_params=_cparams(("arbitrary",)))(a, dh)


def _lru_out_fwd(h, xg):
    T, C = h.shape
    tr = _tile(T, ROW_TILE)
    (y,) = _rowwise("lru_out", T // tr, lambda i, h, gb: (h * _gelu_tanh(gb),), [_rows(h, tr), _rows(xg, tr, C, 1)],
                    [((T, C), BF16, (tr, C), lambda i: (i, 0))])
    return y


def _lru_out_bwd(dy, h, xg):
    T, C = h.shape
    tr = _tile(T, ROW_TILE)

    def fn(i, dy, h, gb):
        _, vjp = jax.vjp(lambda h, gb: h * _gelu_tanh(gb), h, gb)
        return vjp(dy)

    row = lambda i: (i, 0)
    return _rowwise("lru_out_bwd", T // tr, fn, [_rows(dy, tr), _rows(h, tr), _rows(xg, tr, C, 1)],
                    [((T, C), F32, (tr, C), row), ((T, C), BF16, (tr, C), row)])


class _Sharded:
    def __init__(self, kind, size, off=0, width=None):
        self.kind, self.size, self.off, self.width = kind, size, off, width

    def slot(self, cx, cy):
        return (2 * cy + cx) if self.kind == "perm" else (2 * cx + cy)

    def at(self, ref, cx, cy, layers, half=None):
        s = self.slot(cx, cy)
        start = s * self.size
        if not isinstance(start, int):
            start = pl.multiple_of(start, 8 if self.kind in ("rows", "rarx") else 128)
        if self.kind in ("cols", "perm"):
            n = ref.shape[1]
            rows = slice(None) if half is None else pl.ds(pl.multiple_of(half * (n // 2), 8), n // 2)
            return ref.at[layers, rows, pl.ds(start, self.size)]
        if self.kind == "rows":
            if half is None:
                return ref.at[layers, pl.ds(start, self.size), :]
            return ref.at[layers, pl.ds(pl.multiple_of(start + half * (self.size // 2), 8), self.size // 2), :]
        if self.kind == "rarx":
            n = ref.shape[1]
            blocks = slice(None) if half is None else pl.ds(half * (n // 2), n // 2)
            return ref.at[layers, blocks, pl.ds(start, self.size), pl.ds(self.off, self.width)]
        raise ValueError(self.kind)


def _mesh_pos():
    return lax.axis_index("x"), lax.axis_index("y"), lax.axis_index("c")


def _peer_chips(x, y):
    return [(1 - x, y), (x, 1 - y), (1 - x, 1 - y)]


def _place(shard, out_shape, out_dtype, sh, pos, prev=None):
    def body(p_ref, x_ref, *rest):
        rest[-1][...] = x_ref[...].astype(rest[-1].dtype)

    if sh.kind in ("cols", "perm"):
        L, R, Ns = shard.shape
        tr = _tile(R, ROW_TILE)
        k = 1 if sh.kind == "perm" else 0
        grid = (L, R // tr)
        ispec = pl.BlockSpec((None, tr, Ns), lambda l, i, p: (l, i, 0))
        ospec = pl.BlockSpec((None, tr, Ns), lambda l, i, p: (l, i, p[k]))
    elif sh.kind == "rows":
        L, Rs, D = shard.shape
        tr = _tile(Rs, ROW_TILE)
        nt = Rs // tr
        grid = (L, nt)
        ispec = pl.BlockSpec((None, tr, D), lambda l, i, p: (l, i, 0))
        ospec = pl.BlockSpec((None, tr, D), lambda l, i, p: (l, p[0] * nt + i, 0))
    else:
        L, nb, Rs, Wd = shard.shape
        cb = sh.off // Wd
        grid = (L, nb)
        ispec = pl.BlockSpec((None, None, Rs, Wd), lambda l, i, p: (l, i, 0, 0))
        ospec = pl.BlockSpec((None, None, Rs, Wd), lambda l, i, p: (l, i, p[0], cb))
    in_specs = [ispec]
    operands = [pos, shard]
    alias = {}
    if prev is not None:
        in_specs.append(pl.BlockSpec(memory_space=pl.ANY))
        operands.append(prev)
        alias = {2: 0}
    gs = pltpu.PrefetchScalarGridSpec(num_scalar_prefetch=1, grid=grid, in_specs=in_specs, out_specs=ospec)
    return pl.pallas_call(body, name="weight_place", grid_spec=gs, out_shape=jax.ShapeDtypeStruct(out_shape, out_dtype),
                          input_output_aliases=alias, compiler_params=_cparams(("arbitrary", "arbitrary")))(*operands)


def _gather_over_ici(jobs, base=0):
    assert len({id(j[0]) for j in jobs}) == len(jobs)

    def copies(refs, send, recv):
        x, y, c = _mesh_pos()
        out = []
        for t, (_, sh, layer) in enumerate(jobs):
            lay = pl.ds(layer, 1)
            mine = sh.at(refs[t], x, y, lay, c)
            for j, (px, py) in enumerate(_peer_chips(x, y)):
                theirs = sh.at(refs[t], px, py, lay, c)
                k = base + 3 * t + j
                out.append((pltpu.make_async_remote_copy(mine, mine, send.at[k], recv.at[k],
                                                         device_id=(px, py, c), device_id_type=MESH),
                            pltpu.make_async_remote_copy(theirs, theirs, send.at[k], recv.at[k],
                                                         device_id=(px, py, c), device_id_type=MESH)))
        return out

    def start(refs, send, recv):
        for out_cp, _ in copies(refs, send, recv):
            out_cp.start()

    def finish(refs, send, recv):
        for out_cp, in_cp in copies(refs, send, recv):
            in_cp.wait_recv()
            out_cp.wait_send()

    return _SideComm([j[0] for j in jobs], 3 * len(jobs), start, finish)


def _pass_to_sibling(jobs, base=0):
    assert len({id(j[0]) for j in jobs}) == len(jobs)

    def copies(refs, send, recv):
        x, y, c = _mesh_pos()
        out = []
        for t, (_, sh, layer) in enumerate(jobs):
            lay = pl.ds(layer, 1)
            for j, (px, py) in enumerate(_peer_chips(x, y)):
                got = sh.at(refs[t], px, py, lay, c)
                coming = sh.at(refs[t], px, py, lay, 1 - c)
                k = base + 3 * t + j
                out.append((pltpu.make_async_remote_copy(got, got, send.at[k], recv.at[k],
                                                         device_id=(x, y, 1 - c), device_id_type=MESH),
                            pltpu.make_async_remote_copy(coming, coming, send.at[k], recv.at[k],
                                                         device_id=(x, y, 1 - c), device_id_type=MESH)))
        return out

    def start(refs, send, recv):
        for out_cp, _ in copies(refs, send, recv):
            out_cp.start()

    def finish(refs, send, recv):
        for out_cp, in_cp in copies(refs, send, recv):
            in_cp.wait_recv()
            out_cp.wait_send()

    return _SideComm([j[0] for j in jobs], 3 * len(jobs), start, finish)


def _all_gather(jobs):
    n = len(jobs)
    ici, d2d = _gather_over_ici(jobs), _pass_to_sibling(jobs, 3 * n)

    def body(*refs):
        outs = refs[n:2 * n]
        send, recv = refs[2 * n:]
        ici.start(outs, send, recv)
        ici.finish(outs, send, recv)
        d2d.start(outs, send, recv)
        d2d.finish(outs, send, recv)

    hbm = pl.BlockSpec(memory_space=pl.ANY)
    return pl.pallas_call(
        body, name="weights_all_gather", in_specs=[hbm] * n, out_specs=[hbm] * n,
        out_shape=[jax.ShapeDtypeStruct(j[0].shape, j[0].dtype) for j in jobs],
        input_output_aliases={t: t for t in range(n)},
        scratch_shapes=[pltpu.SemaphoreType.DMA((6 * n,)), pltpu.SemaphoreType.DMA((6 * n,))],
    )(*[j[0] for j in jobs])


def _half_view(sh, ref, h):
    if sh.kind == "rows":
        n = ref.shape[-1]
        return ref.at[:, pl.ds(pl.multiple_of(h * (n // 2), 128), n // 2)]
    n = ref.shape[0]
    return ref.at[pl.ds(h * (n // 2), n // 2)]


def _half_shape(sh, layer_shape):
    s = list(layer_shape)
    s[len(s) - 1 if sh.kind == "rows" else 0] //= 2
    return tuple(s)


def _chip_part(sh, ref, cx, cy):
    start = sh.slot(cx, cy) * sh.size
    if sh.kind in ("cols", "perm"):
        return ref.at[:, pl.ds(pl.multiple_of(start, 128), sh.size)]
    if sh.kind == "rows":
        return ref.at[pl.ds(pl.multiple_of(start, 8), sh.size), :]
    return ref.at[:, pl.ds(pl.multiple_of(start, 8), sh.size), :]


def _chip_part_shape(sh, half_shape):
    s = list(half_shape)
    s[{"cols": len(s) - 1, "perm": len(s) - 1, "rows": 0, "rarx": 1}[sh.kind]] = sh.size
    return tuple(s)


def _swap_stage(jobs):
    assert len({id(j[0]) for j in jobs}) == len(jobs)

    def make(base):
        arrays = []
        for g, sh, _ in jobs:
            arrays += [g, jax.ShapeDtypeStruct(_half_shape(sh, g.shape[1:]), g.dtype)]

        def copies(refs, send, recv):
            x, y, c = _mesh_pos()
            return [pltpu.make_async_remote_copy(_half_view(sh, refs[2 * t].at[layer], 1 - c), refs[2 * t + 1],
                                                 send.at[base + t], recv.at[base + t],
                                                 device_id=(x, y, 1 - c), device_id_type=MESH)
                    for t, (_, sh, layer) in enumerate(jobs)]

        def start(refs, send, recv):
            for cp in copies(refs, send, recv):
                cp.start()

        def finish(refs, send, recv):
            for cp in copies(refs, send, recv):
                cp.wait()

        return _SideComm(arrays, len(jobs), start, finish)
    return make


def _exchange_stage(jobs):
    n = len(jobs)

    def make(base):
        arrays = []
        for p, sh in jobs:
            arrays += [p, jax.ShapeDtypeStruct((N_CHIPS,) + _chip_part_shape(sh, p.shape), p.dtype)]

        def copies(refs, send, recv):
            x, y, c = _mesh_pos()
            me = 2 * x + y
            out = []
            for t, (_, sh) in enumerate(jobs):
                part, land = refs[2 * t], refs[2 * t + 1]
                local = pltpu.make_async_copy(_chip_part(sh, part, x, y), land.at[me], send.at[base + 3 * n + t])
                remote = []
                for j, (px, py) in enumerate(_peer_chips(x, y)):
                    k = base + 3 * t + j
                    src = land.at[2 * px + py]
                    remote.append((pltpu.make_async_remote_copy(_chip_part(sh, part, px, py), land.at[me], send.at[k],
                                                                recv.at[k], device_id=(px, py, c), device_id_type=MESH),
                                   pltpu.make_async_remote_copy(src, src, send.at[k], recv.at[k],
                                                                device_id=(px, py, c), device_id_type=MESH)))
                out.append((local, remote))
            return out

        def start(refs, send, recv):
            for local, remote in copies(refs, send, recv):
                local.start()
                for out_cp, _ in remote:
                    out_cp.start()

        def finish(refs, send, recv):
            for local, remote in copies(refs, send, recv):
                for out_cp, in_cp in remote:
                    in_cp.wait_recv()
                    out_cp.wait_send()
                local.wait()

        return _SideComm(arrays, 4 * n, start, finish)
    return make


def _share_stage(jobs):
    assert len({id(j[0]) for j in jobs}) == len(jobs)

    def make(base):
        def copies(refs, send, recv):
            x, y, c = _mesh_pos()
            out = []
            for t, (_, sh, layer) in enumerate(jobs):
                mine = _half_view(sh, refs[t].at[layer], c)
                theirs = _half_view(sh, refs[t].at[layer], 1 - c)
                out.append((pltpu.make_async_remote_copy(mine, mine, send.at[base + t], recv.at[base + t],
                                                         device_id=(x, y, 1 - c), device_id_type=MESH),
                            pltpu.make_async_remote_copy(theirs, theirs, send.at[base + t], recv.at[base + t],
                                                         device_id=(x, y, 1 - c), device_id_type=MESH)))
            return out

        def start(refs, send, recv):
            for out_cp, _ in copies(refs, send, recv):
                out_cp.start()

        def finish(refs, send, recv):
            for out_cp, in_cp in copies(refs, send, recv):
                in_cp.wait_recv()
                out_cp.wait_send()

        return _SideComm([j[0] for j in jobs], len(jobs), start, finish)
    return make


def _pair_add_layer(g, sh, layer, got, c_arr):
    if sh.kind == "rows":
        R, Ch = got.shape
        tr = _tile(R, ROW_TILE)
        ins = [(g, (None, tr, Ch), lambda i, p: (layer, i, p[0])), (got, (tr, Ch), lambda i, p: (i, 0))]
        nt, g2 = R // tr, got
    else:
        C = got.shape[-1]
        g2 = got.reshape(-1, C)
        R = g2.shape[0]
        tr = _tile(R, 256 if C > 2048 else ROW_TILE)
        nt = R // tr
        ins = [(g.reshape(g.shape[0], -1, C), (None, tr, C), lambda i, p: (layer, p[0] * nt + i, 0)),
               (g2, (tr, C), lambda i, p: (i, 0))]
    (out,) = _rowwise("grad_pair_add", nt, lambda i, a, b: (a.astype(F32) + b.astype(F32),), ins,
                      [(g2.shape, got.dtype, ins[1][1], lambda i, p: (i, 0))], prefetch=c_arr)
    return out.reshape(got.shape)


def _chip_sum_layer(land, sh, layer, tot, c_arr):
    def fn(i, blk):
        v = blk.astype(F32)
        return (((v[0] + v[1]) + v[2]) + v[3],)

    if sh.kind == "rows":
        _, R, Ch = land.shape
        tr = _tile(R, ROW_TILE)
        ins = [(land, (N_CHIPS, tr, Ch), lambda i, p: (0, i, 0))]
        outs = [(tot.shape, F32, (None, tr, Ch), lambda i, p: (layer, i, p[0]))]
        (out,) = _rowwise("grad_chip_sum", R // tr, fn, ins, outs, prefetch=c_arr, into=tot)
        return out
    C = land.shape[-1]
    l3 = land.reshape(N_CHIPS, -1, C)
    R = l3.shape[1]
    tr = _tile(R, ROW_TILE)
    nt = R // tr
    t3 = tot.reshape(tot.shape[0], -1, C)
    (out,) = _rowwise("grad_chip_sum", nt, fn, [(l3, (N_CHIPS, tr, C), lambda i, p: (0, i, 0))],
                      [(t3.shape, F32, (None, tr, C), lambda i, p: (layer, p[0] * nt + i, 0))], prefetch=c_arr, into=t3)
    return out.reshape(tot.shape)


def _flat2(shape):
    return (math.prod(shape[:-1]), shape[-1])


def _adamw(w, g, m, v):
    shape = w.shape
    R, C = _flat2(shape)
    tr = _tile(R, 256 if R % 256 == 0 else R)

    def fn(i, w, g, m, v):
        m2 = ADAM_B1 * m + (1.0 - ADAM_B1) * g
        v2 = ADAM_B2 * v + (1.0 - ADAM_B2) * (g * g)
        m_hat = m2 / (1.0 - ADAM_B1 ** ADAM_STEP)
        v_hat = v2 / (1.0 - ADAM_B2 ** ADAM_STEP)
        delta = -ADAM_LR * (m_hat / (jnp.sqrt(v_hat) + ADAM_EPS) + ADAM_WD * w)
        return g, delta, m2, v2

    row = lambda i: (i, 0)
    f2 = lambda a: a.reshape(R, C)
    outs = _rowwise("adamw", R // tr, fn, [_rows(f2(a), tr) for a in (w, g, m, v)], [((R, C), F32, (tr, C), row)] * 4)
    return [o.reshape(shape) for o in outs]


def _loss_and_grad(last, target):
    xhat, g, b = last
    T, D = xhat.shape
    tr = _tile(T, ROW_TILE)

    def fn(i, xhat, t, g, b):
        err = (xhat * g + b) - t
        return err * (1.0 / D), jnp.sum(err * err, axis=0, keepdims=True)

    dh, sq = _rowwise("loss", T // tr, fn, [_rows(xhat, tr), _rows(target, tr), _whole(g), _whole(b)],
                      [((T, D), F32, (tr, D), lambda i: (i, 0))], accs=[(1, D)])
    return dh, (0.5 / D) * jnp.sum(sq)


def kernel(x, ffn1_w_in, ffn1_w_out, ffn2_w_in, ffn2_w_out, ln_g, ln_b, attn_w_qkv, attn_sinks, attn_w_o, lru_w_in, lru_conv_w, lru_conv_b, lru_w_ra, lru_b_ra, lru_w_rx, lru_b_rx, lru_lambda, lru_w_out, loss_target, m_ffn1_w_in, m_ffn1_w_out, m_ffn2_w_in, m_ffn2_w_out, m_ln_g, m_ln_b, m_attn_w_qkv, m_attn_sinks, m_attn_w_o, m_lru_w_in, m_lru_conv_w, m_lru_conv_b, m_lru_w_ra, m_lru_b_ra, m_lru_w_rx, m_lru_b_rx, m_lru_lambda, m_lru_w_out, v_ffn1_w_in, v_ffn1_w_out, v_ffn2_w_in, v_ffn2_w_out, v_ln_g, v_ln_b, v_attn_w_qkv, v_attn_sinks, v_attn_w_o, v_lru_w_in, v_lru_conv_w, v_lru_conv_b, v_lru_w_ra, v_lru_b_ra, v_lru_w_rx, v_lru_b_rx, v_lru_lambda, v_lru_w_out):
    names = ["ffn1_w_in", "ffn1_w_out", "ffn2_w_in", "ffn2_w_out", "ln_g", "ln_b", "attn_w_qkv", "attn_sinks", "attn_w_o",
             "lru_w_in", "lru_conv_w", "lru_conv_b", "lru_w_ra", "lru_b_ra", "lru_w_rx", "lru_b_rx", "lru_lambda", "lru_w_out"]
    W = dict(zip(names, [ffn1_w_in, ffn1_w_out, ffn2_w_in, ffn2_w_out, ln_g, ln_b, attn_w_qkv, attn_sinks, attn_w_o,
                         lru_w_in, lru_conv_w, lru_conv_b, lru_w_ra, lru_b_ra, lru_w_rx, lru_b_rx, lru_lambda, lru_w_out]))
    M = dict(zip(names, [m_ffn1_w_in, m_ffn1_w_out, m_ffn2_w_in, m_ffn2_w_out, m_ln_g, m_ln_b, m_attn_w_qkv, m_attn_sinks, m_attn_w_o,
                         m_lru_w_in, m_lru_conv_w, m_lru_conv_b, m_lru_w_ra, m_lru_b_ra, m_lru_w_rx, m_lru_b_rx, m_lru_lambda, m_lru_w_out]))
    V = dict(zip(names, [v_ffn1_w_in, v_ffn1_w_out, v_ffn2_w_in, v_ffn2_w_out, v_ln_g, v_ln_b, v_attn_w_qkv, v_attn_sinks, v_attn_w_o,
                         v_lru_w_in, v_lru_conv_w, v_lru_conv_b, v_lru_w_ra, v_lru_b_ra, v_lru_w_rx, v_lru_b_rx, v_lru_lambda, v_lru_w_out]))

    T, D = x.shape[1], x.shape[2]
    L = ffn1_w_in.shape[0]
    LA, LR = attn_w_qkv.shape[0], lru_w_in.shape[0]
    N2 = ffn1_w_in.shape[2] * N_CHIPS
    F = N2 // 2
    C = lru_lambda.shape[1] * N_CHIPS
    CW = C // N_CHIPS
    alpha = (2.0 * L) ** 0.25
    c_arr = lax.axis_index("c").astype(jnp.int32).reshape(1)

    n_sink = attn_sinks.size
    assert n_sink <= CW

    up8 = lambda n: -(-n // 8) * 8
    o_g = 0
    o_b = o_g + up8(3 * L)
    o_cw = o_b + up8(3 * L)
    o_cb = o_cw + up8(LR * CONV_W)
    o_ra = o_cb + up8(LR)
    o_rx = o_ra + up8(LR)
    o_lam = o_rx + up8(LR)
    o_sink = o_lam + up8(LR)
    assert o_sink + 8 <= SMALL_ROWS

    def pack_small(d):
        parts = [d["ln_g"].reshape(-1, CW), d["ln_b"].reshape(-1, CW), d["lru_conv_w"].reshape(-1, CW), d["lru_conv_b"],
                 d["lru_b_ra"], d["lru_b_rx"], d["lru_lambda"],
                 jnp.pad(d["attn_sinks"].reshape(1, -1), ((0, 0), (0, CW - n_sink)))]
        parts = [jnp.pad(p, ((0, up8(p.shape[0]) - p.shape[0]), (0, 0))) for p in parts]
        used = sum(p.shape[0] for p in parts)
        return jnp.concatenate(parts + [jnp.zeros((SMALL_ROWS - used, CW), F32)], axis=0)

    cols = lambda a: _Sharded("cols", a.shape[-1])
    rows_ = lambda a: _Sharded("rows", a.shape[-2])
    RW = lru_w_ra.shape[2]
    sh_list = [_Sharded("perm", N2 // 4), rows_(ffn1_w_out), _Sharded("perm", N2 // 4), rows_(ffn2_w_out),
               cols(attn_w_qkv), rows_(attn_w_o), cols(lru_w_in), rows_(lru_w_out)]
    big = [ffn1_w_in, ffn1_w_out, ffn2_w_in, ffn2_w_out, attn_w_qkv, attn_w_o, lru_w_in, lru_w_out]
    blk_w = lru_w_ra.shape[3]
    small_sh = _Sharded("cols", CW)
    full = lambda a, sh: tuple(a.shape[:-1]) + (a.shape[-1] * N_CHIPS,) if sh.kind in ("cols", "perm") else \
        tuple(a.shape[:-2]) + (a.shape[-2] * N_CHIPS, a.shape[-1])
    mx, my = lax.axis_index("x"), lax.axis_index("y")
    pos = jnp.stack([2 * mx + my, 2 * my + mx]).astype(jnp.int32)
    placed = [_place(a, full(a, sh), BF16, sh, pos) for a, sh in zip(big, sh_list)]
    rarx_shape = (LR, RNN_BLOCKS, RW * N_CHIPS, 2 * blk_w)
    p_ra = _place(lru_w_ra, rarx_shape, BF16, _Sharded("rarx", RW, 0, blk_w), pos)
    placed.append(_place(lru_w_rx, rarx_shape, BF16, _Sharded("rarx", RW, blk_w, blk_w), pos, prev=p_ra))
    gnames = ["w_in1", "w_out1", "w_in2", "w_out2", "qkv", "wo", "lin", "lout", "rarx"]
    GW = dict(zip(gnames, placed))
    SH = dict(zip(gnames, sh_list + [_Sharded("rarx", RW, 0, 2 * blk_w)]))
    p_small = _place(pack_small(W).reshape(1, SMALL_ROWS, CW), (1, SMALL_ROWS, C), F32, small_sh, pos)

    def comm_for(stage, jobs):
        return stage([(GW[n], SH[n], l) for n, l in jobs]) if jobs else None

    def rebind(jobs, arrays):
        for (n, _), a in zip(jobs, arrays):
            GW[n] = a

    jobs0 = [(n, 0) for n in ("w_in1", "w_out1", "w_in2", "w_out2", "qkv", "wo")]
    got0 = _all_gather([(GW[n], SH[n], l) for n, l in jobs0] + [(p_small, small_sh, 0)])
    rebind(jobs0, got0)
    small = got0[-1].reshape(SMALL_ROWS, C)
    row_of = lambda r: small[r:r + 1]
    mixer_jobs = lambda l: [("qkv", l // 2), ("wo", l // 2)] if l % 2 == 0 else [("lin", l // 2), ("lout", l // 2), ("rarx", l // 2)]

    def ffn_forward(prev, hb, n_in, n_out, layer, g, b, jobs):
        gu, act, carried = _ffn_up(hb, GW[n_in], layer, comm_for(_gather_over_ici, jobs))
        rebind(jobs, carried)
        res = _proj_ln("ffn_down_ln", act, GW[n_out], layer, prev, g, b, alpha, 0.5, comm_for(_pass_to_sibling, jobs))
        if jobs:
            res, carried = res
            rebind(jobs, carried)
        return res, (gu, act)
    assert D == C, "packed small parameters assume d_model == d_rnn"

    c128, s128 = _rope_tables(T)
    sink_rows = [attn_sinks[j:j + 1] for j in range(LA)]

    x2 = x.reshape(T, D)
    prev = (x2, jnp.ones((1, D), F32), jnp.zeros((1, D), F32))
    hb = x2.astype(BF16)
    saved = []
    for i in range(L):
        j = i // 2
        lay = {}
        lay["hb0"] = hb
        nxt = i + 1
        gains = [row_of(o_g + 3 * i + k) for k in range(3)]
        biases = [row_of(o_b + 3 * i + k) for k in range(3)]
        jobs1 = [("w_in2", nxt), ("w_out2", nxt)] if nxt < L else []
        jobs2 = [("w_in1", nxt), ("w_out1", nxt)] + mixer_jobs(nxt) if nxt < L else []
        (xh, hb, rs), (lay["gu1"], lay["act1"]) = ffn_forward(prev, hb, "w_in1", "w_out1", i, gains[0], biases[0], jobs1)
        lay["ln1"], lay["hb1"], prev = (xh, rs), hb, (xh, gains[0], biases[0])
        if i % 2 == 0:
            qkv = _qkv_rope(hb, GW["qkv"], j, c128, s128)
            o = _attn_fwd(qkv, sink_rows[j])
            lay["qkv"], lay["o"] = qkv, o
            xh, hb, rs = _proj_ln("attn_out_ln", o, GW["wo"], j, prev, gains[1], biases[1], alpha, 1.0)
        else:
            bm = _tile(T, ROW_TILE)
            (xg,) = _matmul("lru_in", (T // bm, 1, 1), (hb, (bm, D), lambda p, q, k: (p, 0)),
                            (GW["lin"], (None, D, 2 * C), lambda p, q, k, j=j: (j, 0, 0)),
                            [((T, 2 * C), F32, (bm, 2 * C), lambda p, q: (p, 0))], (bm, 2 * C))
            cw = [row_of(o_cw + j * CONV_W + k) for k in range(CONV_W)]
            xc, xcb = _conv_fwd(xg, cw, row_of(o_cb + j))
            pre, a, b = _lru_gate_fwd(xc, xcb, GW["rarx"], j, row_of(o_ra + j), row_of(o_rx + j), row_of(o_lam + j))
            hs = _scan_fwd(a, b)
            y = _lru_out_fwd(hs, xg)
            lay.update(xg=xg, xc=xc, xcb=xcb, pre=pre, a=a, hs=hs, y=y, cw=cw)
            xh, hb, rs = _proj_ln("lru_out_ln", y, GW["lout"], j, prev, gains[1], biases[1], alpha, 1.0)
        lay["ln2"], lay["hb2"], prev = (xh, rs), hb, (xh, gains[1], biases[1])
        (xh, hb, rs), (lay["gu2"], lay["act2"]) = ffn_forward(prev, hb, "w_in2", "w_out2", i, gains[2], biases[2], jobs2)
        lay["ln3"], prev = (xh, rs), (xh, gains[2], biases[2])
        saved.append(lay)
    g_w_in1, g_w_out1, g_w_in2, g_w_out2, g_qkv, g_wo, g_lin, g_lout, g_rarx = [GW[n] for n in gnames]

    dh, loss_local = _loss_and_grad(prev, loss_target.reshape(T, D))
    loss = lax.psum(loss_local, ("x", "y", "c"))

    SH2 = dict(SH, small=small_sh)
    shard_stack = {"w_in1": ffn1_w_in.shape, "w_out1": ffn1_w_out.shape, "w_in2": ffn2_w_in.shape, "w_out2": ffn2_w_out.shape,
                   "qkv": attn_w_qkv.shape, "wo": attn_w_o.shape, "lin": lru_w_in.shape, "lout": lru_w_out.shape,
                   "rarx": (LR, RNN_BLOCKS, RW, 2 * blk_w), "small": (1, SMALL_ROWS, CW)}
    TOT = {n: lax.empty(s, F32) for n, s in shard_stack.items()}
    pend_exchange, pend_share = [], []

    def swap_share_comm(swap_jobs):
        share_jobs = list(pend_share)
        pend_share.clear()
        makers = []
        if swap_jobs:
            makers.append(_swap_stage([(g, SH2[n], 0) for n, _, g in swap_jobs]))
        if share_jobs:
            makers.append(_share_stage([(TOT[n], SH2[n], l) for n, l in share_jobs]))
        comm, _ = _combine(makers)

        def after(carried):
            for t, (n, l, _) in enumerate(swap_jobs):
                part = _pair_add_layer(carried[2 * t], SH2[n], 0, carried[2 * t + 1], c_arr)
                pend_exchange.append((n, l, part))
            for t, (n, _) in enumerate(share_jobs):
                TOT[n] = carried[2 * len(swap_jobs) + t]
        return comm, after

    def exchange_comm():
        jobs = list(pend_exchange)
        pend_exchange.clear()
        if not jobs:
            return None, None
        comm, _ = _combine([_exchange_stage([(part, SH2[n]) for n, _, part in jobs])])

        def after(carried):
            for t, (n, l, _) in enumerate(jobs):
                TOT[n] = _chip_sum_layer(carried[2 * t + 1], SH2[n], l, TOT[n], c_arr)
                pend_share.append((n, l))
        return comm, after

    def ffn_backward(dh, ln, g, hb_in, gu, act, n_in, n_out, layer, also_swap):
        dz, dyb, dg, db = _ln_bwd("ffn_ln_bwd", dh, ln[0], ln[1], g, 0.5)
        dgu = _ffn_dact(dyb, GW[n_out], layer, gu)
        g_out = _grad_tn("ffn_dwout", act, dyb, N2 // 4, D // 2, bk=8 * ROW_TILE)
        comm, after = exchange_comm()
        g_in = _grad_tn("ffn_dwin", hb_in, dgu, D // 2, N2 // 4, bk=8 * ROW_TILE, comm=comm)
        if comm is not None:
            g_in, carried = g_in
            after(carried)
        comm, after = swap_share_comm([(n_in, layer, g_in), (n_out, layer, g_out)] + also_swap)
        dh_prev, carried = _ffn_dx(dgu, GW[n_in], layer, dz, alpha, comm)
        after(carried)
        return dh_prev, dg, db

    sg = [None] * SMALL_ROWS
    d_sinks = [None] * LA
    for i in reversed(range(L)):
        j = i // 2
        lay = saved[i]
        dh, sg[o_g + 3 * i + 2], sg[o_b + 3 * i + 2] = ffn_backward(
            dh, lay["ln3"], row_of(o_g + 3 * i + 2), lay["hb2"], lay["gu2"], lay["act2"], "w_in2", "w_out2", i, [])
        if i % 2 == 0:
            dz, dmb, sg[o_g + 3 * i + 1], sg[o_b + 3 * i + 1] = _ln_bwd("attn_ln_bwd", dh, *lay["ln2"], row_of(o_g + 3 * i + 1), 1.0)
            d_wo = _grad_tn("attn_dwo", lay["o"], dmb, _tile(Q_COLS, 1024), D)
            do = _back_proj("attn_do", dmb, g_wo, j, BF16)
            dq, dkv, dsk = _attn_bwd(lay["qkv"], do, sink_rows[j])
            d_sinks[j] = jnp.sum(dsk.reshape(N_HEADS, ATTN_BLOCK), axis=1)
            dqkv = _rope_bwd(dq, dkv, c128, s128)
            d_qkv = _grad_tn("attn_dwqkv", lay["hb1"], dqkv, D, dqkv.shape[1])
            comm, after = swap_share_comm([("wo", j, d_wo), ("qkv", j, d_qkv)])
            dh, carried = _input_grad("attn_dx", dqkv, g_qkv, j, dz, alpha, comm)
            after(carried)
        else:
            dz, dmb, sg[o_g + 3 * i + 1], sg[o_b + 3 * i + 1] = _ln_bwd("lru_ln_bwd", dh, *lay["ln2"], row_of(o_g + 3 * i + 1), 1.0)
            d_lout = _grad_tn("lru_dwout", lay["y"], dmb, C, D)
            dy = _back_proj("lru_dy", dmb, g_lout, j, F32)
            dhs, dgb = _lru_out_bwd(dy, lay["hs"], lay["xg"])
            adj = _scan_bwd(lay["a"], dhs)
            dpre, dxc_direct, sg[o_ra + j], sg[o_rx + j], sg[o_lam + j] = _lru_gate_bwd(
                lay["pre"], lay["xc"], adj, lay["hs"], row_of(o_ra + j), row_of(o_rx + j), row_of(o_lam + j))
            blk = C // RNN_BLOCKS
            d_rarx = _grad_tn("lru_dwgates", lay["xcb"], dpre, blk, 2 * blk, block_diag=True)
            bm = _tile(T, ROW_TILE)
            (dxc,) = _matmul("lru_dxc", (T // bm, RNN_BLOCKS, 1), (dpre, (bm, 2 * blk), lambda p, q, k: (p, q)),
                             (g_rarx, (None, None, blk, 2 * blk), lambda p, q, k, j=j: (j, q, 0, 0)),
                             [((T, C), F32, (bm, blk), lambda p, q: (p, q))], (bm, blk), tb=True,
                             extras=[(dxc_direct, (bm, blk), lambda p, q: (p, q))], epilogue=lambda acc, d: (acc + d,))
            res = _conv_bwd(dxc, lay["xg"], dgb, lay["cw"])
            dxg = res[0]
            for k in range(CONV_W):
                sg[o_cw + j * CONV_W + k] = res[1 + k]
            sg[o_cb + j] = res[1 + CONV_W]
            d_lin = _grad_tn("lru_dwin", lay["hb1"], dxg, D, _tile(2 * C, 1024))
            comm, after = swap_share_comm([("lout", j, d_lout), ("rarx", j, d_rarx), ("lin", j, d_lin)])
            dh, carried = _input_grad("lru_dx", dxg, g_lin, j, dz, alpha, comm)
            after(carried)
        dh, sg[o_g + 3 * i], sg[o_b + 3 * i] = ffn_backward(
            dh, lay["ln1"], row_of(o_g + 3 * i), lay["hb0"], lay["gu1"], lay["act1"], "w_in1", "w_out1", i, [])
    grad_x = dh.reshape(x.shape)

    sink_vec = jnp.concatenate(d_sinks).reshape(1, n_sink)
    sg[o_sink] = jnp.tile(jnp.concatenate([sink_vec, jnp.zeros((1, CW - n_sink), F32)], axis=1), (1, N_CHIPS))
    zero_row = jnp.zeros((1, C), F32)
    d_small = jnp.concatenate([zero_row if r is None else r for r in sg], axis=0).reshape(1, SMALL_ROWS, C)

    comm, after = swap_share_comm([("small", 0, d_small)])
    after(_comm_call("grad_tail_swap", comm))
    comm, after = exchange_comm()
    after(_comm_call("grad_tail_exchange", comm))
    comm, after = swap_share_comm([])
    after(_comm_call("grad_tail_share", comm))
    t_w_in1, t_w_out1, t_w_in2, t_w_out2, t_qkv, t_wo, t_lin, t_lout, t_rarx = [TOT[n] for n in gnames]
    t_small = TOT["small"].reshape(SMALL_ROWS, CW)

    G = {"ffn1_w_in": t_w_in1, "ffn1_w_out": t_w_out1, "ffn2_w_in": t_w_in2, "ffn2_w_out": t_w_out2,
         "attn_w_qkv": t_qkv, "attn_w_o": t_wo, "lru_w_in": t_lin, "lru_w_out": t_lout,
         "lru_w_ra": t_rarx[..., :blk_w], "lru_w_rx": t_rarx[..., blk_w:]}

    def unpack_small(p):
        return {"ln_g": p[o_g:o_g + 3 * L].reshape(ln_g.shape), "ln_b": p[o_b:o_b + 3 * L].reshape(ln_b.shape),
                "lru_conv_w": p[o_cw:o_cw + LR * CONV_W].reshape(lru_conv_w.shape), "lru_conv_b": p[o_cb:o_cb + LR],
                "lru_b_ra": p[o_ra:o_ra + LR], "lru_b_rx": p[o_rx:o_rx + LR], "lru_lambda": p[o_lam:o_lam + LR],
                "attn_sinks": p[o_sink, :n_sink].reshape(attn_sinks.shape)}

    G.update(unpack_small(t_small))

    delta, new_m, new_v = {}, {}, {}
    small_names = ["ln_g", "ln_b", "lru_conv_w", "lru_conv_b", "lru_b_ra", "lru_b_rx", "lru_lambda", "attn_sinks"]
    for n in names:
        if n not in small_names:
            G[n], delta[n], new_m[n], new_v[n] = _adamw(W[n], G[n], M[n], V[n])
    _, ds, ms, vs = _adamw(pack_small(W), t_small, pack_small(M), pack_small(V))
    for d, p in ((delta, ds), (new_m, ms), (new_v, vs)):
        d.update(unpack_small(p))

    return (loss, grad_x, *[G[n] for n in names], *[delta[n] for n in names], *[new_m[n] for n in names], *[new_v[n] for n in names])
```

```python
import functools
import math

import jax
import jax.numpy as jnp
from jax import lax
from jax.experimental import pallas as pl
from jax.experimental.pallas import tpu as pltpu

F32 = jnp.float32
BF16 = jnp.bfloat16
MESH = pl.DeviceIdType.MESH

N_HEADS = 16
N_KV_HEADS = 4
HEAD_DIM = 64
GROUP = N_HEADS // N_KV_HEADS
ATTN_BLOCK = 128
ROPE_THETA = 10000.0
RNN_BLOCKS = 4
CONV_W = 4
LRU_C = 8.0
LN_EPS = 1e-5
ADAM_LR = 0.001
ADAM_B1 = 0.9
ADAM_B2 = 0.999
ADAM_EPS = 1e-08
ADAM_WD = 0.01
ADAM_STEP = 10
N_CHIPS = 4
NEG_BIG = -1e30
VMEM_LIMIT_MB = 56
ROW_TILE = 512
SMALL_ROWS = 96


def _cparams(sem):
    return pltpu.CompilerParams(dimension_semantics=sem, vmem_limit_bytes=VMEM_LIMIT_MB << 20)


def _tile(n, pref):
    if n <= pref:
        return n
    for t in range(pref - pref % 16, 0, -16):
        if n % t == 0:
            return t
    raise ValueError((n, pref))


class _SideComm:
    def __init__(self, arrays, n_sems, start, finish):
        self.arrays, self.n_sems, self.start, self.finish = list(arrays), n_sems, start, finish


def _hosted_call(body, comm, *, name, grid, in_specs, out_specs, out_shape, operands, scratch_shapes=(),
                 input_output_aliases=None, compiler_params=None):
    aliases = dict(input_output_aliases or {})
    in_specs, out_specs, out_shape = list(in_specs), list(out_specs), list(out_shape)
    operands, scratch_shapes = list(operands), list(scratch_shapes)
    n_in, n_out = len(in_specs), len(out_specs)
    if comm is None:
        res = pl.pallas_call(body, name=name, grid=grid, in_specs=in_specs, out_specs=out_specs, out_shape=out_shape,
                             scratch_shapes=scratch_shapes, input_output_aliases=aliases,
                             compiler_params=compiler_params)(*operands)
        return list(res), []
    m = len(comm.arrays)
    hbm = pl.BlockSpec(memory_space=pl.ANY)
    real = [t for t, arr in enumerate(comm.arrays) if not isinstance(arr, jax.ShapeDtypeStruct)]
    mi = len(real)
    for ti, t in enumerate(real):
        aliases[n_in + ti] = n_out + t
    n_scr = len(scratch_shapes)

    def hosted(*refs):
        ins = refs[:n_in]
        outs = refs[n_in + mi:n_in + mi + n_out]
        carried = refs[n_in + mi + n_out:n_in + mi + n_out + m]
        scr = refs[n_in + mi + n_out + m:n_in + mi + n_out + m + n_scr]
        send, recv = refs[-2:]
        first = functools.reduce(jnp.logical_and, [pl.program_id(d) == 0 for d in range(len(grid))])
        last = functools.reduce(jnp.logical_and, [pl.program_id(d) == grid[d] - 1 for d in range(len(grid))])

        @pl.when(first)
        def _():
            comm.start(carried, send, recv)

        body(*ins, *outs, *scr)

        @pl.when(last)
        def _():
            comm.finish(carried, send, recv)

    res = pl.pallas_call(
        hosted, name=name, grid=grid, in_specs=in_specs + [hbm] * mi, out_specs=out_specs + [hbm] * m,
        out_shape=out_shape + [jax.ShapeDtypeStruct(a.shape, a.dtype) for a in comm.arrays],
        scratch_shapes=scratch_shapes + [pltpu.SemaphoreType.DMA((comm.n_sems,)), pltpu.SemaphoreType.DMA((comm.n_sems,))],
        input_output_aliases=aliases, compiler_params=compiler_params)(*operands, *[comm.arrays[t] for t in real])
    return list(res[:n_out]), list(res[n_out:])


def _combine(makers):
    comms, base = [], 0
    for mk in makers:
        comms.append(mk(base))
        base += comms[-1].n_sems
    offs = [0]
    for cm in comms:
        offs.append(offs[-1] + len(cm.arrays))

    def start(refs, send, recv):
        for cm, o in zip(comms, offs):
            cm.start(refs[o:o + len(cm.arrays)], send, recv)

    def finish(refs, send, recv):
        for cm, o in zip(comms, offs):
            cm.finish(refs[o:o + len(cm.arrays)], send, recv)

    return _SideComm(sum((cm.arrays for cm in comms), []), base, start, finish), [len(cm.arrays) for cm in comms]


def _comm_call(name, comm):
    def body():
        pass

    _, carried = _hosted_call(body, comm, name=name, grid=(1,), in_specs=[], out_specs=[], out_shape=[], operands=[])
    return carried


def _matmul(name, grid, a, b, outs, acc_shape, *, ta=False, tb=False, extras=(), epilogue=None,
            n_outer=False, alias_in=None, comm=None):
    gm, gn, gk = grid
    if n_outer:
        g = (gn, gm, gk)
        ijk = lambda p, q, k: (q, p, k)
    else:
        g = (gm, gn, gk)
        ijk = lambda p, q, k: (p, q, k)
    w3 = lambda f: (lambda p, q, k: f(*ijk(p, q, k)))
    w2 = lambda f: (lambda p, q, k: f(*ijk(p, q, k)[:2]))
    in_specs = [pl.BlockSpec(a[1], w3(a[2])), pl.BlockSpec(b[1], w3(b[2]))]
    in_specs += [pl.BlockSpec(e[1], w2(e[2])) for e in extras]
    operands = [a[0], b[0]] + [e[0] for e in extras]
    io_alias = {}
    n_alias = 0
    if alias_in is not None:
        in_specs.append(pl.BlockSpec(memory_space=pl.ANY))
        operands.append(alias_in)
        io_alias = {len(operands) - 1: 0}
        n_alias = 1
    ne, no = len(extras), len(outs)
    dims = (((0 if ta else 1,), (1 if tb else 0,)), ((), ()))

    def body(*refs):
        a_ref, b_ref = refs[0], refs[1]
        e_refs = refs[2:2 + ne]
        o_refs = refs[2 + ne + n_alias:2 + ne + n_alias + no]
        part = lax.dot_general(a_ref[...], b_ref[...], dims, preferred_element_type=F32)

        def finish(acc):
            res = epilogue(acc, *[r[...] for r in e_refs]) if epilogue is not None else (acc,)
            for r, v in zip(o_refs, res):
                r[...] = v.astype(r.dtype)

        if gk == 1:
            finish(part)
        else:
            acc_ref = refs[-1]
            k = pl.program_id(2)

            @pl.when(k == 0)
            def _():
                acc_ref[...] = part

            @pl.when(k > 0)
            def _():
                acc_ref[...] += part

            @pl.when(k == gk - 1)
            def _():
                finish(acc_ref[...])

    res, carried = _hosted_call(
        body, comm, name=name, grid=g, in_specs=in_specs,
        out_specs=[pl.BlockSpec(o[2], w2(o[3])) for o in outs],
        out_shape=[jax.ShapeDtypeStruct(o[0], o[1]) for o in outs],
        operands=operands,
        scratch_shapes=[pltpu.VMEM(acc_shape, F32)] if gk > 1 else [],
        input_output_aliases=io_alias,
        compiler_params=_cparams(("arbitrary", "arbitrary", "arbitrary")))
    return res if comm is None else (res, carried)


def _rowwise(name, nsteps, fn, ins, outs, accs=(), prefetch=None, into=None):
    n_in, n_out, n_acc = len(ins), len(outs), len(accs)
    n_pre = 0 if prefetch is None else 1
    n_into = 0 if into is None else 1

    def body(*refs):
        refs = refs[n_pre:]
        i = pl.program_id(0)
        res = fn(i, *[r[...] for r in refs[:n_in]])
        refs = refs[:n_in] + refs[n_in + n_into:]
        for r, v in zip(refs[n_in:n_in + n_out], res[:n_out]):
            r[...] = v.astype(r.dtype)
        acc_refs = refs[n_in + n_out:n_in + n_out + n_acc]
        if n_acc:
            @pl.when(i == 0)
            def _():
                for r in acc_refs:
                    r[...] = jnp.zeros_like(r)

            for r, v in zip(acc_refs, res[n_out:]):
                r[...] += v

    if prefetch is None:
        zero = lambda shape: (lambda i: (0,) * len(shape))
    else:
        zero = lambda shape: (lambda i, p: (0,) * len(shape))
    in_specs = [pl.BlockSpec(b, m) for _, b, m in ins]
    operands = [x[0] for x in ins]
    alias = {}
    if into is not None:
        in_specs.append(pl.BlockSpec(memory_space=pl.ANY))
        operands.append(into)
        alias = {n_pre + n_in: 0}
    out_specs = [pl.BlockSpec(o[2], o[3]) for o in outs] + [pl.BlockSpec(s, zero(s)) for s in accs]
    out_shape = [jax.ShapeDtypeStruct(o[0], o[1]) for o in outs] + [jax.ShapeDtypeStruct(s, F32) for s in accs]
    cp = _cparams(("arbitrary",))
    if prefetch is None:
        call = pl.pallas_call(body, name=name, grid=(nsteps,), in_specs=in_specs, out_specs=out_specs,
                              out_shape=out_shape, input_output_aliases=alias, compiler_params=cp)
        return call(*operands)
    gs = pltpu.PrefetchScalarGridSpec(num_scalar_prefetch=1, grid=(nsteps,), in_specs=in_specs, out_specs=out_specs)
    call = pl.pallas_call(body, name=name, grid_spec=gs, out_shape=out_shape, input_output_aliases=alias, compiler_params=cp)
    return call(prefetch, *operands)


def _rows(arr, tr, cols=None, cb=0):
    cols = arr.shape[1] if cols is None else cols
    return (arr, (tr, cols), lambda i: (i, cb))


def _whole(arr):
    return (arr, arr.shape, lambda i: (0,) * arr.ndim)


def _gelu_tanh(x):
    c = math.sqrt(2.0 / math.pi)
    return x * (0.5 * (1.0 + jnp.tanh(c * (x + 0.044715 * (x * x * x)))))


@jax.custom_jvp
def _expm1(x):
    return jnp.where(jnp.abs(x) < 0.5, jnp.tanh(0.5 * x) * (jnp.exp(x) + 1.0), jnp.exp(x) - 1.0)


@_expm1.defjvp
def _expm1_jvp(primals, tangents):
    (x,), (t,) = primals, tangents
    return _expm1(x), jnp.exp(x) * t


def _log_sigmoid(x):
    return jnp.minimum(x, 0.0) - jnp.log1p(jnp.exp(-jnp.abs(x)))


def _lru_gates(pre, xc, b_ra, b_rx, lam):
    w = xc.shape[-1]
    r = jax.nn.sigmoid(pre[:, :w] + b_ra)
    ig = jax.nn.sigmoid(pre[:, w:] + b_rx)
    log_a = LRU_C * r * _log_sigmoid(lam)
    a = jnp.exp(log_a)
    b = jnp.sqrt(-_expm1(2.0 * log_a)) * (ig * xc)
    return a, b


def _swap_halves(x):
    n = x.shape[1]
    first = (lax.broadcasted_iota(jnp.int32, x.shape, 1) % HEAD_DIM) < (HEAD_DIM // 2)
    return jnp.where(first, pltpu.roll(x, n - HEAD_DIM // 2, 1), pltpu.roll(x, HEAD_DIM // 2, 1))


def _shift_down(prev8, cur, s):
    ext = jnp.concatenate([prev8, cur], axis=0)
    return pltpu.roll(ext, s, 0)[8:]


def _shift_up(cur, next8, s):
    ext = jnp.concatenate([cur, next8], axis=0)
    return pltpu.roll(ext, ext.shape[0] - s, 0)[:cur.shape[0]]


LANES = 128


def _ln_epilogue(alpha, scale):
    def epi(acc, prev, gp, bp, g, b):
        z = alpha * (prev * gp + bp) + scale * acc
        mu = jnp.mean(z, axis=-1, keepdims=True)
        xc = z - mu
        var = jnp.mean(xc * xc, axis=-1, keepdims=True)
        rstd = lax.rsqrt(var + LN_EPS)
        xhat = xc * rstd
        return xhat, xhat * g + b, jnp.broadcast_to(rstd, (rstd.shape[0], LANES))
    return epi


def _proj_ln(name, act, w, layer, prev, g, b, alpha, scale, comm=None):
    T, K = act.shape
    D = w.shape[2]
    bm = _tile(T, 2 * ROW_TILE)
    row = lambda i, j: (i, 0)
    par = lambda i, j: (0, 0)
    return _matmul(
        name, (T // bm, 1, 1),
        (act, (bm, K), lambda i, j, k: (i, 0)), (w, (None, K, D), lambda i, j, k: (layer, 0, 0)),
        [((T, D), F32, (bm, D), row), ((T, D), BF16, (bm, D), row), ((T, LANES), F32, (bm, LANES), row)],
        (bm, D),
        extras=[(prev[0], (bm, D), row), (prev[1], (1, D), par), (prev[2], (1, D), par), (g, (1, D), par), (b, (1, D), par)],
        epilogue=_ln_epilogue(alpha, scale), comm=comm)


def _ln_bwd(name, dh, xhat, rstd, g, scale):
    T, D = xhat.shape
    tr = _tile(T, ROW_TILE)

    def fn(i, dh, xhat, rstd, g):
        dxh = dh * g
        rs = jnp.tile(rstd, (1, D // LANES))
        dz = rs * (dxh - jnp.mean(dxh, axis=-1, keepdims=True) - xhat * jnp.mean(dxh * xhat, axis=-1, keepdims=True))
        return (dz, scale * dz, jnp.sum(dh * xhat, axis=0, keepdims=True), jnp.sum(dh, axis=0, keepdims=True))

    row = lambda i: (i, 0)
    return _rowwise(name, T // tr, fn, [_rows(dh, tr), _rows(xhat, tr), _rows(rstd, tr), _whole(g)],
                    [((T, D), F32, (tr, D), row), ((T, D), BF16, (tr, D), row)], accs=[(1, D), (1, D)])


def _grad_tn(name, a, b, bm, bn, bk=4 * ROW_TILE, block_diag=False, comm=None):
    T, M = a.shape
    N = b.shape[1]
    bk = _tile(T, bk)
    if not block_diag:
        shape, oblk, omap = (1, M, N), (None, bm, bn), (lambda i, j: (0, i, j))
        amap, gm = (lambda i, j, k: (k, i)), M // bm
    else:
        shape, oblk, omap = (1, N // bn, bm, bn), (None, None, bm, bn), (lambda i, j: (0, j, 0, 0))
        amap, gm = (lambda i, j, k: (k, j)), 1
    res = _matmul(name, (gm, N // bn, T // bk), (a, (bk, bm), amap), (b, (bk, bn), lambda i, j, k: (k, j)),
                  [(shape, BF16, oblk, omap)], (bm, bn), ta=True, comm=comm)
    return res[0] if comm is None else (res[0][0], res[1])


MXU_COLS = 256


def _col_chunks(width):
    return [(s, min(MXU_COLS, width - s)) for s in range(0, width, MXU_COLS)]


def _ffn_up(hb, w_in, layer, comm=None):
    T, D = hb.shape
    N2 = w_in.shape[2]
    wd = N2 // 4
    bm = _tile(T, 2 * ROW_TILE)

    def body(a_ref, w_ref, gu_ref, act_ref):
        a = a_ref[...]
        for s, n in _col_chunks(wd):
            gg = jnp.dot(a, w_ref[:, s:s + n], preferred_element_type=F32)
            uu = jnp.dot(a, w_ref[:, wd + s:wd + s + n], preferred_element_type=F32)
            sg = jax.nn.sigmoid(gg)
            silu = gg * sg
            gu_ref[:, s:s + n] = (uu * (sg + silu * (1.0 - sg))).astype(gu_ref.dtype)
            gu_ref[:, wd + s:wd + s + n] = silu.astype(gu_ref.dtype)
            act_ref[:, s:s + n] = (silu * uu).astype(act_ref.dtype)

    (gu, act), carried = _hosted_call(
        body, comm, name="ffn_up", grid=(2, T // bm),
        in_specs=[pl.BlockSpec((bm, D), lambda j, i: (i, 0)), pl.BlockSpec((None, D, 2 * wd), lambda j, i: (layer, 0, j))],
        out_specs=[pl.BlockSpec((bm, 2 * wd), lambda j, i: (i, j)), pl.BlockSpec((bm, wd), lambda j, i: (i, j))],
        out_shape=[jax.ShapeDtypeStruct((T, N2), BF16), jax.ShapeDtypeStruct((T, N2 // 2), BF16)],
        operands=[hb, w_in], compiler_params=_cparams(("arbitrary", "arbitrary")))
    return gu, act, carried


def _ffn_dact(dyb, w_out, layer, gu):
    T, D = dyb.shape
    N2 = gu.shape[1]
    wd = N2 // 4
    bm = _tile(T, 2 * ROW_TILE)

    def body(dy_ref, w_ref, gu_ref, o_ref):
        dy = dy_ref[...]
        for s, n in _col_chunks(wd):
            dact = lax.dot_general(dy, w_ref[s:s + n, :], (((1,), (1,)), ((), ())), preferred_element_type=F32)
            o_ref[:, s:s + n] = (dact * gu_ref[:, s:s + n].astype(F32)).astype(o_ref.dtype)
            o_ref[:, wd + s:wd + s + n] = (dact * gu_ref[:, wd + s:wd + s + n].astype(F32)).astype(o_ref.dtype)

    return pl.pallas_call(
        body, name="ffn_dact", grid=(2, T // bm),
        in_specs=[pl.BlockSpec((bm, D), lambda j, i: (i, 0)), pl.BlockSpec((None, wd, D), lambda j, i: (layer, j, 0)),
                  pl.BlockSpec((bm, 2 * wd), lambda j, i: (i, j))],
        out_specs=pl.BlockSpec((bm, 2 * wd), lambda j, i: (i, j)),
        out_shape=jax.ShapeDtypeStruct((T, N2), BF16),
        compiler_params=_cparams(("arbitrary", "arbitrary")))(dyb, w_out, gu)


def _ffn_dx(dgu, w_in, layer, dz, alpha, comm=None):
    T, N2 = dgu.shape
    D = w_in.shape[1]
    bm = _tile(T, ROW_TILE)
    res = _matmul(
        "ffn_dx", (T // bm, 1, 1),
        (dgu, (bm, N2), lambda i, j, k: (i, 0)), (w_in, (None, D, N2), lambda i, j, k: (layer, 0, 0)),
        [((T, D), F32, (bm, D), lambda i, j: (i, 0))], (bm, D), tb=True,
        extras=[(dz, (bm, D), lambda i, j: (i, 0))], epilogue=lambda acc, dzb: (alpha * dzb + acc,), comm=comm)
    return res[0] if comm is None else (res[0][0], res[1])


def _input_grad(name, dy, w, layer, dz, alpha, comm):
    T, N = dy.shape
    D = w.shape[1]
    bm = _tile(T, ROW_TILE)
    (out,), carried = _matmul(
        name, (T // bm, 1, 1),
        (dy, (bm, N), lambda i, j, k: (i, 0)), (w, (None, D, N), lambda i, j, k: (layer, 0, 0)),
        [((T, D), F32, (bm, D), lambda i, j: (i, 0))], (bm, D), tb=True,
        extras=[(dz, (bm, D), lambda i, j: (i, 0))], epilogue=lambda acc, dzb: (alpha * dzb + acc,), comm=comm)
    return out, carried


def _back_proj(name, dy, w, layer, dtype):
    T, D = dy.shape
    K = w.shape[1]
    bm = _tile(T, ROW_TILE)
    (out,) = _matmul(
        name, (T // bm, 1, 1),
        (dy, (bm, D), lambda i, j, k: (i, 0)), (w, (None, K, D), lambda i, j, k: (layer, 0, 0)),
        [((T, K), dtype, (bm, K), lambda i, j: (i, 0))], (bm, K), tb=True)
    return out


def _rope_tables(T):
    pos = jnp.arange(T, dtype=F32)
    inv_freq = ROPE_THETA ** (-jnp.arange(0, HEAD_DIM, 2, dtype=F32) / HEAD_DIM)
    ang = pos[:, None] * inv_freq[None, :]
    cos, sin = jnp.cos(ang), jnp.sin(ang)
    c128 = jnp.tile(cos, (1, 4))
    s128 = jnp.tile(jnp.concatenate([-sin, sin], axis=1), (1, 2))
    return c128, s128


QK_COLS = (N_HEADS + N_KV_HEADS) * HEAD_DIM
Q_COLS = N_HEADS * HEAD_DIM
KV_COLS = N_KV_HEADS * HEAD_DIM
Q_SCALE = HEAD_DIM ** -0.5


def _qkv_rope(hb, w_qkv, layer, c128, s128):
    T, D = hb.shape
    N = w_qkv.shape[2]
    bm = _tile(T, ROW_TILE)

    def epi(acc, c, s):
        x = acc[:, :QK_COLS]
        rep = QK_COLS // 128
        r = x * jnp.tile(c, (1, rep)) + _swap_halves(x) * jnp.tile(s, (1, rep))
        return (jnp.concatenate([r[:, :Q_COLS] * Q_SCALE, r[:, Q_COLS:], acc[:, QK_COLS:]], axis=1),)

    (qkv,) = _matmul(
        "qkv_rope", (T // bm, 1, 1),
        (hb, (bm, D), lambda i, j, k: (i, 0)), (w_qkv, (None, D, N), lambda i, j, k: (layer, 0, 0)),
        [((T, N), BF16, (bm, N), lambda i, j: (i, 0))], (bm, N),
        extras=[(c128, (bm, 128), lambda i, j: (i, 0)), (s128, (bm, 128), lambda i, j: (i, 0))], epilogue=epi)
    return qkv


def _rope_bwd(dq, dkv, c128, s128):
    T = dq.shape[0]
    tr = _tile(T, ROW_TILE)

    def fn(i, dq, dkv, c, s):
        dx = jnp.concatenate([dq * Q_SCALE, dkv[:, :KV_COLS]], axis=1)
        rep = QK_COLS // 128
        d = dx * jnp.tile(c, (1, rep)) + _swap_halves(dx * jnp.tile(s, (1, rep)))
        return (jnp.concatenate([d, dkv[:, KV_COLS:]], axis=1),)

    N = Q_COLS + 2 * KV_COLS
    (out,) = _rowwise("rope_bwd", T // tr, fn, [_rows(dq, tr), _rows(dkv, tr), _rows(c128, tr), _rows(s128, tr)],
                      [((T, N), BF16, (tr, N), lambda i: (i, 0))])
    return out


def _attn_mask(first_block):
    q_pos = lax.broadcasted_iota(jnp.int32, (GROUP * ATTN_BLOCK, 2 * ATTN_BLOCK), 0) & (ATTN_BLOCK - 1)
    col = lax.broadcasted_iota(jnp.int32, (GROUP * ATTN_BLOCK, 2 * ATTN_BLOCK), 1)
    dist = q_pos + ATTN_BLOCK - col
    return (dist >= 0) & (dist < ATTN_BLOCK) & ((col >= ATTN_BLOCK) | jnp.logical_not(first_block))


def _sink_column(sk_ref, kvh):
    rg = lax.broadcasted_iota(jnp.int32, (GROUP * ATTN_BLOCK, 1), 0) // ATTN_BLOCK
    col = jnp.full((GROUP * ATTN_BLOCK, 1), sk_ref[0, kvh * GROUP], F32)
    for gi in range(1, GROUP):
        col = jnp.where(rg == gi, sk_ref[0, kvh * GROUP + gi], col)
    return col


def _stack_heads(x, kvh):
    return jnp.concatenate([x[:, (kvh * GROUP + gi) * HEAD_DIM:(kvh * GROUP + gi + 1) * HEAD_DIM] for gi in range(GROUP)], axis=0)


def _unstack_heads(parts):
    cols = []
    for p in parts:
        cols += [p[gi * ATTN_BLOCK:(gi + 1) * ATTN_BLOCK] for gi in range(GROUP)]
    return jnp.concatenate(cols, axis=1)


def _attn_softmax(q4, kb, mask, sink):
    s = lax.dot_general(q4, kb, (((1,), (1,)), ((), ())), preferred_element_type=F32)
    s = jnp.where(mask, s, NEG_BIG)
    m = jnp.maximum(jnp.max(s, axis=1, keepdims=True), sink)
    p = jnp.exp(s - m)
    e_sink = jnp.exp(sink - m)
    den = jnp.sum(p, axis=1, keepdims=True) + e_sink
    return p / den, e_sink / den


def _attn_fwd(qkv, sinks):
    T = qkv.shape[0]
    nb = T // ATTN_BLOCK
    kcb, vcb = Q_COLS // KV_COLS, Q_COLS // KV_COLS + 1

    def body(q_ref, kc_ref, kp_ref, vc_ref, vp_ref, sk_ref, o_ref):
        i = pl.program_id(0)
        mask = _attn_mask(i == 0)
        q = q_ref[...]
        kband = jnp.concatenate([kp_ref[...], kc_ref[...]], axis=0)
        vband = jnp.concatenate([vp_ref[...], vc_ref[...]], axis=0)
        parts = []
        for kvh in range(N_KV_HEADS):
            hs = slice(kvh * HEAD_DIM, (kvh + 1) * HEAD_DIM)
            pn, _ = _attn_softmax(_stack_heads(q, kvh), kband[:, hs], mask, _sink_column(sk_ref, kvh))
            parts.append(jnp.dot(pn.astype(BF16), vband[:, hs], preferred_element_type=F32))
        o_ref[...] = _unstack_heads(parts).astype(o_ref.dtype)

    prev = lambda i: jnp.maximum(i - 1, 0)
    return pl.pallas_call(
        body, name="attn_fwd", grid=(nb,),
        in_specs=[pl.BlockSpec((ATTN_BLOCK, Q_COLS), lambda i: (i, 0)),
                  pl.BlockSpec((ATTN_BLOCK, KV_COLS), lambda i: (i, kcb)),
                  pl.BlockSpec((ATTN_BLOCK, KV_COLS), lambda i: (prev(i), kcb)),
                  pl.BlockSpec((ATTN_BLOCK, KV_COLS), lambda i: (i, vcb)),
                  pl.BlockSpec((ATTN_BLOCK, KV_COLS), lambda i: (prev(i), vcb)),
                  pl.BlockSpec(memory_space=pltpu.SMEM)],
        out_specs=pl.BlockSpec((ATTN_BLOCK, Q_COLS), lambda i: (i, 0)),
        out_shape=jax.ShapeDtypeStruct((T, Q_COLS), BF16),
        compiler_params=_cparams(("arbitrary",)),
    )(qkv, qkv, qkv, qkv, qkv, sinks)


def _attn_bwd(qkv, do, sinks):
    T = qkv.shape[0]
    nb = T // ATTN_BLOCK
    kcb, vcb = Q_COLS // KV_COLS, Q_COLS // KV_COLS + 1
    B = ATTN_BLOCK

    def body(q_ref, kc_ref, kp_ref, vc_ref, vp_ref, do_ref, sk_ref, dq_ref, dkv_ref, dsk_ref, carry_ref):
        i = pl.program_id(0)

        @pl.when(i == 0)
        def _():
            carry_ref[...] = jnp.zeros_like(carry_ref)
            dsk_ref[...] = jnp.zeros_like(dsk_ref)

        @pl.when(i < nb)
        def _():
            mask = _attn_mask(i == 0)
            q = q_ref[...]
            do_blk = do_ref[...]
            kband = jnp.concatenate([kp_ref[...], kc_ref[...]], axis=0)
            vband = jnp.concatenate([vp_ref[...], vc_ref[...]], axis=0)
            dq_parts, dk_parts, dv_parts = [], [], []
            for kvh in range(N_KV_HEADS):
                hs = slice(kvh * HEAD_DIM, (kvh + 1) * HEAD_DIM)
                q4 = _stack_heads(q, kvh)
                do4 = _stack_heads(do_blk, kvh)
                kb, vb = kband[:, hs], vband[:, hs]
                pn, p_sink = _attn_softmax(q4, kb, mask, _sink_column(sk_ref, kvh))
                dp = lax.dot_general(do4, vb, (((1,), (1,)), ((), ())), preferred_element_type=F32)
                delta = jnp.sum(pn * dp, axis=1, keepdims=True)
                ds = (pn * (dp - delta)).astype(BF16)
                dsk_ref[kvh] += -(p_sink * delta)
                dq_parts.append(jnp.dot(ds, kb, preferred_element_type=F32))
                dk_parts.append(lax.dot_general(ds, q4, (((0,), (0,)), ((), ())), preferred_element_type=F32))
                dv_parts.append(lax.dot_general(pn.astype(BF16), do4, (((0,), (0,)), ((), ())), preferred_element_type=F32))
            dq_ref[...] = _unstack_heads(dq_parts)
            dkv = jnp.concatenate(dk_parts + dv_parts, axis=1)
            dkv_ref[...] = carry_ref[...] + dkv[:B]
            carry_ref[...] = dkv[B:]

        @pl.when(i == nb)
        def _():
            dkv_ref[...] = carry_ref[...]

    cur = lambda i: jnp.minimum(i, nb - 1)
    prev = lambda i: jnp.maximum(cur(i) - 1, 0)
    lag = lambda i: jnp.maximum(i - 1, 0)
    return pl.pallas_call(
        body, name="attn_bwd", grid=(nb + 1,),
        in_specs=[pl.BlockSpec((B, Q_COLS), lambda i: (cur(i), 0)),
                  pl.BlockSpec((B, KV_COLS), lambda i: (cur(i), kcb)),
                  pl.BlockSpec((B, KV_COLS), lambda i: (prev(i), kcb)),
                  pl.BlockSpec((B, KV_COLS), lambda i: (cur(i), vcb)),
                  pl.BlockSpec((B, KV_COLS), lambda i: (prev(i), vcb)),
                  pl.BlockSpec((B, Q_COLS), lambda i: (cur(i), 0)),
                  pl.BlockSpec(memory_space=pltpu.SMEM)],
        out_specs=[pl.BlockSpec((B, Q_COLS), lambda i: (cur(i), 0)),
                   pl.BlockSpec((B, 2 * KV_COLS), lambda i: (lag(i), 0)),
                   pl.BlockSpec((N_KV_HEADS, GROUP * B, 1), lambda i: (0, 0, 0))],
        out_shape=[jax.ShapeDtypeStruct((T, Q_COLS), F32), jax.ShapeDtypeStruct((T, 2 * KV_COLS), F32),
                   jax.ShapeDtypeStruct((N_KV_HEADS, GROUP * B, 1), F32)],
        scratch_shapes=[pltpu.VMEM((B, 2 * KV_COLS), F32)],
        compiler_params=_cparams(("arbitrary",)),
    )(qkv, qkv, qkv, qkv, qkv, do, sinks)


def _halo_prev(arr, tr, cols, cb=0):
    per = tr // 8
    return (arr, (8, cols), lambda i: (jnp.maximum(i * per - 1, 0), cb))


def _halo_next(arr, tr, cols, cb=0):
    per = tr // 8
    last = arr.shape[0] // 8 - 1
    return (arr, (8, cols), lambda i: (jnp.minimum((i + 1) * per, last), cb))


def _conv_fwd(xg, cw, cb):
    T = xg.shape[0]
    C = cb.shape[1]
    tr = _tile(T, ROW_TILE)

    def fn(i, cur, prev8, cb, *cw):
        prev8 = jnp.where(i == 0, 0.0, prev8)
        xc = cb + cw[CONV_W - 1] * cur
        for s in range(1, CONV_W):
            xc = xc + cw[CONV_W - 1 - s] * _shift_down(prev8, cur, s)
        return xc, xc

    row = lambda i: (i, 0)
    return _rowwise("lru_conv", T // tr, fn, [_rows(xg, tr, C), _halo_prev(xg, tr, C), _whole(cb)] + [_whole(w) for w in cw],
                    [((T, C), F32, (tr, C), row), ((T, C), BF16, (tr, C), row)])


def _conv_bwd(dxc, xg, dgb, cw):
    T, C = dxc.shape
    tr = _tile(T, ROW_TILE)
    nt = T // tr

    def fn(i, d_cur, d_next8, x_cur, x_prev8, dgb, *cw):
        d_next8 = jnp.where(i == nt - 1, 0.0, d_next8)
        x_prev8 = jnp.where(i == 0, 0.0, x_prev8)
        dxb = cw[CONV_W - 1] * d_cur
        dcw = [jnp.sum(d_cur * x_cur, axis=0, keepdims=True)]
        for s in range(1, CONV_W):
            dxb = dxb + cw[CONV_W - 1 - s] * _shift_up(d_cur, d_next8, s)
            dcw.append(jnp.sum(d_cur * _shift_down(x_prev8, x_cur, s), axis=0, keepdims=True))
        return (jnp.concatenate([dxb.astype(BF16), dgb], axis=1), dcw[3], dcw[2], dcw[1], dcw[0],
                jnp.sum(d_cur, axis=0, keepdims=True))

    return _rowwise("lru_conv_bwd", nt, fn,
                    [_rows(dxc, tr), _halo_next(dxc, tr, C), _rows(xg, tr, C), _halo_prev(xg, tr, C), _rows(dgb, tr)] + [_whole(w) for w in cw],
                    [((T, 2 * C), BF16, (tr, 2 * C), lambda i: (i, 0))], accs=[(1, C)] * (CONV_W + 1))


def _lru_gate_fwd(xc, xcb, w_rarx, layer, b_ra, b_rx, lam):
    T, C = xc.shape
    W = C // RNN_BLOCKS
    bm = _tile(T, ROW_TILE)

    def epi(acc, xc_blk, bra, brx, lm):
        a, b = _lru_gates(acc, xc_blk, bra, brx, lm)
        return acc, a, b

    blk = lambda i, j: (i, j)
    par = lambda i, j: (0, j)
    return _matmul(
        "lru_gates", (T // bm, RNN_BLOCKS, 1),
        (xcb, (bm, W), lambda i, j, k: (i, j)), (w_rarx, (None, None, W, 2 * W), lambda i, j, k: (layer, j, 0, 0)),
        [((T, 2 * C), F32, (bm, 2 * W), blk), ((T, C), F32, (bm, W), blk), ((T, C), F32, (bm, W), blk)],
        (bm, 2 * W),
        extras=[(xc, (bm, W), blk), (b_ra, (1, W), par), (b_rx, (1, W), par), (lam, (1, W), par)], epilogue=epi)


def _lru_gate_bwd(pre, xc, lam_adj, h, b_ra, b_rx, lam):
    T, C = xc.shape
    W = C // RNN_BLOCKS
    tr = _tile(T, ROW_TILE // 2)

    def fn(i, pre, xc, adj, h_cur, h_prev8, bra, brx, lm):
        h_prev8 = jnp.where(i == 0, 0.0, h_prev8)
        da = adj * _shift_down(h_prev8, h_cur, 1)
        dpre, dxc, dbra, dbrx, dlam = [], [], [], [], []
        for n in range(RNN_BLOCKS):
            cs = slice(n * W, (n + 1) * W)
            _, vjp = jax.vjp(_lru_gates, pre[:, 2 * n * W:2 * (n + 1) * W], xc[:, cs], bra[:, cs], brx[:, cs], lm[:, cs])
            g = vjp((da[:, cs], adj[:, cs]))
            for lst, v in zip((dpre, dxc, dbra, dbrx, dlam), g):
                lst.append(v)
        cat = lambda l: jnp.concatenate(l, axis=1)
        return cat(dpre), cat(dxc), cat(dbra), cat(dbrx), cat(dlam)

    row = lambda i: (i, 0)
    return _rowwise("lru_gates_bwd", T // tr, fn,
                    [_rows(pre, tr), _rows(xc, tr), _rows(lam_adj, tr), _rows(h, tr), _halo_prev(h, tr, C),
                     _whole(b_ra), _whole(b_rx), _whole(lam)],
                    [((T, 2 * C), BF16, (tr, 2 * C), row), ((T, C), F32, (tr, C), row)], accs=[(1, C)] * 3)


def _scan_fwd(a, b):
    T, C = a.shape
    tt = _tile(T, ROW_TILE)

    def body(a_ref, b_ref, o_ref, c_ref):
        @pl.when(pl.program_id(0) == 0)
        def _():
            c_ref[...] = jnp.zeros_like(c_ref)

        row = lax.broadcasted_iota(jnp.int32, (8, C), 0)

        def step(j, carry):
            sl = pl.ds(pl.multiple_of(j * 8, 8), 8)
            A, B = a_ref[sl, :], b_ref[sl, :]
            for d in (1, 2, 4):
                ok = row >= d
                B = jnp.where(ok, A * pltpu.roll(B, d, 0) + B, B)
                A = jnp.where(ok, A * pltpu.roll(A, d, 0), A)
            h = A * carry + B
            o_ref[sl, :] = h
            return jnp.sum(jnp.where(row == 7, h, 0.0), axis=0, keepdims=True)

        c_ref[0:1, :] = lax.fori_loop(0, tt // 8, step, c_ref[0:1, :])

    spec = pl.BlockSpec((tt, C), lambda i: (i, 0))
    return pl.pallas_call(body, name="lru_scan", grid=(T // tt,), in_specs=[spec, spec], out_specs=spec,
                          out_shape=jax.ShapeDtypeStruct((T, C), F32), scratch_shapes=[pltpu.VMEM((8, C), F32)],
                          compiler_params=_cparams(("arbitrary",)))(a, b)


def _scan_bwd(a, dh):
    T, C = a.shape
    tt = _tile(T, ROW_TILE)
    nt = T // tt

    def body(a_ref, d_ref, o_ref, c_ref):
        @pl.when(pl.program_id(0) == 0)
        def _():
            c_ref[...] = jnp.zeros_like(c_ref)

        row = lax.broadcasted_iota(jnp.int32, (8, C), 0)

        def step(jj, carry):
            adj_next, a_next = carry
            j = tt // 8 - 1 - jj
            sl = pl.ds(pl.multiple_of(j * 8, 8), 8)
            a_blk = a_ref[sl, :]
            A = jnp.where(row < 7, pltpu.roll(a_blk, 7, 0), a_next)
            B = d_ref[sl, :]
            for d in (1, 2, 4):
                ok = row < 8 - d
                B = jnp.where(ok, A * pltpu.roll(B, 8 - d, 0) + B, B)
                A = jnp.where(ok, A * pltpu.roll(A, 8 - d, 0), A)
            adj = A * adj_next + B
            o_ref[sl, :] = adj
            first = lambda v: jnp.sum(jnp.where(row == 0, v, 0.0), axis=0, keepdims=True)
            return first(adj), first(a_blk)

        adj0, a0 = lax.fori_loop(0, tt // 8, step, (c_ref[0:1, :], c_ref[1:2, :]))
        c_ref[0:1, :] = adj0
        c_ref[1:2, :] = a0

    spec = pl.BlockSpec((tt, C), lambda i: (nt - 1 - i, 0))
    return pl.pallas_call(body, name="lru_scan_bwd", grid=(nt,), in_specs=[spec, spec], out_specs=spec,
                          out_shape=jax.ShapeDtypeStruct((T, C), F32), scratch_shapes=[pltpu.VMEM((8, C), F32)],
                          compiler_params=_cparams(("arbitrary",)))(a, dh)


def _lru_out_fwd(h, xg):
    T, C = h.shape
    tr = _tile(T, ROW_TILE)
    (y,) = _rowwise("lru_out", T // tr, lambda i, h, gb: (h * _gelu_tanh(gb),), [_rows(h, tr), _rows(xg, tr, C, 1)],
                    [((T, C), BF16, (tr, C), lambda i: (i, 0))])
    return y


def _lru_out_bwd(dy, h, xg):
    T, C = h.shape
    tr = _tile(T, ROW_TILE)

    def fn(i, dy, h, gb):
        _, vjp = jax.vjp(lambda h, gb: h * _gelu_tanh(gb), h, gb)
        return vjp(dy)

    row = lambda i: (i, 0)
    return _rowwise("lru_out_bwd", T // tr, fn, [_rows(dy, tr), _rows(h, tr), _rows(xg, tr, C, 1)],
                    [((T, C), F32, (tr, C), row), ((T, C), BF16, (tr, C), row)])


class _Sharded:
    def __init__(self, kind, size, off=0, width=None):
        self.kind, self.size, self.off, self.width = kind, size, off, width

    def slot(self, cx, cy):
        return (2 * cy + cx) if self.kind == "perm" else (2 * cx + cy)

    def at(self, ref, cx, cy, layers, half=None):
        s = self.slot(cx, cy)
        start = s * self.size
        if not isinstance(start, int):
            start = pl.multiple_of(start, 8 if self.kind in ("rows", "rarx") else 128)
        if self.kind in ("cols", "perm"):
            n = ref.shape[1]
            rows = slice(None) if half is None else pl.ds(pl.multiple_of(half * (n // 2), 8), n // 2)
            return ref.at[layers, rows, pl.ds(start, self.size)]
        if self.kind == "rows":
            if half is None:
                return ref.at[layers, pl.ds(start, self.size), :]
            return ref.at[layers, pl.ds(pl.multiple_of(start + half * (self.size // 2), 8), self.size // 2), :]
        if self.kind == "rarx":
            n = ref.shape[1]
            blocks = slice(None) if half is None else pl.ds(half * (n // 2), n // 2)
            return ref.at[layers, blocks, pl.ds(start, self.size), pl.ds(self.off, self.width)]
        raise ValueError(self.kind)


def _mesh_pos():
    return lax.axis_index("x"), lax.axis_index("y"), lax.axis_index("c")


def _peer_chips(x, y):
    return [(1 - x, y), (x, 1 - y), (1 - x, 1 - y)]


def _place(shard, out_shape, out_dtype, sh, pos, prev=None):
    def body(p_ref, x_ref, *rest):
        rest[-1][...] = x_ref[...].astype(rest[-1].dtype)

    if sh.kind in ("cols", "perm"):
        L, R, Ns = shard.shape
        tr = _tile(R, ROW_TILE)
        k = 1 if sh.kind == "perm" else 0
        grid = (L, R // tr)
        ispec = pl.BlockSpec((None, tr, Ns), lambda l, i, p: (l, i, 0))
        ospec = pl.BlockSpec((None, tr, Ns), lambda l, i, p: (l, i, p[k]))
    elif sh.kind == "rows":
        L, Rs, D = shard.shape
        tr = _tile(Rs, ROW_TILE)
        nt = Rs // tr
        grid = (L, nt)
        ispec = pl.BlockSpec((None, tr, D), lambda l, i, p: (l, i, 0))
        ospec = pl.BlockSpec((None, tr, D), lambda l, i, p: (l, p[0] * nt + i, 0))
    else:
        L, nb, Rs, Wd = shard.shape
        cb = sh.off // Wd
        grid = (L, nb)
        ispec = pl.BlockSpec((None, None, Rs, Wd), lambda l, i, p: (l, i, 0, 0))
        ospec = pl.BlockSpec((None, None, Rs, Wd), lambda l, i, p: (l, i, p[0], cb))
    in_specs = [ispec]
    operands = [pos, shard]
    alias = {}
    if prev is not None:
        in_specs.append(pl.BlockSpec(memory_space=pl.ANY))
        operands.append(prev)
        alias = {2: 0}
    gs = pltpu.PrefetchScalarGridSpec(num_scalar_prefetch=1, grid=grid, in_specs=in_specs, out_specs=ospec)
    return pl.pallas_call(body, name="weight_place", grid_spec=gs, out_shape=jax.ShapeDtypeStruct(out_shape, out_dtype),
                          input_output_aliases=alias, compiler_params=_cparams(("arbitrary", "arbitrary")))(*operands)


def _gather_over_ici(jobs, base=0):
    assert len({id(j[0]) for j in jobs}) == len(jobs)

    def copies(refs, send, recv):
        x, y, c = _mesh_pos()
        out = []
        for t, (_, sh, layer) in enumerate(jobs):
            lay = pl.ds(layer, 1)
            mine = sh.at(refs[t], x, y, lay, c)
            for j, (px, py) in enumerate(_peer_chips(x, y)):
                theirs = sh.at(refs[t], px, py, lay, c)
                k = base + 3 * t + j
                out.append((pltpu.make_async_remote_copy(mine, mine, send.at[k], recv.at[k],
                                                         device_id=(px, py, c), device_id_type=MESH),
                            pltpu.make_async_remote_copy(theirs, theirs, send.at[k], recv.at[k],
                                                         device_id=(px, py, c), device_id_type=MESH)))
        return out

    def start(refs, send, recv):
        for out_cp, _ in copies(refs, send, recv):
            out_cp.start()

    def finish(refs, send, recv):
        for out_cp, in_cp in copies(refs, send, recv):
            in_cp.wait_recv()
            out_cp.wait_send()

    return _SideComm([j[0] for j in jobs], 3 * len(jobs), start, finish)


def _pass_to_sibling(jobs, base=0):
    assert len({id(j[0]) for j in jobs}) == len(jobs)

    def copies(refs, send, recv):
        x, y, c = _mesh_pos()
        out = []
        for t, (_, sh, layer) in enumerate(jobs):
            lay = pl.ds(layer, 1)
            for j, (px, py) in enumerate(_peer_chips(x, y)):
                got = sh.at(refs[t], px, py, lay, c)
                coming = sh.at(refs[t], px, py, lay, 1 - c)
                k = base + 3 * t + j
                out.append((pltpu.make_async_remote_copy(got, got, send.at[k], recv.at[k],
                                                         device_id=(x, y, 1 - c), device_id_type=MESH),
                            pltpu.make_async_remote_copy(coming, coming, send.at[k], recv.at[k],
                                                         device_id=(x, y, 1 - c), device_id_type=MESH)))
        return out

    def start(refs, send, recv):
        for out_cp, _ in copies(refs, send, recv):
            out_cp.start()

    def finish(refs, send, recv):
        for out_cp, in_cp in copies(refs, send, recv):
            in_cp.wait_recv()
            out_cp.wait_send()

    return _SideComm([j[0] for j in jobs], 3 * len(jobs), start, finish)


def _all_gather(jobs):
    n = len(jobs)
    ici, d2d = _gather_over_ici(jobs), _pass_to_sibling(jobs, 3 * n)

    def body(*refs):
        outs = refs[n:2 * n]
        send, recv = refs[2 * n:]
        ici.start(outs, send, recv)
        ici.finish(outs, send, recv)
        d2d.start(outs, send, recv)
        d2d.finish(outs, send, recv)

    hbm = pl.BlockSpec(memory_space=pl.ANY)
    return pl.pallas_call(
        body, name="weights_all_gather", in_specs=[hbm] * n, out_specs=[hbm] * n,
        out_shape=[jax.ShapeDtypeStruct(j[0].shape, j[0].dtype) for j in jobs],
        input_output_aliases={t: t for t in range(n)},
        scratch_shapes=[pltpu.SemaphoreType.DMA((6 * n,)), pltpu.SemaphoreType.DMA((6 * n,))],
    )(*[j[0] for j in jobs])


def _half_view(sh, ref, h):
    if sh.kind == "rows":
        n = ref.shape[-1]
        return ref.at[:, pl.ds(pl.multiple_of(h * (n // 2), 128), n // 2)]
    n = ref.shape[0]
    return ref.at[pl.ds(h * (n // 2), n // 2)]


def _half_shape(sh, layer_shape):
    s = list(layer_shape)
    s[len(s) - 1 if sh.kind == "rows" else 0] //= 2
    return tuple(s)


def _chip_part(sh, ref, cx, cy):
    start = sh.slot(cx, cy) * sh.size
    if sh.kind in ("cols", "perm"):
        return ref.at[:, pl.ds(pl.multiple_of(start, 128), sh.size)]
    if sh.kind == "rows":
        return ref.at[pl.ds(pl.multiple_of(start, 8), sh.size), :]
    return ref.at[:, pl.ds(pl.multiple_of(start, 8), sh.size), :]


def _chip_part_shape(sh, half_shape):
    s = list(half_shape)
    s[{"cols": len(s) - 1, "perm": len(s) - 1, "rows": 0, "rarx": 1}[sh.kind]] = sh.size
    return tuple(s)


def _swap_stage(jobs):
    assert len({id(j[0]) for j in jobs}) == len(jobs)

    def make(base):
        arrays = []
        for g, sh, _ in jobs:
            arrays += [g, jax.ShapeDtypeStruct(_half_shape(sh, g.shape[1:]), g.dtype)]

        def copies(refs, send, recv):
            x, y, c = _mesh_pos()
            return [pltpu.make_async_remote_copy(_half_view(sh, refs[2 * t].at[layer], 1 - c), refs[2 * t + 1],
                                                 send.at[base + t], recv.at[base + t],
                                                 device_id=(x, y, 1 - c), device_id_type=MESH)
                    for t, (_, sh, layer) in enumerate(jobs)]

        def start(refs, send, recv):
            for cp in copies(refs, send, recv):
                cp.start()

        def finish(refs, send, recv):
            for cp in copies(refs, send, recv):
                cp.wait()

        return _SideComm(arrays, len(jobs), start, finish)
    return make


def _exchange_stage(jobs):
    n = len(jobs)

    def make(base):
        arrays = []
        for p, sh in jobs:
            arrays += [p, jax.ShapeDtypeStruct((N_CHIPS,) + _chip_part_shape(sh, p.shape), p.dtype)]

        def copies(refs, send, recv):
            x, y, c = _mesh_pos()
            me = 2 * x + y
            out = []
            for t, (_, sh) in enumerate(jobs):
                part, land = refs[2 * t], refs[2 * t + 1]
                local = pltpu.make_async_copy(_chip_part(sh, part, x, y), land.at[me], send.at[base + 3 * n + t])
                remote = []
                for j, (px, py) in enumerate(_peer_chips(x, y)):
                    k = base + 3 * t + j
                    src = land.at[2 * px + py]
                    remote.append((pltpu.make_async_remote_copy(_chip_part(sh, part, px, py), land.at[me], send.at[k],
                                                                recv.at[k], device_id=(px, py, c), device_id_type=MESH),
                                   pltpu.make_async_remote_copy(src, src, send.at[k], recv.at[k],
                                                                device_id=(px, py, c), device_id_type=MESH)))
                out.append((local, remote))
            return out

        def start(refs, send, recv):
            for local, remote in copies(refs, send, recv):
                local.start()
                for out_cp, _ in remote:
                    out_cp.start()

        def finish(refs, send, recv):
            for local, remote in copies(refs, send, recv):
                for out_cp, in_cp in remote:
                    in_cp.wait_recv()
                    out_cp.wait_send()
                local.wait()

        return _SideComm(arrays, 4 * n, start, finish)
    return make


def _share_stage(jobs):
    assert len({id(j[0]) for j in jobs}) == len(jobs)

    def make(base):
        def copies(refs, send, recv):
            x, y, c = _mesh_pos()
            out = []
            for t, (_, sh, layer) in enumerate(jobs):
                mine = _half_view(sh, refs[t].at[layer], c)
                theirs = _half_view(sh, refs[t].at[layer], 1 - c)
                out.append((pltpu.make_async_remote_copy(mine, mine, send.at[base + t], recv.at[base + t],
                                                         device_id=(x, y, 1 - c), device_id_type=MESH),
                            pltpu.make_async_remote_copy(theirs, theirs, send.at[base + t], recv.at[base + t],
                                                         device_id=(x, y, 1 - c), device_id_type=MESH)))
            return out

        def start(refs, send, recv):
            for out_cp, _ in copies(refs, send, recv):
                out_cp.start()

        def finish(refs, send, recv):
            for out_cp, in_cp in copies(refs, send, recv):
                in_cp.wait_recv()
                out_cp.wait_send()

        return _SideComm([j[0] for j in jobs], len(jobs), start, finish)
    return make


def _pair_add_layer(g, sh, layer, got, c_arr):
    if sh.kind == "rows":
        R, Ch = got.shape
        tr = _tile(R, ROW_TILE)
        ins = [(g, (None, tr, Ch), lambda i, p: (layer, i, p[0])), (got, (tr, Ch), lambda i, p: (i, 0))]
        nt, g2 = R // tr, got
    else:
        C = got.shape[-1]
        g2 = got.reshape(-1, C)
        R = g2.shape[0]
        tr = _tile(R, 256 if C > 2048 else ROW_TILE)
        nt = R // tr
        ins = [(g.reshape(g.shape[0], -1, C), (None, tr, C), lambda i, p: (layer, p[0] * nt + i, 0)),
               (g2, (tr, C), lambda i, p: (i, 0))]
    (out,) = _rowwise("grad_pair_add", nt, lambda i, a, b: (a.astype(F32) + b.astype(F32),), ins,
                      [(g2.shape, got.dtype, ins[1][1], lambda i, p: (i, 0))], prefetch=c_arr)
    return out.reshape(got.shape)


def _chip_sum_layer(land, sh, layer, tot, c_arr):
    def fn(i, blk):
        v = blk.astype(F32)
        return (((v[0] + v[1]) + v[2]) + v[3],)

    if sh.kind == "rows":
        _, R, Ch = land.shape
        tr = _tile(R, ROW_TILE)
        ins = [(land, (N_CHIPS, tr, Ch), lambda i, p: (0, i, 0))]
        outs = [(tot.shape, F32, (None, tr, Ch), lambda i, p: (layer, i, p[0]))]
        (out,) = _rowwise("grad_chip_sum", R // tr, fn, ins, outs, prefetch=c_arr, into=tot)
        return out
    C = land.shape[-1]
    l3 = land.reshape(N_CHIPS, -1, C)
    R = l3.shape[1]
    tr = _tile(R, ROW_TILE)
    nt = R // tr
    t3 = tot.reshape(tot.shape[0], -1, C)
    (out,) = _rowwise("grad_chip_sum", nt, fn, [(l3, (N_CHIPS, tr, C), lambda i, p: (0, i, 0))],
                      [(t3.shape, F32, (None, tr, C), lambda i, p: (layer, p[0] * nt + i, 0))], prefetch=c_arr, into=t3)
    return out.reshape(tot.shape)


def _flat2(shape):
    return (math.prod(shape[:-1]), shape[-1])


def _adamw(w, g, m, v):
    shape = w.shape
    R, C = _flat2(shape)
    tr = _tile(R, 256 if R % 256 == 0 else R)

    def fn(i, w, g, m, v):
        m2 = ADAM_B1 * m + (1.0 - ADAM_B1) * g
        v2 = ADAM_B2 * v + (1.0 - ADAM_B2) * (g * g)
        m_hat = m2 / (1.0 - ADAM_B1 ** ADAM_STEP)
        v_hat = v2 / (1.0 - ADAM_B2 ** ADAM_STEP)
        delta = -ADAM_LR * (m_hat / (jnp.sqrt(v_hat) + ADAM_EPS) + ADAM_WD * w)
        return g, delta, m2, v2

    row = lambda i: (i, 0)
    f2 = lambda a: a.reshape(R, C)
    outs = _rowwise("adamw", R // tr, fn, [_rows(f2(a), tr) for a in (w, g, m, v)], [((R, C), F32, (tr, C), row)] * 4)
    return [o.reshape(shape) for o in outs]


def _loss_and_grad(last, target):
    xhat, g, b = last
    T, D = xhat.shape
    tr = _tile(T, ROW_TILE)

    def fn(i, xhat, t, g, b):
        err = (xhat * g + b) - t
        return err * (1.0 / D), jnp.sum(err * err, axis=0, keepdims=True)

    dh, sq = _rowwise("loss", T // tr, fn, [_rows(xhat, tr), _rows(target, tr), _whole(g), _whole(b)],
                      [((T, D), F32, (tr, D), lambda i: (i, 0))], accs=[(1, D)])
    return dh, (0.5 / D) * jnp.sum(sq)


def kernel(x, ffn1_w_in, ffn1_w_out, ffn2_w_in, ffn2_w_out, ln_g, ln_b, attn_w_qkv, attn_sinks, attn_w_o, lru_w_in, lru_conv_w, lru_conv_b, lru_w_ra, lru_b_ra, lru_w_rx, lru_b_rx, lru_lambda, lru_w_out, loss_target, m_ffn1_w_in, m_ffn1_w_out, m_ffn2_w_in, m_ffn2_w_out, m_ln_g, m_ln_b, m_attn_w_qkv, m_attn_sinks, m_attn_w_o, m_lru_w_in, m_lru_conv_w, m_lru_conv_b, m_lru_w_ra, m_lru_b_ra, m_lru_w_rx, m_lru_b_rx, m_lru_lambda, m_lru_w_out, v_ffn1_w_in, v_ffn1_w_out, v_ffn2_w_in, v_ffn2_w_out, v_ln_g, v_ln_b, v_attn_w_qkv, v_attn_sinks, v_attn_w_o, v_lru_w_in, v_lru_conv_w, v_lru_conv_b, v_lru_w_ra, v_lru_b_ra, v_lru_w_rx, v_lru_b_rx, v_lru_lambda, v_lru_w_out):
    names = ["ffn1_w_in", "ffn1_w_out", "ffn2_w_in", "ffn2_w_out", "ln_g", "ln_b", "attn_w_qkv", "attn_sinks", "attn_w_o",
             "lru_w_in", "lru_conv_w", "lru_conv_b", "lru_w_ra", "lru_b_ra", "lru_w_rx", "lru_b_rx", "lru_lambda", "lru_w_out"]
    W = dict(zip(names, [ffn1_w_in, ffn1_w_out, ffn2_w_in, ffn2_w_out, ln_g, ln_b, attn_w_qkv, attn_sinks, attn_w_o,
                         lru_w_in, lru_conv_w, lru_conv_b, lru_w_ra, lru_b_ra, lru_w_rx, lru_b_rx, lru_lambda, lru_w_out]))
    M = dict(zip(names, [m_ffn1_w_in, m_ffn1_w_out, m_ffn2_w_in, m_ffn2_w_out, m_ln_g, m_ln_b, m_attn_w_qkv, m_attn_sinks, m_attn_w_o,
                         m_lru_w_in, m_lru_conv_w, m_lru_conv_b, m_lru_w_ra, m_lru_b_ra, m_lru_w_rx, m_lru_b_rx, m_lru_lambda, m_lru_w_out]))
    V = dict(zip(names, [v_ffn1_w_in, v_ffn1_w_out, v_ffn2_w_in, v_ffn2_w_out, v_ln_g, v_ln_b, v_attn_w_qkv, v_attn_sinks, v_attn_w_o,
                         v_lru_w_in, v_lru_conv_w, v_lru_conv_b, v_lru_w_ra, v_lru_b_ra, v_lru_w_rx, v_lru_b_rx, v_lru_lambda, v_lru_w_out]))

    T, D = x.shape[1], x.shape[2]
    L = ffn1_w_in.shape[0]
    LA, LR = attn_w_qkv.shape[0], lru_w_in.shape[0]
    N2 = ffn1_w_in.shape[2] * N_CHIPS
    F = N2 // 2
    C = lru_lambda.shape[1] * N_CHIPS
    CW = C // N_CHIPS
    alpha = (2.0 * L) ** 0.25
    c_arr = lax.axis_index("c").astype(jnp.int32).reshape(1)

    n_sink = attn_sinks.size
    assert n_sink <= CW

    up8 = lambda n: -(-n // 8) * 8
    o_g = 0
    o_b = o_g + up8(3 * L)
    o_cw = o_b + up8(3 * L)
    o_cb = o_cw + up8(LR * CONV_W)
    o_ra = o_cb + up8(LR)
    o_rx = o_ra + up8(LR)
    o_lam = o_rx + up8(LR)
    o_sink = o_lam + up8(LR)
    assert o_sink + 8 <= SMALL_ROWS

    def pack_small(d):
        parts = [d["ln_g"].reshape(-1, CW), d["ln_b"].reshape(-1, CW), d["lru_conv_w"].reshape(-1, CW), d["lru_conv_b"],
                 d["lru_b_ra"], d["lru_b_rx"], d["lru_lambda"],
                 jnp.pad(d["attn_sinks"].reshape(1, -1), ((0, 0), (0, CW - n_sink)))]
        parts = [jnp.pad(p, ((0, up8(p.shape[0]) - p.shape[0]), (0, 0))) for p in parts]
        used = sum(p.shape[0] for p in parts)
        return jnp.concatenate(parts + [jnp.zeros((SMALL_ROWS - used, CW), F32)], axis=0)

    cols = lambda a: _Sharded("cols", a.shape[-1])
    rows_ = lambda a: _Sharded("rows", a.shape[-2])
    RW = lru_w_ra.shape[2]
    sh_list = [_Sharded("perm", N2 // 4), rows_(ffn1_w_out), _Sharded("perm", N2 // 4), rows_(ffn2_w_out),
               cols(attn_w_qkv), rows_(attn_w_o), cols(lru_w_in), rows_(lru_w_out)]
    big = [ffn1_w_in, ffn1_w_out, ffn2_w_in, ffn2_w_out, attn_w_qkv, attn_w_o, lru_w_in, lru_w_out]
    blk_w = lru_w_ra.shape[3]
    small_sh = _Sharded("cols", CW)
    full = lambda a, sh: tuple(a.shape[:-1]) + (a.shape[-1] * N_CHIPS,) if sh.kind in ("cols", "perm") else \
        tuple(a.shape[:-2]) + (a.shape[-2] * N_CHIPS, a.shape[-1])
    mx, my = lax.axis_index("x"), lax.axis_index("y")
    pos = jnp.stack([2 * mx + my, 2 * my + mx]).astype(jnp.int32)
    placed = [_place(a, full(a, sh), BF16, sh, pos) for a, sh in zip(big, sh_list)]
    rarx_shape = (LR, RNN_BLOCKS, RW * N_CHIPS, 2 * blk_w)
    p_ra = _place(lru_w_ra, rarx_shape, BF16, _Sharded("rarx", RW, 0, blk_w), pos)
    placed.append(_place(lru_w_rx, rarx_shape, BF16, _Sharded("rarx", RW, blk_w, blk_w), pos, prev=p_ra))
    gnames = ["w_in1", "w_out1", "w_in2", "w_out2", "qkv", "wo", "lin", "lout", "rarx"]
    GW = dict(zip(gnames, placed))
    SH = dict(zip(gnames, sh_list + [_Sharded("rarx", RW, 0, 2 * blk_w)]))
    p_small = _place(pack_small(W).reshape(1, SMALL_ROWS, CW), (1, SMALL_ROWS, C), F32, small_sh, pos)

    def comm_for(stage, jobs):
        return stage([(GW[n], SH[n], l) for n, l in jobs]) if jobs else None

    def rebind(jobs, arrays):
        for (n, _), a in zip(jobs, arrays):
            GW[n] = a

    jobs0 = [(n, 0) for n in ("w_in1", "w_out1", "w_in2", "w_out2", "qkv", "wo")]
    got0 = _all_gather([(GW[n], SH[n], l) for n, l in jobs0] + [(p_small, small_sh, 0)])
    rebind(jobs0, got0)
    small = got0[-1].reshape(SMALL_ROWS, C)
    row_of = lambda r: small[r:r + 1]
    mixer_jobs = lambda l: [("qkv", l // 2), ("wo", l // 2)] if l % 2 == 0 else [("lin", l // 2), ("lout", l // 2), ("rarx", l // 2)]

    def ffn_forward(prev, hb, n_in, n_out, layer, g, b, jobs):
        gu, act, carried = _ffn_up(hb, GW[n_in], layer, comm_for(_gather_over_ici, jobs))
        rebind(jobs, carried)
        res = _proj_ln("ffn_down_ln", act, GW[n_out], layer, prev, g, b, alpha, 0.5, comm_for(_pass_to_sibling, jobs))
        if jobs:
            res, carried = res
            rebind(jobs, carried)
        return res, (gu, act)
    assert D == C, "packed small parameters assume d_model == d_rnn"

    c128, s128 = _rope_tables(T)
    sink_rows = [attn_sinks[j:j + 1] for j in range(LA)]

    x2 = x.reshape(T, D)
    prev = (x2, jnp.ones((1, D), F32), jnp.zeros((1, D), F32))
    hb = x2.astype(BF16)
    saved = []
    for i in range(L):
        j = i // 2
        lay = {}
        lay["hb0"] = hb
        nxt = i + 1
        gains = [row_of(o_g + 3 * i + k) for k in range(3)]
        biases = [row_of(o_b + 3 * i + k) for k in range(3)]
        jobs1 = [("w_in2", nxt), ("w_out2", nxt)] if nxt < L else []
        jobs2 = [("w_in1", nxt), ("w_out1", nxt)] + mixer_jobs(nxt) if nxt < L else []
        (xh, hb, rs), (lay["gu1"], lay["act1"]) = ffn_forward(prev, hb, "w_in1", "w_out1", i, gains[0], biases[0], jobs1)
        lay["ln1"], lay["hb1"], prev = (xh, rs), hb, (xh, gains[0], biases[0])
        if i % 2 == 0:
            qkv = _qkv_rope(hb, GW["qkv"], j, c128, s128)
            o = _attn_fwd(qkv, sink_rows[j])
            lay["qkv"], lay["o"] = qkv, o
            xh, hb, rs = _proj_ln("attn_out_ln", o, GW["wo"], j, prev, gains[1], biases[1], alpha, 1.0)
        else:
            bm = _tile(T, ROW_TILE)
            (xg,) = _matmul("lru_in", (T // bm, 1, 1), (hb, (bm, D), lambda p, q, k: (p, 0)),
                            (GW["lin"], (None, D, 2 * C), lambda p, q, k, j=j: (j, 0, 0)),
                            [((T, 2 * C), F32, (bm, 2 * C), lambda p, q: (p, 0))], (bm, 2 * C))
            cw = [row_of(o_cw + j * CONV_W + k) for k in range(CONV_W)]
            xc, xcb = _conv_fwd(xg, cw, row_of(o_cb + j))
            pre, a, b = _lru_gate_fwd(xc, xcb, GW["rarx"], j, row_of(o_ra + j), row_of(o_rx + j), row_of(o_lam + j))
            hs = _scan_fwd(a, b)
            y = _lru_out_fwd(hs, xg)
            lay.update(xg=xg, xc=xc, xcb=xcb, pre=pre, a=a, hs=hs, y=y, cw=cw)
            xh, hb, rs = _proj_ln("lru_out_ln", y, GW["lout"], j, prev, gains[1], biases[1], alpha, 1.0)
        lay["ln2"], lay["hb2"], prev = (xh, rs), hb, (xh, gains[1], biases[1])
        (xh, hb, rs), (lay["gu2"], lay["act2"]) = ffn_forward(prev, hb, "w_in2", "w_out2", i, gains[2], biases[2], jobs2)
        lay["ln3"], prev = (xh, rs), (xh, gains[2], biases[2])
        saved.append(lay)
    g_w_in1, g_w_out1, g_w_in2, g_w_out2, g_qkv, g_wo, g_lin, g_lout, g_rarx = [GW[n] for n in gnames]

    dh, loss_local = _loss_and_grad(prev, loss_target.reshape(T, D))
    loss = lax.psum(loss_local, ("x", "y", "c"))

    SH2 = dict(SH, small=small_sh)
    shard_stack = {"w_in1": ffn1_w_in.shape, "w_out1": ffn1_w_out.shape, "w_in2": ffn2_w_in.shape, "w_out2": ffn2_w_out.shape,
                   "qkv": attn_w_qkv.shape, "wo": attn_w_o.shape, "lin": lru_w_in.shape, "lout": lru_w_out.shape,
                   "rarx": (LR, RNN_BLOCKS, RW, 2 * blk_w), "small": (1, SMALL_ROWS, CW)}
    TOT = {n: lax.empty(s, F32) for n, s in shard_stack.items()}
    pend_exchange, pend_share = [], []

    ffn_names = ("w_in1", "w_out1", "w_in2", "w_out2")

    def take_exchange(pred):
        jobs = [job for job in pend_exchange if pred(job[0])]
        pend_exchange[:] = [job for job in pend_exchange if not pred(job[0])]
        return jobs

    def swap_share_comm(swap_jobs, exchange_too=None):
        share_jobs = list(pend_share)
        pend_share.clear()
        ex_jobs = take_exchange(exchange_too) if exchange_too is not None else []
        makers = []
        if swap_jobs:
            makers.append(_swap_stage([(g, SH2[n], 0) for n, _, g in swap_jobs]))
        if share_jobs:
            makers.append(_share_stage([(TOT[n], SH2[n], l) for n, l in share_jobs]))
        if ex_jobs:
            makers.append(_exchange_stage([(part, SH2[n]) for n, _, part in ex_jobs]))
        comm, _ = _combine(makers)

        def after(carried):
            for t, (n, l, _) in enumerate(swap_jobs):
                part = _pair_add_layer(carried[2 * t], SH2[n], 0, carried[2 * t + 1], c_arr)
                pend_exchange.append((n, l, part))
            k = 2 * len(swap_jobs)
            for t, (n, _) in enumerate(share_jobs):
                TOT[n] = carried[k + t]
            k += len(share_jobs)
            for t, (n, l, _) in enumerate(ex_jobs):
                TOT[n] = _chip_sum_layer(carried[k + 2 * t + 1], SH2[n], l, TOT[n], c_arr)
                pend_share.append((n, l))
        return comm, after

    def exchange_comm(pred=lambda n: True):
        jobs = take_exchange(pred)
        if not jobs:
            return None, None
        comm, _ = _combine([_exchange_stage([(part, SH2[n]) for n, _, part in jobs])])

        def after(carried):
            for t, (n, l, _) in enumerate(jobs):
                TOT[n] = _chip_sum_layer(carried[2 * t + 1], SH2[n], l, TOT[n], c_arr)
                pend_share.append((n, l))
        return comm, after

    def ffn_backward(dh, ln, g, hb_in, gu, act, n_in, n_out, layer, also_swap):
        dz, dyb, dg, db = _ln_bwd("ffn_ln_bwd", dh, ln[0], ln[1], g, 0.5)
        dgu = _ffn_dact(dyb, GW[n_out], layer, gu)
        g_out = _grad_tn("ffn_dwout", act, dyb, N2 // 4, D // 2, bk=8 * ROW_TILE)
        comm, after = exchange_comm(lambda n: n in ffn_names)
        g_in = _grad_tn("ffn_dwin", hb_in, dgu, D // 2, N2 // 4, bk=8 * ROW_TILE, comm=comm)
        if comm is not None:
            g_in, carried = g_in
            after(carried)
        comm, after = swap_share_comm([(n_in, layer, g_in), (n_out, layer, g_out)] + also_swap,
                                      exchange_too=lambda n: n not in ffn_names)
        dh_prev, carried = _ffn_dx(dgu, GW[n_in], layer, dz, alpha, comm)
        after(carried)
        return dh_prev, dg, db

    sg = [None] * SMALL_ROWS
    d_sinks = [None] * LA
    for i in reversed(range(L)):
        j = i // 2
        lay = saved[i]
        dh, sg[o_g + 3 * i + 2], sg[o_b + 3 * i + 2] = ffn_backward(
            dh, lay["ln3"], row_of(o_g + 3 * i + 2), lay["hb2"], lay["gu2"], lay["act2"], "w_in2", "w_out2", i, [])
        if i % 2 == 0:
            dz, dmb, sg[o_g + 3 * i + 1], sg[o_b + 3 * i + 1] = _ln_bwd("attn_ln_bwd", dh, *lay["ln2"], row_of(o_g + 3 * i + 1), 1.0)
            d_wo = _grad_tn("attn_dwo", lay["o"], dmb, _tile(Q_COLS, 1024), D)
            do = _back_proj("attn_do", dmb, g_wo, j, BF16)
            dq, dkv, dsk = _attn_bwd(lay["qkv"], do, sink_rows[j])
            d_sinks[j] = jnp.sum(dsk.reshape(N_HEADS, ATTN_BLOCK), axis=1)
            dqkv = _rope_bwd(dq, dkv, c128, s128)
            d_qkv = _grad_tn("attn_dwqkv", lay["hb1"], dqkv, D, dqkv.shape[1])
            comm, after = swap_share_comm([("wo", j, d_wo), ("qkv", j, d_qkv)])
            dh, carried = _input_grad("attn_dx", dqkv, g_qkv, j, dz, alpha, comm)
            after(carried)
        else:
            dz, dmb, sg[o_g + 3 * i + 1], sg[o_b + 3 * i + 1] = _ln_bwd("lru_ln_bwd", dh, *lay["ln2"], row_of(o_g + 3 * i + 1), 1.0)
            d_lout = _grad_tn("lru_dwout", lay["y"], dmb, C, D)
            dy = _back_proj("lru_dy", dmb, g_lout, j, F32)
            dhs, dgb = _lru_out_bwd(dy, lay["hs"], lay["xg"])
            adj = _scan_bwd(lay["a"], dhs)
            dpre, dxc_direct, sg[o_ra + j], sg[o_rx + j], sg[o_lam + j] = _lru_gate_bwd(
                lay["pre"], lay["xc"], adj, lay["hs"], row_of(o_ra + j), row_of(o_rx + j), row_of(o_lam + j))
            blk = C // RNN_BLOCKS
            d_rarx = _grad_tn("lru_dwgates", lay["xcb"], dpre, blk, 2 * blk, block_diag=True)
            bm = _tile(T, ROW_TILE)
            (dxc,) = _matmul("lru_dxc", (T // bm, RNN_BLOCKS, 1), (dpre, (bm, 2 * blk), lambda p, q, k: (p, q)),
                             (g_rarx, (None, None, blk, 2 * blk), lambda p, q, k, j=j: (j, q, 0, 0)),
                             [((T, C), F32, (bm, blk), lambda p, q: (p, q))], (bm, blk), tb=True,
                             extras=[(dxc_direct, (bm, blk), lambda p, q: (p, q))], epilogue=lambda acc, d: (acc + d,))
            res = _conv_bwd(dxc, lay["xg"], dgb, lay["cw"])
            dxg = res[0]
            for k in range(CONV_W):
                sg[o_cw + j * CONV_W + k] = res[1 + k]
            sg[o_cb + j] = res[1 + CONV_W]
            d_lin = _grad_tn("lru_dwin", lay["hb1"], dxg, D, _tile(2 * C, 1024))
            comm, after = swap_share_comm([("lout", j, d_lout), ("rarx", j, d_rarx), ("lin", j, d_lin)])
            dh, carried = _input_grad("lru_dx", dxg, g_lin, j, dz, alpha, comm)
            after(carried)
        dh, sg[o_g + 3 * i], sg[o_b + 3 * i] = ffn_backward(
            dh, lay["ln1"], row_of(o_g + 3 * i), lay["hb0"], lay["gu1"], lay["act1"], "w_in1", "w_out1", i, [])
    grad_x = dh.reshape(x.shape)

    sink_vec = jnp.concatenate(d_sinks).reshape(1, n_sink)
    sg[o_sink] = jnp.tile(jnp.concatenate([sink_vec, jnp.zeros((1, CW - n_sink), F32)], axis=1), (1, N_CHIPS))
    zero_row = jnp.zeros((1, C), F32)
    d_small = jnp.concatenate([zero_row if r is None else r for r in sg], axis=0).reshape(1, SMALL_ROWS, C)

    comm, after = swap_share_comm([("small", 0, d_small)])
    after(_comm_call("grad_tail_swap", comm))
    comm, after = exchange_comm()
    after(_comm_call("grad_tail_exchange", comm))
    comm, after = swap_share_comm([])
    after(_comm_call("grad_tail_share", comm))
    t_w_in1, t_w_out1, t_w_in2, t_w_out2, t_qkv, t_wo, t_lin, t_lout, t_rarx = [TOT[n] for n in gnames]
    t_small = TOT["small"].reshape(SMALL_ROWS, CW)

    G = {"ffn1_w_in": t_w_in1, "ffn1_w_out": t_w_out1, "ffn2_w_in": t_w_in2, "ffn2_w_out": t_w_out2,
         "attn_w_qkv": t_qkv, "attn_w_o": t_wo, "lru_w_in": t_lin, "lru_w_out": t_lout,
         "lru_w_ra": t_rarx[..., :blk_w], "lru_w_rx": t_rarx[..., blk_w:]}

    def unpack_small(p):
        return {"ln_g": p[o_g:o_g + 3 * L].reshape(ln_g.shape), "ln_b": p[o_b:o_b + 3 * L].reshape(ln_b.shape),
                "lru_conv_w": p[o_cw:o_cw + LR * CONV_W].reshape(lru_conv_w.shape), "lru_conv_b": p[o_cb:o_cb + LR],
                "lru_b_ra": p[o_ra:o_ra + LR], "lru_b_rx": p[o_rx:o_rx + LR], "lru_lambda": p[o_lam:o_lam + LR],
                "attn_sinks": p[o_sink, :n_sink].reshape(attn_sinks.shape)}

    G.update(unpack_small(t_small))

    delta, new_m, new_v = {}, {}, {}
    small_names = ["ln_g", "ln_b", "lru_conv_w", "lru_conv_b", "lru_b_ra", "lru_b_rx", "lru_lambda", "attn_sinks"]
    for n in names:
        if n not in small_names:
            G[n], delta[n], new_m[n], new_v[n] = _adamw(W[n], G[n], M[n], V[n])
    _, ds, ms, vs = _adamw(pack_small(W), t_small, pack_small(M), pack_small(V))
    for d, p in ((delta, ds), (new_m, ms), (new_v, vs)):
        d.update(unpack_small(p))

    return (loss, grad_x, *[G[n] for n in names], *[delta[n] for n in names], *[new_m[n] for n in names], *[new_v[n] for n in names])
```

```python
import functools
import math

import jax
import jax.numpy as jnp
from jax import lax
from jax.experimental import pallas as pl
from jax.experimental.pallas import tpu as pltpu

F32 = jnp.float32
BF16 = jnp.bfloat16
MESH = pl.DeviceIdType.MESH

N_HEADS = 16
N_KV_HEADS = 4
HEAD_DIM = 64
GROUP = N_HEADS // N_KV_HEADS
ATTN_BLOCK = 128
ROPE_THETA = 10000.0
RNN_BLOCKS = 4
CONV_W = 4
LRU_C = 8.0
LN_EPS = 1e-5
ADAM_LR = 0.001
ADAM_B1 = 0.9
ADAM_B2 = 0.999
ADAM_EPS = 1e-08
ADAM_WD = 0.01
ADAM_STEP = 10
N_CHIPS = 4
NEG_BIG = -1e30
VMEM_LIMIT_MB = 56
ROW_TILE = 512
SMALL_ROWS = 96


def _cparams(sem):
    return pltpu.CompilerParams(dimension_semantics=sem, vmem_limit_bytes=VMEM_LIMIT_MB << 20)


def _tile(n, pref):
    if n <= pref:
        return n
    for t in range(pref - pref % 16, 0, -16):
        if n % t == 0:
            return t
    raise ValueError((n, pref))


class _SideComm:
    def __init__(self, arrays, n_sems, start, finish):
        self.arrays, self.n_sems, self.start, self.finish = list(arrays), n_sems, start, finish


def _hosted_call(body, comm, *, name, grid, in_specs, out_specs, out_shape, operands, scratch_shapes=(),
                 input_output_aliases=None, compiler_params=None):
    aliases = dict(input_output_aliases or {})
    in_specs, out_specs, out_shape = list(in_specs), list(out_specs), list(out_shape)
    operands, scratch_shapes = list(operands), list(scratch_shapes)
    n_in, n_out = len(in_specs), len(out_specs)
    if comm is None:
        res = pl.pallas_call(body, name=name, grid=grid, in_specs=in_specs, out_specs=out_specs, out_shape=out_shape,
                             scratch_shapes=scratch_shapes, input_output_aliases=aliases,
                             compiler_params=compiler_params)(*operands)
        return list(res), []
    m = len(comm.arrays)
    hbm = pl.BlockSpec(memory_space=pl.ANY)
    real = [t for t, arr in enumerate(comm.arrays) if not isinstance(arr, jax.ShapeDtypeStruct)]
    mi = len(real)
    for ti, t in enumerate(real):
        aliases[n_in + ti] = n_out + t
    n_scr = len(scratch_shapes)

    def hosted(*refs):
        ins = refs[:n_in]
        outs = refs[n_in + mi:n_in + mi + n_out]
        carried = refs[n_in + mi + n_out:n_in + mi + n_out + m]
        scr = refs[n_in + mi + n_out + m:n_in + mi + n_out + m + n_scr]
        send, recv = refs[-2:]
        first = functools.reduce(jnp.logical_and, [pl.program_id(d) == 0 for d in range(len(grid))])
        last = functools.reduce(jnp.logical_and, [pl.program_id(d) == grid[d] - 1 for d in range(len(grid))])

        @pl.when(first)
        def _():
            comm.start(carried, send, recv)

        body(*ins, *outs, *scr)

        @pl.when(last)
        def _():
            comm.finish(carried, send, recv)

    res = pl.pallas_call(
        hosted, name=name, grid=grid, in_specs=in_specs + [hbm] * mi, out_specs=out_specs + [hbm] * m,
        out_shape=out_shape + [jax.ShapeDtypeStruct(a.shape, a.dtype) for a in comm.arrays],
        scratch_shapes=scratch_shapes + [pltpu.SemaphoreType.DMA((comm.n_sems,)), pltpu.SemaphoreType.DMA((comm.n_sems,))],
        input_output_aliases=aliases, compiler_params=compiler_params)(*operands, *[comm.arrays[t] for t in real])
    return list(res[:n_out]), list(res[n_out:])


def _combine(makers):
    comms, base = [], 0
    for mk in makers:
        comms.append(mk(base))
        base += comms[-1].n_sems
    offs = [0]
    for cm in comms:
        offs.append(offs[-1] + len(cm.arrays))

    def start(refs, send, recv):
        for cm, o in zip(comms, offs):
            cm.start(refs[o:o + len(cm.arrays)], send, recv)

    def finish(refs, send, recv):
        for cm, o in zip(comms, offs):
            cm.finish(refs[o:o + len(cm.arrays)], send, recv)

    return _SideComm(sum((cm.arrays for cm in comms), []), base, start, finish), [len(cm.arrays) for cm in comms]


def _comm_call(name, comm):
    def body():
        pass

    _, carried = _hosted_call(body, comm, name=name, grid=(1,), in_specs=[], out_specs=[], out_shape=[], operands=[])
    return carried


def _matmul(name, grid, a, b, outs, acc_shape, *, ta=False, tb=False, extras=(), epilogue=None,
            n_outer=False, alias_in=None, comm=None):
    gm, gn, gk = grid
    if n_outer:
        g = (gn, gm, gk)
        ijk = lambda p, q, k: (q, p, k)
    else:
        g = (gm, gn, gk)
        ijk = lambda p, q, k: (p, q, k)
    w3 = lambda f: (lambda p, q, k: f(*ijk(p, q, k)))
    w2 = lambda f: (lambda p, q, k: f(*ijk(p, q, k)[:2]))
    in_specs = [pl.BlockSpec(a[1], w3(a[2])), pl.BlockSpec(b[1], w3(b[2]), pipeline_mode=b[3] if len(b) > 3 else None)]
    in_specs += [pl.BlockSpec(e[1], w2(e[2])) for e in extras]
    operands = [a[0], b[0]] + [e[0] for e in extras]
    io_alias = {}
    n_alias = 0
    if alias_in is not None:
        in_specs.append(pl.BlockSpec(memory_space=pl.ANY))
        operands.append(alias_in)
        io_alias = {len(operands) - 1: 0}
        n_alias = 1
    ne, no = len(extras), len(outs)
    dims = (((0 if ta else 1,), (1 if tb else 0,)), ((), ()))

    def body(*refs):
        a_ref, b_ref = refs[0], refs[1]
        e_refs = refs[2:2 + ne]
        o_refs = refs[2 + ne + n_alias:2 + ne + n_alias + no]
        part = lax.dot_general(a_ref[...], b_ref[...], dims, preferred_element_type=F32)

        def finish(acc):
            res = epilogue(acc, *[r[...] for r in e_refs]) if epilogue is not None else (acc,)
            for r, v in zip(o_refs, res):
                r[...] = v.astype(r.dtype)

        if gk == 1:
            finish(part)
        else:
            acc_ref = refs[-1]
            k = pl.program_id(2)

            @pl.when(k == 0)
            def _():
                acc_ref[...] = part

            @pl.when(k > 0)
            def _():
                acc_ref[...] += part

            @pl.when(k == gk - 1)
            def _():
                finish(acc_ref[...])

    res, carried = _hosted_call(
        body, comm, name=name, grid=g, in_specs=in_specs,
        out_specs=[pl.BlockSpec(o[2], w2(o[3])) for o in outs],
        out_shape=[jax.ShapeDtypeStruct(o[0], o[1]) for o in outs],
        operands=operands,
        scratch_shapes=[pltpu.VMEM(acc_shape, F32)] if gk > 1 else [],
        input_output_aliases=io_alias,
        compiler_params=_cparams(("arbitrary", "arbitrary", "arbitrary")))
    return res if comm is None else (res, carried)


def _rowwise(name, nsteps, fn, ins, outs, accs=(), prefetch=None, into=None):
    n_in, n_out, n_acc = len(ins), len(outs), len(accs)
    n_pre = 0 if prefetch is None else 1
    n_into = 0 if into is None else 1

    def body(*refs):
        refs = refs[n_pre:]
        i = pl.program_id(0)
        res = fn(i, *[r[...] for r in refs[:n_in]])
        refs = refs[:n_in] + refs[n_in + n_into:]
        for r, v in zip(refs[n_in:n_in + n_out], res[:n_out]):
            r[...] = v.astype(r.dtype)
        acc_refs = refs[n_in + n_out:n_in + n_out + n_acc]
        if n_acc:
            @pl.when(i == 0)
            def _():
                for r in acc_refs:
                    r[...] = jnp.zeros_like(r)

            for r, v in zip(acc_refs, res[n_out:]):
                r[...] += v

    if prefetch is None:
        zero = lambda shape: (lambda i: (0,) * len(shape))
    else:
        zero = lambda shape: (lambda i, p: (0,) * len(shape))
    in_specs = [pl.BlockSpec(b, m) for _, b, m in ins]
    operands = [x[0] for x in ins]
    alias = {}
    if into is not None:
        in_specs.append(pl.BlockSpec(memory_space=pl.ANY))
        operands.append(into)
        alias = {n_pre + n_in: 0}
    out_specs = [pl.BlockSpec(o[2], o[3]) for o in outs] + [pl.BlockSpec(s, zero(s)) for s in accs]
    out_shape = [jax.ShapeDtypeStruct(o[0], o[1]) for o in outs] + [jax.ShapeDtypeStruct(s, F32) for s in accs]
    cp = _cparams(("arbitrary",))
    if prefetch is None:
        call = pl.pallas_call(body, name=name, grid=(nsteps,), in_specs=in_specs, out_specs=out_specs,
                              out_shape=out_shape, input_output_aliases=alias, compiler_params=cp)
        return call(*operands)
    gs = pltpu.PrefetchScalarGridSpec(num_scalar_prefetch=1, grid=(nsteps,), in_specs=in_specs, out_specs=out_specs)
    call = pl.pallas_call(body, name=name, grid_spec=gs, out_shape=out_shape, input_output_aliases=alias, compiler_params=cp)
    return call(prefetch, *operands)


def _rows(arr, tr, cols=None, cb=0):
    cols = arr.shape[1] if cols is None else cols
    return (arr, (tr, cols), lambda i: (i, cb))


def _whole(arr):
    return (arr, arr.shape, lambda i: (0,) * arr.ndim)


def _gelu_tanh(x):
    c = math.sqrt(2.0 / math.pi)
    return x * (0.5 * (1.0 + jnp.tanh(c * (x + 0.044715 * (x * x * x)))))


@jax.custom_jvp
def _expm1(x):
    return jnp.where(jnp.abs(x) < 0.5, jnp.tanh(0.5 * x) * (jnp.exp(x) + 1.0), jnp.exp(x) - 1.0)


@_expm1.defjvp
def _expm1_jvp(primals, tangents):
    (x,), (t,) = primals, tangents
    return _expm1(x), jnp.exp(x) * t


def _log_sigmoid(x):
    return jnp.minimum(x, 0.0) - jnp.log1p(jnp.exp(-jnp.abs(x)))


def _lru_gates(pre, xc, b_ra, b_rx, lam):
    w = xc.shape[-1]
    r = jax.nn.sigmoid(pre[:, :w] + b_ra)
    ig = jax.nn.sigmoid(pre[:, w:] + b_rx)
    log_a = LRU_C * r * _log_sigmoid(lam)
    a = jnp.exp(log_a)
    b = jnp.sqrt(-_expm1(2.0 * log_a)) * (ig * xc)
    return a, b


def _swap_halves(x):
    n = x.shape[1]
    first = (lax.broadcasted_iota(jnp.int32, x.shape, 1) % HEAD_DIM) < (HEAD_DIM // 2)
    return jnp.where(first, pltpu.roll(x, n - HEAD_DIM // 2, 1), pltpu.roll(x, HEAD_DIM // 2, 1))


def _shift_down(prev8, cur, s):
    ext = jnp.concatenate([prev8, cur], axis=0)
    return pltpu.roll(ext, s, 0)[8:]


def _shift_up(cur, next8, s):
    ext = jnp.concatenate([cur, next8], axis=0)
    return pltpu.roll(ext, ext.shape[0] - s, 0)[:cur.shape[0]]


LANES = 128


def _ln_epilogue(alpha, scale):
    def epi(acc, prev, gp, bp, g, b):
        z = alpha * (prev * gp + bp) + scale * acc
        mu = jnp.mean(z, axis=-1, keepdims=True)
        xc = z - mu
        var = jnp.mean(xc * xc, axis=-1, keepdims=True)
        rstd = lax.rsqrt(var + LN_EPS)
        xhat = xc * rstd
        return xhat, xhat * g + b, jnp.broadcast_to(rstd, (rstd.shape[0], LANES))
    return epi


def _proj_ln(name, act, w, layer, prev, g, b, alpha, scale, comm=None):
    T, K = act.shape
    D = w.shape[2]
    bm = _tile(T, 2 * ROW_TILE)
    row = lambda i, j: (i, 0)
    par = lambda i, j: (0, 0)
    return _matmul(
        name, (T // bm, 1, 1),
        (act, (bm, K), lambda i, j, k: (i, 0)), (w, (None, K, D), lambda i, j, k: (layer, 0, 0)),
        [((T, D), F32, (bm, D), row), ((T, D), BF16, (bm, D), row), ((T, LANES), F32, (bm, LANES), row)],
        (bm, D),
        extras=[(prev[0], (bm, D), row), (prev[1], (1, D), par), (prev[2], (1, D), par), (g, (1, D), par), (b, (1, D), par)],
        epilogue=_ln_epilogue(alpha, scale), comm=comm)


def _ln_bwd(name, dh, xhat, rstd, g, scale):
    T, D = xhat.shape
    tr = _tile(T, ROW_TILE)

    def fn(i, dh, xhat, rstd, g):
        dxh = dh * g
        rs = jnp.tile(rstd, (1, D // LANES))
        dz = rs * (dxh - jnp.mean(dxh, axis=-1, keepdims=True) - xhat * jnp.mean(dxh * xhat, axis=-1, keepdims=True))
        return (dz, scale * dz, jnp.sum(dh * xhat, axis=0, keepdims=True), jnp.sum(dh, axis=0, keepdims=True))

    row = lambda i: (i, 0)
    return _rowwise(name, T // tr, fn, [_rows(dh, tr), _rows(xhat, tr), _rows(rstd, tr), _whole(g)],
                    [((T, D), F32, (tr, D), row), ((T, D), BF16, (tr, D), row)], accs=[(1, D), (1, D)])


def _grad_tn(name, a, b, bm, bn, bk=4 * ROW_TILE, block_diag=False, comm=None):
    T, M = a.shape
    N = b.shape[1]
    bk = _tile(T, bk)
    if not block_diag:
        shape, oblk, omap = (1, M, N), (None, bm, bn), (lambda i, j: (0, i, j))
        amap, gm = (lambda i, j, k: (k, i)), M // bm
    else:
        shape, oblk, omap = (1, N // bn, bm, bn), (None, None, bm, bn), (lambda i, j: (0, j, 0, 0))
        amap, gm = (lambda i, j, k: (k, j)), 1
    res = _matmul(name, (gm, N // bn, T // bk), (a, (bk, bm), amap), (b, (bk, bn), lambda i, j, k: (k, j)),
                  [(shape, BF16, oblk, omap)], (bm, bn), ta=True, comm=comm)
    return res[0] if comm is None else (res[0][0], res[1])


MXU_COLS = 256


def _col_chunks(width):
    return [(s, min(MXU_COLS, width - s)) for s in range(0, width, MXU_COLS)]


def _ffn_up(hb, w_in, layer, comm=None):
    T, D = hb.shape
    N2 = w_in.shape[2]
    wd = N2 // 4
    bm = _tile(T, 2 * ROW_TILE)

    def body(a_ref, w_ref, gu_ref, act_ref):
        a = a_ref[...]
        for s, n in _col_chunks(wd):
            gg = jnp.dot(a, w_ref[:, s:s + n], preferred_element_type=F32)
            uu = jnp.dot(a, w_ref[:, wd + s:wd + s + n], preferred_element_type=F32)
            sg = jax.nn.sigmoid(gg)
            silu = gg * sg
            gu_ref[:, s:s + n] = (uu * (sg + silu * (1.0 - sg))).astype(gu_ref.dtype)
            gu_ref[:, wd + s:wd + s + n] = silu.astype(gu_ref.dtype)
            act_ref[:, s:s + n] = (silu * uu).astype(act_ref.dtype)

    (gu, act), carried = _hosted_call(
        body, comm, name="ffn_up", grid=(2, T // bm),
        in_specs=[pl.BlockSpec((bm, D), lambda j, i: (i, 0)), pl.BlockSpec((None, D, 2 * wd), lambda j, i: (layer, 0, j))],
        out_specs=[pl.BlockSpec((bm, 2 * wd), lambda j, i: (i, j)), pl.BlockSpec((bm, wd), lambda j, i: (i, j))],
        out_shape=[jax.ShapeDtypeStruct((T, N2), BF16), jax.ShapeDtypeStruct((T, N2 // 2), BF16)],
        operands=[hb, w_in], compiler_params=_cparams(("arbitrary", "arbitrary")))
    return gu, act, carried


def _ffn_dact(dyb, w_out, layer, gu):
    T, D = dyb.shape
    N2 = gu.shape[1]
    wd = N2 // 4
    bm = _tile(T, 2 * ROW_TILE)

    def body(dy_ref, w_ref, gu_ref, o_ref):
        dy = dy_ref[...]
        for s, n in _col_chunks(wd):
            dact = lax.dot_general(dy, w_ref[s:s + n, :], (((1,), (1,)), ((), ())), preferred_element_type=F32)
            o_ref[:, s:s + n] = (dact * gu_ref[:, s:s + n].astype(F32)).astype(o_ref.dtype)
            o_ref[:, wd + s:wd + s + n] = (dact * gu_ref[:, wd + s:wd + s + n].astype(F32)).astype(o_ref.dtype)

    return pl.pallas_call(
        body, name="ffn_dact", grid=(2, T // bm),
        in_specs=[pl.BlockSpec((bm, D), lambda j, i: (i, 0)), pl.BlockSpec((None, wd, D), lambda j, i: (layer, j, 0)),
                  pl.BlockSpec((bm, 2 * wd), lambda j, i: (i, j))],
        out_specs=pl.BlockSpec((bm, 2 * wd), lambda j, i: (i, j)),
        out_shape=jax.ShapeDtypeStruct((T, N2), BF16),
        compiler_params=_cparams(("arbitrary", "arbitrary")))(dyb, w_out, gu)


def _ffn_dx(dgu, w_in, layer, dz, alpha, comm=None):
    T, N2 = dgu.shape
    D = w_in.shape[1]
    bm = _tile(T, 2 * ROW_TILE)
    res = _matmul(
        "ffn_dx", (T // bm, 1, 1),
        (dgu, (bm, N2), lambda i, j, k: (i, 0)), (w_in, (None, D, N2), lambda i, j, k: (layer, 0, 0), pl.Buffered(1)),
        [((T, D), F32, (bm, D), lambda i, j: (i, 0))], (bm, D), tb=True,
        extras=[(dz, (bm, D), lambda i, j: (i, 0))], epilogue=lambda acc, dzb: (alpha * dzb + acc,), comm=comm)
    return res[0] if comm is None else (res[0][0], res[1])


def _input_grad(name, dy, w, layer, dz, alpha, comm):
    T, N = dy.shape
    D = w.shape[1]
    bm = _tile(T, ROW_TILE)
    (out,), carried = _matmul(
        name, (T // bm, 1, 1),
        (dy, (bm, N), lambda i, j, k: (i, 0)), (w, (None, D, N), lambda i, j, k: (layer, 0, 0)),
        [((T, D), F32, (bm, D), lambda i, j: (i, 0))], (bm, D), tb=True,
        extras=[(dz, (bm, D), lambda i, j: (i, 0))], epilogue=lambda acc, dzb: (alpha * dzb + acc,), comm=comm)
    return out, carried


def _back_proj(name, dy, w, layer, dtype):
    T, D = dy.shape
    K = w.shape[1]
    bm = _tile(T, ROW_TILE)
    (out,) = _matmul(
        name, (T // bm, 1, 1),
        (dy, (bm, D), lambda i, j, k: (i, 0)), (w, (None, K, D), lambda i, j, k: (layer, 0, 0)),
        [((T, K), dtype, (bm, K), lambda i, j: (i, 0))], (bm, K), tb=True)
    return out


def _rope_tables(T):
    pos = jnp.arange(T, dtype=F32)
    inv_freq = ROPE_THETA ** (-jnp.arange(0, HEAD_DIM, 2, dtype=F32) / HEAD_DIM)
    ang = pos[:, None] * inv_freq[None, :]
    cos, sin = jnp.cos(ang), jnp.sin(ang)
    c128 = jnp.tile(cos, (1, 4))
    s128 = jnp.tile(jnp.concatenate([-sin, sin], axis=1), (1, 2))
    return c128, s128


QK_COLS = (N_HEADS + N_KV_HEADS) * HEAD_DIM
Q_COLS = N_HEADS * HEAD_DIM
KV_COLS = N_KV_HEADS * HEAD_DIM
Q_SCALE = HEAD_DIM ** -0.5


def _qkv_rope(hb, w_qkv, layer, c128, s128):
    T, D = hb.shape
    N = w_qkv.shape[2]
    bm = _tile(T, ROW_TILE)

    def epi(acc, c, s):
        x = acc[:, :QK_COLS]
        rep = QK_COLS // 128
        r = x * jnp.tile(c, (1, rep)) + _swap_halves(x) * jnp.tile(s, (1, rep))
        return (jnp.concatenate([r[:, :Q_COLS] * Q_SCALE, r[:, Q_COLS:], acc[:, QK_COLS:]], axis=1),)

    (qkv,) = _matmul(
        "qkv_rope", (T // bm, 1, 1),
        (hb, (bm, D), lambda i, j, k: (i, 0)), (w_qkv, (None, D, N), lambda i, j, k: (layer, 0, 0)),
        [((T, N), BF16, (bm, N), lambda i, j: (i, 0))], (bm, N),
        extras=[(c128, (bm, 128), lambda i, j: (i, 0)), (s128, (bm, 128), lambda i, j: (i, 0))], epilogue=epi)
    return qkv


def _rope_bwd(dq, dkv, c128, s128):
    T = dq.shape[0]
    tr = _tile(T, ROW_TILE)

    def fn(i, dq, dkv, c, s):
        dx = jnp.concatenate([dq * Q_SCALE, dkv[:, :KV_COLS]], axis=1)
        rep = QK_COLS // 128
        d = dx * jnp.tile(c, (1, rep)) + _swap_halves(dx * jnp.tile(s, (1, rep)))
        return (jnp.concatenate([d, dkv[:, KV_COLS:]], axis=1),)

    N = Q_COLS + 2 * KV_COLS
    (out,) = _rowwise("rope_bwd", T // tr, fn, [_rows(dq, tr), _rows(dkv, tr), _rows(c128, tr), _rows(s128, tr)],
                      [((T, N), BF16, (tr, N), lambda i: (i, 0))])
    return out


def _attn_mask(first_block):
    q_pos = lax.broadcasted_iota(jnp.int32, (GROUP * ATTN_BLOCK, 2 * ATTN_BLOCK), 0) & (ATTN_BLOCK - 1)
    col = lax.broadcasted_iota(jnp.int32, (GROUP * ATTN_BLOCK, 2 * ATTN_BLOCK), 1)
    dist = q_pos + ATTN_BLOCK - col
    return (dist >= 0) & (dist < ATTN_BLOCK) & ((col >= ATTN_BLOCK) | jnp.logical_not(first_block))


def _sink_column(sk_ref, kvh):
    rg = lax.broadcasted_iota(jnp.int32, (GROUP * ATTN_BLOCK, 1), 0) // ATTN_BLOCK
    col = jnp.full((GROUP * ATTN_BLOCK, 1), sk_ref[0, kvh * GROUP], F32)
    for gi in range(1, GROUP):
        col = jnp.where(rg == gi, sk_ref[0, kvh * GROUP + gi], col)
    return col


def _stack_heads(x, kvh):
    return jnp.concatenate([x[:, (kvh * GROUP + gi) * HEAD_DIM:(kvh * GROUP + gi + 1) * HEAD_DIM] for gi in range(GROUP)], axis=0)


def _unstack_heads(parts):
    cols = []
    for p in parts:
        cols += [p[gi * ATTN_BLOCK:(gi + 1) * ATTN_BLOCK] for gi in range(GROUP)]
    return jnp.concatenate(cols, axis=1)


def _attn_softmax(q4, kb, mask, sink):
    s = lax.dot_general(q4, kb, (((1,), (1,)), ((), ())), preferred_element_type=F32)
    s = jnp.where(mask, s, NEG_BIG)
    m = jnp.maximum(jnp.max(s, axis=1, keepdims=True), sink)
    p = jnp.exp(s - m)
    e_sink = jnp.exp(sink - m)
    den = jnp.sum(p, axis=1, keepdims=True) + e_sink
    return p / den, e_sink / den


def _attn_fwd(qkv, sinks):
    T = qkv.shape[0]
    nb = T // ATTN_BLOCK
    kcb, vcb = Q_COLS // KV_COLS, Q_COLS // KV_COLS + 1

    def body(q_ref, kc_ref, kp_ref, vc_ref, vp_ref, sk_ref, o_ref):
        i = pl.program_id(0)
        mask = _attn_mask(i == 0)
        q = q_ref[...]
        kband = jnp.concatenate([kp_ref[...], kc_ref[...]], axis=0)
        vband = jnp.concatenate([vp_ref[...], vc_ref[...]], axis=0)
        parts = []
        for kvh in range(N_KV_HEADS):
            hs = slice(kvh * HEAD_DIM, (kvh + 1) * HEAD_DIM)
            pn, _ = _attn_softmax(_stack_heads(q, kvh), kband[:, hs], mask, _sink_column(sk_ref, kvh))
            parts.append(jnp.dot(pn.astype(BF16), vband[:, hs], preferred_element_type=F32))
        o_ref[...] = _unstack_heads(parts).astype(o_ref.dtype)

    prev = lambda i: jnp.maximum(i - 1, 0)
    return pl.pallas_call(
        body, name="attn_fwd", grid=(nb,),
        in_specs=[pl.BlockSpec((ATTN_BLOCK, Q_COLS), lambda i: (i, 0)),
                  pl.BlockSpec((ATTN_BLOCK, KV_COLS), lambda i: (i, kcb)),
                  pl.BlockSpec((ATTN_BLOCK, KV_COLS), lambda i: (prev(i), kcb)),
                  pl.BlockSpec((ATTN_BLOCK, KV_COLS), lambda i: (i, vcb)),
                  pl.BlockSpec((ATTN_BLOCK, KV_COLS), lambda i: (prev(i), vcb)),
                  pl.BlockSpec(memory_space=pltpu.SMEM)],
        out_specs=pl.BlockSpec((ATTN_BLOCK, Q_COLS), lambda i: (i, 0)),
        out_shape=jax.ShapeDtypeStruct((T, Q_COLS), BF16),
        compiler_params=_cparams(("arbitrary",)),
    )(qkv, qkv, qkv, qkv, qkv, sinks)


def _attn_bwd(qkv, do, sinks):
    T = qkv.shape[0]
    nb = T // ATTN_BLOCK
    kcb, vcb = Q_COLS // KV_COLS, Q_COLS // KV_COLS + 1
    B = ATTN_BLOCK

    def body(q_ref, kc_ref, kp_ref, vc_ref, vp_ref, do_ref, sk_ref, dq_ref, dkv_ref, dsk_ref, carry_ref):
        i = pl.program_id(0)

        @pl.when(i == 0)
        def _():
            carry_ref[...] = jnp.zeros_like(carry_ref)
            dsk_ref[...] = jnp.zeros_like(dsk_ref)

        @pl.when(i < nb)
        def _():
            mask = _attn_mask(i == 0)
            q = q_ref[...]
            do_blk = do_ref[...]
            kband = jnp.concatenate([kp_ref[...], kc_ref[...]], axis=0)
            vband = jnp.concatenate([vp_ref[...], vc_ref[...]], axis=0)
            dq_parts, dk_parts, dv_parts = [], [], []
            for kvh in range(N_KV_HEADS):
                hs = slice(kvh * HEAD_DIM, (kvh + 1) * HEAD_DIM)
                q4 = _stack_heads(q, kvh)
                do4 = _stack_heads(do_blk, kvh)
                kb, vb = kband[:, hs], vband[:, hs]
                pn, p_sink = _attn_softmax(q4, kb, mask, _sink_column(sk_ref, kvh))
                dp = lax.dot_general(do4, vb, (((1,), (1,)), ((), ())), preferred_element_type=F32)
                delta = jnp.sum(pn * dp, axis=1, keepdims=True)
                ds = (pn * (dp - delta)).astype(BF16)
                dsk_ref[kvh] += -(p_sink * delta)
                dq_parts.append(jnp.dot(ds, kb, preferred_element_type=F32))
                dk_parts.append(lax.dot_general(ds, q4, (((0,), (0,)), ((), ())), preferred_element_type=F32))
                dv_parts.append(lax.dot_general(pn.astype(BF16), do4, (((0,), (0,)), ((), ())), preferred_element_type=F32))
            dq_ref[...] = _unstack_heads(dq_parts)
            dkv = jnp.concatenate(dk_parts + dv_parts, axis=1)
            dkv_ref[...] = carry_ref[...] + dkv[:B]
            carry_ref[...] = dkv[B:]

        @pl.when(i == nb)
        def _():
            dkv_ref[...] = carry_ref[...]

    cur = lambda i: jnp.minimum(i, nb - 1)
    prev = lambda i: jnp.maximum(cur(i) - 1, 0)
    lag = lambda i: jnp.maximum(i - 1, 0)
    return pl.pallas_call(
        body, name="attn_bwd", grid=(nb + 1,),
        in_specs=[pl.BlockSpec((B, Q_COLS), lambda i: (cur(i), 0)),
                  pl.BlockSpec((B, KV_COLS), lambda i: (cur(i), kcb)),
                  pl.BlockSpec((B, KV_COLS), lambda i: (prev(i), kcb)),
                  pl.BlockSpec((B, KV_COLS), lambda i: (cur(i), vcb)),
                  pl.BlockSpec((B, KV_COLS), lambda i: (prev(i), vcb)),
                  pl.BlockSpec((B, Q_COLS), lambda i: (cur(i), 0)),
                  pl.BlockSpec(memory_space=pltpu.SMEM)],
        out_specs=[pl.BlockSpec((B, Q_COLS), lambda i: (cur(i), 0)),
                   pl.BlockSpec((B, 2 * KV_COLS), lambda i: (lag(i), 0)),
                   pl.BlockSpec((N_KV_HEADS, GROUP * B, 1), lambda i: (0, 0, 0))],
        out_shape=[jax.ShapeDtypeStruct((T, Q_COLS), F32), jax.ShapeDtypeStruct((T, 2 * KV_COLS), F32),
                   jax.ShapeDtypeStruct((N_KV_HEADS, GROUP * B, 1), F32)],
        scratch_shapes=[pltpu.VMEM((B, 2 * KV_COLS), F32)],
        compiler_params=_cparams(("arbitrary",)),
    )(qkv, qkv, qkv, qkv, qkv, do, sinks)


def _halo_prev(arr, tr, cols, cb=0):
    per = tr // 8
    return (arr, (8, cols), lambda i: (jnp.maximum(i * per - 1, 0), cb))


def _halo_next(arr, tr, cols, cb=0):
    per = tr // 8
    last = arr.shape[0] // 8 - 1
    return (arr, (8, cols), lambda i: (jnp.minimum((i + 1) * per, last), cb))


def _conv_fwd(xg, cw, cb):
    T = xg.shape[0]
    C = cb.shape[1]
    tr = _tile(T, ROW_TILE)

    def fn(i, cur, prev8, cb, *cw):
        prev8 = jnp.where(i == 0, 0.0, prev8)
        xc = cb + cw[CONV_W - 1] * cur
        for s in range(1, CONV_W):
            xc = xc + cw[CONV_W - 1 - s] * _shift_down(prev8, cur, s)
        return xc, xc

    row = lambda i: (i, 0)
    return _rowwise("lru_conv", T // tr, fn, [_rows(xg, tr, C), _halo_prev(xg, tr, C), _whole(cb)] + [_whole(w) for w in cw],
                    [((T, C), F32, (tr, C), row), ((T, C), BF16, (tr, C), row)])


def _conv_bwd(dxc, xg, dgb, cw):
    T, C = dxc.shape
    tr = _tile(T, ROW_TILE)
    nt = T // tr

    def fn(i, d_cur, d_next8, x_cur, x_prev8, dgb, *cw):
        d_next8 = jnp.where(i == nt - 1, 0.0, d_next8)
        x_prev8 = jnp.where(i == 0, 0.0, x_prev8)
        dxb = cw[CONV_W - 1] * d_cur
        dcw = [jnp.sum(d_cur * x_cur, axis=0, keepdims=True)]
        for s in range(1, CONV_W):
            dxb = dxb + cw[CONV_W - 1 - s] * _shift_up(d_cur, d_next8, s)
            dcw.append(jnp.sum(d_cur * _shift_down(x_prev8, x_cur, s), axis=0, keepdims=True))
        return (jnp.concatenate([dxb.astype(BF16), dgb], axis=1), dcw[3], dcw[2], dcw[1], dcw[0],
                jnp.sum(d_cur, axis=0, keepdims=True))

    return _rowwise("lru_conv_bwd", nt, fn,
                    [_rows(dxc, tr), _halo_next(dxc, tr, C), _rows(xg, tr, C), _halo_prev(xg, tr, C), _rows(dgb, tr)] + [_whole(w) for w in cw],
                    [((T, 2 * C), BF16, (tr, 2 * C), lambda i: (i, 0))], accs=[(1, C)] * (CONV_W + 1))


def _lru_gate_fwd(xc, xcb, w_rarx, layer, b_ra, b_rx, lam):
    T, C = xc.shape
    W = C // RNN_BLOCKS
    bm = _tile(T, ROW_TILE)

    def epi(acc, xc_blk, bra, brx, lm):
        a, b = _lru_gates(acc, xc_blk, bra, brx, lm)
        return acc, a, b

    blk = lambda i, j: (i, j)
    par = lambda i, j: (0, j)
    return _matmul(
        "lru_gates", (T // bm, RNN_BLOCKS, 1),
        (xcb, (bm, W), lambda i, j, k: (i, j)), (w_rarx, (None, None, W, 2 * W), lambda i, j, k: (layer, j, 0, 0)),
        [((T, 2 * C), F32, (bm, 2 * W), blk), ((T, C), F32, (bm, W), blk), ((T, C), F32, (bm, W), blk)],
        (bm, 2 * W),
        extras=[(xc, (bm, W), blk), (b_ra, (1, W), par), (b_rx, (1, W), par), (lam, (1, W), par)], epilogue=epi)


def _lru_gate_bwd(pre, xc, lam_adj, h, b_ra, b_rx, lam):
    T, C = xc.shape
    W = C // RNN_BLOCKS
    tr = _tile(T, ROW_TILE // 2)

    def fn(i, pre, xc, adj, h_cur, h_prev8, bra, brx, lm):
        h_prev8 = jnp.where(i == 0, 0.0, h_prev8)
        da = adj * _shift_down(h_prev8, h_cur, 1)
        dpre, dxc, dbra, dbrx, dlam = [], [], [], [], []
        for n in range(RNN_BLOCKS):
            cs = slice(n * W, (n + 1) * W)
            _, vjp = jax.vjp(_lru_gates, pre[:, 2 * n * W:2 * (n + 1) * W], xc[:, cs], bra[:, cs], brx[:, cs], lm[:, cs])
            g = vjp((da[:, cs], adj[:, cs]))
            for lst, v in zip((dpre, dxc, dbra, dbrx, dlam), g):
                lst.append(v)
        cat = lambda l: jnp.concatenate(l, axis=1)
        return cat(dpre), cat(dxc), cat(dbra), cat(dbrx), cat(dlam)

    row = lambda i: (i, 0)
    return _rowwise("lru_gates_bwd", T // tr, fn,
                    [_rows(pre, tr), _rows(xc, tr), _rows(lam_adj, tr), _rows(h, tr), _halo_prev(h, tr, C),
                     _whole(b_ra), _whole(b_rx), _whole(lam)],
                    [((T, 2 * C), BF16, (tr, 2 * C), row), ((T, C), F32, (tr, C), row)], accs=[(1, C)] * 3)


def _scan_fwd(a, b):
    T, C = a.shape
    tt = _tile(T, ROW_TILE)

    def body(a_ref, b_ref, o_ref, c_ref):
        @pl.when(pl.program_id(0) == 0)
        def _():
            c_ref[...] = jnp.zeros_like(c_ref)

        row = lax.broadcasted_iota(jnp.int32, (8, C), 0)

        def step(j, carry):
            sl = pl.ds(pl.multiple_of(j * 8, 8), 8)
            A, B = a_ref[sl, :], b_ref[sl, :]
            for d in (1, 2, 4):
                ok = row >= d
                B = jnp.where(ok, A * pltpu.roll(B, d, 0) + B, B)
                A = jnp.where(ok, A * pltpu.roll(A, d, 0), A)
            h = A * carry + B
            o_ref[sl, :] = h
            return jnp.sum(jnp.where(row == 7, h, 0.0), axis=0, keepdims=True)

        c_ref[0:1, :] = lax.fori_loop(0, tt // 8, step, c_ref[0:1, :])

    spec = pl.BlockSpec((tt, C), lambda i: (i, 0))
    return pl.pallas_call(body, name="lru_scan", grid=(T // tt,), in_specs=[spec, spec], out_specs=spec,
                          out_shape=jax.ShapeDtypeStruct((T, C), F32), scratch_shapes=[pltpu.VMEM((8, C), F32)],
                          compiler_params=_cparams(("arbitrary",)))(a, b)


def _scan_bwd(a, dh):
    T, C = a.shape
    tt = _tile(T, ROW_TILE)
    nt = T // tt

    def body(a_ref, d_ref, o_ref, c_ref):
        @pl.when(pl.program_id(0) == 0)
        def _():
            c_ref[...] = jnp.zeros_like(c_ref)

        row = lax.broadcasted_iota(jnp.int32, (8, C), 0)

        def step(jj, carry):
            adj_next, a_next = carry
            j = tt // 8 - 1 - jj
            sl = pl.ds(pl.multiple_of(j * 8, 8), 8)
            a_blk = a_ref[sl, :]
            A = jnp.where(row < 7, pltpu.roll(a_blk, 7, 0), a_next)
            B = d_ref[sl, :]
            for d in (1, 2, 4):
                ok = row < 8 - d
                B = jnp.where(ok, A * pltpu.roll(B, 8 - d, 0) + B, B)
                A = jnp.where(ok, A * pltpu.roll(A, 8 - d, 0), A)
            adj = A * adj_next + B
            o_ref[sl, :] = adj
            first = lambda v: jnp.sum(jnp.where(row == 0, v, 0.0), axis=0, keepdims=True)
            return first(adj), first(a_blk)

        adj0, a0 = lax.fori_loop(0, tt // 8, step, (c_ref[0:1, :], c_ref[1:2, :]))
        c_ref[0:1, :] = adj0
        c_ref[1:2, :] = a0

    spec = pl.BlockSpec((tt, C), lambda i: (nt - 1 - i, 0))
    return pl.pallas_call(body, name="lru_scan_bwd", grid=(nt,), in_specs=[spec, spec], out_specs=spec,
                          out_shape=jax.ShapeDtypeStruct((T, C), F32), scratch_shapes=[pltpu.VMEM((8, C), F32)],
                          compiler_params=_cparams(("arbitrary",)))(a, dh)


def _lru_out_fwd(h, xg):
    T, C = h.shape
    tr = _tile(T, ROW_TILE)
    (y,) = _rowwise("lru_out", T // tr, lambda i, h, gb: (h * _gelu_tanh(gb),), [_rows(h, tr), _rows(xg, tr, C, 1)],
                    [((T, C), BF16, (tr, C), lambda i: (i, 0))])
    return y


def _lru_out_bwd(dy, h, xg):
    T, C = h.shape
    tr = _tile(T, ROW_TILE)

    def fn(i, dy, h, gb):
        _, vjp = jax.vjp(lambda h, gb: h * _gelu_tanh(gb), h, gb)
        return vjp(dy)

    row = lambda i: (i, 0)
    return _rowwise("lru_out_bwd", T // tr, fn, [_rows(dy, tr), _rows(h, tr), _rows(xg, tr, C, 1)],
                    [((T, C), F32, (tr, C), row), ((T, C), BF16, (tr, C), row)])


class _Sharded:
    def __init__(self, kind, size, off=0, width=None):
        self.kind, self.size, self.off, self.width = kind, size, off, width

    def slot(self, cx, cy):
        return (2 * cy + cx) if self.kind == "perm" else (2 * cx + cy)

    def at(self, ref, cx, cy, layers, half=None):
        s = self.slot(cx, cy)
        start = s * self.size
        if not isinstance(start, int):
            start = pl.multiple_of(start, 8 if self.kind in ("rows", "rarx") else 128)
        if self.kind in ("cols", "perm"):
            n = ref.shape[1]
            rows = slice(None) if half is None else pl.ds(pl.multiple_of(half * (n // 2), 8), n // 2)
            return ref.at[layers, rows, pl.ds(start, self.size)]
        if self.kind == "rows":
            if half is None:
                return ref.at[layers, pl.ds(start, self.size), :]
            return ref.at[layers, pl.ds(pl.multiple_of(start + half * (self.size // 2), 8), self.size // 2), :]
        if self.kind == "rarx":
            n = ref.shape[1]
            blocks = slice(None) if half is None else pl.ds(half * (n // 2), n // 2)
            return ref.at[layers, blocks, pl.ds(start, self.size), pl.ds(self.off, self.width)]
        raise ValueError(self.kind)


def _mesh_pos():
    return lax.axis_index("x"), lax.axis_index("y"), lax.axis_index("c")


def _peer_chips(x, y):
    return [(1 - x, y), (x, 1 - y), (1 - x, 1 - y)]


def _place(shard, out_shape, out_dtype, sh, pos, prev=None):
    def body(p_ref, x_ref, *rest):
        rest[-1][...] = x_ref[...].astype(rest[-1].dtype)

    if sh.kind in ("cols", "perm"):
        L, R, Ns = shard.shape
        tr = _tile(R, ROW_TILE)
        k = 1 if sh.kind == "perm" else 0
        grid = (L, R // tr)
        ispec = pl.BlockSpec((None, tr, Ns), lambda l, i, p: (l, i, 0))
        ospec = pl.BlockSpec((None, tr, Ns), lambda l, i, p: (l, i, p[k]))
    elif sh.kind == "rows":
        L, Rs, D = shard.shape
        tr = _tile(Rs, ROW_TILE)
        nt = Rs // tr
        grid = (L, nt)
        ispec = pl.BlockSpec((None, tr, D), lambda l, i, p: (l, i, 0))
        ospec = pl.BlockSpec((None, tr, D), lambda l, i, p: (l, p[0] * nt + i, 0))
    else:
        L, nb, Rs, Wd = shard.shape
        cb = sh.off // Wd
        grid = (L, nb)
        ispec = pl.BlockSpec((None, None, Rs, Wd), lambda l, i, p: (l, i, 0, 0))
        ospec = pl.BlockSpec((None, None, Rs, Wd), lambda l, i, p: (l, i, p[0], cb))
    in_specs = [ispec]
    operands = [pos, shard]
    alias = {}
    if prev is not None:
        in_specs.append(pl.BlockSpec(memory_space=pl.ANY))
        operands.append(prev)
        alias = {2: 0}
    gs = pltpu.PrefetchScalarGridSpec(num_scalar_prefetch=1, grid=grid, in_specs=in_specs, out_specs=ospec)
    return pl.pallas_call(body, name="weight_place", grid_spec=gs, out_shape=jax.ShapeDtypeStruct(out_shape, out_dtype),
                          input_output_aliases=alias, compiler_params=_cparams(("arbitrary", "arbitrary")))(*operands)


def _gather_over_ici(jobs, base=0):
    assert len({id(j[0]) for j in jobs}) == len(jobs)

    def copies(refs, send, recv):
        x, y, c = _mesh_pos()
        out = []
        for t, (_, sh, layer) in enumerate(jobs):
            lay = pl.ds(layer, 1)
            mine = sh.at(refs[t], x, y, lay, c)
            for j, (px, py) in enumerate(_peer_chips(x, y)):
                theirs = sh.at(refs[t], px, py, lay, c)
                k = base + 3 * t + j
                out.append((pltpu.make_async_remote_copy(mine, mine, send.at[k], recv.at[k],
                                                         device_id=(px, py, c), device_id_type=MESH),
                            pltpu.make_async_remote_copy(theirs, theirs, send.at[k], recv.at[k],
                                                         device_id=(px, py, c), device_id_type=MESH)))
        return out

    def start(refs, send, recv):
        for out_cp, _ in copies(refs, send, recv):
            out_cp.start()

    def finish(refs, send, recv):
        for out_cp, in_cp in copies(refs, send, recv):
            in_cp.wait_recv()
            out_cp.wait_send()

    return _SideComm([j[0] for j in jobs], 3 * len(jobs), start, finish)


def _pass_to_sibling(jobs, base=0):
    assert len({id(j[0]) for j in jobs}) == len(jobs)

    def copies(refs, send, recv):
        x, y, c = _mesh_pos()
        out = []
        for t, (_, sh, layer) in enumerate(jobs):
            lay = pl.ds(layer, 1)
            for j, (px, py) in enumerate(_peer_chips(x, y)):
                got = sh.at(refs[t], px, py, lay, c)
                coming = sh.at(refs[t], px, py, lay, 1 - c)
                k = base + 3 * t + j
                out.append((pltpu.make_async_remote_copy(got, got, send.at[k], recv.at[k],
                                                         device_id=(x, y, 1 - c), device_id_type=MESH),
                            pltpu.make_async_remote_copy(coming, coming, send.at[k], recv.at[k],
                                                         device_id=(x, y, 1 - c), device_id_type=MESH)))
        return out

    def start(refs, send, recv):
        for out_cp, _ in copies(refs, send, recv):
            out_cp.start()

    def finish(refs, send, recv):
        for out_cp, in_cp in copies(refs, send, recv):
            in_cp.wait_recv()
            out_cp.wait_send()

    return _SideComm([j[0] for j in jobs], 3 * len(jobs), start, finish)


def _all_gather(jobs):
    n = len(jobs)
    ici, d2d = _gather_over_ici(jobs), _pass_to_sibling(jobs, 3 * n)

    def body(*refs):
        outs = refs[n:2 * n]
        send, recv = refs[2 * n:]
        ici.start(outs, send, recv)
        ici.finish(outs, send, recv)
        d2d.start(outs, send, recv)
        d2d.finish(outs, send, recv)

    hbm = pl.BlockSpec(memory_space=pl.ANY)
    return pl.pallas_call(
        body, name="weights_all_gather", in_specs=[hbm] * n, out_specs=[hbm] * n,
        out_shape=[jax.ShapeDtypeStruct(j[0].shape, j[0].dtype) for j in jobs],
        input_output_aliases={t: t for t in range(n)},
        scratch_shapes=[pltpu.SemaphoreType.DMA((6 * n,)), pltpu.SemaphoreType.DMA((6 * n,))],
    )(*[j[0] for j in jobs])


def _half_view(sh, ref, h):
    if sh.kind == "rows":
        n = ref.shape[-1]
        return ref.at[:, pl.ds(pl.multiple_of(h * (n // 2), 128), n // 2)]
    n = ref.shape[0]
    return ref.at[pl.ds(h * (n // 2), n // 2)]


def _half_shape(sh, layer_shape):
    s = list(layer_shape)
    s[len(s) - 1 if sh.kind == "rows" else 0] //= 2
    return tuple(s)


def _chip_part(sh, ref, cx, cy):
    start = sh.slot(cx, cy) * sh.size
    if sh.kind in ("cols", "perm"):
        return ref.at[:, pl.ds(pl.multiple_of(start, 128), sh.size)]
    if sh.kind == "rows":
        return ref.at[pl.ds(pl.multiple_of(start, 8), sh.size), :]
    return ref.at[:, pl.ds(pl.multiple_of(start, 8), sh.size), :]


def _chip_part_shape(sh, half_shape):
    s = list(half_shape)
    s[{"cols": len(s) - 1, "perm": len(s) - 1, "rows": 0, "rarx": 1}[sh.kind]] = sh.size
    return tuple(s)


def _swap_stage(jobs):
    assert len({id(j[0]) for j in jobs}) == len(jobs)

    def make(base):
        arrays = []
        for g, sh, _ in jobs:
            arrays += [g, jax.ShapeDtypeStruct(_half_shape(sh, g.shape[1:]), g.dtype)]

        def copies(refs, send, recv):
            x, y, c = _mesh_pos()
            return [pltpu.make_async_remote_copy(_half_view(sh, refs[2 * t].at[layer], 1 - c), refs[2 * t + 1],
                                                 send.at[base + t], recv.at[base + t],
                                                 device_id=(x, y, 1 - c), device_id_type=MESH)
                    for t, (_, sh, layer) in enumerate(jobs)]

        def start(refs, send, recv):
            for cp in copies(refs, send, recv):
                cp.start()

        def finish(refs, send, recv):
            for cp in copies(refs, send, recv):
                cp.wait()

        return _SideComm(arrays, len(jobs), start, finish)
    return make


def _exchange_stage(jobs):
    n = len(jobs)

    def make(base):
        arrays = []
        for p, sh in jobs:
            arrays += [p, jax.ShapeDtypeStruct((N_CHIPS,) + _chip_part_shape(sh, p.shape), p.dtype)]

        def copies(refs, send, recv):
            x, y, c = _mesh_pos()
            me = 2 * x + y
            out = []
            for t, (_, sh) in enumerate(jobs):
                part, land = refs[2 * t], refs[2 * t + 1]
                local = pltpu.make_async_copy(_chip_part(sh, part, x, y), land.at[me], send.at[base + 3 * n + t])
                remote = []
                for j, (px, py) in enumerate(_peer_chips(x, y)):
                    k = base + 3 * t + j
                    src = land.at[2 * px + py]
                    remote.append((pltpu.make_async_remote_copy(_chip_part(sh, part, px, py), land.at[me], send.at[k],
                                                                recv.at[k], device_id=(px, py, c), device_id_type=MESH),
                                   pltpu.make_async_remote_copy(src, src, send.at[k], recv.at[k],
                                                                device_id=(px, py, c), device_id_type=MESH)))
                out.append((local, remote))
            return out

        def start(refs, send, recv):
            for local, remote in copies(refs, send, recv):
                local.start()
                for out_cp, _ in remote:
                    out_cp.start()

        def finish(refs, send, recv):
            for local, remote in copies(refs, send, recv):
                for out_cp, in_cp in remote:
                    in_cp.wait_recv()
                    out_cp.wait_send()
                local.wait()

        return _SideComm(arrays, 4 * n, start, finish)
    return make


def _share_stage(jobs):
    assert len({id(j[0]) for j in jobs}) == len(jobs)

    def make(base):
        def copies(refs, send, recv):
            x, y, c = _mesh_pos()
            out = []
            for t, (_, sh, layer) in enumerate(jobs):
                mine = _half_view(sh, refs[t].at[layer], c)
                theirs = _half_view(sh, refs[t].at[layer], 1 - c)
                out.append((pltpu.make_async_remote_copy(mine, mine, send.at[base + t], recv.at[base + t],
                                                         device_id=(x, y, 1 - c), device_id_type=MESH),
                            pltpu.make_async_remote_copy(theirs, theirs, send.at[base + t], recv.at[base + t],
                                                         device_id=(x, y, 1 - c), device_id_type=MESH)))
            return out

        def start(refs, send, recv):
            for out_cp, _ in copies(refs, send, recv):
                out_cp.start()

        def finish(refs, send, recv):
            for out_cp, in_cp in copies(refs, send, recv):
                in_cp.wait_recv()
                out_cp.wait_send()

        return _SideComm([j[0] for j in jobs], len(jobs), start, finish)
    return make


def _pair_add_layer(g, sh, layer, got, c_arr):
    if sh.kind == "rows":
        R, Ch = got.shape
        tr = _tile(R, ROW_TILE)
        ins = [(g, (None, tr, Ch), lambda i, p: (layer, i, p[0])), (got, (tr, Ch), lambda i, p: (i, 0))]
        nt, g2 = R // tr, got
    else:
        C = got.shape[-1]
        g2 = got.reshape(-1, C)
        R = g2.shape[0]
        tr = _tile(R, 256 if C > 2048 else ROW_TILE)
        nt = R // tr
        ins = [(g.reshape(g.shape[0], -1, C), (None, tr, C), lambda i, p: (layer, p[0] * nt + i, 0)),
               (g2, (tr, C), lambda i, p: (i, 0))]
    (out,) = _rowwise("grad_pair_add", nt, lambda i, a, b: (a.astype(F32) + b.astype(F32),), ins,
                      [(g2.shape, got.dtype, ins[1][1], lambda i, p: (i, 0))], prefetch=c_arr)
    return out.reshape(got.shape)


def _chip_sum_layer(land, sh, layer, tot, c_arr):
    def fn(i, blk):
        v = blk.astype(F32)
        return (((v[0] + v[1]) + v[2]) + v[3],)

    if sh.kind == "rows":
        _, R, Ch = land.shape
        tr = _tile(R, ROW_TILE)
        ins = [(land, (N_CHIPS, tr, Ch), lambda i, p: (0, i, 0))]
        outs = [(tot.shape, F32, (None, tr, Ch), lambda i, p: (layer, i, p[0]))]
        (out,) = _rowwise("grad_chip_sum", R // tr, fn, ins, outs, prefetch=c_arr, into=tot)
        return out
    C = land.shape[-1]
    l3 = land.reshape(N_CHIPS, -1, C)
    R = l3.shape[1]
    tr = _tile(R, ROW_TILE)
    nt = R // tr
    t3 = tot.reshape(tot.shape[0], -1, C)
    (out,) = _rowwise("grad_chip_sum", nt, fn, [(l3, (N_CHIPS, tr, C), lambda i, p: (0, i, 0))],
                      [(t3.shape, F32, (None, tr, C), lambda i, p: (layer, p[0] * nt + i, 0))], prefetch=c_arr, into=t3)
    return out.reshape(tot.shape)


def _flat2(shape):
    return (math.prod(shape[:-1]), shape[-1])


def _adamw(w, g, m, v):
    shape = w.shape
    R, C = _flat2(shape)
    tr = _tile(R, 256 if R % 256 == 0 else R)

    def fn(i, w, g, m, v):
        m2 = ADAM_B1 * m + (1.0 - ADAM_B1) * g
        v2 = ADAM_B2 * v + (1.0 - ADAM_B2) * (g * g)
        m_hat = m2 / (1.0 - ADAM_B1 ** ADAM_STEP)
        v_hat = v2 / (1.0 - ADAM_B2 ** ADAM_STEP)
        delta = -ADAM_LR * (m_hat / (jnp.sqrt(v_hat) + ADAM_EPS) + ADAM_WD * w)
        return g, delta, m2, v2

    row = lambda i: (i, 0)
    f2 = lambda a: a.reshape(R, C)
    outs = _rowwise("adamw", R // tr, fn, [_rows(f2(a), tr) for a in (w, g, m, v)], [((R, C), F32, (tr, C), row)] * 4)
    return [o.reshape(shape) for o in outs]


def _loss_and_grad(last, target):
    xhat, g, b = last
    T, D = xhat.shape
    tr = _tile(T, ROW_TILE)

    def fn(i, xhat, t, g, b):
        err = (xhat * g + b) - t
        return err * (1.0 / D), jnp.sum(err * err, axis=0, keepdims=True)

    dh, sq = _rowwise("loss", T // tr, fn, [_rows(xhat, tr), _rows(target, tr), _whole(g), _whole(b)],
                      [((T, D), F32, (tr, D), lambda i: (i, 0))], accs=[(1, D)])
    return dh, (0.5 / D) * jnp.sum(sq)


def kernel(x, ffn1_w_in, ffn1_w_out, ffn2_w_in, ffn2_w_out, ln_g, ln_b, attn_w_qkv, attn_sinks, attn_w_o, lru_w_in, lru_conv_w, lru_conv_b, lru_w_ra, lru_b_ra, lru_w_rx, lru_b_rx, lru_lambda, lru_w_out, loss_target, m_ffn1_w_in, m_ffn1_w_out, m_ffn2_w_in, m_ffn2_w_out, m_ln_g, m_ln_b, m_attn_w_qkv, m_attn_sinks, m_attn_w_o, m_lru_w_in, m_lru_conv_w, m_lru_conv_b, m_lru_w_ra, m_lru_b_ra, m_lru_w_rx, m_lru_b_rx, m_lru_lambda, m_lru_w_out, v_ffn1_w_in, v_ffn1_w_out, v_ffn2_w_in, v_ffn2_w_out, v_ln_g, v_ln_b, v_attn_w_qkv, v_attn_sinks, v_attn_w_o, v_lru_w_in, v_lru_conv_w, v_lru_conv_b, v_lru_w_ra, v_lru_b_ra, v_lru_w_rx, v_lru_b_rx, v_lru_lambda, v_lru_w_out):
    names = ["ffn1_w_in", "ffn1_w_out", "ffn2_w_in", "ffn2_w_out", "ln_g", "ln_b", "attn_w_qkv", "attn_sinks", "attn_w_o",
             "lru_w_in", "lru_conv_w", "lru_conv_b", "lru_w_ra", "lru_b_ra", "lru_w_rx", "lru_b_rx", "lru_lambda", "lru_w_out"]
    W = dict(zip(names, [ffn1_w_in, ffn1_w_out, ffn2_w_in, ffn2_w_out, ln_g, ln_b, attn_w_qkv, attn_sinks, attn_w_o,
                         lru_w_in, lru_conv_w, lru_conv_b, lru_w_ra, lru_b_ra, lru_w_rx, lru_b_rx, lru_lambda, lru_w_out]))
    M = dict(zip(names, [m_ffn1_w_in, m_ffn1_w_out, m_ffn2_w_in, m_ffn2_w_out, m_ln_g, m_ln_b, m_attn_w_qkv, m_attn_sinks, m_attn_w_o,
                         m_lru_w_in, m_lru_conv_w, m_lru_conv_b, m_lru_w_ra, m_lru_b_ra, m_lru_w_rx, m_lru_b_rx, m_lru_lambda, m_lru_w_out]))
    V = dict(zip(names, [v_ffn1_w_in, v_ffn1_w_out, v_ffn2_w_in, v_ffn2_w_out, v_ln_g, v_ln_b, v_attn_w_qkv, v_attn_sinks, v_attn_w_o,
                         v_lru_w_in, v_lru_conv_w, v_lru_conv_b, v_lru_w_ra, v_lru_b_ra, v_lru_w_rx, v_lru_b_rx, v_lru_lambda, v_lru_w_out]))

    T, D = x.shape[1], x.shape[2]
    L = ffn1_w_in.shape[0]
    LA, LR = attn_w_qkv.shape[0], lru_w_in.shape[0]
    N2 = ffn1_w_in.shape[2] * N_CHIPS
    F = N2 // 2
    C = lru_lambda.shape[1] * N_CHIPS
    CW = C // N_CHIPS
    alpha = (2.0 * L) ** 0.25
    c_arr = lax.axis_index("c").astype(jnp.int32).reshape(1)

    n_sink = attn_sinks.size
    assert n_sink <= CW

    up8 = lambda n: -(-n // 8) * 8
    o_g = 0
    o_b = o_g + up8(3 * L)
    o_cw = o_b + up8(3 * L)
    o_cb = o_cw + up8(LR * CONV_W)
    o_ra = o_cb + up8(LR)
    o_rx = o_ra + up8(LR)
    o_lam = o_rx + up8(LR)
    o_sink = o_lam + up8(LR)
    assert o_sink + 8 <= SMALL_ROWS

    def pack_small(d):
        parts = [d["ln_g"].reshape(-1, CW), d["ln_b"].reshape(-1, CW), d["lru_conv_w"].reshape(-1, CW), d["lru_conv_b"],
                 d["lru_b_ra"], d["lru_b_rx"], d["lru_lambda"],
                 jnp.pad(d["attn_sinks"].reshape(1, -1), ((0, 0), (0, CW - n_sink)))]
        parts = [jnp.pad(p, ((0, up8(p.shape[0]) - p.shape[0]), (0, 0))) for p in parts]
        used = sum(p.shape[0] for p in parts)
        return jnp.concatenate(parts + [jnp.zeros((SMALL_ROWS - used, CW), F32)], axis=0)

    cols = lambda a: _Sharded("cols", a.shape[-1])
    rows_ = lambda a: _Sharded("rows", a.shape[-2])
    RW = lru_w_ra.shape[2]
    sh_list = [_Sharded("perm", N2 // 4), rows_(ffn1_w_out), _Sharded("perm", N2 // 4), rows_(ffn2_w_out),
               cols(attn_w_qkv), rows_(attn_w_o), cols(lru_w_in), rows_(lru_w_out)]
    big = [ffn1_w_in, ffn1_w_out, ffn2_w_in, ffn2_w_out, attn_w_qkv, attn_w_o, lru_w_in, lru_w_out]
    blk_w = lru_w_ra.shape[3]
    small_sh = _Sharded("cols", CW)
    full = lambda a, sh: tuple(a.shape[:-1]) + (a.shape[-1] * N_CHIPS,) if sh.kind in ("cols", "perm") else \
        tuple(a.shape[:-2]) + (a.shape[-2] * N_CHIPS, a.shape[-1])
    mx, my = lax.axis_index("x"), lax.axis_index("y")
    pos = jnp.stack([2 * mx + my, 2 * my + mx]).astype(jnp.int32)
    placed = [_place(a, full(a, sh), BF16, sh, pos) for a, sh in zip(big, sh_list)]
    rarx_shape = (LR, RNN_BLOCKS, RW * N_CHIPS, 2 * blk_w)
    p_ra = _place(lru_w_ra, rarx_shape, BF16, _Sharded("rarx", RW, 0, blk_w), pos)
    placed.append(_place(lru_w_rx, rarx_shape, BF16, _Sharded("rarx", RW, blk_w, blk_w), pos, prev=p_ra))
    gnames = ["w_in1", "w_out1", "w_in2", "w_out2", "qkv", "wo", "lin", "lout", "rarx"]
    GW = dict(zip(gnames, placed))
    SH = dict(zip(gnames, sh_list + [_Sharded("rarx", RW, 0, 2 * blk_w)]))
    p_small = _place(pack_small(W).reshape(1, SMALL_ROWS, CW), (1, SMALL_ROWS, C), F32, small_sh, pos)

    def comm_for(stage, jobs):
        return stage([(GW[n], SH[n], l) for n, l in jobs]) if jobs else None

    def rebind(jobs, arrays):
        for (n, _), a in zip(jobs, arrays):
            GW[n] = a

    jobs0 = [(n, 0) for n in ("w_in1", "w_out1", "w_in2", "w_out2", "qkv", "wo")]
    got0 = _all_gather([(GW[n], SH[n], l) for n, l in jobs0] + [(p_small, small_sh, 0)])
    rebind(jobs0, got0)
    small = got0[-1].reshape(SMALL_ROWS, C)
    row_of = lambda r: small[r:r + 1]
    mixer_jobs = lambda l: [("qkv", l // 2), ("wo", l // 2)] if l % 2 == 0 else [("lin", l // 2), ("lout", l // 2), ("rarx", l // 2)]

    def ffn_forward(prev, hb, n_in, n_out, layer, g, b, jobs):
        gu, act, carried = _ffn_up(hb, GW[n_in], layer, comm_for(_gather_over_ici, jobs))
        rebind(jobs, carried)
        res = _proj_ln("ffn_down_ln", act, GW[n_out], layer, prev, g, b, alpha, 0.5, comm_for(_pass_to_sibling, jobs))
        if jobs:
            res, carried = res
            rebind(jobs, carried)
        return res, (gu, act)
    assert D == C, "packed small parameters assume d_model == d_rnn"

    c128, s128 = _rope_tables(T)
    sink_rows = [attn_sinks[j:j + 1] for j in range(LA)]

    x2 = x.reshape(T, D)
    prev = (x2, jnp.ones((1, D), F32), jnp.zeros((1, D), F32))
    hb = x2.astype(BF16)
    saved = []
    for i in range(L):
        j = i // 2
        lay = {}
        lay["hb0"] = hb
        nxt = i + 1
        gains = [row_of(o_g + 3 * i + k) for k in range(3)]
        biases = [row_of(o_b + 3 * i + k) for k in range(3)]
        jobs1 = [("w_in2", nxt), ("w_out2", nxt)] if nxt < L else []
        jobs2 = [("w_in1", nxt), ("w_out1", nxt)] + mixer_jobs(nxt) if nxt < L else []
        (xh, hb, rs), (lay["gu1"], lay["act1"]) = ffn_forward(prev, hb, "w_in1", "w_out1", i, gains[0], biases[0], jobs1)
        lay["ln1"], lay["hb1"], prev = (xh, rs), hb, (xh, gains[0], biases[0])
        if i % 2 == 0:
            qkv = _qkv_rope(hb, GW["qkv"], j, c128, s128)
            o = _attn_fwd(qkv, sink_rows[j])
            lay["qkv"], lay["o"] = qkv, o
            xh, hb, rs = _proj_ln("attn_out_ln", o, GW["wo"], j, prev, gains[1], biases[1], alpha, 1.0)
        else:
            bm = _tile(T, ROW_TILE)
            (xg,) = _matmul("lru_in", (T // bm, 1, 1), (hb, (bm, D), lambda p, q, k: (p, 0)),
                            (GW["lin"], (None, D, 2 * C), lambda p, q, k, j=j: (j, 0, 0)),
                            [((T, 2 * C), F32, (bm, 2 * C), lambda p, q: (p, 0))], (bm, 2 * C))
            cw = [row_of(o_cw + j * CONV_W + k) for k in range(CONV_W)]
            xc, xcb = _conv_fwd(xg, cw, row_of(o_cb + j))
            pre, a, b = _lru_gate_fwd(xc, xcb, GW["rarx"], j, row_of(o_ra + j), row_of(o_rx + j), row_of(o_lam + j))
            hs = _scan_fwd(a, b)
            y = _lru_out_fwd(hs, xg)
            lay.update(xg=xg, xc=xc, xcb=xcb, pre=pre, a=a, hs=hs, y=y, cw=cw)
            xh, hb, rs = _proj_ln("lru_out_ln", y, GW["lout"], j, prev, gains[1], biases[1], alpha, 1.0)
        lay["ln2"], lay["hb2"], prev = (xh, rs), hb, (xh, gains[1], biases[1])
        (xh, hb, rs), (lay["gu2"], lay["act2"]) = ffn_forward(prev, hb, "w_in2", "w_out2", i, gains[2], biases[2], jobs2)
        lay["ln3"], prev = (xh, rs), (xh, gains[2], biases[2])
        saved.append(lay)
    g_w_in1, g_w_out1, g_w_in2, g_w_out2, g_qkv, g_wo, g_lin, g_lout, g_rarx = [GW[n] for n in gnames]

    dh, loss_local = _loss_and_grad(prev, loss_target.reshape(T, D))
    loss = lax.psum(loss_local, ("x", "y", "c"))

    SH2 = dict(SH, small=small_sh)
    shard_stack = {"w_in1": ffn1_w_in.shape, "w_out1": ffn1_w_out.shape, "w_in2": ffn2_w_in.shape, "w_out2": ffn2_w_out.shape,
                   "qkv": attn_w_qkv.shape, "wo": attn_w_o.shape, "lin": lru_w_in.shape, "lout": lru_w_out.shape,
                   "rarx": (LR, RNN_BLOCKS, RW, 2 * blk_w), "small": (1, SMALL_ROWS, CW)}
    TOT = {n: lax.empty(s, F32) for n, s in shard_stack.items()}
    pend_exchange, pend_share = [], []

    ffn_names = ("w_in1", "w_out1", "w_in2", "w_out2")

    def take_exchange(pred):
        jobs = [job for job in pend_exchange if pred(job[0])]
        pend_exchange[:] = [job for job in pend_exchange if not pred(job[0])]
        return jobs

    def swap_share_comm(swap_jobs, exchange_too=None):
        share_jobs = list(pend_share)
        pend_share.clear()
        ex_jobs = take_exchange(exchange_too) if exchange_too is not None else []
        makers = []
        if swap_jobs:
            makers.append(_swap_stage([(g, SH2[n], 0) for n, _, g in swap_jobs]))
        if share_jobs:
            makers.append(_share_stage([(TOT[n], SH2[n], l) for n, l in share_jobs]))
        if ex_jobs:
            makers.append(_exchange_stage([(part, SH2[n]) for n, _, part in ex_jobs]))
        comm, _ = _combine(makers)

        def after(carried):
            for t, (n, l, _) in enumerate(swap_jobs):
                part = _pair_add_layer(carried[2 * t], SH2[n], 0, carried[2 * t + 1], c_arr)
                pend_exchange.append((n, l, part))
            k = 2 * len(swap_jobs)
            for t, (n, _) in enumerate(share_jobs):
                TOT[n] = carried[k + t]
            k += len(share_jobs)
            for t, (n, l, _) in enumerate(ex_jobs):
                TOT[n] = _chip_sum_layer(carried[k + 2 * t + 1], SH2[n], l, TOT[n], c_arr)
                pend_share.append((n, l))
        return comm, after

    def exchange_comm(pred=lambda n: True):
        jobs = take_exchange(pred)
        if not jobs:
            return None, None
        comm, _ = _combine([_exchange_stage([(part, SH2[n]) for n, _, part in jobs])])

        def after(carried):
            for t, (n, l, _) in enumerate(jobs):
                TOT[n] = _chip_sum_layer(carried[2 * t + 1], SH2[n], l, TOT[n], c_arr)
                pend_share.append((n, l))
        return comm, after

    def ffn_backward(dh, ln, g, hb_in, gu, act, n_in, n_out, layer, also_swap):
        dz, dyb, dg, db = _ln_bwd("ffn_ln_bwd", dh, ln[0], ln[1], g, 0.5)
        dgu = _ffn_dact(dyb, GW[n_out], layer, gu)
        g_out = _grad_tn("ffn_dwout", act, dyb, N2 // 4, D // 2, bk=8 * ROW_TILE)
        comm, after = exchange_comm(lambda n: n in ffn_names)
        g_in = _grad_tn("ffn_dwin", hb_in, dgu, D // 2, N2 // 4, bk=8 * ROW_TILE, comm=comm)
        if comm is not None:
            g_in, carried = g_in
            after(carried)
        comm, after = swap_share_comm([(n_in, layer, g_in), (n_out, layer, g_out)] + also_swap,
                                      exchange_too=lambda n: n not in ffn_names)
        dh_prev, carried = _ffn_dx(dgu, GW[n_in], layer, dz, alpha, comm)
        after(carried)
        return dh_prev, dg, db

    sg = [None] * SMALL_ROWS
    d_sinks = [None] * LA
    for i in reversed(range(L)):
        j = i // 2
        lay = saved[i]
        dh, sg[o_g + 3 * i + 2], sg[o_b + 3 * i + 2] = ffn_backward(
            dh, lay["ln3"], row_of(o_g + 3 * i + 2), lay["hb2"], lay["gu2"], lay["act2"], "w_in2", "w_out2", i, [])
        if i % 2 == 0:
            dz, dmb, sg[o_g + 3 * i + 1], sg[o_b + 3 * i + 1] = _ln_bwd("attn_ln_bwd", dh, *lay["ln2"], row_of(o_g + 3 * i + 1), 1.0)
            d_wo = _grad_tn("attn_dwo", lay["o"], dmb, _tile(Q_COLS, 1024), D)
            do = _back_proj("attn_do", dmb, g_wo, j, BF16)
            dq, dkv, dsk = _attn_bwd(lay["qkv"], do, sink_rows[j])
            d_sinks[j] = jnp.sum(dsk.reshape(N_HEADS, ATTN_BLOCK), axis=1)
            dqkv = _rope_bwd(dq, dkv, c128, s128)
            d_qkv = _grad_tn("attn_dwqkv", lay["hb1"], dqkv, D, dqkv.shape[1])
            comm, after = swap_share_comm([("wo", j, d_wo), ("qkv", j, d_qkv)])
            dh, carried = _input_grad("attn_dx", dqkv, g_qkv, j, dz, alpha, comm)
            after(carried)
        else:
            dz, dmb, sg[o_g + 3 * i + 1], sg[o_b + 3 * i + 1] = _ln_bwd("lru_ln_bwd", dh, *lay["ln2"], row_of(o_g + 3 * i + 1), 1.0)
            d_lout = _grad_tn("lru_dwout", lay["y"], dmb, C, D)
            dy = _back_proj("lru_dy", dmb, g_lout, j, F32)
            dhs, dgb = _lru_out_bwd(dy, lay["hs"], lay["xg"])
            adj = _scan_bwd(lay["a"], dhs)
            dpre, dxc_direct, sg[o_ra + j], sg[o_rx + j], sg[o_lam + j] = _lru_gate_bwd(
                lay["pre"], lay["xc"], adj, lay["hs"], row_of(o_ra + j), row_of(o_rx + j), row_of(o_lam + j))
            blk = C // RNN_BLOCKS
            d_rarx = _grad_tn("lru_dwgates", lay["xcb"], dpre, blk, 2 * blk, block_diag=True)
            bm = _tile(T, ROW_TILE)
            (dxc,) = _matmul("lru_dxc", (T // bm, RNN_BLOCKS, 1), (dpre, (bm, 2 * blk), lambda p, q, k: (p, q)),
                             (g_rarx, (None, None, blk, 2 * blk), lambda p, q, k, j=j: (j, q, 0, 0)),
                             [((T, C), F32, (bm, blk), lambda p, q: (p, q))], (bm, blk), tb=True,
                             extras=[(dxc_direct, (bm, blk), lambda p, q: (p, q))], epilogue=lambda acc, d: (acc + d,))
            res = _conv_bwd(dxc, lay["xg"], dgb, lay["cw"])
            dxg = res[0]
            for k in range(CONV_W):
                sg[o_cw + j * CONV_W + k] = res[1 + k]
            sg[o_cb + j] = res[1 + CONV_W]
            d_lin = _grad_tn("lru_dwin", lay["hb1"], dxg, D, _tile(2 * C, 1024))
            comm, after = swap_share_comm([("lout", j, d_lout), ("rarx", j, d_rarx), ("lin", j, d_lin)])
            dh, carried = _input_grad("lru_dx", dxg, g_lin, j, dz, alpha, comm)
            after(carried)
        dh, sg[o_g + 3 * i], sg[o_b + 3 * i] = ffn_backward(
            dh, lay["ln1"], row_of(o_g + 3 * i), lay["hb0"], lay["gu1"], lay["act1"], "w_in1", "w_out1", i, [])
    grad_x = dh.reshape(x.shape)

    sink_vec = jnp.concatenate(d_sinks).reshape(1, n_sink)
    sg[o_sink] = jnp.tile(jnp.concatenate([sink_vec, jnp.zeros((1, CW - n_sink), F32)], axis=1), (1, N_CHIPS))
    zero_row = jnp.zeros((1, C), F32)
    d_small = jnp.concatenate([zero_row if r is None else r for r in sg], axis=0).reshape(1, SMALL_ROWS, C)

    comm, after = swap_share_comm([("small", 0, d_small)])
    after(_comm_call("grad_tail_swap", comm))
    comm, after = exchange_comm()
    after(_comm_call("grad_tail_exchange", comm))
    comm, after = swap_share_comm([])
    after(_comm_call("grad_tail_share", comm))
    t_w_in1, t_w_out1, t_w_in2, t_w_out2, t_qkv, t_wo, t_lin, t_lout, t_rarx = [TOT[n] for n in gnames]
    t_small = TOT["small"].reshape(SMALL_ROWS, CW)

    G = {"ffn1_w_in": t_w_in1, "ffn1_w_out": t_w_out1, "ffn2_w_in": t_w_in2, "ffn2_w_out": t_w_out2,
         "attn_w_qkv": t_qkv, "attn_w_o": t_wo, "lru_w_in": t_lin, "lru_w_out": t_lout,
         "lru_w_ra": t_rarx[..., :blk_w], "lru_w_rx": t_rarx[..., blk_w:]}

    def unpack_small(p):
        return {"ln_g": p[o_g:o_g + 3 * L].reshape(ln_g.shape), "ln_b": p[o_b:o_b + 3 * L].reshape(ln_b.shape),
                "lru_conv_w": p[o_cw:o_cw + LR * CONV_W].reshape(lru_conv_w.shape), "lru_conv_b": p[o_cb:o_cb + LR],
                "lru_b_ra": p[o_ra:o_ra + LR], "lru_b_rx": p[o_rx:o_rx + LR], "lru_lambda": p[o_lam:o_lam + LR],
                "attn_sinks": p[o_sink, :n_sink].reshape(attn_sinks.shape)}

    G.update(unpack_small(t_small))

    delta, new_m, new_v = {}, {}, {}
    small_names = ["ln_g", "ln_b", "lru_conv_w", "lru_conv_b", "lru_b_ra", "lru_b_rx", "lru_lambda", "attn_sinks"]
    for n in names:
        if n not in small_names:
            G[n], delta[n], new_m[n], new_v[n] = _adamw(W[n], G[n], M[n], V[n])
    _, ds, ms, vs = _adamw(pack_small(W), t_small, pack_small(M), pack_small(V))
    for d, p in ((delta, ds), (new_m, ms), (new_v, vs)):
        d.update(unpack_small(p))

    return (loss, grad_x, *[G[n] for n in names], *[delta[n] for n in names], *[new_m[n] for n in names], *[new_v[n] for n in names])
```
